```python
import math
import jax, jax.numpy as jnp
from jax import lax
import numpy as np

D_MODEL = 1024
BATCH = 8
SEQ = 2048
DEPTH = 2
DEC_BATCH = 128
DEC_SEQ = 4
PAST_LEN = 16384
PAGE_SIZE = 128

F32 = jnp.float32
NORM_EPS = 1e-6
LB_FLOOR = 1e-30
CHUNK = 64
RW_HEADS = 8
RW_HD = 64
RW_W = RW_HEADS * RW_HD
RW_DECAY_LORA = 64
RW_AAA_LORA = 64
RW_GATE_LORA = 128
RW_COLS = 3 * RW_W + RW_DECAY_LORA + RW_AAA_LORA + RW_GATE_LORA
RW_GN_EPS = 64e-5
GD_HEADS = 4
GD_DK = 128
GD_DV = 128
GD_QKV = GD_HEADS * (2 * GD_DK + GD_DV)
GD_W = GD_HEADS * GD_DV
CONV_W = 4
GD_COLS = GD_QKV + GD_W + 2 * GD_HEADS
HG_HEADS = 4
HG_DF = 128
HG_DV = 128
HG_W = HG_HEADS * HG_DV
HG_COLS = 2 * HG_HEADS * HG_DF + 2 * HG_W
RT_HEADS = 4
RT_DK = 64
RT_DV = 128
RT_W = RT_HEADS * RT_DV
RT_COLS = 2 * RT_HEADS * RT_DK + 2 * RT_W
ROPE_BASE = 10000.0
N_BRANCH = 4
GATE_COLS = N_BRANCH * D_MODEL
N_IN = RW_COLS + GD_COLS + HG_COLS + RT_COLS + GATE_COLS
N_GROUPS = 4
EXPERTS_PER_GROUP = 8
N_EXPERTS = N_GROUPS * EXPERTS_PER_GROUP
TOP_K_IN_GROUP = 2
D_EXPERT = 256

kernel_name = 'hybrid_rwkv7_gdn_hgrn2_retnet_hmoe_step'


def rmsnorm(x, g):
    x32 = x.astype(F32)
    y = x32 * lax.rsqrt(jnp.mean(x32 * x32, -1, keepdims=True) + NORM_EPS) * g.astype(F32)
    return y.astype(x.dtype)


def head_rms(x):
    return x * lax.rsqrt(jnp.mean(x * x, -1, keepdims=True) + NORM_EPS)


def head_layernorm(x, eps):
    mu = jnp.mean(x, -1, keepdims=True)
    xc = x - mu
    return xc * lax.rsqrt(jnp.mean(xc * xc, -1, keepdims=True) + eps)


def l2norm(x):
    return x * lax.rsqrt(jnp.sum(x * x, -1, keepdims=True) + NORM_EPS)


def masked_exp(diff, mask):
    return jnp.where(mask, jnp.exp(jnp.where(mask, diff, 0.0)), 0.0)


def split_cols(x, sizes):
    return jnp.split(x, np.cumsum(sizes)[:-1].tolist(), axis=-1)


def to_chunks(x, c):
    b, l = x.shape[:2]
    x = x.reshape((b, l // c, c) + x.shape[2:])
    return jnp.moveaxis(jnp.moveaxis(x, 1, 0), 3, 2)


def from_chunks(o):
    o = jnp.moveaxis(jnp.moveaxis(o, 2, 3), 0, 1)
    b, n, c, h, d = o.shape
    return o.reshape(b, n * c, h, d)


def rotary(x, pos):
    half = x.shape[-1] // 2
    inv = ROPE_BASE ** (-jnp.arange(half, dtype=F32) / half)
    ang = pos[:, None] * inv[None, :]
    cos = jnp.cos(ang)[None, :, None, :]
    sin = jnp.sin(ang)[None, :, None, :]
    x1, x2 = x[..., :half], x[..., half:]
    return jnp.concatenate([x1 * cos - x2 * sin, x2 * cos + x1 * sin], -1)


def causal_conv(u, buf, w):
    l = u.shape[1]
    padded = jnp.concatenate([buf, u], axis=1)
    out = sum(padded[:, i:i + l] * w[i] for i in range(CONV_W))
    return out, padded[:, -(CONV_W - 1):]


def rwkv7_mix(p, shift0, s0, lp):
    b, l, _ = p.shape
    prev = jnp.concatenate([shift0[:, None], p[:, :-1]], axis=1)
    h = p + (prev - p) * lp['rwkv_mu']
    r, k, v, wl, al, gl = split_cols(h, (RW_W, RW_W, RW_W, RW_DECAY_LORA, RW_AAA_LORA, RW_GATE_LORA))
    w_log = -jax.nn.softplus(-(lp['rwkv_w0'] + jnp.tanh(wl) @ lp['rwkv_w2'])) - 0.5
    decay = jnp.exp(-jnp.exp(w_log))
    a = jax.nn.sigmoid(lp['rwkv_a0'] + al @ lp['rwkv_a2'])
    g = jax.nn.sigmoid(gl) @ lp['rwkv_g2']
    hs = lambda t: t.reshape(b, l, RW_HEADS, RW_HD)
    r, k, v, decay, a = hs(r), hs(k), hs(v), hs(decay), hs(a)
    kk = l2norm(k * lp['rwkv_k_k'].reshape(RW_HEADS, RW_HD))
    k = k * (1.0 + (a - 1.0) * lp['rwkv_k_a'].reshape(RW_HEADS, RW_HD))

    def step(s, inp):
        r_t, k_t, v_t, w_t, kk_t, a_t = inp
        sa = jnp.einsum('bhvk,bhk->bhv', s, -kk_t)
        s = (s * w_t[:, :, None, :] + jnp.einsum('bhv,bhk->bhvk', sa, kk_t * a_t)
             + jnp.einsum('bhv,bhk->bhvk', v_t, k_t))
        return s, jnp.einsum('bhvk,bhk->bhv', s, r_t)

    xs = tuple(jnp.moveaxis(t, 1, 0) for t in (r, k, v, decay, kk, a))
    s, o = lax.scan(step, s0, xs)
    o = jnp.moveaxis(o, 0, 1)
    o = head_layernorm(o, RW_GN_EPS).reshape(b, l, RW_W) * lp['rwkv_ln_g'] + lp['rwkv_ln_b']
    bonus = jnp.sum(r * k * lp['rwkv_r_k'], -1, keepdims=True) * v
    o = (o + bonus.reshape(b, l, RW_W)) * g
    return o, p[:, -1], s


def gated_delta_chunked(q, k, v, beta, g, s0):
    l = q.shape[1]
    c = math.gcd(l, CHUNK)
    dv = v.shape[-1]
    incl = jnp.tril(jnp.ones((c, c), bool))
    strict = jnp.tril(jnp.ones((c, c), bool), -1)

    def body(s, inp):
        qc, kc, vc, bc, gc = inp
        gcum = jnp.cumsum(gc, axis=-1)
        dec = masked_exp(gcum[..., :, None] - gcum[..., None, :], incl)
        m = jnp.where(strict, bc[..., :, None] * jnp.einsum('bhid,bhjd->bhij', kc, kc) * dec, 0.0)
        rhs = jnp.concatenate([vc * bc[..., None], kc * (bc * jnp.exp(gcum))[..., None]], axis=-1)
        sol = lax.linalg.triangular_solve(m, rhs, left_side=True, lower=True, unit_diagonal=True)
        u, w = sol[..., :dv], sol[..., dv:]
        v_new = u - jnp.einsum('bhik,bhkv->bhiv', w, s)
        attn = jnp.einsum('bhid,bhjd->bhij', qc, kc) * dec
        o = (jnp.einsum('bhik,bhkv->bhiv', qc * jnp.exp(gcum)[..., None], s)
             + jnp.einsum('bhij,bhjv->bhiv', attn, v_new))
        glast = gcum[..., -1:]
        s = s * jnp.exp(glast)[..., None] + jnp.einsum(
            'bhjk,bhjv->bhkv', kc * jnp.exp(glast - gcum)[..., None], v_new)
        return s, o

    xs = tuple(to_chunks(t, c) for t in (q, k, v, beta, g))
    s, o = lax.scan(body, s0, xs)
    return from_chunks(o), s


def gla_chunked(q, k, v, logf, s0):
    l = q.shape[1]
    c = math.gcd(l, CHUNK)
    incl = jnp.tril(jnp.ones((c, c), bool))[:, :, None]

    def body(s, inp):
        qc, kc, vc, fc = inp
        bcum = jnp.cumsum(fc, axis=2)
        pair = masked_exp(bcum[:, :, :, None, :] - bcum[:, :, None, :, :], incl)
        attn = jnp.einsum('bhid,bhijd,bhjd->bhij', qc, pair, kc)
        o = (jnp.einsum('bhid,bhdv->bhiv', qc * jnp.exp(bcum), s)
             + jnp.einsum('bhij,bhjv->bhiv', attn, vc))
        blast = bcum[:, :, -1:]
        s = s * jnp.exp(blast[:, :, 0])[..., None] + jnp.einsum(
            'bhjd,bhjv->bhdv', kc * jnp.exp(blast - bcum), vc)
        return s, o

    xs = tuple(to_chunks(t, c) for t in (q, k, v, logf))
    s, o = lax.scan(body, s0, xs)
    return from_chunks(o), s


def retention_chunked(q, k, v, loggamma, s0):
    l = q.shape[1]
    c = math.gcd(l, CHUNK)
    incl = jnp.tril(jnp.ones((c, c), bool))
    gcum = loggamma[:, None] * jnp.arange(1, c + 1, dtype=F32)
    dec = masked_exp(gcum[:, :, None] - gcum[:, None, :], incl)
    q_dec = jnp.exp(gcum)[:, :, None]
    k_dec = jnp.exp(gcum[:, -1:] - gcum)[:, :, None]
    s_dec = jnp.exp(gcum[:, -1])[:, None, None]

    def body(s, inp):
        qc, kc, vc = inp
        attn = jnp.einsum('bhid,bhjd->bhij', qc, kc) * dec
        o = (jnp.einsum('bhik,bhkv->bhiv', qc * q_dec, s)
             + jnp.einsum('bhij,bhjv->bhiv', attn, vc))
        s = s * s_dec + jnp.einsum('bhjk,bhjv->bhkv', kc * k_dec, vc)
        return s, o

    xs = tuple(to_chunks(t, c) for t in (q, k, v))
    s, o = lax.scan(body, s0, xs)
    return from_chunks(o), s


def gdn_mix(p, conv0, s0, lp):
    b, l, _ = p.shape
    pqkv, pz, pb, pa = split_cols(p, (GD_QKV, GD_W, GD_HEADS, GD_HEADS))
    cq, conv_new = causal_conv(pqkv, conv0, lp['gdn_conv'])
    cq = jax.nn.silu(cq)
    q, k, v = split_cols(cq, (GD_HEADS * GD_DK, GD_HEADS * GD_DK, GD_W))
    q = l2norm(q.reshape(b, l, GD_HEADS, GD_DK)) * (GD_DK ** -0.5)
    k = l2norm(k.reshape(b, l, GD_HEADS, GD_DK))
    v = v.reshape(b, l, GD_HEADS, GD_DV)
    beta = jax.nn.sigmoid(pb)
    g = -jnp.exp(lp['gdn_a_log']) * jax.nn.softplus(pa + lp['gdn_dt_bias'])
    o, s = gated_delta_chunked(q, k, v, beta, g, s0)
    o = (head_rms(o) * lp['gdn_norm_g']).reshape(b, l, GD_W) * jax.nn.silu(pz)
    return o, conv_new, s


def hgrn2_mix(p, s0, lb, lp):
    b, l, _ = p.shape
    pq, pf, pi, pog = split_cols(p, (HG_HEADS * HG_DF, HG_HEADS * HG_DF, HG_W, HG_W))
    q = jax.nn.silu(pq)
    logf = jnp.logaddexp(jnp.log(jnp.maximum(lb, LB_FLOOR)), jnp.log1p(-lb) + jax.nn.log_sigmoid(pf))
    k = (1.0 - lb) * jax.nn.sigmoid(-pf)
    hs = lambda t, d: t.reshape(b, l, HG_HEADS, d)
    o, s = gla_chunked(hs(q, HG_DF), hs(k, HG_DF), hs(pi, HG_DV), hs(logf, HG_DF), s0)
    o = head_rms(o).reshape(b, l, HG_W) * lp['hgrn_norm_g'] * jax.nn.sigmoid(pog)
    return o, s


def retention_mix(p, s0, pos0, lp):
    b, l, _ = p.shape
    pq, pk, pv, pg = split_cols(p, (RT_HEADS * RT_DK, RT_HEADS * RT_DK, RT_W, RT_W))
    pos = (pos0 + jnp.arange(l)).astype(F32)
    q = rotary(pq.reshape(b, l, RT_HEADS, RT_DK), pos)
    k = rotary(pk.reshape(b, l, RT_HEADS, RT_DK), pos) * (RT_DK ** -0.5)
    v = pv.reshape(b, l, RT_HEADS, RT_DV)
    loggamma = jnp.log(1.0 - jnp.exp2(-5.0 - jnp.arange(RT_HEADS, dtype=F32)))
    o, s = retention_chunked(q, k, v, loggamma, s0)
    o = head_layernorm(o, NORM_EPS).reshape(b, l, RT_W) * lp['ret_gn_g'] * jax.nn.silu(pg)
    return o, s


def hier_moe(xn, lp):
    b, l, d = xn.shape
    t = xn.reshape(b * l, d)
    glog = (t @ lp['router_group']).astype(F32) + lp['router_group_b']
    gprob = jax.nn.softmax(glog, axis=-1)
    gidx = jnp.argmax(gprob, axis=-1)
    gp = jnp.take_along_axis(gprob, gidx[:, None], axis=-1)
    elog = ((t @ lp['router_expert']).astype(F32) + lp['router_expert_b']).reshape(-1, N_GROUPS, EXPERTS_PER_GROUP)
    elog = jnp.take_along_axis(elog, gidx[:, None, None], axis=1)[:, 0]
    top_p, top_i = lax.top_k(jax.nn.softmax(elog, axis=-1), TOP_K_IN_GROUP)
    wts = gp * top_p / jnp.sum(top_p, -1, keepdims=True)
    eid = gidx[:, None] * EXPERTS_PER_GROUP + top_i
    comb = jnp.sum(jax.nn.one_hot(eid, N_EXPERTS, dtype=F32) * wts[..., None], axis=1)

    def expert(acc, inp):
        wg, wu, wd, cw = inp
        hid = jax.nn.silu(t @ wg) * (t @ wu)
        return acc + cw[:, None] * (hid @ wd).astype(F32), None

    acc, _ = lax.scan(expert, jnp.zeros((b * l, d), F32),
                      (lp['moe_w_gate'], lp['moe_w_up'], lp['moe_w_down'], comb.T))
    return acc.astype(xn.dtype).reshape(b, l, d)


def hybrid_layer(x, lp, lb, st, pos0):
    shift0, wkv0, conv0, gdn0, hgrn0, ret0 = st
    b, l, d = x.shape
    xn = rmsnorm(x, lp['norm1_g'])
    proj = (xn @ lp['w_in']).astype(F32)
    p_rw, p_gd, p_hg, p_rt, p_gate = split_cols(proj, (RW_COLS, GD_COLS, HG_COLS, RT_COLS, GATE_COLS))
    o_rw, shift1, wkv1 = rwkv7_mix(p_rw, shift0.astype(F32), wkv0.astype(F32), lp)
    o_gd, conv1, gdn1 = gdn_mix(p_gd, conv0.astype(F32), gdn0.astype(F32), lp)
    o_hg, hgrn1 = hgrn2_mix(p_hg, hgrn0.astype(F32), lb, lp)
    o_rt, ret1 = retention_mix(p_rt, ret0.astype(F32), pos0, lp)
    gates = jax.nn.sigmoid(p_gate).reshape(b, l, N_BRANCH, d)
    branches = (o_rw.astype(x.dtype) @ lp['w_out_rwkv'], o_gd.astype(x.dtype) @ lp['w_out_gdn'],
                o_hg.astype(x.dtype) @ lp['w_out_hgrn'], o_rt.astype(x.dtype) @ lp['w_out_ret'])
    merged = sum(gates[:, :, i] * br.astype(F32) for i, br in enumerate(branches))
    x = x + merged.astype(x.dtype) @ lp['w_o']
    x = x + hier_moe(rmsnorm(x, lp['norm2_g']), lp)
    new = (shift1, wkv1, conv1, gdn1, hgrn1, ret1)
    return x, tuple(n.astype(o.dtype) for n, o in zip(new, st))


def hgrn_lower_bounds(logits):
    sm = jax.nn.softmax(logits.astype(F32), axis=0)
    return jnp.cumsum(sm, axis=0) - sm[0]


def setup_inputs(seed: int = 0) -> dict:
    key = jax.random.key(seed)
    keys = jax.random.split(key, 64)
    counter = [0]

    def nk():
        kk = keys[counter[0]]
        counter[0] += 1
        return kk

    def nrm(shape, scale=1.0):
        return jax.random.normal(nk(), shape, F32) * scale

    def gain(shape):
        return 1.0 + nrm(shape, 0.01)

    L_ = DEPTH
    D = D_MODEL
    dt = jnp.exp(jax.random.uniform(nk(), (L_, GD_HEADS), F32, math.log(1e-3), math.log(1e-1)))
    inp = {}
    inp['x_prompt'] = nrm((BATCH, SEQ, D))
    inp['x_sample'] = nrm((DEC_BATCH, DEC_SEQ, D))
    inp['state_rwkv_shift'] = nrm((L_, DEC_BATCH, RW_COLS))
    inp['state_rwkv_wkv'] = nrm((L_, DEC_BATCH, RW_HEADS, RW_HD, RW_HD), 0.5)
    inp['state_gdn_conv'] = nrm((L_, DEC_BATCH, CONV_W - 1, GD_QKV))
    inp['state_gdn'] = nrm((L_, DEC_BATCH, GD_HEADS, GD_DK, GD_DV), 0.1)
    inp['state_hgrn'] = nrm((L_, DEC_BATCH, HG_HEADS, HG_DF, HG_DV), 0.5)
    inp['state_ret'] = nrm((L_, DEC_BATCH, RT_HEADS, RT_DK, RT_DV), 0.5)
    inp['norm1_g'] = gain((L_, D))
    inp['w_in'] = nrm((L_, D, N_IN), D ** -0.5)
    inp['rwkv_mu'] = jax.random.uniform(nk(), (L_, RW_COLS), F32)
    inp['rwkv_w0'] = -2.0 + nrm((L_, RW_W), 0.5)
    inp['rwkv_w2'] = nrm((L_, RW_DECAY_LORA, RW_W), 0.1)
    inp['rwkv_a0'] = nrm((L_, RW_W), 0.5)
    inp['rwkv_a2'] = nrm((L_, RW_AAA_LORA, RW_W), 0.1)
    inp['rwkv_g2'] = nrm((L_, RW_GATE_LORA, RW_W), RW_GATE_LORA ** -0.5)
    inp['rwkv_k_k'] = 0.85 + nrm((L_, RW_W), 0.1)
    inp['rwkv_k_a'] = 1.0 + nrm((L_, RW_W), 0.1)
    inp['rwkv_r_k'] = nrm((L_, RW_HEADS, RW_HD), 0.1)
    inp['rwkv_ln_g'] = gain((L_, RW_W))
    inp['rwkv_ln_b'] = nrm((L_, RW_W), 0.01)
    inp['w_out_rwkv'] = nrm((L_, RW_W, D), RW_W ** -0.5)
    inp['gdn_conv'] = nrm((L_, CONV_W, GD_QKV), CONV_W ** -0.5)
    inp['gdn_a_log'] = jnp.log(jax.random.uniform(nk(), (L_, GD_HEADS), F32, 1.0, 16.0))
    inp['gdn_dt_bias'] = dt + jnp.log(-jnp.expm1(-dt))
    inp['gdn_norm_g'] = gain((L_, GD_DV))
    inp['w_out_gdn'] = nrm((L_, GD_W, D), GD_W ** -0.5)
    inp['hgrn_lb_logits'] = nrm((L_, HG_HEADS * HG_DF))
    inp['hgrn_norm_g'] = gain((L_, HG_W))
    inp['w_out_hgrn'] = nrm((L_, HG_W, D), HG_W ** -0.5)
    inp['ret_gn_g'] = gain((L_, RT_W))
    inp['w_out_ret'] = nrm((L_, RT_W, D), RT_W ** -0.5)
    inp['w_o'] = nrm((L_, D, D), D ** -0.5)
    inp['norm2_g'] = gain((L_, D))
    inp['router_group'] = nrm((L_, D, N_GROUPS), D ** -0.5)
    inp['router_group_b'] = nrm((L_, N_GROUPS), 0.01)
    inp['router_expert'] = nrm((L_, D, N_EXPERTS), D ** -0.5)
    inp['router_expert_b'] = nrm((L_, N_EXPERTS), 0.01)
    inp['moe_w_gate'] = nrm((L_, N_EXPERTS, D, D_EXPERT), D ** -0.5)
    inp['moe_w_up'] = nrm((L_, N_EXPERTS, D, D_EXPERT), D ** -0.5)
    inp['moe_w_down'] = nrm((L_, N_EXPERTS, D_EXPERT, D), D_EXPERT ** -0.5)
    inp['final_norm_g'] = gain((D,))
    return inp


def reference(x_prompt, x_sample, state_rwkv_shift, state_rwkv_wkv, state_gdn_conv, state_gdn,
              state_hgrn, state_ret, norm1_g, w_in, rwkv_mu, rwkv_w0, rwkv_w2, rwkv_a0, rwkv_a2,
              rwkv_g2, rwkv_k_k, rwkv_k_a, rwkv_r_k, rwkv_ln_g, rwkv_ln_b, w_out_rwkv, gdn_conv,
              gdn_a_log, gdn_dt_bias, gdn_norm_g, w_out_gdn, hgrn_lb_logits, hgrn_norm_g, w_out_hgrn,
              ret_gn_g, w_out_ret, w_o, norm2_g, router_group, router_group_b, router_expert,
              router_expert_b, moe_w_gate, moe_w_up, moe_w_down, final_norm_g):
    params = dict(norm1_g=norm1_g, w_in=w_in, rwkv_mu=rwkv_mu, rwkv_w0=rwkv_w0, rwkv_w2=rwkv_w2,
                  rwkv_a0=rwkv_a0, rwkv_a2=rwkv_a2, rwkv_g2=rwkv_g2, rwkv_k_k=rwkv_k_k,
                  rwkv_k_a=rwkv_k_a, rwkv_r_k=rwkv_r_k, rwkv_ln_g=rwkv_ln_g, rwkv_ln_b=rwkv_ln_b,
                  w_out_rwkv=w_out_rwkv, gdn_conv=gdn_conv, gdn_a_log=gdn_a_log,
                  gdn_dt_bias=gdn_dt_bias, gdn_norm_g=gdn_norm_g, w_out_gdn=w_out_gdn,
                  hgrn_norm_g=hgrn_norm_g, w_out_hgrn=w_out_hgrn, ret_gn_g=ret_gn_g,
                  w_out_ret=w_out_ret, w_o=w_o, norm2_g=norm2_g, router_group=router_group,
                  router_group_b=router_group_b, router_expert=router_expert,
                  router_expert_b=router_expert_b, moe_w_gate=moe_w_gate, moe_w_up=moe_w_up,
                  moe_w_down=moe_w_down)
    lower_bounds = hgrn_lower_bounds(hgrn_lb_logits)

    def run_group(x, states, pos0):
        outs = [[] for _ in states]
        for layer in range(DEPTH):
            lp = {name: arr[layer] for name, arr in params.items()}
            x, new = hybrid_layer(x, lp, lower_bounds[layer], tuple(s[layer] for s in states), pos0)
            for lst, n in zip(outs, new):
                lst.append(n)
        return rmsnorm(x, final_norm_g), [jnp.stack(lst) for lst in outs]

    sample_states = (state_rwkv_shift, state_rwkv_wkv, state_gdn_conv, state_gdn, state_hgrn, state_ret)
    prompt_init = tuple(jnp.zeros((DEPTH, x_prompt.shape[0]) + s.shape[2:], s.dtype) for s in sample_states)
    y_prompt, (p_shift, p_wkv, p_conv, p_gdn, p_hgrn, p_ret) = run_group(x_prompt, prompt_init, 0)
    y_sample, (s_shift, s_wkv, s_conv, s_gdn, s_hgrn, s_ret) = run_group(x_sample, sample_states, PAST_LEN)
    return (y_prompt, y_sample, p_shift, p_wkv, p_conv, p_gdn, p_hgrn, p_ret,
            s_shift, s_wkv, s_conv, s_gdn, s_hgrn, s_ret)
```

```python
import functools
import math

import numpy as np
import jax
import jax.numpy as jnp
from jax import lax
from jax.experimental import pallas as pl
from jax.experimental.pallas import tpu as pltpu

F32 = jnp.float32
BF16 = jnp.bfloat16
HI = lax.Precision.HIGHEST

NORM_EPS = 1e-6
LB_FLOOR = 1e-30
PAST_LEN = 16384
RW_HEADS = 8
RW_HD = 64
RW_W = RW_HEADS * RW_HD
RW_DECAY_LORA = 64
RW_AAA_LORA = 64
RW_GATE_LORA = 128
RW_COLS = 3 * RW_W + RW_DECAY_LORA + RW_AAA_LORA + RW_GATE_LORA
RW_GN_EPS = 64e-5
GD_HEADS = 4
GD_DK = 128
GD_DV = 128
GD_QKV = GD_HEADS * (2 * GD_DK + GD_DV)
GD_W = GD_HEADS * GD_DV
CONV_W = 4
HG_HEADS = 4
HG_DF = 128
HG_DV = 128
HG_W = HG_HEADS * HG_DV
HG_COLS = 2 * HG_HEADS * HG_DF + 2 * HG_W
RT_HEADS = 4
RT_DK = 64
RT_DV = 128
RT_W = RT_HEADS * RT_DV
RT_QK = RT_HEADS * RT_DK
RT_COLS = 2 * RT_QK + 2 * RT_W
ROPE_BASE = 10000.0
N_BRANCH = 4
N_GROUPS = 4
EXPERTS_PER_GROUP = 8
N_EXPERTS = N_GROUPS * EXPERTS_PER_GROUP

LANES = 128
SUBLANES = 8
SAMPLE_PAD_LEN = 8
VMEM_LIMIT = 48 * 1024 * 1024


def _pick(n, cands):
    for c in cands:
        if n % c == 0:
            return c
    raise ValueError(f"no tile for {n} in {cands}")


def _cparams(sem):
    return pltpu.CompilerParams(dimension_semantics=sem, vmem_limit_bytes=VMEM_LIMIT)


def _dot(a, b, prec=None):
    return lax.dot_general(a, b, (((1,), (0,)), ((), ())), precision=prec, preferred_element_type=F32)


def _dot_nt(a, b, prec=None):
    return lax.dot_general(a, b, (((1,), (1,)), ((), ())), precision=prec, preferred_element_type=F32)


def _dot_tn(a, b, prec=None):
    return lax.dot_general(a, b, (((0,), (0,)), ((), ())), precision=prec, preferred_element_type=F32)


def _softplus(x):
    return jnp.maximum(x, 0.0) + jnp.log1p(jnp.exp(-jnp.abs(x)))


def _sigmoid(x):
    return jax.nn.sigmoid(x)


def _silu(x):
    return x * jax.nn.sigmoid(x)


def _segsum(x, hm):
    hi = x.astype(BF16)
    r1 = x - hi.astype(F32)
    mid = r1.astype(BF16)
    lo = (r1 - mid.astype(F32)).astype(BF16)
    return _dot(hi, hm) + _dot(mid, hm) + _dot(lo, hm)


def _iota(shape, dim):
    return lax.broadcasted_iota(jnp.int32, shape, dim)


def _rms_kernel(x_ref, g_ref, o_ref):
    x = x_ref[...]
    ms = jnp.mean(x * x, axis=-1, keepdims=True)
    o_ref[...] = (x * lax.rsqrt(ms + NORM_EPS) * g_ref[...]).astype(o_ref.dtype)


def _rmsnorm(x, g, out_dtype):
    n, d = x.shape
    tm = _pick(n, (1024, 512, 256, 128, 64, 32, 16))
    return pl.pallas_call(
        _rms_kernel,
        out_shape=jax.ShapeDtypeStruct((n, d), out_dtype),
        grid=(n // tm,),
        in_specs=[pl.BlockSpec((tm, d), lambda i: (i, 0)), pl.BlockSpec((1, d), lambda i: (0, 0))],
        out_specs=pl.BlockSpec((tm, d), lambda i: (i, 0)),
        compiler_params=_cparams(("parallel",)),
        name="rmsnorm",
    )(x, g.reshape(1, d))


def _mm_kernel(x_ref, w_ref, o_ref):
    o_ref[...] = _dot(x_ref[...], w_ref[...])


def _matmul(x, w, name):
    n, k = x.shape
    m = w.shape[1]
    tm = _pick(n, (1024, 512, 256, 128, 64, 32, 16))
    tn = _pick(m, (1024, 896, 768, 512, 256, 128))
    return pl.pallas_call(
        _mm_kernel,
        out_shape=jax.ShapeDtypeStruct((n, m), F32),
        grid=(n // tm, m // tn),
        in_specs=[pl.BlockSpec((tm, k), lambda i, j: (i, 0)), pl.BlockSpec((k, tn), lambda i, j: (0, j))],
        out_specs=pl.BlockSpec((tm, tn), lambda i, j: (i, j)),
        compiler_params=_cparams(("parallel", "parallel")),
        name=name,
    )(x, w)


def _rwkv_prep_kernel(p_ref, prev_ref, mu_ref, vec_ref, w2_ref, a2_ref, g2_ref, hm_ref,
                      w_o, nkk_o, bb_o, km_o, wr_o, v_o, br_o, kr_o, bon_o, g_o):
    p = p_ref[...]
    h = p + (prev_ref[...] - p) * mu_ref[...]
    r = h[:, 0:RW_W]
    k = h[:, RW_W:2 * RW_W]
    v = h[:, 2 * RW_W:3 * RW_W]
    lo = h[:, 3 * RW_W:3 * RW_W + LANES]
    gl = h[:, 3 * RW_W + LANES:3 * RW_W + 2 * LANES]
    vec = vec_ref[...]
    w0, a0, k_k, k_a, r_k = vec[0:1], vec[1:2], vec[2:3], vec[3:4], vec[4:5]
    hm = hm_ref[...]
    w_log = -_softplus(-(w0 + _dot(jnp.tanh(lo), w2_ref[...], HI))) - 0.5
    decay = jnp.exp(-jnp.exp(w_log))
    a = _sigmoid(a0 + _dot(lo, a2_ref[...], HI))
    g = _dot(_sigmoid(gl), g2_ref[...], HI)
    kk = k * k_k
    kk = kk * lax.rsqrt(_segsum(kk * kk, hm) + NORM_EPS)
    km = k * (1.0 + (a - 1.0) * k_a)
    bb = kk * a
    w_o[...] = decay
    nkk_o[...] = -kk
    bb_o[...] = bb
    km_o[...] = km
    wr_o[...] = decay * r
    v_o[...] = v
    br_o[...] = _segsum(bb * r, hm)
    kr_o[...] = _segsum(km * r, hm)
    bon_o[...] = _segsum(r * km * r_k, hm) * v
    g_o[...] = g


def _head_sum_matrix(width, seg):
    i = np.arange(width)
    return jnp.asarray((i[:, None] // seg) == (i[None, :] // seg), BF16)


def _rwkv_prep(p, prev, lp):
    n = p.shape[0]
    tm = _pick(n, (256, 128, 64, 32, 16, 8))
    vec = jnp.zeros((SUBLANES, RW_W), F32)
    vec = vec.at[0].set(lp['rwkv_w0']).at[1].set(lp['rwkv_a0']).at[2].set(lp['rwkv_k_k'])
    vec = vec.at[3].set(lp['rwkv_k_a']).at[4].set(lp['rwkv_r_k'].reshape(RW_W))
    zeros = jnp.zeros((RW_DECAY_LORA, RW_W), F32)
    w2p = jnp.concatenate([lp['rwkv_w2'], zeros], axis=0)
    a2p = jnp.concatenate([zeros, lp['rwkv_a2']], axis=0)
    hm = _head_sum_matrix(RW_W, RW_HD)
    row = lambda w: pl.BlockSpec((tm, w), lambda i: (i, 0))
    full = lambda a: pl.BlockSpec(a.shape, lambda i: (0,) * a.ndim)
    mu = lp['rwkv_mu'].reshape(1, RW_COLS)
    outs = pl.pallas_call(
        _rwkv_prep_kernel,
        out_shape=[jax.ShapeDtypeStruct((n, RW_W), F32)] * 10,
        grid=(n // tm,),
        in_specs=[row(RW_COLS), row(RW_COLS), full(mu), full(vec), full(w2p), full(a2p),
                  full(lp['rwkv_g2']), full(hm)],
        out_specs=[row(RW_W)] * 10,
        compiler_params=_cparams(("parallel",)),
        name="rwkv_prep",
    )(p, prev, mu, vec, w2p, a2p, lp['rwkv_g2'], hm)
    return outs


def _wkv_kernel(nkk_ref, w_ref, b_ref, k_ref, wr_ref, v_ref, aux_ref, s0_ref, o_ref, s1_ref, s_scr, *, tb, nv):
    t_blk = pl.program_id(1)

    @pl.when(t_blk == 0)
    def _():
        s_scr[...] = s0_ref[0]

    def step(t, carry):
        nkk = nkk_ref[0, t]
        w = w_ref[0, t]
        bb = b_ref[0, t]
        kt = k_ref[0, t]
        wr = wr_ref[0, t]
        vt = v_ref[0, t]
        aux = aux_ref[0, t]
        br = aux[0:1]
        kr = aux[1:2]
        rows = []
        for vi in range(nv):
            sv = s_scr[vi]
            sa = jnp.sum(sv * nkk, axis=0, keepdims=True)
            ow = jnp.sum(sv * wr, axis=0, keepdims=True)
            vrow = vt[vi:vi + 1]
            rows.append(ow + sa * br + vrow * kr)
            s_scr[vi] = sv * w + sa * bb + vrow * kt
        o_ref[0, t] = jnp.concatenate(rows, axis=0)
        return carry

    lax.fori_loop(0, tb, step, 0)

    @pl.when(t_blk == pl.num_programs(1) - 1)
    def _():
        s1_ref[0] = s_scr[...]


def _wkv_scan(ktype, vv, aux, s0):
    g, l, _, _ = ktype[0].shape
    nv = vv.shape[2]
    tb = _pick(l, (32, 16, 8, 4, 2, 1))
    kspec = pl.BlockSpec((1, tb, RW_HD, LANES), lambda i, j: (i, j, 0, 0))
    vspec = pl.BlockSpec((1, tb, nv, LANES), lambda i, j: (i, j, 0, 0))
    aspec = pl.BlockSpec((1, tb, SUBLANES, LANES), lambda i, j: (i, j, 0, 0))
    sspec = pl.BlockSpec((1, nv, RW_HD, LANES), lambda i, j: (i, 0, 0, 0))
    return pl.pallas_call(
        functools.partial(_wkv_kernel, tb=tb, nv=nv),
        out_shape=[jax.ShapeDtypeStruct((g, l, nv, LANES), F32),
                   jax.ShapeDtypeStruct((g, nv, RW_HD, LANES), F32)],
        grid=(g, l // tb),
        in_specs=[kspec] * 5 + [vspec, aspec, sspec],
        out_specs=[vspec, sspec],
        scratch_shapes=[pltpu.VMEM((nv, RW_HD, LANES), F32)],
        compiler_params=_cparams(("parallel", "arbitrary")),
        name="wkv_scan",
    )(*ktype, vv, aux, s0)


def _rwkv_post_kernel(o_ref, bon_ref, g_ref, ln_ref, hm_ref, out_ref):
    o = o_ref[...]
    hm = hm_ref[...]
    ln = ln_ref[...]
    mu = _segsum(o, hm) * (1.0 / RW_HD)
    xc = o - mu
    var = _segsum(xc * xc, hm) * (1.0 / RW_HD)
    y = xc * lax.rsqrt(var + RW_GN_EPS) * ln[0:1] + ln[1:2]
    out_ref[...] = ((y + bon_ref[...]) * g_ref[...]).astype(out_ref.dtype)


def _rwkv_post(o, bon, g, lp):
    n = o.shape[0]
    tm = _pick(n, (512, 256, 128, 64, 32, 16))
    ln = jnp.zeros((SUBLANES, RW_W), F32).at[0].set(lp['rwkv_ln_g']).at[1].set(lp['rwkv_ln_b'])
    hm = _head_sum_matrix(RW_W, RW_HD)
    row = pl.BlockSpec((tm, RW_W), lambda i: (i, 0))
    full = lambda a: pl.BlockSpec(a.shape, lambda i: (0,) * a.ndim)
    return pl.pallas_call(
        _rwkv_post_kernel,
        out_shape=jax.ShapeDtypeStruct((n, RW_W), BF16),
        grid=(n // tm,),
        in_specs=[row, row, row, full(ln), full(hm)],
        out_specs=row,
        compiler_params=_cparams(("parallel",)),
        name="rwkv_post",
    )(o, bon, g, ln, hm)


def _scan_layout(bsz, heads):
    bh = bsz * heads
    if bh >= LANES:
        groups, bg, rep = bh // LANES, LANES // heads, 1
    else:
        groups, bg, rep = 1, bsz, LANES // bh
    return groups, bg, rep, RW_HD // rep


def _rwkv_scan_group(arrs, bsz, seq, lpad, row0, s0):
    h, hd = RW_HEADS, RW_HD
    groups, bg, rep, nv = _scan_layout(bsz, h)

    def rows(a):
        return a[row0:row0 + bsz * lpad].reshape(bsz, lpad, h, hd)[:, :seq]

    def ktype(a):
        a = rows(a).reshape(groups, bg, seq, h, hd)
        a = jnp.transpose(a, (0, 2, 4, 1, 3)).reshape(groups, seq, hd, bg * h)
        return jnp.tile(a, (1, 1, 1, rep))

    def vtype(a):
        a = rows(a).reshape(groups, bg, seq, h, rep, nv)
        return jnp.transpose(a, (0, 2, 5, 4, 1, 3)).reshape(groups, seq, nv, LANES)

    def scalar(a):
        a = rows(a)[..., 0].reshape(groups, bg, seq, h)
        a = jnp.transpose(a, (0, 2, 1, 3)).reshape(groups, seq, bg * h)
        return jnp.tile(a, (1, 1, rep))

    aux = jnp.zeros((groups, seq, SUBLANES, LANES), F32)
    aux = aux.at[:, :, 0].set(scalar(arrs['br'])).at[:, :, 1].set(scalar(arrs['kr']))
    s = s0.reshape(groups, bg, h, rep, nv, hd)
    s = jnp.transpose(s, (0, 4, 5, 3, 1, 2)).reshape(groups, nv, hd, LANES)
    o, s1 = _wkv_scan([ktype(arrs[n]) for n in ('nkk', 'w', 'bb', 'km', 'wr')], vtype(arrs['v']), aux, s)
    o = o.reshape(groups, seq, nv, rep, bg, h)
    o = jnp.transpose(o, (0, 4, 1, 5, 3, 2)).reshape(bsz, seq, RW_W)
    s1 = s1.reshape(groups, nv, hd, rep, bg, h)
    s1 = jnp.transpose(s1, (0, 4, 5, 3, 1, 2)).reshape(bsz, h, hd, hd)
    return o, s1


def _gdn_kernel(qkv_ref, z_ref, ba_ref, convw_ref, hp_ref, ng_ref, tail0_ref, s0_ref,
                o_ref, s1_ref, ext_scr, s_scr, x_scr, *, c, n_valid):
    ci = pl.program_id(1)

    @pl.when(ci == 0)
    def _():
        ext_scr[0:SUBLANES, :] = tail0_ref[0]
        s_scr[...] = s0_ref[0]

    ext_scr[SUBLANES:SUBLANES + c, :] = qkv_ref[...]
    cw = convw_ref[...]
    off = SUBLANES - (CONV_W - 1)
    cq = ext_scr[off:off + c, :] * cw[0:1]
    for j in range(1, CONV_W):
        cq = cq + ext_scr[off + j:off + j + c, :] * cw[j:j + 1]
    ext_scr[0:SUBLANES, :] = ext_scr[c:c + SUBLANES, :]
    cq = _silu(cq)

    rid = _iota((c, c), 0)
    cid = _iota((c, c), 1)
    incl = rid >= cid
    tri = incl.astype(F32)
    strict_l = (rid > cid).astype(F32)
    strict_u = (rid < cid).astype(F32)
    upper_i = (rid <= cid).astype(F32)

    ba = ba_ref[...]
    hp = hp_ref[...]
    beta_all = _sigmoid(ba)
    g_all = -jnp.exp(hp[0:1]) * _softplus(ba + hp[1:2])
    if n_valid < c:
        valid = _iota((c, LANES), 0) < n_valid
        beta_all = jnp.where(valid, beta_all, 0.0)
        g_all = jnp.where(valid, g_all, 0.0)
    gcum_all = _dot(tri, g_all, HI)
    ng = ng_ref[...]
    z = z_ref[...]

    for h in range(GD_HEADS):
        q = cq[:, h * GD_DK:(h + 1) * GD_DK]
        k = cq[:, GD_HEADS * GD_DK + h * GD_DK:GD_HEADS * GD_DK + (h + 1) * GD_DK]
        v = cq[:, 2 * GD_HEADS * GD_DK + h * GD_DV:2 * GD_HEADS * GD_DK + (h + 1) * GD_DV]
        q = q * lax.rsqrt(jnp.sum(q * q, axis=-1, keepdims=True) + NORM_EPS) * (GD_DK ** -0.5)
        k = k * lax.rsqrt(jnp.sum(k * k, axis=-1, keepdims=True) + NORM_EPS)
        beta = beta_all[:, h:h + 1]
        g_col = g_all[:, GD_HEADS + h:GD_HEADS + h + 1]
        gc = gcum_all[:, GD_HEADS + h:GD_HEADS + h + 1]
        glast = gc[c - 1:c, :]
        d = _dot(tri, g_col * strict_l, HI)
        dt = _dot(strict_u, g_col * upper_i, HI)
        dec = jnp.where(incl, jnp.exp(d), 0.0)
        mt = strict_u * _dot_nt(k, k * beta, HI) * jnp.exp(jnp.minimum(dt, 0.0))
        egc = jnp.exp(gc)
        x_scr[h] = jnp.concatenate([v * beta, k * (beta * egc)], axis=1)
        for i in range(1, c):
            col = mt[:, i:i + 1]
            x_scr[h, i:i + 1, :] = x_scr[h, i:i + 1, :] - jnp.sum(col * x_scr[h], axis=0, keepdims=True)
        xs = x_scr[h]
        u = xs[:, 0:GD_DV]
        wm = xs[:, GD_DV:GD_DV + GD_DK]
        s = s_scr[h]
        v_new = u - _dot(wm, s, HI)
        attn = _dot_nt(q, k, HI) * dec
        o = _dot(q * egc, s, HI) + _dot(attn, v_new, HI)
        s_scr[h] = s * jnp.exp(glast) + _dot_tn(k * jnp.exp(glast - gc), v_new, HI)
        o = o * lax.rsqrt(jnp.mean(o * o, axis=-1, keepdims=True) + NORM_EPS) * ng
        o_ref[:, h * GD_DV:(h + 1) * GD_DV] = (o * _silu(z[:, h * GD_DV:(h + 1) * GD_DV])).astype(o_ref.dtype)

    @pl.when(ci == pl.num_programs(1) - 1)
    def _():
        s1_ref[0] = s_scr[...]


def _gdn_group(pqkv, pz, pba, lp, bsz, lpad, n_valid_len, row0, conv0, s0):
    c = _pick(lpad, (64, 32, 16, 8))
    nc = lpad // c
    n_valid = c if nc > 1 else n_valid_len
    rb = row0 // c
    tail0 = jnp.concatenate([jnp.zeros((bsz, SUBLANES - (CONV_W - 1), GD_QKV), F32), conv0], axis=1)
    hp = jnp.zeros((SUBLANES, LANES), F32)
    hp = hp.at[0, GD_HEADS:2 * GD_HEADS].set(lp['gdn_a_log']).at[1, GD_HEADS:2 * GD_HEADS].set(lp['gdn_dt_bias'])
    ng = lp['gdn_norm_g'].reshape(1, GD_DV)
    rowspec = lambda w: pl.BlockSpec((c, w), lambda b, j: (rb + b * nc + j, 0))
    full = lambda a: pl.BlockSpec(a.shape, lambda b, j: (0,) * a.ndim)
    sspec = pl.BlockSpec((1, GD_HEADS, GD_DK, GD_DV), lambda b, j: (b, 0, 0, 0))
    o, s1 = pl.pallas_call(
        functools.partial(_gdn_kernel, c=c, n_valid=n_valid),
        out_shape=[jax.ShapeDtypeStruct((bsz * lpad, GD_W), BF16),
                   jax.ShapeDtypeStruct((bsz, GD_HEADS, GD_DK, GD_DV), F32)],
        grid=(bsz, nc),
        in_specs=[rowspec(GD_QKV), rowspec(GD_W), rowspec(LANES), full(lp['gdn_conv']), full(hp), full(ng),
                  pl.BlockSpec((1, SUBLANES, GD_QKV), lambda b, j: (b, 0, 0)), sspec],
        out_specs=[pl.BlockSpec((c, GD_W), lambda b, j: (b * nc + j, 0)), sspec],
        scratch_shapes=[pltpu.VMEM((c + SUBLANES, GD_QKV), F32), pltpu.VMEM((GD_HEADS, GD_DK, GD_DV), F32),
                        pltpu.VMEM((GD_HEADS, c, GD_DV + GD_DK), F32)],
        compiler_params=_cparams(("parallel", "arbitrary")),
        name="gdn_chunk",
    )(pqkv, pz, pba, lp['gdn_conv'], hp, ng, tail0, s0)
    return o, s1


def _hgrn_kernel(p_ref, lbv_ref, ng_ref, s0_ref, o_ref, s1_ref, st_scr, *, c, n_valid):
    ci = pl.program_id(1)

    @pl.when(ci == 0)
    def _():
        for h in range(HG_HEADS):
            st_scr[h] = jnp.transpose(s0_ref[0, h])

    rid = _iota((c, c), 0)
    cid = _iota((c, c), 1)
    tri = (rid >= cid).astype(F32)
    rows = _iota((c, HG_DF), 0)
    lbv = lbv_ref[...]
    ng = ng_ref[...]
    w = HG_HEADS * HG_DF

    for h in range(HG_HEADS):
        pq = p_ref[:, h * HG_DF:(h + 1) * HG_DF]
        pf = p_ref[:, w + h * HG_DF:w + (h + 1) * HG_DF]
        v = p_ref[:, 2 * w + h * HG_DV:2 * w + (h + 1) * HG_DV]
        pog = p_ref[:, 2 * w + HG_W + h * HG_DV:2 * w + HG_W + (h + 1) * HG_DV]
        llb = lbv[0:1, h * HG_DF:(h + 1) * HG_DF]
        l1m = lbv[1:2, h * HG_DF:(h + 1) * HG_DF]
        oml = lbv[2:3, h * HG_DF:(h + 1) * HG_DF]
        q = _silu(pq)
        a = llb
        b = l1m - _softplus(-pf)
        logf = jnp.maximum(a, b) + jnp.log1p(jnp.exp(-jnp.abs(a - b)))
        k = oml * _sigmoid(-pf)
        if n_valid < c:
            logf = jnp.where(rows < n_valid, logf, 0.0)
            k = jnp.where(rows < n_valid, k, 0.0)
        bc = _dot(tri, logf, HI)
        st = st_scr[h]
        o = _dot_nt(q * jnp.exp(bc), st, HI)
        orows = []
        for i in range(c):
            e = jnp.where(rows <= i, jnp.exp(jnp.minimum(bc[i:i + 1] - bc, 0.0)), 0.0)
            s_col = jnp.sum(q[i:i + 1] * k * e, axis=-1, keepdims=True)
            orows.append(jnp.sum(s_col * v, axis=0, keepdims=True))
        o = o + jnp.concatenate(orows, axis=0)
        blast = bc[c - 1:c]
        st_scr[h] = st * jnp.exp(blast) + _dot_tn(v, k * jnp.exp(blast - bc), HI)
        o = o * lax.rsqrt(jnp.mean(o * o, axis=-1, keepdims=True) + NORM_EPS)
        o = o * ng[:, h * HG_DV:(h + 1) * HG_DV] * _sigmoid(pog)
        o_ref[:, h * HG_DV:(h + 1) * HG_DV] = o.astype(o_ref.dtype)

    @pl.when(ci == pl.num_programs(1) - 1)
    def _():
        for h in range(HG_HEADS):
            s1_ref[0, h] = jnp.transpose(st_scr[h])


def _hgrn_group(phg, lb, lp, bsz, lpad, n_valid_len, row0, s0):
    c = _pick(lpad, (16, 8))
    nc = lpad // c
    n_valid = c if nc > 1 else n_valid_len
    rb = row0 // c
    lbv = jnp.zeros((SUBLANES, HG_HEADS * HG_DF), F32)
    lbv = lbv.at[0].set(jnp.log(jnp.maximum(lb, LB_FLOOR))).at[1].set(jnp.log1p(-lb)).at[2].set(1.0 - lb)
    ng = lp['hgrn_norm_g'].reshape(1, HG_W)
    full = lambda a: pl.BlockSpec(a.shape, lambda b, j: (0,) * a.ndim)
    sspec = pl.BlockSpec((1, HG_HEADS, HG_DF, HG_DV), lambda b, j: (b, 0, 0, 0))
    return pl.pallas_call(
        functools.partial(_hgrn_kernel, c=c, n_valid=n_valid),
        out_shape=[jax.ShapeDtypeStruct((bsz * lpad, HG_W), BF16),
                   jax.ShapeDtypeStruct((bsz, HG_HEADS, HG_DF, HG_DV), F32)],
        grid=(bsz, nc),
        in_specs=[pl.BlockSpec((c, HG_COLS), lambda b, j: (rb + b * nc + j, 0)), full(lbv), full(ng), sspec],
        out_specs=[pl.BlockSpec((c, HG_W), lambda b, j: (b * nc + j, 0)), sspec],
        scratch_shapes=[pltpu.VMEM((HG_HEADS, HG_DV, HG_DF), F32)],
        compiler_params=_cparams(("parallel", "arbitrary")),
        name="hgrn_chunk",
    )(phg, lbv, ng, s0)


def _ret_kernel(p_ref, cos_ref, sin_ref, qd_ref, kd_ref, dec_ref, dm_ref, gn_ref, s0_ref,
                o_ref, s1_ref, s_scr, *, c):
    ci = pl.program_id(1)

    @pl.when(ci == 0)
    def _():
        s_scr[...] = jnp.zeros_like(s_scr)
        for h in range(RT_HEADS):
            s_scr[h * RT_DK:(h + 1) * RT_DK, h * RT_DV:(h + 1) * RT_DV] = s0_ref[0, h]

    pq = p_ref[:, 0:RT_QK]
    pk = p_ref[:, RT_QK:2 * RT_QK]
    pv = p_ref[:, 2 * RT_QK:2 * RT_QK + RT_W]
    cos = cos_ref[...]
    sin = sin_ref[...]
    lane = _iota((c, RT_QK), 1)
    first_half = (lane % RT_DK) < (RT_DK // 2)

    def rope(x):
        partner = jnp.where(first_half, pltpu.roll(x, RT_QK - RT_DK // 2, 1), pltpu.roll(x, RT_DK // 2, 1))
        return x * cos + partner * sin

    q = rope(pq)
    k = rope(pk) * (RT_DK ** -0.5)
    sblk = s_scr[...]
    o_inter = _dot(q * qd_ref[...], sblk, HI)
    gn = gn_ref[...]
    for h in range(RT_HEADS):
        qh = jnp.where(lane // RT_DK == h, q, 0.0)
        attn = _dot_nt(qh, k, HI) * dec_ref[h]
        v = pv[:, h * RT_DV:(h + 1) * RT_DV]
        o = o_inter[:, h * RT_DV:(h + 1) * RT_DV] + _dot(attn, v, HI)
        mu = jnp.mean(o, axis=-1, keepdims=True)
        xc = o - mu
        o = xc * lax.rsqrt(jnp.mean(xc * xc, axis=-1, keepdims=True) + NORM_EPS)
        pg = p_ref[:, 2 * RT_QK + RT_W + h * RT_DV:2 * RT_QK + RT_W + (h + 1) * RT_DV]
        o_ref[:, h * RT_DV:(h + 1) * RT_DV] = (o * gn[:, h * RT_DV:(h + 1) * RT_DV] * _silu(pg)).astype(o_ref.dtype)
    dm = dm_ref[...]
    s_scr[...] = sblk * dm + jnp.where(dm > 0.0, _dot_tn(k * kd_ref[...], pv, HI), 0.0)

    @pl.when(ci == pl.num_programs(1) - 1)
    def _():
        for h in range(RT_HEADS):
            s1_ref[0, h] = s_scr[h * RT_DK:(h + 1) * RT_DK, h * RT_DV:(h + 1) * RT_DV]


def _ret_group(prt, lp, bsz, lpad, n_valid_len, row0, pos0, s0):
    c = _pick(lpad, (64, 32, 16, 8))
    nc = lpad // c
    n_valid = c if nc > 1 else n_valid_len
    rb = row0 // c
    half = RT_DK // 2
    inv = ROPE_BASE ** (-np.arange(half, dtype=np.float64) / half)
    ang = (pos0 + np.arange(lpad, dtype=np.float64))[:, None] * inv[None, :]
    cos = np.tile(np.cos(ang), (1, 2 * RT_HEADS))
    sin = np.tile(np.concatenate([-np.sin(ang), np.sin(ang)], axis=1), (1, RT_HEADS))
    loggamma = np.log(1.0 - np.exp2(-5.0 - np.arange(RT_HEADS, dtype=np.float64)))
    gcum = loggamma[:, None] * np.arange(1, c + 1, dtype=np.float64)[None, :]
    idx = np.arange(c)
    dec = np.where(idx[:, None] >= idx[None, :], np.exp(gcum[:, :, None] - gcum[:, None, :]), 0.0)
    qd = np.repeat(np.exp(gcum).T, RT_DK, axis=1)
    kdec = np.where(idx[None, :] < n_valid, np.exp(gcum[:, n_valid - 1:n_valid] - gcum), 0.0)
    kd = np.repeat(kdec.T, RT_DK, axis=1)
    sdec = np.exp(gcum[:, n_valid - 1])
    dm = np.zeros((RT_QK, RT_W))
    for h in range(RT_HEADS):
        dm[h * RT_DK:(h + 1) * RT_DK, h * RT_DV:(h + 1) * RT_DV] = sdec[h]
    cos, sin, qd, kd, dec, dm = (jnp.asarray(a, F32) for a in (cos, sin, qd, kd, dec, dm))
    gn = lp['ret_gn_g'].reshape(1, RT_W)
    full = lambda a: pl.BlockSpec(a.shape, lambda b, j: (0,) * a.ndim)
    posspec = pl.BlockSpec((c, RT_QK), lambda b, j: (j, 0))
    sspec = pl.BlockSpec((1, RT_HEADS, RT_DK, RT_DV), lambda b, j: (b, 0, 0, 0))
    return pl.pallas_call(
        functools.partial(_ret_kernel, c=c),
        out_shape=[jax.ShapeDtypeStruct((bsz * lpad, RT_W), BF16),
                   jax.ShapeDtypeStruct((bsz, RT_HEADS, RT_DK, RT_DV), F32)],
        grid=(bsz, nc),
        in_specs=[pl.BlockSpec((c, RT_COLS), lambda b, j: (rb + b * nc + j, 0)), posspec, posspec,
                  full(qd), full(kd), full(dec), full(dm), full(gn), sspec],
        out_specs=[pl.BlockSpec((c, RT_W), lambda b, j: (b * nc + j, 0)), sspec],
        scratch_shapes=[pltpu.VMEM((RT_QK, RT_W), F32)],
        compiler_params=_cparams(("parallel", "arbitrary")),
        name="ret_chunk",
    )(prt, cos, sin, qd, kd, dec, dm, gn, s0)


def _merge_kernel(x_ref, o1, o2, o3, o4, gate_ref, w1, w2, w3, w4, wo_ref, out_ref, *, d):
    acc = None
    for i, (o, w) in enumerate(((o1, w1), (o2, w2), (o3, w3), (o4, w4))):
        term = _sigmoid(gate_ref[:, i * d:(i + 1) * d]) * _dot(o[...], w[...])
        acc = term if acc is None else acc + term
    out_ref[...] = x_ref[...] + _dot(acc.astype(BF16), wo_ref[...])


def _merge(x, outs, gate, wouts, wo):
    n, d = x.shape
    tm = _pick(n, (256, 128, 64, 32, 16))
    row = lambda w: pl.BlockSpec((tm, w), lambda i: (i, 0))
    full = lambda a: pl.BlockSpec(a.shape, lambda i: (0,) * a.ndim)
    return pl.pallas_call(
        functools.partial(_merge_kernel, d=d),
        out_shape=jax.ShapeDtypeStruct((n, d), F32),
        grid=(n // tm,),
        in_specs=[row(d)] + [row(o.shape[1]) for o in outs] + [row(N_BRANCH * d)]
        + [full(w) for w in wouts] + [full(wo)],
        out_specs=row(d),
        compiler_params=_cparams(("parallel",)),
        name="merge",
    )(x, *outs, gate, *wouts, wo)


def _moe_kernel(x_ref, g_ref, rg_ref, re_ref, rb_ref, wg_ref, wu_ref, wd_ref, out_ref,
                xn_scr, comb_scr, acc_scr):
    e = pl.program_id(1)
    tm = x_ref.shape[0]
    lane = _iota((tm, LANES), 1)
    lanef = lane.astype(F32)

    @pl.when(e == 0)
    def _():
        x = x_ref[...]
        xn = x * lax.rsqrt(jnp.mean(x * x, axis=-1, keepdims=True) + NORM_EPS) * g_ref[...]
        xn_scr[...] = xn.astype(BF16)
        rb = rb_ref[...]
        neg = jnp.float32(-jnp.inf)
        glog = jnp.where(lane < N_GROUPS, _dot(xn, rg_ref[...], HI) + rb[0:1], neg)
        gmax = jnp.max(glog, axis=-1, keepdims=True)
        gsum = jnp.sum(jnp.exp(glog - gmax), axis=-1, keepdims=True)
        gidx = jnp.min(jnp.where(glog == gmax, lanef, float(LANES)), axis=-1, keepdims=True)
        gp = 1.0 / gsum
        in_group = (lanef >= gidx * EXPERTS_PER_GROUP) & (lanef < (gidx + 1.0) * EXPERTS_PER_GROUP)
        elog = jnp.where(in_group, _dot(xn, re_ref[...], HI) + rb[1:2], neg)
        emax = jnp.max(elog, axis=-1, keepdims=True)
        eexp = jnp.exp(elog - emax)
        ep = eexp / jnp.sum(eexp, axis=-1, keepdims=True)
        ep = jnp.where(in_group, ep, -1.0)
        p1 = jnp.max(ep, axis=-1, keepdims=True)
        i1 = jnp.min(jnp.where(ep == p1, lanef, float(LANES)), axis=-1, keepdims=True)
        ep2 = jnp.where(lanef == i1, -1.0, ep)
        p2 = jnp.max(ep2, axis=-1, keepdims=True)
        i2 = jnp.min(jnp.where(ep2 == p2, lanef, float(LANES)), axis=-1, keepdims=True)
        denom = p1 + p2
        comb_scr[...] = (jnp.where(lanef == i1, gp * p1 / denom, 0.0)
                         + jnp.where(lanef == i2, gp * p2 / denom, 0.0))
        acc_scr[...] = jnp.zeros_like(acc_scr)

    xb = xn_scr[...]
    hid = _silu(_dot(xb, wg_ref[0])) * _dot(xb, wu_ref[0])
    y = _dot(hid.astype(BF16), wd_ref[0])
    cw = jnp.sum(jnp.where(lane == e, comb_scr[...], 0.0), axis=-1, keepdims=True)
    acc_scr[...] += cw * y

    @pl.when(e == pl.num_programs(1) - 1)
    def _():
        out_ref[...] = x_ref[...] + acc_scr[...]


def _moe(x, lp):
    n, d = x.shape
    tm = _pick(n, (512, 256, 128, 64, 32, 16))
    de = lp['moe_w_gate'].shape[-1]
    rg = jnp.zeros((d, LANES), F32).at[:, :N_GROUPS].set(lp['router_group'])
    re = jnp.zeros((d, LANES), F32).at[:, :N_EXPERTS].set(lp['router_expert'])
    rb = jnp.zeros((SUBLANES, LANES), F32)
    rb = rb.at[0, :N_GROUPS].set(lp['router_group_b']).at[1, :N_EXPERTS].set(lp['router_expert_b'])
    g = lp['norm2_g'].reshape(1, d)
    full = lambda a: pl.BlockSpec(a.shape, lambda i, e: (0,) * a.ndim)
    return pl.pallas_call(
        _moe_kernel,
        out_shape=jax.ShapeDtypeStruct((n, d), F32),
        grid=(n // tm, N_EXPERTS),
        in_specs=[pl.BlockSpec((tm, d), lambda i, e: (i, 0)), full(g), full(rg), full(re), full(rb),
                  pl.BlockSpec((1, d, de), lambda i, e: (e, 0, 0)),
                  pl.BlockSpec((1, d, de), lambda i, e: (e, 0, 0)),
                  pl.BlockSpec((1, de, d), lambda i, e: (e, 0, 0))],
        out_specs=pl.BlockSpec((tm, d), lambda i, e: (i, 0)),
        scratch_shapes=[pltpu.VMEM((tm, d), BF16), pltpu.VMEM((tm, LANES), F32), pltpu.VMEM((tm, d), F32)],
        compiler_params=_cparams(("parallel", "arbitrary")),
        name="moe",
    )(x, g, rg, re, rb, lp['moe_w_gate'].astype(BF16), lp['moe_w_up'].astype(BF16),
      lp['moe_w_down'].astype(BF16))


def _layer(x, lp, lb, groups):
    d = x.shape[1]
    xn = _rmsnorm(x, lp['norm1_g'], BF16)
    w_in = lp['w_in']
    offs = np.cumsum([0, RW_COLS, GD_QKV, GD_W, 2 * GD_HEADS, HG_COLS, RT_COLS, N_BRANCH * d])
    seg = [w_in[:, offs[i]:offs[i + 1]] for i in range(7)]
    seg[3] = jnp.pad(seg[3], ((0, 0), (0, LANES - 2 * GD_HEADS)))
    p_rw, p_gqkv, p_gz, p_gba, p_hg, p_rt, p_gate = (
        _matmul(xn, s.astype(BF16), f"proj{i}") for i, s in enumerate(seg))

    prev_parts = []
    for gr in groups:
        b, lpad = gr['bsz'], gr['lpad']
        pg = p_rw[gr['row0']:gr['row0'] + b * lpad].reshape(b, lpad, RW_COLS)
        prev_parts.append(jnp.concatenate([gr['states'][0][:, None], pg[:, :-1]], axis=1).reshape(b * lpad, RW_COLS))
    prev = jnp.concatenate(prev_parts, axis=0)
    names = ('w', 'nkk', 'bb', 'km', 'wr', 'v', 'br', 'kr', 'bon', 'g')
    arrs = dict(zip(names, _rwkv_prep(p_rw, prev, lp)))

    o_rw_parts, o_gd_parts, o_hg_parts, o_rt_parts, new_states = [], [], [], [], []
    for gr in groups:
        b, seq, lpad, row0 = gr['bsz'], gr['seq'], gr['lpad'], gr['row0']
        shift0, wkv0, conv0, gdn0, hgrn0, ret0 = gr['states']
        o_rw, wkv1 = _rwkv_scan_group(arrs, b, seq, lpad, row0, wkv0)
        o_rw_parts.append(jnp.pad(o_rw, ((0, 0), (0, lpad - seq), (0, 0))).reshape(b * lpad, RW_W))
        o_gd, gdn1 = _gdn_group(p_gqkv, p_gz, p_gba, lp, b, lpad, seq, row0, conv0, gdn0)
        o_hg, hgrn1 = _hgrn_group(p_hg, lb, lp, b, lpad, seq, row0, hgrn0)
        o_rt, ret1 = _ret_group(p_rt, lp, b, lpad, seq, row0, gr['pos0'], ret0)
        o_gd_parts.append(o_gd)
        o_hg_parts.append(o_hg)
        o_rt_parts.append(o_rt)
        rows_rw = p_rw[row0:row0 + b * lpad].reshape(b, lpad, RW_COLS)
        rows_qkv = p_gqkv[row0:row0 + b * lpad].reshape(b, lpad, GD_QKV)
        shift1 = rows_rw[:, seq - 1]
        conv1 = jnp.concatenate([conv0, rows_qkv[:, :seq]], axis=1)[:, -(CONV_W - 1):]
        new_states.append((shift1, wkv1, conv1, gdn1, hgrn1, ret1))

    o_rw = _rwkv_post(jnp.concatenate(o_rw_parts, axis=0), arrs['bon'], arrs['g'], lp)
    outs = [o_rw, jnp.concatenate(o_gd_parts, axis=0), jnp.concatenate(o_hg_parts, axis=0),
            jnp.concatenate(o_rt_parts, axis=0)]
    wouts = [lp[n].astype(BF16) for n in ('w_out_rwkv', 'w_out_gdn', 'w_out_hgrn', 'w_out_ret')]
    x = _merge(x, outs, p_gate, wouts, lp['w_o'].astype(BF16))
    x = _moe(x, lp)
    return x, new_states


def kernel(x_prompt, x_sample, state_rwkv_shift, state_rwkv_wkv, state_gdn_conv, state_gdn, state_hgrn, state_ret, norm1_g, w_in, rwkv_mu, rwkv_w0, rwkv_w2, rwkv_a0, rwkv_a2, rwkv_g2, rwkv_k_k, rwkv_k_a, rwkv_r_k, rwkv_ln_g, rwkv_ln_b, w_out_rwkv, gdn_conv, gdn_a_log, gdn_dt_bias, gdn_norm_g, w_out_gdn, hgrn_lb_logits, hgrn_norm_g, w_out_hgrn, ret_gn_g, w_out_ret, w_o, norm2_g, router_group, router_group_b, router_expert, router_expert_b, moe_w_gate, moe_w_up, moe_w_down, final_norm_g):
    params = dict(norm1_g=norm1_g, w_in=w_in, rwkv_mu=rwkv_mu, rwkv_w0=rwkv_w0, rwkv_w2=rwkv_w2,
                  rwkv_a0=rwkv_a0, rwkv_a2=rwkv_a2, rwkv_g2=rwkv_g2, rwkv_k_k=rwkv_k_k,
                  rwkv_k_a=rwkv_k_a, rwkv_r_k=rwkv_r_k, rwkv_ln_g=rwkv_ln_g, rwkv_ln_b=rwkv_ln_b,
                  w_out_rwkv=w_out_rwkv, gdn_conv=gdn_conv, gdn_a_log=gdn_a_log,
                  gdn_dt_bias=gdn_dt_bias, gdn_norm_g=gdn_norm_g, w_out_gdn=w_out_gdn,
                  hgrn_norm_g=hgrn_norm_g, w_out_hgrn=w_out_hgrn, ret_gn_g=ret_gn_g,
                  w_out_ret=w_out_ret, w_o=w_o, norm2_g=norm2_g, router_group=router_group,
                  router_group_b=router_group_b, router_expert=router_expert,
                  router_expert_b=router_expert_b, moe_w_gate=moe_w_gate, moe_w_up=moe_w_up,
                  moe_w_down=moe_w_down)
    depth = w_in.shape[0]
    bp, lp_len, d = x_prompt.shape
    bs, ls, _ = x_sample.shape
    ls_pad = -(-ls // SAMPLE_PAD_LEN) * SAMPLE_PAD_LEN
    sm = jax.nn.softmax(hgrn_lb_logits.astype(F32), axis=0)
    lower_bounds = jnp.cumsum(sm, axis=0) - sm[0]

    sample_states = (state_rwkv_shift, state_rwkv_wkv, state_gdn_conv, state_gdn, state_hgrn, state_ret)
    xs = jnp.pad(x_sample, ((0, 0), (0, ls_pad - ls), (0, 0)))
    x = jnp.concatenate([x_prompt.reshape(bp * lp_len, d), xs.reshape(bs * ls_pad, d)], axis=0)

    prompt_out = [[] for _ in sample_states]
    sample_out = [[] for _ in sample_states]
    for layer in range(depth):
        lpar = {name: arr[layer] for name, arr in params.items()}
        groups = [
            dict(bsz=bp, seq=lp_len, lpad=lp_len, row0=0, pos0=0,
                 states=tuple(jnp.zeros((bp,) + s.shape[2:], F32) for s in sample_states)),
            dict(bsz=bs, seq=ls, lpad=ls_pad, row0=bp * lp_len, pos0=PAST_LEN,
                 states=tuple(s[layer].astype(F32) for s in sample_states)),
        ]
        x, new = _layer(x, lpar, lower_bounds[layer], groups)
        for lst, n in zip(prompt_out, new[0]):
            lst.append(n)
        for lst, n in zip(sample_out, new[1]):
            lst.append(n)

    y = _rmsnorm(x, final_norm_g, F32)
    y_prompt = y[:bp * lp_len].reshape(bp, lp_len, d)
    y_sample = y[bp * lp_len:].reshape(bs, ls_pad, d)[:, :ls]
    p_states = [jnp.stack(lst) for lst in prompt_out]
    s_states = [jnp.stack(lst).astype(o.dtype) for lst, o in zip(sample_out, sample_states)]
    return (y_prompt, y_sample, *p_states, *s_states)
```

```python
import functools
import math

import numpy as np
import jax
import jax.numpy as jnp
from jax import lax
from jax.experimental import pallas as pl
from jax.experimental.pallas import tpu as pltpu

F32 = jnp.float32
BF16 = jnp.bfloat16
HI = lax.Precision.HIGHEST

NORM_EPS = 1e-6
LB_FLOOR = 1e-30
PAST_LEN = 16384
RW_HEADS = 8
RW_HD = 64
RW_W = RW_HEADS * RW_HD
RW_DECAY_LORA = 64
RW_AAA_LORA = 64
RW_GATE_LORA = 128
RW_COLS = 3 * RW_W + RW_DECAY_LORA + RW_AAA_LORA + RW_GATE_LORA
RW_GN_EPS = 64e-5
GD_HEADS = 4
GD_DK = 128
GD_DV = 128
GD_QKV = GD_HEADS * (2 * GD_DK + GD_DV)
GD_W = GD_HEADS * GD_DV
CONV_W = 4
HG_HEADS = 4
HG_DF = 128
HG_DV = 128
HG_W = HG_HEADS * HG_DV
HG_COLS = 2 * HG_HEADS * HG_DF + 2 * HG_W
HG_SUB = 16
RT_HEADS = 4
RT_DK = 64
RT_DV = 128
RT_W = RT_HEADS * RT_DV
RT_QK = RT_HEADS * RT_DK
RT_COLS = 2 * RT_QK + 2 * RT_W
ROPE_BASE = 10000.0
N_BRANCH = 4
N_GROUPS = 4
EXPERTS_PER_GROUP = 8
N_EXPERTS = N_GROUPS * EXPERTS_PER_GROUP

LANES = 128
SUBLANES = 8
SAMPLE_PAD_LEN = 8
VMEM_LIMIT = 48 * 1024 * 1024


def _pick(n, cands):
    for c in cands:
        if n % c == 0:
            return c
    raise ValueError(f"no tile for {n} in {cands}")


def _cparams(sem):
    return pltpu.CompilerParams(dimension_semantics=sem, vmem_limit_bytes=VMEM_LIMIT)


def _dot(a, b, prec=None):
    return lax.dot_general(a, b, (((1,), (0,)), ((), ())), precision=prec, preferred_element_type=F32)


def _dot_nt(a, b, prec=None):
    return lax.dot_general(a, b, (((1,), (1,)), ((), ())), precision=prec, preferred_element_type=F32)


def _dot_tn(a, b, prec=None):
    return lax.dot_general(a, b, (((0,), (0,)), ((), ())), precision=prec, preferred_element_type=F32)


def _softplus(x):
    return jnp.maximum(x, 0.0) + jnp.log1p(jnp.exp(-jnp.abs(x)))


def _sigmoid(x):
    return jax.nn.sigmoid(x)


def _silu(x):
    return x * jax.nn.sigmoid(x)


def _segsum(x, hm):
    hi = x.astype(BF16)
    r1 = x - hi.astype(F32)
    mid = r1.astype(BF16)
    lo = (r1 - mid.astype(F32)).astype(BF16)
    return _dot(hi, hm) + _dot(mid, hm) + _dot(lo, hm)


_DIMS = {'nn': (((1,), (0,)), ((), ())), 'nt': (((1,), (1,)), ((), ())), 'tn': (((0,), (0,)), ((), ()))}


def _split2(x):
    hi = x.astype(BF16)
    return hi, (x - hi.astype(F32)).astype(BF16)


def _dot3(a, b, form='nn'):
    ah, al = _split2(a)
    bh, bl = _split2(b)
    f = lambda x, y: lax.dot_general(x, y, _DIMS[form], preferred_element_type=F32)
    return f(ah, bh) + f(al, bh) + f(ah, bl)


def _mdot(mask, x):
    hi = x.astype(BF16)
    r1 = x - hi.astype(F32)
    mid = r1.astype(BF16)
    lo = (r1 - mid.astype(F32)).astype(BF16)
    return _dot(mask, hi) + _dot(mask, mid) + _dot(mask, lo)


def _unit_lower_inverses(ms, c):
    rid = _iota((c, c), 0)
    cid = _iota((c, c), 1)
    same = lambda s: (rid // s) == (cid // s)
    ns = [jnp.where(same(SUBLANES), -m, 0.0) for m in ms]
    n2s = [_dot3(n, n) for n in ns]
    n4s = [_dot3(n2, n2) for n2 in n2s]
    eye = (rid == cid).astype(F32)
    ps = [eye + n for n in ns]
    ps = [p + _dot3(p, n2) for p, n2 in zip(ps, n2s)]
    ps = [p + _dot3(p, n4) for p, n4 in zip(ps, n4s)]
    s = SUBLANES
    while s < c:
        offs = [jnp.where(same(2 * s) & jnp.logical_not(same(s)), m, 0.0) for m in ms]
        ts = [_dot3(p, off) for p, off in zip(ps, offs)]
        ps = [p - _dot3(t, p) for p, t in zip(ps, ts)]
        s *= 2
    return ps


def _iota(shape, dim):
    return lax.broadcasted_iota(jnp.int32, shape, dim)


def _rms_kernel(x_ref, g_ref, o_ref):
    x = x_ref[...]
    ms = jnp.mean(x * x, axis=-1, keepdims=True)
    o_ref[...] = (x * lax.rsqrt(ms + NORM_EPS) * g_ref[...]).astype(o_ref.dtype)


def _rmsnorm(x, g, out_dtype):
    n, d = x.shape
    tm = _pick(n, (1024, 512, 256, 128, 64, 32, 16))
    return pl.pallas_call(
        _rms_kernel,
        out_shape=jax.ShapeDtypeStruct((n, d), out_dtype),
        grid=(n // tm,),
        in_specs=[pl.BlockSpec((tm, d), lambda i: (i, 0)), pl.BlockSpec((1, d), lambda i: (0, 0))],
        out_specs=pl.BlockSpec((tm, d), lambda i: (i, 0)),
        compiler_params=_cparams(("parallel",)),
        name="rmsnorm",
    )(x, g.reshape(1, d))


def _mm_kernel(x_ref, w_ref, o_ref):
    o_ref[...] = _dot(x_ref[...], w_ref[...])


def _matmul(x, w, name):
    n, k = x.shape
    m = w.shape[1]
    tm = _pick(n, (1024, 512, 256, 128, 64, 32, 16))
    tn = _pick(m, (1024, 896, 768, 512, 256, 128))
    return pl.pallas_call(
        _mm_kernel,
        out_shape=jax.ShapeDtypeStruct((n, m), F32),
        grid=(n // tm, m // tn),
        in_specs=[pl.BlockSpec((tm, k), lambda i, j: (i, 0)), pl.BlockSpec((k, tn), lambda i, j: (0, j))],
        out_specs=pl.BlockSpec((tm, tn), lambda i, j: (i, j)),
        compiler_params=_cparams(("parallel", "parallel")),
        name=name,
    )(x, w)


def _rwkv_prep_kernel(p_ref, prev_ref, mu_ref, vec_ref, w2_ref, a2_ref, g2_ref, hm_ref,
                      w_o, nkk_o, bb_o, km_o, wr_o, v_o, br_o, kr_o, bon_o, g_o):
    p = p_ref[...]
    h = p + (prev_ref[...] - p) * mu_ref[...]
    r = h[:, 0:RW_W]
    k = h[:, RW_W:2 * RW_W]
    v = h[:, 2 * RW_W:3 * RW_W]
    lo = h[:, 3 * RW_W:3 * RW_W + LANES]
    gl = h[:, 3 * RW_W + LANES:3 * RW_W + 2 * LANES]
    vec = vec_ref[...]
    w0, a0, k_k, k_a, r_k = vec[0:1], vec[1:2], vec[2:3], vec[3:4], vec[4:5]
    hm = hm_ref[...]
    w_log = -_softplus(-(w0 + _dot(jnp.tanh(lo), w2_ref[...], HI))) - 0.5
    decay = jnp.exp(-jnp.exp(w_log))
    a = _sigmoid(a0 + _dot(lo, a2_ref[...], HI))
    g = _dot(_sigmoid(gl), g2_ref[...], HI)
    kk = k * k_k
    kk = kk * lax.rsqrt(_segsum(kk * kk, hm) + NORM_EPS)
    km = k * (1.0 + (a - 1.0) * k_a)
    bb = kk * a
    w_o[...] = decay
    nkk_o[...] = -kk
    bb_o[...] = bb
    km_o[...] = km
    wr_o[...] = decay * r
    v_o[...] = v
    br_o[...] = _segsum(bb * r, hm)
    kr_o[...] = _segsum(km * r, hm)
    bon_o[...] = _segsum(r * km * r_k, hm) * v
    g_o[...] = g


def _head_sum_matrix(width, seg):
    i = np.arange(width)
    return jnp.asarray((i[:, None] // seg) == (i[None, :] // seg), BF16)


def _rwkv_prep(p, prev, lp):
    n = p.shape[0]
    tm = _pick(n, (256, 128, 64, 32, 16, 8))
    vec = jnp.zeros((SUBLANES, RW_W), F32)
    vec = vec.at[0].set(lp['rwkv_w0']).at[1].set(lp['rwkv_a0']).at[2].set(lp['rwkv_k_k'])
    vec = vec.at[3].set(lp['rwkv_k_a']).at[4].set(lp['rwkv_r_k'].reshape(RW_W))
    zeros = jnp.zeros((RW_DECAY_LORA, RW_W), F32)
    w2p = jnp.concatenate([lp['rwkv_w2'], zeros], axis=0)
    a2p = jnp.concatenate([zeros, lp['rwkv_a2']], axis=0)
    hm = _head_sum_matrix(RW_W, RW_HD)
    row = lambda w: pl.BlockSpec((tm, w), lambda i: (i, 0))
    full = lambda a: pl.BlockSpec(a.shape, lambda i: (0,) * a.ndim)
    mu = lp['rwkv_mu'].reshape(1, RW_COLS)
    outs = pl.pallas_call(
        _rwkv_prep_kernel,
        out_shape=[jax.ShapeDtypeStruct((n, RW_W), F32)] * 10,
        grid=(n // tm,),
        in_specs=[row(RW_COLS), row(RW_COLS), full(mu), full(vec), full(w2p), full(a2p),
                  full(lp['rwkv_g2']), full(hm)],
        out_specs=[row(RW_W)] * 10,
        compiler_params=_cparams(("parallel",)),
        name="rwkv_prep",
    )(p, prev, mu, vec, w2p, a2p, lp['rwkv_g2'], hm)
    return outs


def _wkv_kernel(nkk_ref, w_ref, b_ref, k_ref, wr_ref, v_ref, aux_ref, s0_ref, o_ref, s1_ref, s_scr, *, tb, nv):
    t_blk = pl.program_id(1)

    @pl.when(t_blk == 0)
    def _():
        s_scr[...] = s0_ref[0]

    def step(t, carry):
        nkk = nkk_ref[0, t]
        w = w_ref[0, t]
        bb = b_ref[0, t]
        kt = k_ref[0, t]
        wr = wr_ref[0, t]
        vt = v_ref[0, t]
        aux = aux_ref[0, t]
        br = aux[0:1]
        kr = aux[1:2]
        rows = []
        for vi in range(nv):
            sv = s_scr[vi]
            sa = jnp.sum(sv * nkk, axis=0, keepdims=True)
            ow = jnp.sum(sv * wr, axis=0, keepdims=True)
            vrow = vt[vi:vi + 1]
            rows.append(ow + sa * br + vrow * kr)
            s_scr[vi] = sv * w + sa * bb + vrow * kt
        o_ref[0, t] = jnp.concatenate(rows, axis=0)
        return carry

    lax.fori_loop(0, tb, step, 0)

    @pl.when(t_blk == pl.num_programs(1) - 1)
    def _():
        s1_ref[0] = s_scr[...]


def _wkv_scan(ktype, vv, aux, s0):
    g, l, _, _ = ktype[0].shape
    nv = vv.shape[2]
    tb = _pick(l, (32, 16, 8, 4, 2, 1))
    kspec = pl.BlockSpec((1, tb, RW_HD, LANES), lambda i, j: (i, j, 0, 0))
    vspec = pl.BlockSpec((1, tb, nv, LANES), lambda i, j: (i, j, 0, 0))
    aspec = pl.BlockSpec((1, tb, SUBLANES, LANES), lambda i, j: (i, j, 0, 0))
    sspec = pl.BlockSpec((1, nv, RW_HD, LANES), lambda i, j: (i, 0, 0, 0))
    return pl.pallas_call(
        functools.partial(_wkv_kernel, tb=tb, nv=nv),
        out_shape=[jax.ShapeDtypeStruct((g, l, nv, LANES), F32),
                   jax.ShapeDtypeStruct((g, nv, RW_HD, LANES), F32)],
        grid=(g, l // tb),
        in_specs=[kspec] * 5 + [vspec, aspec, sspec],
        out_specs=[vspec, sspec],
        scratch_shapes=[pltpu.VMEM((nv, RW_HD, LANES), F32)],
        compiler_params=_cparams(("parallel", "arbitrary")),
        name="wkv_scan",
    )(*ktype, vv, aux, s0)


def _rwkv_post_kernel(o_ref, bon_ref, g_ref, ln_ref, hm_ref, out_ref):
    o = o_ref[...]
    hm = hm_ref[...]
    ln = ln_ref[...]
    mu = _segsum(o, hm) * (1.0 / RW_HD)
    xc = o - mu
    var = _segsum(xc * xc, hm) * (1.0 / RW_HD)
    y = xc * lax.rsqrt(var + RW_GN_EPS) * ln[0:1] + ln[1:2]
    out_ref[...] = ((y + bon_ref[...]) * g_ref[...]).astype(out_ref.dtype)


def _rwkv_post(o, bon, g, lp):
    n = o.shape[0]
    tm = _pick(n, (512, 256, 128, 64, 32, 16))
    ln = jnp.zeros((SUBLANES, RW_W), F32).at[0].set(lp['rwkv_ln_g']).at[1].set(lp['rwkv_ln_b'])
    hm = _head_sum_matrix(RW_W, RW_HD)
    row = pl.BlockSpec((tm, RW_W), lambda i: (i, 0))
    full = lambda a: pl.BlockSpec(a.shape, lambda i: (0,) * a.ndim)
    return pl.pallas_call(
        _rwkv_post_kernel,
        out_shape=jax.ShapeDtypeStruct((n, RW_W), BF16),
        grid=(n // tm,),
        in_specs=[row, row, row, full(ln), full(hm)],
        out_specs=row,
        compiler_params=_cparams(("parallel",)),
        name="rwkv_post",
    )(o, bon, g, ln, hm)


def _scan_layout(bsz, heads):
    bh = bsz * heads
    if bh >= LANES:
        groups, bg, rep = bh // LANES, LANES // heads, 1
    else:
        groups, bg, rep = 1, bsz, LANES // bh
    return groups, bg, rep, RW_HD // rep


def _rwkv_scan_group(arrs, bsz, seq, lpad, row0, s0):
    h, hd = RW_HEADS, RW_HD
    groups, bg, rep, nv = _scan_layout(bsz, h)

    def rows(a):
        return a[row0:row0 + bsz * lpad].reshape(bsz, lpad, h, hd)[:, :seq]

    def ktype(a):
        a = rows(a).reshape(groups, bg, seq, h, hd)
        a = jnp.transpose(a, (0, 2, 4, 1, 3)).reshape(groups, seq, hd, bg * h)
        return jnp.tile(a, (1, 1, 1, rep))

    def vtype(a):
        a = rows(a).reshape(groups, bg, seq, h, rep, nv)
        return jnp.transpose(a, (0, 2, 5, 4, 1, 3)).reshape(groups, seq, nv, LANES)

    def scalar(a):
        a = rows(a)[..., 0].reshape(groups, bg, seq, h)
        a = jnp.transpose(a, (0, 2, 1, 3)).reshape(groups, seq, bg * h)
        return jnp.tile(a, (1, 1, rep))

    aux = jnp.zeros((groups, seq, SUBLANES, LANES), F32)
    aux = aux.at[:, :, 0].set(scalar(arrs['br'])).at[:, :, 1].set(scalar(arrs['kr']))
    s = s0.reshape(groups, bg, h, rep, nv, hd)
    s = jnp.transpose(s, (0, 4, 5, 3, 1, 2)).reshape(groups, nv, hd, LANES)
    o, s1 = _wkv_scan([ktype(arrs[n]) for n in ('nkk', 'w', 'bb', 'km', 'wr')], vtype(arrs['v']), aux, s)
    o = o.reshape(groups, seq, nv, rep, bg, h)
    o = jnp.transpose(o, (0, 4, 1, 5, 3, 2)).reshape(bsz, seq, RW_W)
    s1 = s1.reshape(groups, nv, hd, rep, bg, h)
    s1 = jnp.transpose(s1, (0, 4, 5, 3, 1, 2)).reshape(bsz, h, hd, hd)
    return o, s1


def _gdn_kernel(qkv_ref, z_ref, ba_ref, convw_ref, hp_ref, ng_ref, tail0_ref, s0_ref,
                o_ref, s1_ref, ext_scr, s_scr, x_scr, *, c, n_valid):
    ci = pl.program_id(1)

    @pl.when(ci == 0)
    def _():
        ext_scr[0:SUBLANES, :] = tail0_ref[0]
        s_scr[...] = s0_ref[0]

    ext_scr[SUBLANES:SUBLANES + c, :] = qkv_ref[...]
    cw = convw_ref[...]
    off = SUBLANES - (CONV_W - 1)
    cq = ext_scr[off:off + c, :] * cw[0:1]
    for j in range(1, CONV_W):
        cq = cq + ext_scr[off + j:off + j + c, :] * cw[j:j + 1]
    ext_scr[0:SUBLANES, :] = ext_scr[c:c + SUBLANES, :]
    cq = _silu(cq)

    rid = _iota((c, c), 0)
    cid = _iota((c, c), 1)
    incl = rid >= cid
    tri = incl.astype(BF16)
    strict_l = (rid > cid).astype(F32)

    ba = ba_ref[...]
    hp = hp_ref[...]
    beta_all = _sigmoid(ba)
    g_all = -jnp.exp(hp[0:1]) * _softplus(ba + hp[1:2])
    if n_valid < c:
        valid = _iota((c, LANES), 0) < n_valid
        beta_all = jnp.where(valid, beta_all, 0.0)
        g_all = jnp.where(valid, g_all, 0.0)
    gcum_all = _mdot(tri, g_all)
    ng = ng_ref[...]
    z = z_ref[...]
    heads = range(GD_HEADS)
    kbase, vbase = GD_HEADS * GD_DK, 2 * GD_HEADS * GD_DK
    qs = [cq[:, h * GD_DK:(h + 1) * GD_DK] for h in heads]
    ks = [cq[:, kbase + h * GD_DK:kbase + (h + 1) * GD_DK] for h in heads]
    vs = [cq[:, vbase + h * GD_DV:vbase + (h + 1) * GD_DV] for h in heads]
    qs = [q * lax.rsqrt(jnp.sum(q * q, axis=-1, keepdims=True) + NORM_EPS) * (GD_DK ** -0.5) for q in qs]
    ks = [k * lax.rsqrt(jnp.sum(k * k, axis=-1, keepdims=True) + NORM_EPS) for k in ks]
    betas = [beta_all[:, h:h + 1] for h in heads]
    g_cols = [g_all[:, GD_HEADS + h:GD_HEADS + h + 1] for h in heads]
    gcs = [gcum_all[:, GD_HEADS + h:GD_HEADS + h + 1] for h in heads]
    glasts = [gc[c - 1:c, :] for gc in gcs]
    egcs = [jnp.exp(gc) for gc in gcs]
    s_old = [s_scr[h] for h in heads]
    decs = [jnp.where(incl, jnp.exp(_mdot(tri, g * strict_l)), 0.0) for g in g_cols]
    qk_kts = [_dot3(jnp.concatenate([q, k], axis=0), k, 'nt') for q, k in zip(qs, ks)]
    ms = [strict_l * b * qk[c:2 * c] * dec for b, qk, dec in zip(betas, qk_kts, decs)]
    rhss = [jnp.concatenate([v * b, k * (b * e)], axis=1) for v, k, b, e in zip(vs, ks, betas, egcs)]
    if c >= 2 * SUBLANES:
        xss = [_dot3(p, rhs) for p, rhs in zip(_unit_lower_inverses(ms, c), rhss)]
    else:
        xss = []
        for h in heads:
            dt = _mdot((rid < cid).astype(BF16), g_cols[h] * (rid <= cid).astype(F32))
            mt = ((rid < cid).astype(F32) * _dot3(ks[h], ks[h] * betas[h], 'nt')
                  * jnp.exp(jnp.minimum(dt, 0.0)))
            x_scr[h] = rhss[h]
            for i in range(1, c):
                x_scr[h, i:i + 1, :] = (x_scr[h, i:i + 1, :]
                                        - jnp.sum(mt[:, i:i + 1] * x_scr[h], axis=0, keepdims=True))
            xss.append(x_scr[h])
    ws_qss = [_dot3(jnp.concatenate([xs[:, GD_DV:GD_DV + GD_DK], q * e], axis=0), s)
              for xs, q, e, s in zip(xss, qs, egcs, s_old)]
    v_news = [xs[:, 0:GD_DV] - wq[0:c] for xs, wq in zip(xss, ws_qss)]
    os_ = [wq[c:2 * c] + _dot3(qk[0:c] * dec, vn) for wq, qk, dec, vn in zip(ws_qss, qk_kts, decs, v_news)]
    s_new = [s * jnp.exp(gl) + _dot3(k * jnp.exp(gl - gc), vn, 'tn')
             for s, gl, gc, k, vn in zip(s_old, glasts, gcs, ks, v_news)]
    os_ = [o * lax.rsqrt(jnp.mean(o * o, axis=-1, keepdims=True) + NORM_EPS) * ng for o in os_]
    o_ref[...] = jnp.concatenate(
        [(o * _silu(z[:, h * GD_DV:(h + 1) * GD_DV])).astype(o_ref.dtype) for h, o in zip(heads, os_)], axis=1)
    for h in heads:
        s_scr[h] = s_new[h]

    @pl.when(ci == pl.num_programs(1) - 1)
    def _():
        s1_ref[0] = s_scr[...]


def _gdn_group(pqkv, pz, pba, lp, bsz, lpad, n_valid_len, row0, conv0, s0):
    c = _pick(lpad, (64, 32, 16, 8))
    nc = lpad // c
    n_valid = c if nc > 1 else n_valid_len
    rb = row0 // c
    tail0 = jnp.concatenate([jnp.zeros((bsz, SUBLANES - (CONV_W - 1), GD_QKV), F32), conv0], axis=1)
    hp = jnp.zeros((SUBLANES, LANES), F32)
    hp = hp.at[0, GD_HEADS:2 * GD_HEADS].set(lp['gdn_a_log']).at[1, GD_HEADS:2 * GD_HEADS].set(lp['gdn_dt_bias'])
    ng = lp['gdn_norm_g'].reshape(1, GD_DV)
    rowspec = lambda w: pl.BlockSpec((c, w), lambda b, j: (rb + b * nc + j, 0))
    full = lambda a: pl.BlockSpec(a.shape, lambda b, j: (0,) * a.ndim)
    sspec = pl.BlockSpec((1, GD_HEADS, GD_DK, GD_DV), lambda b, j: (b, 0, 0, 0))
    o, s1 = pl.pallas_call(
        functools.partial(_gdn_kernel, c=c, n_valid=n_valid),
        out_shape=[jax.ShapeDtypeStruct((bsz * lpad, GD_W), BF16),
                   jax.ShapeDtypeStruct((bsz, GD_HEADS, GD_DK, GD_DV), F32)],
        grid=(bsz, nc),
        in_specs=[rowspec(GD_QKV), rowspec(GD_W), rowspec(LANES), full(lp['gdn_conv']), full(hp), full(ng),
                  pl.BlockSpec((1, SUBLANES, GD_QKV), lambda b, j: (b, 0, 0)), sspec],
        out_specs=[pl.BlockSpec((c, GD_W), lambda b, j: (b * nc + j, 0)), sspec],
        scratch_shapes=[pltpu.VMEM((c + SUBLANES, GD_QKV), F32), pltpu.VMEM((GD_HEADS, GD_DK, GD_DV), F32),
                        pltpu.VMEM((GD_HEADS, c, GD_DV + GD_DK), F32)],
        compiler_params=_cparams(("parallel", "arbitrary")),
        name="gdn_chunk",
    )(pqkv, pz, pba, lp['gdn_conv'], hp, ng, tail0, s0)
    return o, s1


def _hgrn_kernel(p_ref, lbv_ref, ng_ref, s0_ref, o_ref, s1_ref, st_scr, *, c, n_valid):
    ci = pl.program_id(1)

    @pl.when(ci == 0)
    def _():
        for h in range(HG_HEADS):
            st_scr[h] = jnp.transpose(s0_ref[0, h])

    rid = _iota((c, c), 0)
    cid = _iota((c, c), 1)
    tri = (rid >= cid).astype(BF16)
    w = HG_HEADS * HG_DF
    rows = _iota((c, w), 0)
    sb = min(c, HG_SUB)
    row_in_sub = rows % sb
    lbv = lbv_ref[...]
    heads = range(HG_HEADS)
    hs = lambda x, h: x[:, h * HG_DF:(h + 1) * HG_DF]

    pf = p_ref[:, w:2 * w]
    v = p_ref[:, 2 * w:2 * w + HG_W]
    q = _silu(p_ref[:, 0:w])
    a = lbv[0:1]
    b = lbv[1:2] - _softplus(-pf)
    logf = jnp.maximum(a, b) + jnp.log1p(jnp.exp(-jnp.abs(a - b)))
    k = lbv[2:3] * _sigmoid(-pf)
    if n_valid < c:
        logf = jnp.where(rows < n_valid, logf, 0.0)
        k = jnp.where(rows < n_valid, k, 0.0)
    bc = _mdot(tri, logf)
    st_old = [st_scr[h] for h in heads]
    qe = q * jnp.exp(bc)
    os_ = [_dot3(hs(qe, h), st_old[h], 'nt') for h in heads]
    for delta in range(sb):
        if delta == 0:
            k_s, b_s, v_s = k, bc, v
        else:
            k_s, b_s, v_s = (pltpu.roll(t, delta, 0) for t in (k, bc, v))
        prod = jnp.where(row_in_sub >= delta, q * k_s * jnp.exp(jnp.minimum(bc - b_s, 0.0)), 0.0)
        os_ = [o + jnp.sum(hs(prod, h), axis=-1, keepdims=True) * hs(v_s, h) for h, o in zip(heads, os_)]
    if c > sb:
        parts = [[jnp.zeros((sb, HG_DV), F32)] for _ in heads]
        for r0 in range(sb, c, sb):
            ref = bc[r0 - 1:r0]
            qi = q[r0:r0 + sb] * jnp.exp(bc[r0:r0 + sb] - ref)
            kp = k[0:r0] * jnp.exp(ref - bc[0:r0])
            att = [_dot3(hs(qi, h), hs(kp, h), 'nt') for h in heads]
            for h in heads:
                parts[h].append(_dot3(att[h], hs(v, h)[0:r0]))
        os_ = [o + jnp.concatenate(p, axis=0) for o, p in zip(os_, parts)]
    blast = bc[c - 1:c]
    kd = k * jnp.exp(blast - bc)
    eb = jnp.exp(blast)
    st_new = [st_old[h] * hs(eb, h) + _dot3(hs(v, h), hs(kd, h), 'tn') for h in heads]
    os_ = [o * lax.rsqrt(jnp.mean(o * o, axis=-1, keepdims=True) + NORM_EPS) for o in os_]
    o_all = jnp.concatenate(os_, axis=1) * ng_ref[...] * _sigmoid(p_ref[:, 2 * w + HG_W:2 * w + 2 * HG_W])
    o_ref[...] = o_all.astype(o_ref.dtype)
    for h in heads:
        st_scr[h] = st_new[h]

    @pl.when(ci == pl.num_programs(1) - 1)
    def _():
        for h in range(HG_HEADS):
            s1_ref[0, h] = jnp.transpose(st_scr[h])


def _hgrn_group(phg, lb, lp, bsz, lpad, n_valid_len, row0, s0):
    c = _pick(lpad, (64, 32, 16, 8))
    nc = lpad // c
    n_valid = c if nc > 1 else n_valid_len
    rb = row0 // c
    lbv = jnp.zeros((SUBLANES, HG_HEADS * HG_DF), F32)
    lbv = lbv.at[0].set(jnp.log(jnp.maximum(lb, LB_FLOOR))).at[1].set(jnp.log1p(-lb)).at[2].set(1.0 - lb)
    ng = lp['hgrn_norm_g'].reshape(1, HG_W)
    full = lambda a: pl.BlockSpec(a.shape, lambda b, j: (0,) * a.ndim)
    sspec = pl.BlockSpec((1, HG_HEADS, HG_DF, HG_DV), lambda b, j: (b, 0, 0, 0))
    return pl.pallas_call(
        functools.partial(_hgrn_kernel, c=c, n_valid=n_valid),
        out_shape=[jax.ShapeDtypeStruct((bsz * lpad, HG_W), BF16),
                   jax.ShapeDtypeStruct((bsz, HG_HEADS, HG_DF, HG_DV), F32)],
        grid=(bsz, nc),
        in_specs=[pl.BlockSpec((c, HG_COLS), lambda b, j: (rb + b * nc + j, 0)), full(lbv), full(ng), sspec],
        out_specs=[pl.BlockSpec((c, HG_W), lambda b, j: (b * nc + j, 0)), sspec],
        scratch_shapes=[pltpu.VMEM((HG_HEADS, HG_DV, HG_DF), F32)],
        compiler_params=_cparams(("parallel", "arbitrary")),
        name="hgrn_chunk",
    )(phg, lbv, ng, s0)


def _ret_kernel(p_ref, cos_ref, sin_ref, qd_ref, kd_ref, dec_ref, dm_ref, gn_ref, s0_ref,
                o_ref, s1_ref, s_scr, *, c):
    ci = pl.program_id(1)

    @pl.when(ci == 0)
    def _():
        s_scr[...] = jnp.zeros_like(s_scr)
        for h in range(RT_HEADS):
            s_scr[h * RT_DK:(h + 1) * RT_DK, h * RT_DV:(h + 1) * RT_DV] = s0_ref[0, h]

    pq = p_ref[:, 0:RT_QK]
    pk = p_ref[:, RT_QK:2 * RT_QK]
    pv = p_ref[:, 2 * RT_QK:2 * RT_QK + RT_W]
    cos = cos_ref[...]
    sin = sin_ref[...]
    lane = _iota((c, RT_QK), 1)
    first_half = (lane % RT_DK) < (RT_DK // 2)

    def rope(x):
        partner = jnp.where(first_half, pltpu.roll(x, RT_QK - RT_DK // 2, 1), pltpu.roll(x, RT_DK // 2, 1))
        return x * cos + partner * sin

    q = rope(pq)
    k = rope(pk) * (RT_DK ** -0.5)
    sblk = s_scr[...]
    o_inter = _dot3(q * qd_ref[...], sblk)
    gn = gn_ref[...]
    q_heads = jnp.concatenate([jnp.where(lane // RT_DK == h, q, 0.0) for h in range(RT_HEADS)], axis=0)
    qk = _dot3(q_heads, k, 'nt')
    for h in range(RT_HEADS):
        attn = qk[h * c:(h + 1) * c] * dec_ref[h]
        v = pv[:, h * RT_DV:(h + 1) * RT_DV]
        o = o_inter[:, h * RT_DV:(h + 1) * RT_DV] + _dot3(attn, v)
        mu = jnp.mean(o, axis=-1, keepdims=True)
        xc = o - mu
        o = xc * lax.rsqrt(jnp.mean(xc * xc, axis=-1, keepdims=True) + NORM_EPS)
        pg = p_ref[:, 2 * RT_QK + RT_W + h * RT_DV:2 * RT_QK + RT_W + (h + 1) * RT_DV]
        o_ref[:, h * RT_DV:(h + 1) * RT_DV] = (o * gn[:, h * RT_DV:(h + 1) * RT_DV] * _silu(pg)).astype(o_ref.dtype)
    dm = dm_ref[...]
    s_scr[...] = sblk * dm + jnp.where(dm > 0.0, _dot3(k * kd_ref[...], pv, 'tn'), 0.0)

    @pl.when(ci == pl.num_programs(1) - 1)
    def _():
        for h in range(RT_HEADS):
            s1_ref[0, h] = s_scr[h * RT_DK:(h + 1) * RT_DK, h * RT_DV:(h + 1) * RT_DV]


def _ret_group(prt, lp, bsz, lpad, n_valid_len, row0, pos0, s0):
    c = _pick(lpad, (64, 32, 16, 8))
    nc = lpad // c
    n_valid = c if nc > 1 else n_valid_len
    rb = row0 // c
    half = RT_DK // 2
    inv = ROPE_BASE ** (-np.arange(half, dtype=np.float64) / half)
    ang = (pos0 + np.arange(lpad, dtype=np.float64))[:, None] * inv[None, :]
    cos = np.tile(np.cos(ang), (1, 2 * RT_HEADS))
    sin = np.tile(np.concatenate([-np.sin(ang), np.sin(ang)], axis=1), (1, RT_HEADS))
    loggamma = np.log(1.0 - np.exp2(-5.0 - np.arange(RT_HEADS, dtype=np.float64)))
    gcum = loggamma[:, None] * np.arange(1, c + 1, dtype=np.float64)[None, :]
    idx = np.arange(c)
    dec = np.where(idx[:, None] >= idx[None, :], np.exp(gcum[:, :, None] - gcum[:, None, :]), 0.0)
    qd = np.repeat(np.exp(gcum).T, RT_DK, axis=1)
    kdec = np.where(idx[None, :] < n_valid, np.exp(gcum[:, n_valid - 1:n_valid] - gcum), 0.0)
    kd = np.repeat(kdec.T, RT_DK, axis=1)
    sdec = np.exp(gcum[:, n_valid - 1])
    dm = np.zeros((RT_QK, RT_W))
    for h in range(RT_HEADS):
        dm[h * RT_DK:(h + 1) * RT_DK, h * RT_DV:(h + 1) * RT_DV] = sdec[h]
    cos, sin, qd, kd, dec, dm = (jnp.asarray(a, F32) for a in (cos, sin, qd, kd, dec, dm))
    gn = lp['ret_gn_g'].reshape(1, RT_W)
    full = lambda a: pl.BlockSpec(a.shape, lambda b, j: (0,) * a.ndim)
    posspec = pl.BlockSpec((c, RT_QK), lambda b, j: (j, 0))
    sspec = pl.BlockSpec((1, RT_HEADS, RT_DK, RT_DV), lambda b, j: (b, 0, 0, 0))
    return pl.pallas_call(
        functools.partial(_ret_kernel, c=c),
        out_shape=[jax.ShapeDtypeStruct((bsz * lpad, RT_W), BF16),
                   jax.ShapeDtypeStruct((bsz, RT_HEADS, RT_DK, RT_DV), F32)],
        grid=(bsz, nc),
        in_specs=[pl.BlockSpec((c, RT_COLS), lambda b, j: (rb + b * nc + j, 0)), posspec, posspec,
                  full(qd), full(kd), full(dec), full(dm), full(gn), sspec],
        out_specs=[pl.BlockSpec((c, RT_W), lambda b, j: (b * nc + j, 0)), sspec],
        scratch_shapes=[pltpu.VMEM((RT_QK, RT_W), F32)],
        compiler_params=_cparams(("parallel", "arbitrary")),
        name="ret_chunk",
    )(prt, cos, sin, qd, kd, dec, dm, gn, s0)


def _merge_kernel(x_ref, o1, o2, o3, o4, gate_ref, w1, w2, w3, w4, wo_ref, out_ref, *, d):
    acc = None
    for i, (o, w) in enumerate(((o1, w1), (o2, w2), (o3, w3), (o4, w4))):
        term = _sigmoid(gate_ref[:, i * d:(i + 1) * d]) * _dot(o[...], w[...])
        acc = term if acc is None else acc + term
    out_ref[...] = x_ref[...] + _dot(acc.astype(BF16), wo_ref[...])


def _merge(x, outs, gate, wouts, wo):
    n, d = x.shape
    tm = _pick(n, (256, 128, 64, 32, 16))
    row = lambda w: pl.BlockSpec((tm, w), lambda i: (i, 0))
    full = lambda a: pl.BlockSpec(a.shape, lambda i: (0,) * a.ndim)
    return pl.pallas_call(
        functools.partial(_merge_kernel, d=d),
        out_shape=jax.ShapeDtypeStruct((n, d), F32),
        grid=(n // tm,),
        in_specs=[row(d)] + [row(o.shape[1]) for o in outs] + [row(N_BRANCH * d)]
        + [full(w) for w in wouts] + [full(wo)],
        out_specs=row(d),
        compiler_params=_cparams(("parallel",)),
        name="merge",
    )(x, *outs, gate, *wouts, wo)


def _moe_kernel(x_ref, g_ref, rg_ref, re_ref, rb_ref, wg_ref, wu_ref, wd_ref, out_ref,
                xn_scr, comb_scr, acc_scr):
    e = pl.program_id(1)
    tm = x_ref.shape[0]
    lane = _iota((tm, LANES), 1)
    lanef = lane.astype(F32)

    @pl.when(e == 0)
    def _():
        x = x_ref[...]
        xn = x * lax.rsqrt(jnp.mean(x * x, axis=-1, keepdims=True) + NORM_EPS) * g_ref[...]
        xn_scr[...] = xn.astype(BF16)
        rb = rb_ref[...]
        neg = jnp.float32(-jnp.inf)
        glog = jnp.where(lane < N_GROUPS, _dot(xn, rg_ref[...], HI) + rb[0:1], neg)
        gmax = jnp.max(glog, axis=-1, keepdims=True)
        gsum = jnp.sum(jnp.exp(glog - gmax), axis=-1, keepdims=True)
        gidx = jnp.min(jnp.where(glog == gmax, lanef, float(LANES)), axis=-1, keepdims=True)
        gp = 1.0 / gsum
        in_group = (lanef >= gidx * EXPERTS_PER_GROUP) & (lanef < (gidx + 1.0) * EXPERTS_PER_GROUP)
        elog = jnp.where(in_group, _dot(xn, re_ref[...], HI) + rb[1:2], neg)
        emax = jnp.max(elog, axis=-1, keepdims=True)
        eexp = jnp.exp(elog - emax)
        ep = eexp / jnp.sum(eexp, axis=-1, keepdims=True)
        ep = jnp.where(in_group, ep, -1.0)
        p1 = jnp.max(ep, axis=-1, keepdims=True)
        i1 = jnp.min(jnp.where(ep == p1, lanef, float(LANES)), axis=-1, keepdims=True)
        ep2 = jnp.where(lanef == i1, -1.0, ep)
        p2 = jnp.max(ep2, axis=-1, keepdims=True)
        i2 = jnp.min(jnp.where(ep2 == p2, lanef, float(LANES)), axis=-1, keepdims=True)
        denom = p1 + p2
        comb_scr[...] = (jnp.where(lanef == i1, gp * p1 / denom, 0.0)
                         + jnp.where(lanef == i2, gp * p2 / denom, 0.0))
        acc_scr[...] = jnp.zeros_like(acc_scr)

    xb = xn_scr[...]
    hid = _silu(_dot(xb, wg_ref[0])) * _dot(xb, wu_ref[0])
    y = _dot(hid.astype(BF16), wd_ref[0])
    cw = jnp.sum(jnp.where(lane == e, comb_scr[...], 0.0), axis=-1, keepdims=True)
    acc_scr[...] += cw * y

    @pl.when(e == pl.num_programs(1) - 1)
    def _():
        out_ref[...] = x_ref[...] + acc_scr[...]


def _moe(x, lp):
    n, d = x.shape
    tm = _pick(n, (512, 256, 128, 64, 32, 16))
    de = lp['moe_w_gate'].shape[-1]
    rg = jnp.zeros((d, LANES), F32).at[:, :N_GROUPS].set(lp['router_group'])
    re = jnp.zeros((d, LANES), F32).at[:, :N_EXPERTS].set(lp['router_expert'])
    rb = jnp.zeros((SUBLANES, LANES), F32)
    rb = rb.at[0, :N_GROUPS].set(lp['router_group_b']).at[1, :N_EXPERTS].set(lp['router_expert_b'])
    g = lp['norm2_g'].reshape(1, d)
    full = lambda a: pl.BlockSpec(a.shape, lambda i, e: (0,) * a.ndim)
    return pl.pallas_call(
        _moe_kernel,
        out_shape=jax.ShapeDtypeStruct((n, d), F32),
        grid=(n // tm, N_EXPERTS),
        in_specs=[pl.BlockSpec((tm, d), lambda i, e: (i, 0)), full(g), full(rg), full(re), full(rb),
                  pl.BlockSpec((1, d, de), lambda i, e: (e, 0, 0)),
                  pl.BlockSpec((1, d, de), lambda i, e: (e, 0, 0)),
                  pl.BlockSpec((1, de, d), lambda i, e: (e, 0, 0))],
        out_specs=pl.BlockSpec((tm, d), lambda i, e: (i, 0)),
        scratch_shapes=[pltpu.VMEM((tm, d), BF16), pltpu.VMEM((tm, LANES), F32), pltpu.VMEM((tm, d), F32)],
        compiler_params=_cparams(("parallel", "arbitrary")),
        name="moe",
    )(x, g, rg, re, rb, lp['moe_w_gate'].astype(BF16), lp['moe_w_up'].astype(BF16),
      lp['moe_w_down'].astype(BF16))


def _layer(x, lp, lb, groups):
    d = x.shape[1]
    xn = _rmsnorm(x, lp['norm1_g'], BF16)
    w_in = lp['w_in']
    offs = np.cumsum([0, RW_COLS, GD_QKV, GD_W, 2 * GD_HEADS, HG_COLS, RT_COLS, N_BRANCH * d])
    seg = [w_in[:, offs[i]:offs[i + 1]] for i in range(7)]
    seg[3] = jnp.pad(seg[3], ((0, 0), (0, LANES - 2 * GD_HEADS)))
    p_rw, p_gqkv, p_gz, p_gba, p_hg, p_rt, p_gate = (
        _matmul(xn, s.astype(BF16), f"proj{i}") for i, s in enumerate(seg))

    prev_parts = []
    for gr in groups:
        b, lpad = gr['bsz'], gr['lpad']
        pg = p_rw[gr['row0']:gr['row0'] + b * lpad].reshape(b, lpad, RW_COLS)
        prev_parts.append(jnp.concatenate([gr['states'][0][:, None], pg[:, :-1]], axis=1).reshape(b * lpad, RW_COLS))
    prev = jnp.concatenate(prev_parts, axis=0)
    names = ('w', 'nkk', 'bb', 'km', 'wr', 'v', 'br', 'kr', 'bon', 'g')
    arrs = dict(zip(names, _rwkv_prep(p_rw, prev, lp)))

    o_rw_parts, o_gd_parts, o_hg_parts, o_rt_parts, new_states = [], [], [], [], []
    for gr in groups:
        b, seq, lpad, row0 = gr['bsz'], gr['seq'], gr['lpad'], gr['row0']
        shift0, wkv0, conv0, gdn0, hgrn0, ret0 = gr['states']
        o_rw, wkv1 = _rwkv_scan_group(arrs, b, seq, lpad, row0, wkv0)
        o_rw_parts.append(jnp.pad(o_rw, ((0, 0), (0, lpad - seq), (0, 0))).reshape(b * lpad, RW_W))
        o_gd, gdn1 = _gdn_group(p_gqkv, p_gz, p_gba, lp, b, lpad, seq, row0, conv0, gdn0)
        o_hg, hgrn1 = _hgrn_group(p_hg, lb, lp, b, lpad, seq, row0, hgrn0)
        o_rt, ret1 = _ret_group(p_rt, lp, b, lpad, seq, row0, gr['pos0'], ret0)
        o_gd_parts.append(o_gd)
        o_hg_parts.append(o_hg)
        o_rt_parts.append(o_rt)
        rows_rw = p_rw[row0:row0 + b * lpad].reshape(b, lpad, RW_COLS)
        rows_qkv = p_gqkv[row0:row0 + b * lpad].reshape(b, lpad, GD_QKV)
        shift1 = rows_rw[:, seq - 1]
        conv1 = jnp.concatenate([conv0, rows_qkv[:, :seq]], axis=1)[:, -(CONV_W - 1):]
        new_states.append((shift1, wkv1, conv1, gdn1, hgrn1, ret1))

    o_rw = _rwkv_post(jnp.concatenate(o_rw_parts, axis=0), arrs['bon'], arrs['g'], lp)
    outs = [o_rw, jnp.concatenate(o_gd_parts, axis=0), jnp.concatenate(o_hg_parts, axis=0),
            jnp.concatenate(o_rt_parts, axis=0)]
    wouts = [lp[n].astype(BF16) for n in ('w_out_rwkv', 'w_out_gdn', 'w_out_hgrn', 'w_out_ret')]
    x = _merge(x, outs, p_gate, wouts, lp['w_o'].astype(BF16))
    x = _moe(x, lp)
    return x, new_states


def kernel(x_prompt, x_sample, state_rwkv_shift, state_rwkv_wkv, state_gdn_conv, state_gdn, state_hgrn, state_ret, norm1_g, w_in, rwkv_mu, rwkv_w0, rwkv_w2, rwkv_a0, rwkv_a2, rwkv_g2, rwkv_k_k, rwkv_k_a, rwkv_r_k, rwkv_ln_g, rwkv_ln_b, w_out_rwkv, gdn_conv, gdn_a_log, gdn_dt_bias, gdn_norm_g, w_out_gdn, hgrn_lb_logits, hgrn_norm_g, w_out_hgrn, ret_gn_g, w_out_ret, w_o, norm2_g, router_group, router_group_b, router_expert, router_expert_b, moe_w_gate, moe_w_up, moe_w_down, final_norm_g):
    params = dict(norm1_g=norm1_g, w_in=w_in, rwkv_mu=rwkv_mu, rwkv_w0=rwkv_w0, rwkv_w2=rwkv_w2,
                  rwkv_a0=rwkv_a0, rwkv_a2=rwkv_a2, rwkv_g2=rwkv_g2, rwkv_k_k=rwkv_k_k,
                  rwkv_k_a=rwkv_k_a, rwkv_r_k=rwkv_r_k, rwkv_ln_g=rwkv_ln_g, rwkv_ln_b=rwkv_ln_b,
                  w_out_rwkv=w_out_rwkv, gdn_conv=gdn_conv, gdn_a_log=gdn_a_log,
                  gdn_dt_bias=gdn_dt_bias, gdn_norm_g=gdn_norm_g, w_out_gdn=w_out_gdn,
                  hgrn_norm_g=hgrn_norm_g, w_out_hgrn=w_out_hgrn, ret_gn_g=ret_gn_g,
                  w_out_ret=w_out_ret, w_o=w_o, norm2_g=norm2_g, router_group=router_group,
                  router_group_b=router_group_b, router_expert=router_expert,
                  router_expert_b=router_expert_b, moe_w_gate=moe_w_gate, moe_w_up=moe_w_up,
                  moe_w_down=moe_w_down)
    depth = w_in.shape[0]
    bp, lp_len, d = x_prompt.shape
    bs, ls, _ = x_sample.shape
    ls_pad = -(-ls // SAMPLE_PAD_LEN) * SAMPLE_PAD_LEN
    sm = jax.nn.softmax(hgrn_lb_logits.astype(F32), axis=0)
    lower_bounds = jnp.cumsum(sm, axis=0) - sm[0]

    sample_states = (state_rwkv_shift, state_rwkv_wkv, state_gdn_conv, state_gdn, state_hgrn, state_ret)
    xs = jnp.pad(x_sample, ((0, 0), (0, ls_pad - ls), (0, 0)))
    x = jnp.concatenate([x_prompt.reshape(bp * lp_len, d), xs.reshape(bs * ls_pad, d)], axis=0)

    prompt_out = [[] for _ in sample_states]
    sample_out = [[] for _ in sample_states]
    for layer in range(depth):
        lpar = {name: arr[layer] for name, arr in params.items()}
        groups = [
            dict(bsz=bp, seq=lp_len, lpad=lp_len, row0=0, pos0=0,
                 states=tuple(jnp.zeros((bp,) + s.shape[2:], F32) for s in sample_states)),
            dict(bsz=bs, seq=ls, lpad=ls_pad, row0=bp * lp_len, pos0=PAST_LEN,
                 states=tuple(s[layer].astype(F32) for s in sample_states)),
        ]
        x, new = _layer(x, lpar, lower_bounds[layer], groups)
        for lst, n in zip(prompt_out, new[0]):
            lst.append(n)
        for lst, n in zip(sample_out, new[1]):
            lst.append(n)

    y = _rmsnorm(x, final_norm_g, F32)
    y_prompt = y[:bp * lp_len].reshape(bp, lp_len, d)
    y_sample = y[bp * lp_len:].reshape(bs, ls_pad, d)[:, :ls]
    p_states = [jnp.stack(lst) for lst in prompt_out]
    s_states = [jnp.stack(lst).astype(o.dtype) for lst, o in zip(sample_out, sample_states)]
    return (y_prompt, y_sample, *p_states, *s_states)
```

```python
import functools
import math

import numpy as np
import jax
import jax.numpy as jnp
from jax import lax
from jax.experimental import pallas as pl
from jax.experimental.pallas import tpu as pltpu

F32 = jnp.float32
BF16 = jnp.bfloat16
HI = lax.Precision.HIGHEST

NORM_EPS = 1e-6
LB_FLOOR = 1e-30
PAST_LEN = 16384
RW_HEADS = 8
RW_HD = 64
RW_W = RW_HEADS * RW_HD
RW_DECAY_LORA = 64
RW_AAA_LORA = 64
RW_GATE_LORA = 128
RW_COLS = 3 * RW_W + RW_DECAY_LORA + RW_AAA_LORA + RW_GATE_LORA
RW_GN_EPS = 64e-5
GD_HEADS = 4
GD_DK = 128
GD_DV = 128
GD_QKV = GD_HEADS * (2 * GD_DK + GD_DV)
GD_W = GD_HEADS * GD_DV
CONV_W = 4
HG_HEADS = 4
HG_DF = 128
HG_DV = 128
HG_W = HG_HEADS * HG_DV
HG_COLS = 2 * HG_HEADS * HG_DF + 2 * HG_W
HG_SUB = 16
RT_HEADS = 4
RT_DK = 64
RT_DV = 128
RT_W = RT_HEADS * RT_DV
RT_QK = RT_HEADS * RT_DK
RT_COLS = 2 * RT_QK + 2 * RT_W
ROPE_BASE = 10000.0
N_BRANCH = 4
N_GROUPS = 4
EXPERTS_PER_GROUP = 8
N_EXPERTS = N_GROUPS * EXPERTS_PER_GROUP

LANES = 128
SUBLANES = 8
SAMPLE_PAD_LEN = 8
VMEM_LIMIT = 48 * 1024 * 1024
MOE_TILE = 1024
MOE_SLOT_BLOCK = 256
MOE_ROW_BLOCK = 64
MOE_VMEM_LIMIT = 56 * 1024 * 1024


def _pick(n, cands):
    for c in cands:
        if n % c == 0:
            return c
    raise ValueError(f"no tile for {n} in {cands}")


def _cparams(sem):
    return pltpu.CompilerParams(dimension_semantics=sem, vmem_limit_bytes=VMEM_LIMIT)


def _dot(a, b, prec=None):
    return lax.dot_general(a, b, (((1,), (0,)), ((), ())), precision=prec, preferred_element_type=F32)


def _softplus(x):
    return jnp.maximum(x, 0.0) + jnp.log1p(jnp.exp(-jnp.abs(x)))


def _sigmoid(x):
    return jax.nn.sigmoid(x)


def _silu(x):
    return x * jax.nn.sigmoid(x)


def _segsum(x, hm):
    hi = x.astype(BF16)
    r1 = x - hi.astype(F32)
    mid = r1.astype(BF16)
    lo = (r1 - mid.astype(F32)).astype(BF16)
    return _dot(hi, hm) + _dot(mid, hm) + _dot(lo, hm)


_DIMS = {'nn': (((1,), (0,)), ((), ())), 'nt': (((1,), (1,)), ((), ())), 'tn': (((0,), (0,)), ((), ()))}


def _split2(x):
    hi = x.astype(BF16)
    return hi, (x - hi.astype(F32)).astype(BF16)


def _dot3(a, b, form='nn'):
    ah, al = _split2(a)
    bh, bl = _split2(b)
    f = lambda x, y: lax.dot_general(x, y, _DIMS[form], preferred_element_type=F32)
    return f(ah, bh) + f(al, bh) + f(ah, bl)


def _mdot(mask, x):
    hi = x.astype(BF16)
    r1 = x - hi.astype(F32)
    mid = r1.astype(BF16)
    lo = (r1 - mid.astype(F32)).astype(BF16)
    return _dot(mask, hi) + _dot(mask, mid) + _dot(mask, lo)


def _unit_lower_inverses(ms, rid, cid, c, expand=None):
    prod = _dot3 if expand is None else (lambda x, y: _dot3(x, expand(y)))
    same = lambda s: (rid // s) == (cid // s)
    ns = [jnp.where(same(SUBLANES), -m, 0.0) for m in ms]
    n2s = [prod(n, n) for n in ns]
    n4s = [prod(n2, n2) for n2 in n2s]
    eye = (rid == cid).astype(F32)
    ps = [eye + n for n in ns]
    ps = [p + prod(p, n2) for p, n2 in zip(ps, n2s)]
    ps = [p + prod(p, n4) for p, n4 in zip(ps, n4s)]
    s = SUBLANES
    while s < c:
        offs = [jnp.where(same(2 * s) & jnp.logical_not(same(s)), m, 0.0) for m in ms]
        ts = [prod(p, off) for p, off in zip(ps, offs)]
        ps = [p - prod(t, p) for p, t in zip(ps, ts)]
        s *= 2
    return ps


def _iota(shape, dim):
    return lax.broadcasted_iota(jnp.int32, shape, dim)


def _rms_kernel(x_ref, g_ref, o_ref):
    x = x_ref[...]
    ms = jnp.mean(x * x, axis=-1, keepdims=True)
    o_ref[...] = (x * lax.rsqrt(ms + NORM_EPS) * g_ref[...]).astype(o_ref.dtype)


def _rmsnorm(x, g, out_dtype):
    n, d = x.shape
    tm = _pick(n, (1024, 512, 256, 128, 64, 32, 16))
    return pl.pallas_call(
        _rms_kernel,
        out_shape=jax.ShapeDtypeStruct((n, d), out_dtype),
        grid=(n // tm,),
        in_specs=[pl.BlockSpec((tm, d), lambda i: (i, 0)), pl.BlockSpec((1, d), lambda i: (0, 0))],
        out_specs=pl.BlockSpec((tm, d), lambda i: (i, 0)),
        compiler_params=_cparams(("parallel",)),
        name="rmsnorm",
    )(x, g.reshape(1, d))


def _mm_kernel(x_ref, w_ref, o_ref):
    o_ref[...] = _dot(x_ref[...], w_ref[...])


def _matmul(x, w, name):
    n, k = x.shape
    m = w.shape[1]
    tm = _pick(n, (1024, 512, 256, 128, 64, 32, 16))
    tn = _pick(m, (1024, 896, 768, 512, 256, 128))
    return pl.pallas_call(
        _mm_kernel,
        out_shape=jax.ShapeDtypeStruct((n, m), F32),
        grid=(n // tm, m // tn),
        in_specs=[pl.BlockSpec((tm, k), lambda i, j: (i, 0)), pl.BlockSpec((k, tn), lambda i, j: (0, j))],
        out_specs=pl.BlockSpec((tm, tn), lambda i, j: (i, j)),
        compiler_params=_cparams(("parallel", "parallel")),
        name=name,
    )(x, w)


def _head_sum_matrix(width, seg):
    i = np.arange(width)
    return jnp.asarray((i[:, None] // seg) == (i[None, :] // seg), BF16)


def _wkv_chunk_kernel(p_ref, first_ref, mu_ref, vec_ref, w2_ref, a2_ref, g2_ref, hm_ref, ln_ref, s0_ref,
                      o_ref, s1_ref, shift_ref, carry_scr, ht_scr, *, cs, nseq, n_valid, multi_chunk):
    ci = pl.program_id(1)
    rows = cs * nseq
    npair = RW_HEADS // 2
    f_zero = jnp.zeros((RW_HD, RW_HD), F32)

    @pl.when(ci == 0)
    def _():
        for q in range(nseq):
            carry_scr[q] = first_ref[q]
            for pr in range(npair):
                top = jnp.concatenate([s0_ref[q, 2 * pr], f_zero], axis=1)
                bot = jnp.concatenate([f_zero, s0_ref[q, 2 * pr + 1]], axis=1)
                ht_scr[q, pr] = jnp.concatenate([top, bot], axis=0)

    p = p_ref[...]
    row_w = _iota((rows, RW_COLS), 0)
    prev = pltpu.roll(p, 1, 0)
    for q in range(nseq):
        prev = jnp.where(row_w == q * cs, carry_scr[q], prev)
    h = p + (prev - p) * mu_ref[...]
    r = h[:, 0:RW_W]
    k = h[:, RW_W:2 * RW_W]
    v = h[:, 2 * RW_W:3 * RW_W]
    lo = h[:, 3 * RW_W:3 * RW_W + LANES]
    gl = h[:, 3 * RW_W + LANES:3 * RW_W + 2 * LANES]
    vec = vec_ref[...]
    w0, a0, k_k, k_a, r_k = vec[0:1], vec[1:2], vec[2:3], vec[3:4], vec[4:5]
    hm = hm_ref[...]
    w_log = -_softplus(-(w0 + _dot3(jnp.tanh(lo), w2_ref[...]))) - 0.5
    logw = -jnp.exp(w_log)
    a = _sigmoid(a0 + _dot3(lo, a2_ref[...]))
    g = _dot3(_sigmoid(gl), g2_ref[...])
    kk = k * k_k
    kk = kk * lax.rsqrt(_segsum(kk * kk, hm) + NORM_EPS)
    km = k * (1.0 + (a - 1.0) * k_a)
    bb = kk * a
    alpha = -kk
    bonus = _segsum(r * km * r_k, hm) * v
    row_f = _iota((rows, RW_W), 0)
    if n_valid < cs:
        valid = (row_f % cs) < n_valid
        logw, alpha, bb, km = (jnp.where(valid, t, 0.0) for t in (logw, alpha, bb, km))

    rid = _iota((rows, rows), 0)
    cid = _iota((rows, rows), 1)
    tri = ((rid >= cid) & (rid // cs == cid // cs)).astype(BF16)
    gcum = _mdot(tri, logw)
    g_last = jnp.concatenate(
        [jnp.broadcast_to(gcum[q * cs + cs - 1:q * cs + cs], (cs, RW_W)) for q in range(nseq)], axis=0)
    abar = alpha * jnp.exp(gcum - logw)
    emg = jnp.exp(-gcum)
    bbar = bb * emg
    kbar = km * emg
    rbar = r * jnp.exp(gcum)
    etil = jnp.exp(g_last - gcum)
    btil = bb * etil
    ktil = km * etil
    w_last = jnp.exp(g_last)

    prow = _iota((rows, LANES), 0)
    plane = _iota((rows, LANES), 1)
    pcol = plane % RW_HD
    same_seq = (prow // cs) == (pcol // cs)
    strict = (pcol < prow) & same_seq
    incl = (pcol <= prow) & same_seq
    first_head = plane < RW_HD
    blk_mask = (_iota((LANES, LANES), 0) // RW_HD) == (_iota((LANES, LANES), 1) // RW_HD)

    def expand(x):
        return jnp.concatenate([jnp.where(first_head, x, 0.0), jnp.where(first_head, 0.0, x)], axis=0)

    pairs = range(npair)
    ps = lambda x, pr: x[:, pr * LANES:(pr + 1) * LANES]
    seq_rows = lambda x, q: x[q * cs:(q + 1) * cs]

    lhs = [jnp.concatenate([ps(abar, pr), ps(rbar, pr)], axis=0) for pr in pairs]
    xb = [_dot3(lhs[pr], expand(ps(bbar, pr)), 'nt') for pr in pairs]
    xk = [_dot3(lhs[pr], expand(ps(kbar, pr)), 'nt') for pr in pairs]
    a_m = [jnp.where(strict, x[0:rows], 0.0) for x in xb]
    rb_m = [jnp.where(incl, x[rows:2 * rows], 0.0) for x in xb]
    b_m = [jnp.where(strict, x[0:rows], 0.0) for x in xk]
    rk_m = [jnp.where(incl, x[rows:2 * rows], 0.0) for x in xk]
    pinv = _unit_lower_inverses([-m for m in a_m], prow, pcol, cs, expand=expand)

    ht_old = [[ht_scr[q, pr] for pr in pairs] for q in range(nseq)]
    xh = [[_dot3(jnp.concatenate([seq_rows(ps(abar, pr), q), seq_rows(ps(rbar, pr), q)], axis=0),
                 ht_old[q][pr], 'nt') for pr in pairs] for q in range(nseq)]
    ah = [jnp.concatenate([xh[q][pr][0:cs] for q in range(nseq)], axis=0) for pr in pairs]
    rh = [jnp.concatenate([xh[q][pr][cs:2 * cs] for q in range(nseq)], axis=0) for pr in pairs]
    vexp = [expand(ps(v, pr)) for pr in pairs]
    rhs = [ah[pr] + _dot3(b_m[pr], vexp[pr]) for pr in pairs]
    u = [_dot3(pinv[pr], expand(rhs[pr])) for pr in pairs]
    o_pairs = [rh[pr] + _dot3(jnp.concatenate([rb_m[pr], rk_m[pr]], axis=1),
                              jnp.concatenate([expand(u[pr]), vexp[pr]], axis=0)) for pr in pairs]
    for q in range(nseq):
        for pr in pairs:
            uv = jnp.concatenate([seq_rows(u[pr], q), seq_rows(ps(v, pr), q)], axis=0)
            bk = jnp.concatenate([seq_rows(ps(btil, pr), q), seq_rows(ps(ktil, pr), q)], axis=0)
            upd = jnp.where(blk_mask, _dot3(uv, bk, 'tn'), 0.0)
            ht_scr[q, pr] = ht_old[q][pr] * seq_rows(ps(w_last, pr), q)[0:1] + upd

    o = jnp.concatenate(o_pairs, axis=1)
    ln = ln_ref[...]
    mean = _segsum(o, hm) * (1.0 / RW_HD)
    xc = o - mean
    var = _segsum(xc * xc, hm) * (1.0 / RW_HD)
    y = xc * lax.rsqrt(var + RW_GN_EPS) * ln[0:1] + ln[1:2]
    o_ref[...] = ((y + bonus) * g).astype(o_ref.dtype)

    if multi_chunk:
        for q in range(nseq):
            carry_scr[q] = p[q * cs + cs - 1:q * cs + cs]

    @pl.when(ci == pl.num_programs(1) - 1)
    def _():
        for q in range(nseq):
            shift_ref[q] = p[q * cs + n_valid - 1:q * cs + n_valid]
            for pr in pairs:
                ht = ht_scr[q, pr]
                s1_ref[q, 2 * pr] = ht[0:RW_HD, 0:RW_HD]
                s1_ref[q, 2 * pr + 1] = ht[RW_HD:2 * RW_HD, RW_HD:2 * RW_HD]


def _wkv_group(p_rw, lp, bsz, lpad, n_valid_len, row0, shift0, s0):
    rows = RW_HD
    if lpad >= rows:
        cs, nseq = rows, 1
    else:
        cs, nseq = lpad, rows // lpad
    nc = lpad // cs
    n_valid = cs if nc > 1 else n_valid_len
    rb = row0 // rows
    vec = jnp.zeros((SUBLANES, RW_W), F32)
    vec = vec.at[0].set(lp['rwkv_w0']).at[1].set(lp['rwkv_a0']).at[2].set(lp['rwkv_k_k'])
    vec = vec.at[3].set(lp['rwkv_k_a']).at[4].set(lp['rwkv_r_k'].reshape(RW_W))
    zeros = jnp.zeros((RW_DECAY_LORA, RW_W), F32)
    w2p = jnp.concatenate([lp['rwkv_w2'], zeros], axis=0)
    a2p = jnp.concatenate([zeros, lp['rwkv_a2']], axis=0)
    hm = _head_sum_matrix(RW_W, RW_HD)
    ln = jnp.zeros((SUBLANES, RW_W), F32).at[0].set(lp['rwkv_ln_g']).at[1].set(lp['rwkv_ln_b'])
    mu = lp['rwkv_mu'].reshape(1, RW_COLS)
    first = shift0.reshape(bsz, 1, RW_COLS)
    full = lambda a: pl.BlockSpec(a.shape, lambda b, j: (0,) * a.ndim)
    sspec = pl.BlockSpec((nseq, RW_HEADS, RW_HD, RW_HD), lambda b, j: (b, 0, 0, 0))
    fspec = pl.BlockSpec((nseq, 1, RW_COLS), lambda b, j: (b, 0, 0))
    o, s1, shift1 = pl.pallas_call(
        functools.partial(_wkv_chunk_kernel, cs=cs, nseq=nseq, n_valid=n_valid, multi_chunk=nc > 1),
        out_shape=[jax.ShapeDtypeStruct((bsz * lpad, RW_W), BF16),
                   jax.ShapeDtypeStruct((bsz, RW_HEADS, RW_HD, RW_HD), F32),
                   jax.ShapeDtypeStruct((bsz, 1, RW_COLS), F32)],
        grid=(bsz // nseq, nc),
        in_specs=[pl.BlockSpec((rows, RW_COLS), lambda b, j: (rb + b * nc + j, 0)), fspec, full(mu), full(vec),
                  full(w2p), full(a2p), full(lp['rwkv_g2']), full(hm), full(ln), sspec],
        out_specs=[pl.BlockSpec((rows, RW_W), lambda b, j: (b * nc + j, 0)), sspec, fspec],
        scratch_shapes=[pltpu.VMEM((nseq, 1, RW_COLS), F32),
                        pltpu.VMEM((nseq, RW_HEADS // 2, LANES, LANES), F32)],
        compiler_params=_cparams(("parallel", "arbitrary")),
        name="wkv_chunk",
    )(p_rw, first, mu, vec, w2p, a2p, lp['rwkv_g2'], hm, ln, s0)
    return o, shift1.reshape(bsz, RW_COLS), s1


def _gdn_kernel(qkv_ref, z_ref, ba_ref, convw_ref, hp_ref, ng_ref, tail0_ref, s0_ref,
                o_ref, s1_ref, conv_ref, ext_scr, s_scr, x_scr, *, c, n_valid):
    ci = pl.program_id(1)

    @pl.when(ci == 0)
    def _():
        ext_scr[0:SUBLANES, :] = tail0_ref[0]
        s_scr[...] = s0_ref[0]

    ext_scr[SUBLANES:SUBLANES + c, :] = qkv_ref[...]
    cw = convw_ref[...]
    off = SUBLANES - (CONV_W - 1)
    cq = ext_scr[off:off + c, :] * cw[0:1]
    for j in range(1, CONV_W):
        cq = cq + ext_scr[off + j:off + j + c, :] * cw[j:j + 1]

    @pl.when(ci == pl.num_programs(1) - 1)
    def _():
        conv_ref[0] = ext_scr[off + n_valid:off + n_valid + CONV_W - 1, :]

    ext_scr[0:SUBLANES, :] = ext_scr[c:c + SUBLANES, :]
    cq = _silu(cq)

    rid = _iota((c, c), 0)
    cid = _iota((c, c), 1)
    incl = rid >= cid
    tri = incl.astype(BF16)
    strict_l = (rid > cid).astype(F32)

    ba = ba_ref[...]
    hp = hp_ref[...]
    beta_all = _sigmoid(ba)
    g_all = -jnp.exp(hp[0:1]) * _softplus(ba + hp[1:2])
    if n_valid < c:
        valid = _iota((c, LANES), 0) < n_valid
        beta_all = jnp.where(valid, beta_all, 0.0)
        g_all = jnp.where(valid, g_all, 0.0)
    gcum_all = _mdot(tri, g_all)
    ng = ng_ref[...]
    z = z_ref[...]
    heads = range(GD_HEADS)
    kbase, vbase = GD_HEADS * GD_DK, 2 * GD_HEADS * GD_DK
    qs = [cq[:, h * GD_DK:(h + 1) * GD_DK] for h in heads]
    ks = [cq[:, kbase + h * GD_DK:kbase + (h + 1) * GD_DK] for h in heads]
    vs = [cq[:, vbase + h * GD_DV:vbase + (h + 1) * GD_DV] for h in heads]
    qs = [q * lax.rsqrt(jnp.sum(q * q, axis=-1, keepdims=True) + NORM_EPS) * (GD_DK ** -0.5) for q in qs]
    ks = [k * lax.rsqrt(jnp.sum(k * k, axis=-1, keepdims=True) + NORM_EPS) for k in ks]
    betas = [beta_all[:, h:h + 1] for h in heads]
    g_cols = [g_all[:, GD_HEADS + h:GD_HEADS + h + 1] for h in heads]
    gcs = [gcum_all[:, GD_HEADS + h:GD_HEADS + h + 1] for h in heads]
    glasts = [gc[c - 1:c, :] for gc in gcs]
    egcs = [jnp.exp(gc) for gc in gcs]
    s_old = [s_scr[h] for h in heads]
    decs = [jnp.where(incl, jnp.exp(_mdot(tri, g * strict_l)), 0.0) for g in g_cols]
    qk_kts = [_dot3(jnp.concatenate([q, k], axis=0), k, 'nt') for q, k in zip(qs, ks)]
    ms = [strict_l * b * qk[c:2 * c] * dec for b, qk, dec in zip(betas, qk_kts, decs)]
    rhss = [jnp.concatenate([v * b, k * (b * e)], axis=1) for v, k, b, e in zip(vs, ks, betas, egcs)]
    if c >= 2 * SUBLANES:
        xss = [_dot3(p, rhs) for p, rhs in zip(_unit_lower_inverses(ms, rid, cid, c), rhss)]
    else:
        xss = []
        for h in heads:
            dt = _mdot((rid < cid).astype(BF16), g_cols[h] * (rid <= cid).astype(F32))
            mt = ((rid < cid).astype(F32) * _dot3(ks[h], ks[h] * betas[h], 'nt')
                  * jnp.exp(jnp.minimum(dt, 0.0)))
            x_scr[h] = rhss[h]
            for i in range(1, c):
                x_scr[h, i:i + 1, :] = (x_scr[h, i:i + 1, :]
                                        - jnp.sum(mt[:, i:i + 1] * x_scr[h], axis=0, keepdims=True))
            xss.append(x_scr[h])
    ws_qss = [_dot3(jnp.concatenate([xs[:, GD_DV:GD_DV + GD_DK], q * e], axis=0), s)
              for xs, q, e, s in zip(xss, qs, egcs, s_old)]
    v_news = [xs[:, 0:GD_DV] - wq[0:c] for xs, wq in zip(xss, ws_qss)]
    os_ = [wq[c:2 * c] + _dot3(qk[0:c] * dec, vn) for wq, qk, dec, vn in zip(ws_qss, qk_kts, decs, v_news)]
    s_new = [s * jnp.exp(gl) + _dot3(k * jnp.exp(gl - gc), vn, 'tn')
             for s, gl, gc, k, vn in zip(s_old, glasts, gcs, ks, v_news)]
    os_ = [o * lax.rsqrt(jnp.mean(o * o, axis=-1, keepdims=True) + NORM_EPS) * ng for o in os_]
    o_ref[...] = jnp.concatenate(
        [(o * _silu(z[:, h * GD_DV:(h + 1) * GD_DV])).astype(o_ref.dtype) for h, o in zip(heads, os_)], axis=1)
    for h in heads:
        s_scr[h] = s_new[h]

    @pl.when(ci == pl.num_programs(1) - 1)
    def _():
        s1_ref[0] = s_scr[...]


def _gdn_group(pqkv, pz, pba, lp, bsz, lpad, n_valid_len, row0, conv0, s0):
    c = _pick(lpad, (64, 32, 16, 8))
    nc = lpad // c
    n_valid = c if nc > 1 else n_valid_len
    rb = row0 // c
    tail0 = jnp.concatenate([jnp.zeros((bsz, SUBLANES - (CONV_W - 1), GD_QKV), F32), conv0], axis=1)
    hp = jnp.zeros((SUBLANES, LANES), F32)
    hp = hp.at[0, GD_HEADS:2 * GD_HEADS].set(lp['gdn_a_log']).at[1, GD_HEADS:2 * GD_HEADS].set(lp['gdn_dt_bias'])
    ng = lp['gdn_norm_g'].reshape(1, GD_DV)
    rowspec = lambda w: pl.BlockSpec((c, w), lambda b, j: (rb + b * nc + j, 0))
    full = lambda a: pl.BlockSpec(a.shape, lambda b, j: (0,) * a.ndim)
    sspec = pl.BlockSpec((1, GD_HEADS, GD_DK, GD_DV), lambda b, j: (b, 0, 0, 0))
    return pl.pallas_call(
        functools.partial(_gdn_kernel, c=c, n_valid=n_valid),
        out_shape=[jax.ShapeDtypeStruct((bsz * lpad, GD_W), BF16),
                   jax.ShapeDtypeStruct((bsz, GD_HEADS, GD_DK, GD_DV), F32),
                   jax.ShapeDtypeStruct((bsz, CONV_W - 1, GD_QKV), F32)],
        grid=(bsz, nc),
        in_specs=[rowspec(GD_QKV), rowspec(GD_W), rowspec(LANES), full(lp['gdn_conv']), full(hp), full(ng),
                  pl.BlockSpec((1, SUBLANES, GD_QKV), lambda b, j: (b, 0, 0)), sspec],
        out_specs=[pl.BlockSpec((c, GD_W), lambda b, j: (b * nc + j, 0)), sspec,
                   pl.BlockSpec((1, CONV_W - 1, GD_QKV), lambda b, j: (b, 0, 0))],
        scratch_shapes=[pltpu.VMEM((c + SUBLANES, GD_QKV), F32), pltpu.VMEM((GD_HEADS, GD_DK, GD_DV), F32),
                        pltpu.VMEM((GD_HEADS, c, GD_DV + GD_DK), F32)],
        compiler_params=_cparams(("parallel", "arbitrary")),
        name="gdn_chunk",
    )(pqkv, pz, pba, lp['gdn_conv'], hp, ng, tail0, s0)


def _hgrn_kernel(p_ref, lbv_ref, ng_ref, s0_ref, o_ref, s1_ref, st_scr, *, c, n_valid):
    ci = pl.program_id(1)

    @pl.when(ci == 0)
    def _():
        for h in range(HG_HEADS):
            st_scr[h] = jnp.transpose(s0_ref[0, h])

    rid = _iota((c, c), 0)
    cid = _iota((c, c), 1)
    tri = (rid >= cid).astype(BF16)
    w = HG_HEADS * HG_DF
    rows = _iota((c, w), 0)
    sb = min(c, HG_SUB)
    row_in_sub = rows % sb
    lbv = lbv_ref[...]
    heads = range(HG_HEADS)
    hs = lambda x, h: x[:, h * HG_DF:(h + 1) * HG_DF]

    pf = p_ref[:, w:2 * w]
    v = p_ref[:, 2 * w:2 * w + HG_W]
    q = _silu(p_ref[:, 0:w])
    a = lbv[0:1]
    b = lbv[1:2] - _softplus(-pf)
    logf = jnp.maximum(a, b) + jnp.log1p(jnp.exp(-jnp.abs(a - b)))
    k = lbv[2:3] * _sigmoid(-pf)
    if n_valid < c:
        logf = jnp.where(rows < n_valid, logf, 0.0)
        k = jnp.where(rows < n_valid, k, 0.0)
    bc = _mdot(tri, logf)
    st_old = [st_scr[h] for h in heads]
    qe = q * jnp.exp(bc)
    os_ = [_dot3(hs(qe, h), st_old[h], 'nt') for h in heads]
    for delta in range(sb):
        if delta == 0:
            k_s, b_s, v_s = k, bc, v
        else:
            k_s, b_s, v_s = (pltpu.roll(t, delta, 0) for t in (k, bc, v))
        prod = jnp.where(row_in_sub >= delta, q * k_s * jnp.exp(jnp.minimum(bc - b_s, 0.0)), 0.0)
        os_ = [o + jnp.sum(hs(prod, h), axis=-1, keepdims=True) * hs(v_s, h) for h, o in zip(heads, os_)]
    if c > sb:
        parts = [[jnp.zeros((sb, HG_DV), F32)] for _ in heads]
        for r0 in range(sb, c, sb):
            ref = bc[r0 - 1:r0]
            qi = q[r0:r0 + sb] * jnp.exp(bc[r0:r0 + sb] - ref)
            kp = k[0:r0] * jnp.exp(ref - bc[0:r0])
            att = [_dot3(hs(qi, h), hs(kp, h), 'nt') for h in heads]
            for h in heads:
                parts[h].append(_dot3(att[h], hs(v, h)[0:r0]))
        os_ = [o + jnp.concatenate(p, axis=0) for o, p in zip(os_, parts)]
    blast = bc[c - 1:c]
    kd = k * jnp.exp(blast - bc)
    eb = jnp.exp(blast)
    st_new = [st_old[h] * hs(eb, h) + _dot3(hs(v, h), hs(kd, h), 'tn') for h in heads]
    os_ = [o * lax.rsqrt(jnp.mean(o * o, axis=-1, keepdims=True) + NORM_EPS) for o in os_]
    o_all = jnp.concatenate(os_, axis=1) * ng_ref[...] * _sigmoid(p_ref[:, 2 * w + HG_W:2 * w + 2 * HG_W])
    o_ref[...] = o_all.astype(o_ref.dtype)
    for h in heads:
        st_scr[h] = st_new[h]

    @pl.when(ci == pl.num_programs(1) - 1)
    def _():
        for h in range(HG_HEADS):
            s1_ref[0, h] = jnp.transpose(st_scr[h])


def _hgrn_group(phg, lb, lp, bsz, lpad, n_valid_len, row0, s0):
    c = _pick(lpad, (64, 32, 16, 8))
    nc = lpad // c
    n_valid = c if nc > 1 else n_valid_len
    rb = row0 // c
    lbv = jnp.zeros((SUBLANES, HG_HEADS * HG_DF), F32)
    lbv = lbv.at[0].set(jnp.log(jnp.maximum(lb, LB_FLOOR))).at[1].set(jnp.log1p(-lb)).at[2].set(1.0 - lb)
    ng = lp['hgrn_norm_g'].reshape(1, HG_W)
    full = lambda a: pl.BlockSpec(a.shape, lambda b, j: (0,) * a.ndim)
    sspec = pl.BlockSpec((1, HG_HEADS, HG_DF, HG_DV), lambda b, j: (b, 0, 0, 0))
    return pl.pallas_call(
        functools.partial(_hgrn_kernel, c=c, n_valid=n_valid),
        out_shape=[jax.ShapeDtypeStruct((bsz * lpad, HG_W), BF16),
                   jax.ShapeDtypeStruct((bsz, HG_HEADS, HG_DF, HG_DV), F32)],
        grid=(bsz, nc),
        in_specs=[pl.BlockSpec((c, HG_COLS), lambda b, j: (rb + b * nc + j, 0)), full(lbv), full(ng), sspec],
        out_specs=[pl.BlockSpec((c, HG_W), lambda b, j: (b * nc + j, 0)), sspec],
        scratch_shapes=[pltpu.VMEM((HG_HEADS, HG_DV, HG_DF), F32)],
        compiler_params=_cparams(("parallel", "arbitrary")),
        name="hgrn_chunk",
    )(phg, lbv, ng, s0)


def _ret_kernel(p_ref, cos_ref, sin_ref, qd_ref, kd_ref, dec_ref, dm_ref, gn_ref, s0_ref,
                o_ref, s1_ref, s_scr, *, c):
    ci = pl.program_id(1)

    @pl.when(ci == 0)
    def _():
        s_scr[...] = jnp.zeros_like(s_scr)
        for h in range(RT_HEADS):
            s_scr[h * RT_DK:(h + 1) * RT_DK, h * RT_DV:(h + 1) * RT_DV] = s0_ref[0, h]

    pq = p_ref[:, 0:RT_QK]
    pk = p_ref[:, RT_QK:2 * RT_QK]
    pv = p_ref[:, 2 * RT_QK:2 * RT_QK + RT_W]
    cos = cos_ref[...]
    sin = sin_ref[...]
    lane = _iota((c, RT_QK), 1)
    first_half = (lane % RT_DK) < (RT_DK // 2)

    def rope(x):
        partner = jnp.where(first_half, pltpu.roll(x, RT_QK - RT_DK // 2, 1), pltpu.roll(x, RT_DK // 2, 1))
        return x * cos + partner * sin

    q = rope(pq)
    k = rope(pk) * (RT_DK ** -0.5)
    sblk = s_scr[...]
    o_inter = _dot3(q * qd_ref[...], sblk)
    gn = gn_ref[...]
    q_heads = jnp.concatenate([jnp.where(lane // RT_DK == h, q, 0.0) for h in range(RT_HEADS)], axis=0)
    qk = _dot3(q_heads, k, 'nt')
    for h in range(RT_HEADS):
        attn = qk[h * c:(h + 1) * c] * dec_ref[h]
        v = pv[:, h * RT_DV:(h + 1) * RT_DV]
        o = o_inter[:, h * RT_DV:(h + 1) * RT_DV] + _dot3(attn, v)
        mu = jnp.mean(o, axis=-1, keepdims=True)
        xc = o - mu
        o = xc * lax.rsqrt(jnp.mean(xc * xc, axis=-1, keepdims=True) + NORM_EPS)
        pg = p_ref[:, 2 * RT_QK + RT_W + h * RT_DV:2 * RT_QK + RT_W + (h + 1) * RT_DV]
        o_ref[:, h * RT_DV:(h + 1) * RT_DV] = (o * gn[:, h * RT_DV:(h + 1) * RT_DV] * _silu(pg)).astype(o_ref.dtype)
    dm = dm_ref[...]
    s_scr[...] = sblk * dm + jnp.where(dm > 0.0, _dot3(k * kd_ref[...], pv, 'tn'), 0.0)

    @pl.when(ci == pl.num_programs(1) - 1)
    def _():
        for h in range(RT_HEADS):
            s1_ref[0, h] = s_scr[h * RT_DK:(h + 1) * RT_DK, h * RT_DV:(h + 1) * RT_DV]


def _ret_group(prt, lp, bsz, lpad, n_valid_len, row0, pos0, s0):
    c = _pick(lpad, (64, 32, 16, 8))
    nc = lpad // c
    n_valid = c if nc > 1 else n_valid_len
    rb = row0 // c
    half = RT_DK // 2
    inv = ROPE_BASE ** (-np.arange(half, dtype=np.float64) / half)
    ang = (pos0 + np.arange(lpad, dtype=np.float64))[:, None] * inv[None, :]
    cos = np.tile(np.cos(ang), (1, 2 * RT_HEADS))
    sin = np.tile(np.concatenate([-np.sin(ang), np.sin(ang)], axis=1), (1, RT_HEADS))
    loggamma = np.log(1.0 - np.exp2(-5.0 - np.arange(RT_HEADS, dtype=np.float64)))
    gcum = loggamma[:, None] * np.arange(1, c + 1, dtype=np.float64)[None, :]
    idx = np.arange(c)
    dec = np.where(idx[:, None] >= idx[None, :], np.exp(gcum[:, :, None] - gcum[:, None, :]), 0.0)
    qd = np.repeat(np.exp(gcum).T, RT_DK, axis=1)
    kdec = np.where(idx[None, :] < n_valid, np.exp(gcum[:, n_valid - 1:n_valid] - gcum), 0.0)
    kd = np.repeat(kdec.T, RT_DK, axis=1)
    sdec = np.exp(gcum[:, n_valid - 1])
    dm = np.zeros((RT_QK, RT_W))
    for h in range(RT_HEADS):
        dm[h * RT_DK:(h + 1) * RT_DK, h * RT_DV:(h + 1) * RT_DV] = sdec[h]
    cos, sin, qd, kd, dec, dm = (jnp.asarray(a, F32) for a in (cos, sin, qd, kd, dec, dm))
    gn = lp['ret_gn_g'].reshape(1, RT_W)
    full = lambda a: pl.BlockSpec(a.shape, lambda b, j: (0,) * a.ndim)
    posspec = pl.BlockSpec((c, RT_QK), lambda b, j: (j, 0))
    sspec = pl.BlockSpec((1, RT_HEADS, RT_DK, RT_DV), lambda b, j: (b, 0, 0, 0))
    return pl.pallas_call(
        functools.partial(_ret_kernel, c=c),
        out_shape=[jax.ShapeDtypeStruct((bsz * lpad, RT_W), BF16),
                   jax.ShapeDtypeStruct((bsz, RT_HEADS, RT_DK, RT_DV), F32)],
        grid=(bsz, nc),
        in_specs=[pl.BlockSpec((c, RT_COLS), lambda b, j: (rb + b * nc + j, 0)), posspec, posspec,
                  full(qd), full(kd), full(dec), full(dm), full(gn), sspec],
        out_specs=[pl.BlockSpec((c, RT_W), lambda b, j: (b * nc + j, 0)), sspec],
        scratch_shapes=[pltpu.VMEM((RT_QK, RT_W), F32)],
        compiler_params=_cparams(("parallel", "arbitrary")),
        name="ret_chunk",
    )(prt, cos, sin, qd, kd, dec, dm, gn, s0)


def _merge_kernel(x_ref, o1, o2, o3, o4, gate_ref, w1, w2, w3, w4, wo_ref, out_ref, *, d):
    acc = None
    for i, (o, w) in enumerate(((o1, w1), (o2, w2), (o3, w3), (o4, w4))):
        term = _sigmoid(gate_ref[:, i * d:(i + 1) * d]) * _dot(o[...], w[...])
        acc = term if acc is None else acc + term
    out_ref[...] = x_ref[...] + _dot(acc.astype(BF16), wo_ref[...])


def _merge(x, outs, gate, wouts, wo):
    n, d = x.shape
    tm = _pick(n, (256, 128, 64, 32, 16))
    row = lambda w: pl.BlockSpec((tm, w), lambda i: (i, 0))
    full = lambda a: pl.BlockSpec(a.shape, lambda i: (0,) * a.ndim)
    return pl.pallas_call(
        functools.partial(_merge_kernel, d=d),
        out_shape=jax.ShapeDtypeStruct((n, d), F32),
        grid=(n // tm,),
        in_specs=[row(d)] + [row(o.shape[1]) for o in outs] + [row(N_BRANCH * d)]
        + [full(w) for w in wouts] + [full(wo)],
        out_specs=row(d),
        compiler_params=_cparams(("parallel",)),
        name="merge",
    )(x, *outs, gate, *wouts, wo)


def _route_kernel(x_ref, g_ref, rg_ref, re_ref, rb_ref, xn_ref, col_ref, row_ref, cnt_ref):
    tm = x_ref.shape[0]
    lane = _iota((tm, LANES), 1)
    lanef = lane.astype(F32)
    x = x_ref[...]
    xn = x * lax.rsqrt(jnp.mean(x * x, axis=-1, keepdims=True) + NORM_EPS) * g_ref[...]
    xn_ref[...] = xn.astype(BF16)
    rb = rb_ref[...]
    neg = jnp.float32(-jnp.inf)
    glog = jnp.where(lane < N_GROUPS, _dot(xn, rg_ref[...], HI) + rb[0:1], neg)
    gmax = jnp.max(glog, axis=-1, keepdims=True)
    gsum = jnp.sum(jnp.exp(glog - gmax), axis=-1, keepdims=True)
    gidx = jnp.min(jnp.where(glog == gmax, lanef, float(LANES)), axis=-1, keepdims=True)
    gp = 1.0 / gsum
    in_group = (lanef >= gidx * EXPERTS_PER_GROUP) & (lanef < (gidx + 1.0) * EXPERTS_PER_GROUP)
    elog = jnp.where(in_group, _dot(xn, re_ref[...], HI) + rb[1:2], neg)
    emax = jnp.max(elog, axis=-1, keepdims=True)
    eexp = jnp.exp(elog - emax)
    ep = eexp / jnp.sum(eexp, axis=-1, keepdims=True)
    ep = jnp.where(in_group, ep, -1.0)
    p1 = jnp.max(ep, axis=-1, keepdims=True)
    i1 = jnp.min(jnp.where(ep == p1, lanef, float(LANES)), axis=-1, keepdims=True)
    ep2 = jnp.where(lanef == i1, -1.0, ep)
    p2 = jnp.max(ep2, axis=-1, keepdims=True)
    i2 = jnp.min(jnp.where(ep2 == p2, lanef, float(LANES)), axis=-1, keepdims=True)
    denom = p1 + p2
    wt1 = gp * p1 / denom
    wt2 = gp * p2 / denom

    onehot = ((lanef == i1) | (lanef == i2)).astype(F32)
    cnt = jnp.sum(onehot, axis=0, keepdims=True)
    before = (_iota((LANES, LANES), 0) < _iota((LANES, LANES), 1)).astype(BF16)
    off = _segsum(jnp.broadcast_to(cnt, (SUBLANES, LANES)), before)[0:1]
    tri = (_iota((tm, tm), 0) >= _iota((tm, tm), 1)).astype(BF16)
    slot = off + _dot(tri, onehot.astype(BF16)) - 1.0
    pos1 = jnp.sum(jnp.where(lanef == i1, slot, 0.0), axis=-1, keepdims=True)
    pos2 = jnp.sum(jnp.where(lanef == i2, slot, 0.0), axis=-1, keepdims=True)
    col = jnp.where(lane == 0, pos1, jnp.where(lane == 1, pos2, jnp.where(lane == 2, wt1,
                                                                         jnp.where(lane == 3, wt2, 0.0))))
    col_ref[...] = col
    row_ref[0] = jnp.transpose(col)[0:SUBLANES]
    cnt_ref[0] = jnp.concatenate([cnt, off, jnp.zeros((SUBLANES - 2, LANES), F32)], axis=0)


def _moe_kernel(cnt_sm, off_sm, x_ref, xn_ref, col_ref, row_ref, wg_ref, wu_ref, wd_ref, out_ref,
                xs_scr, ys_scr, ws_scr, *, pb, rb):
    i = pl.program_id(0)
    e = pl.program_id(1)
    tm, d = x_ref.shape
    ns = 2 * tm

    @pl.when(e == 0)
    def _():
        rowd = row_ref[0]
        pos1, pos2, wt1, wt2 = rowd[0:1], rowd[1:2], rowd[2:3], rowd[3:4]
        for blk in range(ns // pb):
            sid = (_iota((pb, tm), 0) + blk * pb).astype(F32)
            m1 = sid == pos1
            m2 = sid == pos2
            xs_scr[blk * pb:(blk + 1) * pb, :] = _dot((m1 | m2).astype(BF16), xn_ref[...]).astype(BF16)
            wsl = jnp.sum(jnp.where(m1, wt1, 0.0) + jnp.where(m2, wt2, 0.0), axis=-1, keepdims=True)
            ws_scr[blk * pb:(blk + 1) * pb, :] = jnp.broadcast_to(wsl, (pb, LANES))
        ys_scr[...] = jnp.zeros_like(ys_scr)

    cnt = cnt_sm[i, e]
    off = off_sm[i, e]

    @pl.when(cnt > 0)
    def _():
        def body(j, carry):
            r0 = pl.multiple_of(j * rb, rb)
            xb = xs_scr[pl.ds(r0, rb), :]
            hid = _silu(_dot(xb, wg_ref[0])) * _dot(xb, wu_ref[0])
            y = _dot(hid.astype(BF16), wd_ref[0])
            srow = r0 + _iota((rb, d), 0)
            mine = (srow >= off) & (srow < off + cnt)
            ys_scr[pl.ds(r0, rb), :] += jnp.where(mine, y, 0.0)
            return carry

        lax.fori_loop(off // rb, (off + cnt - 1) // rb + 1, body, 0)

    @pl.when(e == pl.num_programs(1) - 1)
    def _():
        col = col_ref[...]
        pos1, pos2 = col[:, 0:1], col[:, 1:2]
        acc = x_ref[...]
        for blk in range(ns // pb):
            sid = (_iota((tm, pb), 1) + blk * pb).astype(F32)
            pt = ((sid == pos1) | (sid == pos2)).astype(BF16)
            ysw = ys_scr[blk * pb:(blk + 1) * pb, :] * ws_scr[blk * pb:(blk + 1) * pb, 0:1]
            hi, lo = _split2(ysw)
            acc = acc + _dot(pt, hi) + _dot(pt, lo)
        out_ref[...] = acc


def _moe(x, lp):
    n, d = x.shape
    tm = _pick(n, (MOE_TILE, 512, 256, 128, 64))
    nt = n // tm
    de = lp['moe_w_gate'].shape[-1]
    rg = jnp.zeros((d, LANES), F32).at[:, :N_GROUPS].set(lp['router_group'])
    re = jnp.zeros((d, LANES), F32).at[:, :N_EXPERTS].set(lp['router_expert'])
    rbias = jnp.zeros((SUBLANES, LANES), F32)
    rbias = rbias.at[0, :N_GROUPS].set(lp['router_group_b']).at[1, :N_EXPERTS].set(lp['router_expert_b'])
    g = lp['norm2_g'].reshape(1, d)
    full1 = lambda a: pl.BlockSpec(a.shape, lambda i: (0,) * a.ndim)
    xn, col, row, cnt = pl.pallas_call(
        _route_kernel,
        out_shape=[jax.ShapeDtypeStruct((n, d), BF16), jax.ShapeDtypeStruct((n, LANES), F32),
                   jax.ShapeDtypeStruct((nt, SUBLANES, tm), F32), jax.ShapeDtypeStruct((nt, SUBLANES, LANES), F32)],
        grid=(nt,),
        in_specs=[pl.BlockSpec((tm, d), lambda i: (i, 0)), full1(g), full1(rg), full1(re), full1(rbias)],
        out_specs=[pl.BlockSpec((tm, d), lambda i: (i, 0)), pl.BlockSpec((tm, LANES), lambda i: (i, 0)),
                   pl.BlockSpec((1, SUBLANES, tm), lambda i: (i, 0, 0)),
                   pl.BlockSpec((1, SUBLANES, LANES), lambda i: (i, 0, 0))],
        compiler_params=_cparams(("parallel",)),
        name="moe_route",
    )(x, g, rg, re, rbias)
    cnt_i = cnt[:, 0, :N_EXPERTS].astype(jnp.int32)
    off_i = cnt[:, 1, :N_EXPERTS].astype(jnp.int32)
    ns = 2 * tm
    pb = min(MOE_SLOT_BLOCK, ns)
    rb = min(MOE_ROW_BLOCK, ns)
    grid_spec = pltpu.PrefetchScalarGridSpec(
        num_scalar_prefetch=2,
        grid=(nt, N_EXPERTS),
        in_specs=[pl.BlockSpec((tm, d), lambda i, e, c, o: (i, 0)),
                  pl.BlockSpec((tm, d), lambda i, e, c, o: (i, 0)),
                  pl.BlockSpec((tm, LANES), lambda i, e, c, o: (i, 0)),
                  pl.BlockSpec((1, SUBLANES, tm), lambda i, e, c, o: (i, 0, 0)),
                  pl.BlockSpec((1, d, de), lambda i, e, c, o: (e, 0, 0)),
                  pl.BlockSpec((1, d, de), lambda i, e, c, o: (e, 0, 0)),
                  pl.BlockSpec((1, de, d), lambda i, e, c, o: (e, 0, 0))],
        out_specs=pl.BlockSpec((tm, d), lambda i, e, c, o: (i, 0)),
        scratch_shapes=[pltpu.VMEM((ns, d), BF16), pltpu.VMEM((ns, d), F32), pltpu.VMEM((ns, LANES), F32)],
    )
    return pl.pallas_call(
        functools.partial(_moe_kernel, pb=pb, rb=rb),
        out_shape=jax.ShapeDtypeStruct((n, d), F32),
        grid_spec=grid_spec,
        compiler_params=pltpu.CompilerParams(dimension_semantics=("parallel", "arbitrary"),
                                             vmem_limit_bytes=MOE_VMEM_LIMIT),
        name="moe",
    )(cnt_i, off_i, x, xn, col, row, lp['moe_w_gate'].astype(BF16), lp['moe_w_up'].astype(BF16),
      lp['moe_w_down'].astype(BF16))


def _layer(x, lp, lb, groups):
    d = x.shape[1]
    xn = _rmsnorm(x, lp['norm1_g'], BF16)
    w_in = lp['w_in']
    offs = np.cumsum([0, RW_COLS, GD_QKV, GD_W, 2 * GD_HEADS, HG_COLS, RT_COLS, N_BRANCH * d])
    seg = [w_in[:, offs[i]:offs[i + 1]] for i in range(7)]
    seg[3] = jnp.pad(seg[3], ((0, 0), (0, LANES - 2 * GD_HEADS)))
    p_rw, p_gqkv, p_gz, p_gba, p_hg, p_rt, p_gate = (
        _matmul(xn, s.astype(BF16), f"proj{i}") for i, s in enumerate(seg))

    o_rw_parts, o_gd_parts, o_hg_parts, o_rt_parts, new_states = [], [], [], [], []
    for gr in groups:
        b, seq, lpad, row0 = gr['bsz'], gr['seq'], gr['lpad'], gr['row0']
        shift0, wkv0, conv0, gdn0, hgrn0, ret0 = gr['states']
        o_rw, shift1, wkv1 = _wkv_group(p_rw, lp, b, lpad, seq, row0, shift0, wkv0)
        o_gd, gdn1, conv1 = _gdn_group(p_gqkv, p_gz, p_gba, lp, b, lpad, seq, row0, conv0, gdn0)
        o_hg, hgrn1 = _hgrn_group(p_hg, lb, lp, b, lpad, seq, row0, hgrn0)
        o_rt, ret1 = _ret_group(p_rt, lp, b, lpad, seq, row0, gr['pos0'], ret0)
        o_rw_parts.append(o_rw)
        o_gd_parts.append(o_gd)
        o_hg_parts.append(o_hg)
        o_rt_parts.append(o_rt)
        new_states.append((shift1, wkv1, conv1, gdn1, hgrn1, ret1))

    outs = [jnp.concatenate(parts, axis=0) for parts in (o_rw_parts, o_gd_parts, o_hg_parts, o_rt_parts)]
    wouts = [lp[n].astype(BF16) for n in ('w_out_rwkv', 'w_out_gdn', 'w_out_hgrn', 'w_out_ret')]
    x = _merge(x, outs, p_gate, wouts, lp['w_o'].astype(BF16))
    x = _moe(x, lp)
    return x, new_states


def kernel(x_prompt, x_sample, state_rwkv_shift, state_rwkv_wkv, state_gdn_conv, state_gdn, state_hgrn, state_ret, norm1_g, w_in, rwkv_mu, rwkv_w0, rwkv_w2, rwkv_a0, rwkv_a2, rwkv_g2, rwkv_k_k, rwkv_k_a, rwkv_r_k, rwkv_ln_g, rwkv_ln_b, w_out_rwkv, gdn_conv, gdn_a_log, gdn_dt_bias, gdn_norm_g, w_out_gdn, hgrn_lb_logits, hgrn_norm_g, w_out_hgrn, ret_gn_g, w_out_ret, w_o, norm2_g, router_group, router_group_b, router_expert, router_expert_b, moe_w_gate, moe_w_up, moe_w_down, final_norm_g):
    params = dict(norm1_g=norm1_g, w_in=w_in, rwkv_mu=rwkv_mu, rwkv_w0=rwkv_w0, rwkv_w2=rwkv_w2,
                  rwkv_a0=rwkv_a0, rwkv_a2=rwkv_a2, rwkv_g2=rwkv_g2, rwkv_k_k=rwkv_k_k,
                  rwkv_k_a=rwkv_k_a, rwkv_r_k=rwkv_r_k, rwkv_ln_g=rwkv_ln_g, rwkv_ln_b=rwkv_ln_b,
                  w_out_rwkv=w_out_rwkv, gdn_conv=gdn_conv, gdn_a_log=gdn_a_log,
                  gdn_dt_bias=gdn_dt_bias, gdn_norm_g=gdn_norm_g, w_out_gdn=w_out_gdn,
                  hgrn_norm_g=hgrn_norm_g, w_out_hgrn=w_out_hgrn, ret_gn_g=ret_gn_g,
                  w_out_ret=w_out_ret, w_o=w_o, norm2_g=norm2_g, router_group=router_group,
                  router_group_b=router_group_b, router_expert=router_expert,
                  router_expert_b=router_expert_b, moe_w_gate=moe_w_gate, moe_w_up=moe_w_up,
                  moe_w_down=moe_w_down)
    depth = w_in.shape[0]
    bp, lp_len, d = x_prompt.shape
    bs, ls, _ = x_sample.shape
    ls_pad = -(-ls // SAMPLE_PAD_LEN) * SAMPLE_PAD_LEN
    sm = jax.nn.softmax(hgrn_lb_logits.astype(F32), axis=0)
    lower_bounds = jnp.cumsum(sm, axis=0) - sm[0]

    sample_states = (state_rwkv_shift, state_rwkv_wkv, state_gdn_conv, state_gdn, state_hgrn, state_ret)
    xs = jnp.pad(x_sample, ((0, 0), (0, ls_pad - ls), (0, 0)))
    x = jnp.concatenate([x_prompt.reshape(bp * lp_len, d), xs.reshape(bs * ls_pad, d)], axis=0)

    prompt_out = [[] for _ in sample_states]
    sample_out = [[] for _ in sample_states]
    for layer in range(depth):
        lpar = {name: arr[layer] for name, arr in params.items()}
        groups = [
            dict(bsz=bp, seq=lp_len, lpad=lp_len, row0=0, pos0=0,
                 states=tuple(jnp.zeros((bp,) + s.shape[2:], F32) for s in sample_states)),
            dict(bsz=bs, seq=ls, lpad=ls_pad, row0=bp * lp_len, pos0=PAST_LEN,
                 states=tuple(s[layer].astype(F32) for s in sample_states)),
        ]
        x, new = _layer(x, lpar, lower_bounds[layer], groups)
        for lst, n in zip(prompt_out, new[0]):
            lst.append(n)
        for lst, n in zip(sample_out, new[1]):
            lst.append(n)

    y = _rmsnorm(x, final_norm_g, F32)
    y_prompt = y[:bp * lp_len].reshape(bp, lp_len, d)
    y_sample = y[bp * lp_len:].reshape(bs, ls_pad, d)[:, :ls]
    p_states = [jnp.stack(lst) for lst in prompt_out]
    s_states = [jnp.stack(lst).astype(o.dtype) for lst, o in zip(sample_out, sample_states)]
    return (y_prompt, y_sample, *p_states, *s_states)
```

```python
import functools
import math

import numpy as np
import jax
import jax.numpy as jnp
from jax import lax
from jax.experimental import pallas as pl
from jax.experimental.pallas import tpu as pltpu

F32 = jnp.float32
BF16 = jnp.bfloat16
HI = lax.Precision.HIGHEST

NORM_EPS = 1e-6
LB_FLOOR = 1e-30
PAST_LEN = 16384
RW_HEADS = 8
RW_HD = 64
RW_W = RW_HEADS * RW_HD
RW_DECAY_LORA = 64
RW_AAA_LORA = 64
RW_GATE_LORA = 128
RW_COLS = 3 * RW_W + RW_DECAY_LORA + RW_AAA_LORA + RW_GATE_LORA
RW_GN_EPS = 64e-5
GD_HEADS = 4
GD_DK = 128
GD_DV = 128
GD_QKV = GD_HEADS * (2 * GD_DK + GD_DV)
GD_W = GD_HEADS * GD_DV
CONV_W = 4
HG_HEADS = 4
HG_DF = 128
HG_DV = 128
HG_W = HG_HEADS * HG_DV
HG_COLS = 2 * HG_HEADS * HG_DF + 2 * HG_W
HG_SUB = 16
RT_HEADS = 4
RT_DK = 64
RT_DV = 128
RT_W = RT_HEADS * RT_DV
RT_QK = RT_HEADS * RT_DK
RT_COLS = 2 * RT_QK + 2 * RT_W
ROPE_BASE = 10000.0
N_BRANCH = 4
N_GROUPS = 4
EXPERTS_PER_GROUP = 8
N_EXPERTS = N_GROUPS * EXPERTS_PER_GROUP

LANES = 128
SUBLANES = 8
SAMPLE_PAD_LEN = 8
CHUNK_ROWS = 64
SLOTS_MIN = 2
VMEM_LIMIT = 48 * 1024 * 1024
MOE_TILE = 1024
MOE_SLOT_BLOCK = 256
MOE_ROW_BLOCK = 128
BF16_ROWS = 16
MOE_VMEM_LIMIT = 56 * 1024 * 1024


def _pick(n, cands):
    for c in cands:
        if n % c == 0:
            return c
    raise ValueError(f"no tile for {n} in {cands}")


def _cparams(sem):
    return pltpu.CompilerParams(dimension_semantics=sem, vmem_limit_bytes=VMEM_LIMIT)


def _dot(a, b, prec=None):
    return lax.dot_general(a, b, (((1,), (0,)), ((), ())), precision=prec, preferred_element_type=F32)


def _softplus(x):
    return jnp.maximum(x, 0.0) + jnp.log1p(jnp.exp(-jnp.abs(x)))


def _sigmoid(x):
    return jax.nn.sigmoid(x)


def _silu(x):
    return x * jax.nn.sigmoid(x)


def _segsum(x, hm):
    hi = x.astype(BF16)
    r1 = x - hi.astype(F32)
    mid = r1.astype(BF16)
    lo = (r1 - mid.astype(F32)).astype(BF16)
    return _dot(hi, hm) + _dot(mid, hm) + _dot(lo, hm)


_DIMS = {'nn': (((1,), (0,)), ((), ())), 'nt': (((1,), (1,)), ((), ())), 'tn': (((0,), (0,)), ((), ()))}


def _split2(x):
    hi = x.astype(BF16)
    return hi, (x - hi.astype(F32)).astype(BF16)


def _dot3(a, b, form='nn'):
    ah, al = _split2(a)
    bh, bl = _split2(b)
    f = lambda x, y: lax.dot_general(x, y, _DIMS[form], preferred_element_type=F32)
    free = 1 if form == 'tn' else 0
    m = a.shape[free]
    both = f(jnp.concatenate([ah, al], axis=free), bh)
    return both[0:m] + both[m:2 * m] + f(ah, bl)


def _mdot(mask, x):
    hi = x.astype(BF16)
    r1 = x - hi.astype(F32)
    mid = r1.astype(BF16)
    lo = (r1 - mid.astype(F32)).astype(BF16)
    return _dot(mask, hi) + _dot(mask, mid) + _dot(mask, lo)


def _unit_lower_inverses(ms, rid, cid, c, expand=None):
    prod = _dot3 if expand is None else (lambda x, y: _dot3(x, expand(y)))
    same = lambda s: (rid // s) == (cid // s)
    ns = [jnp.where(same(SUBLANES), -m, 0.0) for m in ms]
    n2s = [prod(n, n) for n in ns]
    n4s = [prod(n2, n2) for n2 in n2s]
    eye = (rid == cid).astype(F32)
    ps = [eye + n for n in ns]
    ps = [p + prod(p, n2) for p, n2 in zip(ps, n2s)]
    ps = [p + prod(p, n4) for p, n4 in zip(ps, n4s)]
    s = SUBLANES
    while s < c:
        offs = [jnp.where(same(2 * s) & jnp.logical_not(same(s)), m, 0.0) for m in ms]
        ts = [prod(p, off) for p, off in zip(ps, offs)]
        ps = [p - prod(t, p) for p, t in zip(ps, ts)]
        s *= 2
    return ps


def _iota(shape, dim):
    return lax.broadcasted_iota(jnp.int32, shape, dim)


def _rms_kernel(x_ref, g_ref, o_ref):
    x = x_ref[...]
    ms = jnp.mean(x * x, axis=-1, keepdims=True)
    o_ref[...] = (x * lax.rsqrt(ms + NORM_EPS) * g_ref[...]).astype(o_ref.dtype)


def _rmsnorm(x, g, out_dtype):
    n, d = x.shape
    tm = _pick(n, (1024, 512, 256, 128, 64, 32, 16))
    return pl.pallas_call(
        _rms_kernel,
        out_shape=jax.ShapeDtypeStruct((n, d), out_dtype),
        grid=(n // tm,),
        in_specs=[pl.BlockSpec((tm, d), lambda i: (i, 0)), pl.BlockSpec((1, d), lambda i: (0, 0))],
        out_specs=pl.BlockSpec((tm, d), lambda i: (i, 0)),
        compiler_params=_cparams(("parallel",)),
        name="rmsnorm",
    )(x, g.reshape(1, d))


def _mm_kernel(x_ref, w_ref, o_ref):
    o_ref[...] = _dot(x_ref[...], w_ref[...])


def _matmul(x, w, name):
    n, k = x.shape
    m = w.shape[1]
    tm = _pick(n, (1024, 512, 256, 128, 64, 32, 16))
    tn = _pick(m, (1024, 896, 768, 512, 256, 128))
    return pl.pallas_call(
        _mm_kernel,
        out_shape=jax.ShapeDtypeStruct((n, m), F32),
        grid=(n // tm, m // tn),
        in_specs=[pl.BlockSpec((tm, k), lambda i, j: (i, 0)), pl.BlockSpec((k, tn), lambda i, j: (0, j))],
        out_specs=pl.BlockSpec((tm, tn), lambda i, j: (i, j)),
        compiler_params=_cparams(("parallel", "parallel")),
        name=name,
    )(x, w)


def _head_sum_matrix(width, seg):
    i = np.arange(width)
    return jnp.asarray((i[:, None] // seg) == (i[None, :] // seg), BF16)


def _wkv_chunk_kernel(p_ref, first_ref, mu_ref, vec_ref, w2_ref, a2_ref, g2_ref, hm_ref, ln_ref, s0_ref,
                      o_ref, s1_ref, shift_ref, carry_scr, ht_scr, *, cs, nseq, n_valid, multi_chunk):
    ci = pl.program_id(1)
    rows = cs * nseq
    npair = RW_HEADS // 2
    f_zero = jnp.zeros((RW_HD, RW_HD), F32)

    @pl.when(ci == 0)
    def _():
        for q in range(nseq):
            carry_scr[q] = first_ref[q]
            for pr in range(npair):
                top = jnp.concatenate([s0_ref[q, 2 * pr], f_zero], axis=1)
                bot = jnp.concatenate([f_zero, s0_ref[q, 2 * pr + 1]], axis=1)
                ht_scr[q, pr] = jnp.concatenate([top, bot], axis=0)

    p = p_ref[...]
    row_w = _iota((rows, RW_COLS), 0)
    prev = pltpu.roll(p, 1, 0)
    for q in range(nseq):
        prev = jnp.where(row_w == q * cs, carry_scr[q], prev)
    h = p + (prev - p) * mu_ref[...]
    r = h[:, 0:RW_W]
    k = h[:, RW_W:2 * RW_W]
    v = h[:, 2 * RW_W:3 * RW_W]
    lo = h[:, 3 * RW_W:3 * RW_W + LANES]
    gl = h[:, 3 * RW_W + LANES:3 * RW_W + 2 * LANES]
    vec = vec_ref[...]
    w0, a0, k_k, k_a, r_k = vec[0:1], vec[1:2], vec[2:3], vec[3:4], vec[4:5]
    hm = hm_ref[...]
    w_log = -_softplus(-(w0 + _dot3(jnp.tanh(lo), w2_ref[...]))) - 0.5
    logw = -jnp.exp(w_log)
    a = _sigmoid(a0 + _dot3(lo, a2_ref[...]))
    g = _dot3(_sigmoid(gl), g2_ref[...])
    kk = k * k_k
    kk = kk * lax.rsqrt(_segsum(kk * kk, hm) + NORM_EPS)
    km = k * (1.0 + (a - 1.0) * k_a)
    bb = kk * a
    alpha = -kk
    bonus = _segsum(r * km * r_k, hm) * v
    row_f = _iota((rows, RW_W), 0)
    if n_valid < cs:
        valid = (row_f % cs) < n_valid
        logw, alpha, bb, km = (jnp.where(valid, t, 0.0) for t in (logw, alpha, bb, km))

    rid = _iota((rows, rows), 0)
    cid = _iota((rows, rows), 1)
    tri = ((rid >= cid) & (rid // cs == cid // cs)).astype(BF16)
    gcum = _mdot(tri, logw)
    g_last = jnp.concatenate(
        [jnp.broadcast_to(gcum[q * cs + cs - 1:q * cs + cs], (cs, RW_W)) for q in range(nseq)], axis=0)
    abar = alpha * jnp.exp(gcum - logw)
    emg = jnp.exp(-gcum)
    bbar = bb * emg
    kbar = km * emg
    rbar = r * jnp.exp(gcum)
    etil = jnp.exp(g_last - gcum)
    btil = bb * etil
    ktil = km * etil
    w_last = jnp.exp(g_last)

    prow = _iota((rows, LANES), 0)
    plane = _iota((rows, LANES), 1)
    pcol = plane % RW_HD
    same_seq = (prow // cs) == (pcol // cs)
    strict = (pcol < prow) & same_seq
    incl = (pcol <= prow) & same_seq
    first_head = plane < RW_HD
    blk_mask = (_iota((LANES, LANES), 0) // RW_HD) == (_iota((LANES, LANES), 1) // RW_HD)

    def expand(x):
        return jnp.concatenate([jnp.where(first_head, x, 0.0), jnp.where(first_head, 0.0, x)], axis=0)

    pairs = range(npair)
    ps = lambda x, pr: x[:, pr * LANES:(pr + 1) * LANES]
    seq_rows = lambda x, q: x[q * cs:(q + 1) * cs]

    lhs = [jnp.concatenate([ps(abar, pr), ps(rbar, pr)], axis=0) for pr in pairs]
    xb = [_dot3(lhs[pr], expand(ps(bbar, pr)), 'nt') for pr in pairs]
    xk = [_dot3(lhs[pr], expand(ps(kbar, pr)), 'nt') for pr in pairs]
    a_m = [jnp.where(strict, x[0:rows], 0.0) for x in xb]
    rb_m = [jnp.where(incl, x[rows:2 * rows], 0.0) for x in xb]
    b_m = [jnp.where(strict, x[0:rows], 0.0) for x in xk]
    rk_m = [jnp.where(incl, x[rows:2 * rows], 0.0) for x in xk]
    pinv = _unit_lower_inverses([-m for m in a_m], prow, pcol, cs, expand=expand)

    ht_old = [[ht_scr[q, pr] for pr in pairs] for q in range(nseq)]
    xh = [[_dot3(jnp.concatenate([seq_rows(ps(abar, pr), q), seq_rows(ps(rbar, pr), q)], axis=0),
                 ht_old[q][pr], 'nt') for pr in pairs] for q in range(nseq)]
    ah = [jnp.concatenate([xh[q][pr][0:cs] for q in range(nseq)], axis=0) for pr in pairs]
    rh = [jnp.concatenate([xh[q][pr][cs:2 * cs] for q in range(nseq)], axis=0) for pr in pairs]
    vexp = [expand(ps(v, pr)) for pr in pairs]
    rhs = [ah[pr] + _dot3(b_m[pr], vexp[pr]) for pr in pairs]
    u = [_dot3(pinv[pr], expand(rhs[pr])) for pr in pairs]
    o_pairs = [rh[pr] + _dot3(jnp.concatenate([rb_m[pr], rk_m[pr]], axis=1),
                              jnp.concatenate([expand(u[pr]), vexp[pr]], axis=0)) for pr in pairs]
    for q in range(nseq):
        for pr in pairs:
            uv = jnp.concatenate([seq_rows(u[pr], q), seq_rows(ps(v, pr), q)], axis=0)
            bk = jnp.concatenate([seq_rows(ps(btil, pr), q), seq_rows(ps(ktil, pr), q)], axis=0)
            upd = jnp.where(blk_mask, _dot3(uv, bk, 'tn'), 0.0)
            ht_scr[q, pr] = ht_old[q][pr] * seq_rows(ps(w_last, pr), q)[0:1] + upd

    o = jnp.concatenate(o_pairs, axis=1)
    ln = ln_ref[...]
    mean = _segsum(o, hm) * (1.0 / RW_HD)
    xc = o - mean
    var = _segsum(xc * xc, hm) * (1.0 / RW_HD)
    y = xc * lax.rsqrt(var + RW_GN_EPS) * ln[0:1] + ln[1:2]
    o_ref[...] = ((y + bonus) * g).astype(o_ref.dtype)

    if multi_chunk:
        for q in range(nseq):
            carry_scr[q] = p[q * cs + cs - 1:q * cs + cs]

    @pl.when(ci == pl.num_programs(1) - 1)
    def _():
        for q in range(nseq):
            shift_ref[q] = p[q * cs + n_valid - 1:q * cs + n_valid]
            for pr in pairs:
                ht = ht_scr[q, pr]
                s1_ref[q, 2 * pr] = ht[0:RW_HD, 0:RW_HD]
                s1_ref[q, 2 * pr + 1] = ht[RW_HD:2 * RW_HD, RW_HD:2 * RW_HD]


def _wkv_group(p_rw, lp, bsz, lpad, n_valid_len, row0, shift0, s0):
    rows = RW_HD
    if lpad >= rows:
        cs, nseq = rows, 1
    else:
        cs, nseq = lpad, rows // lpad
    nc = lpad // cs
    n_valid = cs if nc > 1 else n_valid_len
    rb = row0 // rows
    vec = jnp.zeros((SUBLANES, RW_W), F32)
    vec = vec.at[0].set(lp['rwkv_w0']).at[1].set(lp['rwkv_a0']).at[2].set(lp['rwkv_k_k'])
    vec = vec.at[3].set(lp['rwkv_k_a']).at[4].set(lp['rwkv_r_k'].reshape(RW_W))
    zeros = jnp.zeros((RW_DECAY_LORA, RW_W), F32)
    w2p = jnp.concatenate([lp['rwkv_w2'], zeros], axis=0)
    a2p = jnp.concatenate([zeros, lp['rwkv_a2']], axis=0)
    hm = _head_sum_matrix(RW_W, RW_HD)
    ln = jnp.zeros((SUBLANES, RW_W), F32).at[0].set(lp['rwkv_ln_g']).at[1].set(lp['rwkv_ln_b'])
    mu = lp['rwkv_mu'].reshape(1, RW_COLS)
    first = shift0.reshape(bsz, 1, RW_COLS)
    full = lambda a: pl.BlockSpec(a.shape, lambda b, j: (0,) * a.ndim)
    sspec = pl.BlockSpec((nseq, RW_HEADS, RW_HD, RW_HD), lambda b, j: (b, 0, 0, 0))
    fspec = pl.BlockSpec((nseq, 1, RW_COLS), lambda b, j: (b, 0, 0))
    o, s1, shift1 = pl.pallas_call(
        functools.partial(_wkv_chunk_kernel, cs=cs, nseq=nseq, n_valid=n_valid, multi_chunk=nc > 1),
        out_shape=[jax.ShapeDtypeStruct((bsz * lpad, RW_W), BF16),
                   jax.ShapeDtypeStruct((bsz, RW_HEADS, RW_HD, RW_HD), F32),
                   jax.ShapeDtypeStruct((bsz, 1, RW_COLS), F32)],
        grid=(bsz // nseq, nc),
        in_specs=[pl.BlockSpec((rows, RW_COLS), lambda b, j: (rb + b * nc + j, 0)), fspec, full(mu), full(vec),
                  full(w2p), full(a2p), full(lp['rwkv_g2']), full(hm), full(ln), sspec],
        out_specs=[pl.BlockSpec((rows, RW_W), lambda b, j: (b * nc + j, 0)), sspec, fspec],
        scratch_shapes=[pltpu.VMEM((nseq, 1, RW_COLS), F32),
                        pltpu.VMEM((nseq, RW_HEADS // 2, LANES, LANES), F32)],
        compiler_params=_cparams(("parallel", "arbitrary")),
        name="wkv_chunk",
    )(p_rw, first, mu, vec, w2p, a2p, lp['rwkv_g2'], hm, ln, s0)
    return o, shift1.reshape(bsz, RW_COLS), s1


def _gdn_kernel(qkv_ref, z_ref, ba_ref, convw_ref, hp_ref, ng_ref, tail0_ref, s0_ref,
                o_ref, s1_ref, conv_ref, ext_scr, s_scr, *, c, nb, n_valid):
    ci = pl.program_id(1)
    last = ci == pl.num_programs(1) - 1
    slots = range(nb)
    heads = range(GD_HEADS)
    items = [(s, h) for s in slots for h in heads]

    @pl.when(ci == 0)
    def _():
        for s in slots:
            ext_scr[s, 0:SUBLANES, :] = tail0_ref[s]
            s_scr[s] = s0_ref[s]

    cw = convw_ref[...]
    hp = hp_ref[...]
    ng = ng_ref[...]
    off = SUBLANES - (CONV_W - 1)
    rid = _iota((c, c), 0)
    cid = _iota((c, c), 1)
    incl = rid >= cid
    tri = incl.astype(BF16)
    strict_l = (rid > cid).astype(F32)
    kbase, vbase = GD_HEADS * GD_DK, 2 * GD_HEADS * GD_DK

    cqs, beta_alls, g_alls, gcum_alls = [], [], [], []
    for s in slots:
        ext_scr[s, SUBLANES:SUBLANES + c, :] = qkv_ref[0, s, 0]
        cq = ext_scr[s, off:off + c, :] * cw[0:1]
        for j in range(1, CONV_W):
            cq = cq + ext_scr[s, off + j:off + j + c, :] * cw[j:j + 1]

        @pl.when(last)
        def _():
            conv_ref[s] = ext_scr[s, off + n_valid:off + n_valid + CONV_W - 1, :]

        ext_scr[s, 0:SUBLANES, :] = ext_scr[s, c:c + SUBLANES, :]
        cqs.append(_silu(cq))
        ba = ba_ref[0, s, 0]
        beta_all = _sigmoid(ba)
        g_all = -jnp.exp(hp[0:1]) * _softplus(ba + hp[1:2])
        if n_valid < c:
            valid = _iota((c, LANES), 0) < n_valid
            beta_all = jnp.where(valid, beta_all, 0.0)
            g_all = jnp.where(valid, g_all, 0.0)
        beta_alls.append(beta_all)
        g_alls.append(g_all)
        gcum_alls.append(_mdot(tri, g_all))

    qs = [cqs[s][:, h * GD_DK:(h + 1) * GD_DK] for s, h in items]
    ks = [cqs[s][:, kbase + h * GD_DK:kbase + (h + 1) * GD_DK] for s, h in items]
    vs = [cqs[s][:, vbase + h * GD_DV:vbase + (h + 1) * GD_DV] for s, h in items]
    qs = [q * lax.rsqrt(jnp.sum(q * q, axis=-1, keepdims=True) + NORM_EPS) * (GD_DK ** -0.5) for q in qs]
    ks = [k * lax.rsqrt(jnp.sum(k * k, axis=-1, keepdims=True) + NORM_EPS) for k in ks]
    betas = [beta_alls[s][:, h:h + 1] for s, h in items]
    g_cols = [g_alls[s][:, GD_HEADS + h:GD_HEADS + h + 1] for s, h in items]
    gcs = [gcum_alls[s][:, GD_HEADS + h:GD_HEADS + h + 1] for s, h in items]
    glasts = [gc[c - 1:c, :] for gc in gcs]
    egcs = [jnp.exp(gc) for gc in gcs]
    s_old = [s_scr[s, h] for s, h in items]
    decs = [jnp.where(incl, jnp.exp(_mdot(tri, g * strict_l)), 0.0) for g in g_cols]
    qk_kts = [_dot3(jnp.concatenate([q, k], axis=0), k, 'nt') for q, k in zip(qs, ks)]
    ms = [strict_l * b * qk[c:2 * c] * dec for b, qk, dec in zip(betas, qk_kts, decs)]
    rhss = [jnp.concatenate([v * b, k * (b * e)], axis=1) for v, k, b, e in zip(vs, ks, betas, egcs)]
    if c >= 2 * SUBLANES:
        xss = [_dot3(p, rhs) for p, rhs in zip(_unit_lower_inverses(ms, rid, cid, c), rhss)]
    else:
        strict_u = (rid < cid).astype(F32)
        dts = [_mdot((rid < cid).astype(BF16), g * (rid <= cid).astype(F32)) for g in g_cols]
        mts = [strict_u * _dot3(k, k * b, 'nt') * jnp.exp(jnp.minimum(dt, 0.0))
               for k, b, dt in zip(ks, betas, dts)]
        xss = rhss
        for i in range(1, c):
            row_i = _iota((c, GD_DV + GD_DK), 0) == i
            xss = [jnp.where(row_i, xs - jnp.sum(mt[:, i:i + 1] * xs, axis=0, keepdims=True), xs)
                   for xs, mt in zip(xss, mts)]
    ws_qss = [_dot3(jnp.concatenate([xs[:, GD_DV:GD_DV + GD_DK], q * e], axis=0), s)
              for xs, q, e, s in zip(xss, qs, egcs, s_old)]
    v_news = [xs[:, 0:GD_DV] - wq[0:c] for xs, wq in zip(xss, ws_qss)]
    os_ = [wq[c:2 * c] + _dot3(qk[0:c] * dec, vn) for wq, qk, dec, vn in zip(ws_qss, qk_kts, decs, v_news)]
    s_new = [s * jnp.exp(gl) + _dot3(k * jnp.exp(gl - gc), vn, 'tn')
             for s, gl, gc, k, vn in zip(s_old, glasts, gcs, ks, v_news)]
    os_ = [o * lax.rsqrt(jnp.mean(o * o, axis=-1, keepdims=True) + NORM_EPS) * ng for o in os_]
    for s in slots:
        z = z_ref[0, s, 0]
        o_ref[0, s, 0] = jnp.concatenate(
            [(os_[s * GD_HEADS + h] * _silu(z[:, h * GD_DV:(h + 1) * GD_DV])).astype(o_ref.dtype) for h in heads],
            axis=1)
    for i, (s, h) in enumerate(items):
        s_scr[s, h] = s_new[i]

    @pl.when(last)
    def _():
        for s in slots:
            s1_ref[s] = s_scr[s]


def _slot_plan(bsz, lpad):
    c = _pick(lpad, (CHUNK_ROWS, 32, 16, 8))
    nb = max(SLOTS_MIN, CHUNK_ROWS // c)
    if bsz % nb:
        nb = 1
    return c, lpad // c, nb


def _slot_view(a, bsz, nb, nc, c):
    return a.reshape(bsz // nb, nb, nc, c, a.shape[-1])


def _slot_spec(nb, c, w):
    return pl.BlockSpec((1, nb, 1, c, w), lambda b, j: (b, 0, j, 0, 0))


def _gdn_group(pqkv, pz, pba, lp, bsz, lpad, n_valid_len, conv0, s0):
    c, nc, nb = _slot_plan(bsz, lpad)
    n_valid = c if nc > 1 else n_valid_len
    tail0 = jnp.concatenate([jnp.zeros((bsz, SUBLANES - (CONV_W - 1), GD_QKV), F32), conv0], axis=1)
    hp = jnp.zeros((SUBLANES, LANES), F32)
    hp = hp.at[0, GD_HEADS:2 * GD_HEADS].set(lp['gdn_a_log']).at[1, GD_HEADS:2 * GD_HEADS].set(lp['gdn_dt_bias'])
    ng = lp['gdn_norm_g'].reshape(1, GD_DV)
    view = lambda a: _slot_view(a, bsz, nb, nc, c)
    full = lambda a: pl.BlockSpec(a.shape, lambda b, j: (0,) * a.ndim)
    sspec = pl.BlockSpec((nb, GD_HEADS, GD_DK, GD_DV), lambda b, j: (b, 0, 0, 0))
    o, s1, conv1 = pl.pallas_call(
        functools.partial(_gdn_kernel, c=c, nb=nb, n_valid=n_valid),
        out_shape=[jax.ShapeDtypeStruct((bsz // nb, nb, nc, c, GD_W), BF16),
                   jax.ShapeDtypeStruct((bsz, GD_HEADS, GD_DK, GD_DV), F32),
                   jax.ShapeDtypeStruct((bsz, CONV_W - 1, GD_QKV), F32)],
        grid=(bsz // nb, nc),
        in_specs=[_slot_spec(nb, c, GD_QKV), _slot_spec(nb, c, GD_W), _slot_spec(nb, c, LANES),
                  full(lp['gdn_conv']), full(hp), full(ng),
                  pl.BlockSpec((nb, SUBLANES, GD_QKV), lambda b, j: (b, 0, 0)), sspec],
        out_specs=[_slot_spec(nb, c, GD_W), sspec,
                   pl.BlockSpec((nb, CONV_W - 1, GD_QKV), lambda b, j: (b, 0, 0))],
        scratch_shapes=[pltpu.VMEM((nb, c + SUBLANES, GD_QKV), F32),
                        pltpu.VMEM((nb, GD_HEADS, GD_DK, GD_DV), F32)],
        compiler_params=_cparams(("parallel", "arbitrary")),
        name="gdn_chunk",
    )(view(pqkv), view(pz), view(pba), lp['gdn_conv'], hp, ng, tail0, s0)
    return o.reshape(bsz * lpad, GD_W), s1, conv1


def _hgrn_kernel(p_ref, lbv_ref, ng_ref, s0_ref, o_ref, s1_ref, st_scr, *, c, nb, n_valid):
    ci = pl.program_id(1)
    slots = range(nb)
    heads = range(HG_HEADS)
    items = [(s, h) for s in slots for h in heads]

    @pl.when(ci == 0)
    def _():
        for s, h in items:
            st_scr[s, h] = jnp.transpose(s0_ref[s, h])

    rid = _iota((c, c), 0)
    cid = _iota((c, c), 1)
    tri = (rid >= cid).astype(BF16)
    w = HG_HEADS * HG_DF
    rows = _iota((c, w), 0)
    sb = min(c, HG_SUB)
    row_in_sub = rows % sb
    lbv = lbv_ref[...]
    hs = lambda x, h: x[:, h * HG_DF:(h + 1) * HG_DF]

    qs, ks, vs, bcs = [], [], [], []
    for s in slots:
        p = p_ref[0, s, 0]
        pf = p[:, w:2 * w]
        a = lbv[0:1]
        b = lbv[1:2] - _softplus(-pf)
        logf = jnp.maximum(a, b) + jnp.log1p(jnp.exp(-jnp.abs(a - b)))
        k = lbv[2:3] * _sigmoid(-pf)
        if n_valid < c:
            logf = jnp.where(rows < n_valid, logf, 0.0)
            k = jnp.where(rows < n_valid, k, 0.0)
        qs.append(_silu(p[:, 0:w]))
        ks.append(k)
        vs.append(p[:, 2 * w:2 * w + HG_W])
        bcs.append(_mdot(tri, logf))
    st_old = [st_scr[s, h] for s, h in items]
    qes = [q * jnp.exp(bc) for q, bc in zip(qs, bcs)]
    os_ = [_dot3(hs(qes[s], h), st, 'nt') for (s, h), st in zip(items, st_old)]
    for delta in range(sb):
        prods, v_ss = [], []
        for s in slots:
            if delta == 0:
                k_s, b_s, v_s = ks[s], bcs[s], vs[s]
            else:
                k_s, b_s, v_s = (pltpu.roll(t, delta, 0) for t in (ks[s], bcs[s], vs[s]))
            prods.append(jnp.where(row_in_sub >= delta,
                                   qs[s] * k_s * jnp.exp(jnp.minimum(bcs[s] - b_s, 0.0)), 0.0))
            v_ss.append(v_s)
        os_ = [o + jnp.sum(hs(prods[s], h), axis=-1, keepdims=True) * hs(v_ss[s], h)
               for (s, h), o in zip(items, os_)]
    if c > sb:
        parts = [[jnp.zeros((sb, HG_DV), F32)] for _ in items]
        for r0 in range(sb, c, sb):
            qis, kps = [], []
            for s in slots:
                ref = bcs[s][r0 - 1:r0]
                qis.append(qs[s][r0:r0 + sb] * jnp.exp(bcs[s][r0:r0 + sb] - ref))
                kps.append(ks[s][0:r0] * jnp.exp(ref - bcs[s][0:r0]))
            att = [_dot3(hs(qis[s], h), hs(kps[s], h), 'nt') for s, h in items]
            for i, (s, h) in enumerate(items):
                parts[i].append(_dot3(att[i], hs(vs[s], h)[0:r0]))
        os_ = [o + jnp.concatenate(p, axis=0) for o, p in zip(os_, parts)]
    blasts = [bc[c - 1:c] for bc in bcs]
    kds = [k * jnp.exp(bl - bc) for k, bl, bc in zip(ks, blasts, bcs)]
    ebs = [jnp.exp(bl) for bl in blasts]
    st_new = [st * hs(ebs[s], h) + _dot3(hs(vs[s], h), hs(kds[s], h), 'tn') for (s, h), st in zip(items, st_old)]
    os_ = [o * lax.rsqrt(jnp.mean(o * o, axis=-1, keepdims=True) + NORM_EPS) for o in os_]
    ng = ng_ref[...]
    for s in slots:
        gate = _sigmoid(p_ref[0, s, 0][:, 2 * w + HG_W:2 * w + 2 * HG_W])
        o_all = jnp.concatenate(os_[s * HG_HEADS:(s + 1) * HG_HEADS], axis=1) * ng * gate
        o_ref[0, s, 0] = o_all.astype(o_ref.dtype)
    for i, (s, h) in enumerate(items):
        st_scr[s, h] = st_new[i]

    @pl.when(ci == pl.num_programs(1) - 1)
    def _():
        for s, h in items:
            s1_ref[s, h] = jnp.transpose(st_scr[s, h])


def _hgrn_group(phg, lb, lp, bsz, lpad, n_valid_len, s0):
    c, nc, nb = _slot_plan(bsz, lpad)
    n_valid = c if nc > 1 else n_valid_len
    lbv = jnp.zeros((SUBLANES, HG_HEADS * HG_DF), F32)
    lbv = lbv.at[0].set(jnp.log(jnp.maximum(lb, LB_FLOOR))).at[1].set(jnp.log1p(-lb)).at[2].set(1.0 - lb)
    ng = lp['hgrn_norm_g'].reshape(1, HG_W)
    full = lambda a: pl.BlockSpec(a.shape, lambda b, j: (0,) * a.ndim)
    sspec = pl.BlockSpec((nb, HG_HEADS, HG_DF, HG_DV), lambda b, j: (b, 0, 0, 0))
    o, s1 = pl.pallas_call(
        functools.partial(_hgrn_kernel, c=c, nb=nb, n_valid=n_valid),
        out_shape=[jax.ShapeDtypeStruct((bsz // nb, nb, nc, c, HG_W), BF16),
                   jax.ShapeDtypeStruct((bsz, HG_HEADS, HG_DF, HG_DV), F32)],
        grid=(bsz // nb, nc),
        in_specs=[_slot_spec(nb, c, HG_COLS), full(lbv), full(ng), sspec],
        out_specs=[_slot_spec(nb, c, HG_W), sspec],
        scratch_shapes=[pltpu.VMEM((nb, HG_HEADS, HG_DV, HG_DF), F32)],
        compiler_params=_cparams(("parallel", "arbitrary")),
        name="hgrn_chunk",
    )(_slot_view(phg, bsz, nb, nc, c), lbv, ng, s0)
    return o.reshape(bsz * lpad, HG_W), s1


def _ret_kernel(p_ref, cos_ref, sin_ref, qd_ref, kd_ref, dec_ref, dm_ref, gn_ref, s0_ref,
                o_ref, s1_ref, s_scr, *, c, nb):
    ci = pl.program_id(1)
    slots = range(nb)
    heads = range(RT_HEADS)
    items = [(s, h) for s in slots for h in heads]

    @pl.when(ci == 0)
    def _():
        for s in slots:
            s_scr[s] = jnp.zeros((RT_QK, RT_W), F32)
            for h in heads:
                s_scr[s, h * RT_DK:(h + 1) * RT_DK, h * RT_DV:(h + 1) * RT_DV] = s0_ref[s, h]

    cos = cos_ref[...]
    sin = sin_ref[...]
    lane = _iota((c, RT_QK), 1)
    first_half = (lane % RT_DK) < (RT_DK // 2)

    def rope(x):
        partner = jnp.where(first_half, pltpu.roll(x, RT_QK - RT_DK // 2, 1), pltpu.roll(x, RT_DK // 2, 1))
        return x * cos + partner * sin

    ps = [p_ref[0, s, 0] for s in slots]
    qs = [rope(p[:, 0:RT_QK]) for p in ps]
    ks = [rope(p[:, RT_QK:2 * RT_QK]) * (RT_DK ** -0.5) for p in ps]
    pvs = [p[:, 2 * RT_QK:2 * RT_QK + RT_W] for p in ps]
    sblks = [s_scr[s] for s in slots]
    qd = qd_ref[...]
    o_inters = [_dot3(q * qd, sblk) for q, sblk in zip(qs, sblks)]
    qks = [_dot3(jnp.concatenate([jnp.where(lane // RT_DK == h, q, 0.0) for h in heads], axis=0), k, 'nt')
           for q, k in zip(qs, ks)]
    attns = [qks[s][h * c:(h + 1) * c] * dec_ref[h] for s, h in items]
    os_ = [o_inters[s][:, h * RT_DV:(h + 1) * RT_DV] + _dot3(attn, pvs[s][:, h * RT_DV:(h + 1) * RT_DV])
           for (s, h), attn in zip(items, attns)]
    dm = dm_ref[...]
    kd = kd_ref[...]
    s_new = [sblk * dm + jnp.where(dm > 0.0, _dot3(k * kd, pv, 'tn'), 0.0) for sblk, k, pv in zip(sblks, ks, pvs)]
    gn = gn_ref[...]
    normed = []
    for o in os_:
        xc = o - jnp.mean(o, axis=-1, keepdims=True)
        normed.append(xc * lax.rsqrt(jnp.mean(xc * xc, axis=-1, keepdims=True) + NORM_EPS))
    for s in slots:
        gate = _silu(ps[s][:, 2 * RT_QK + RT_W:2 * RT_QK + 2 * RT_W])
        o_all = jnp.concatenate(normed[s * RT_HEADS:(s + 1) * RT_HEADS], axis=1) * gn * gate
        o_ref[0, s, 0] = o_all.astype(o_ref.dtype)
        s_scr[s] = s_new[s]

    @pl.when(ci == pl.num_programs(1) - 1)
    def _():
        for s, h in items:
            s1_ref[s, h] = s_scr[s, h * RT_DK:(h + 1) * RT_DK, h * RT_DV:(h + 1) * RT_DV]


def _ret_group(prt, lp, bsz, lpad, n_valid_len, pos0, s0):
    c, nc, nb = _slot_plan(bsz, lpad)
    n_valid = c if nc > 1 else n_valid_len
    half = RT_DK // 2
    inv = ROPE_BASE ** (-np.arange(half, dtype=np.float64) / half)
    ang = (pos0 + np.arange(lpad, dtype=np.float64))[:, None] * inv[None, :]
    cos = np.tile(np.cos(ang), (1, 2 * RT_HEADS))
    sin = np.tile(np.concatenate([-np.sin(ang), np.sin(ang)], axis=1), (1, RT_HEADS))
    loggamma = np.log(1.0 - np.exp2(-5.0 - np.arange(RT_HEADS, dtype=np.float64)))
    gcum = loggamma[:, None] * np.arange(1, c + 1, dtype=np.float64)[None, :]
    idx = np.arange(c)
    dec = np.where(idx[:, None] >= idx[None, :], np.exp(gcum[:, :, None] - gcum[:, None, :]), 0.0)
    qd = np.repeat(np.exp(gcum).T, RT_DK, axis=1)
    kdec = np.where(idx[None, :] < n_valid, np.exp(gcum[:, n_valid - 1:n_valid] - gcum), 0.0)
    kd = np.repeat(kdec.T, RT_DK, axis=1)
    sdec = np.exp(gcum[:, n_valid - 1])
    dm = np.zeros((RT_QK, RT_W))
    for h in range(RT_HEADS):
        dm[h * RT_DK:(h + 1) * RT_DK, h * RT_DV:(h + 1) * RT_DV] = sdec[h]
    cos, sin, qd, kd, dec, dm = (jnp.asarray(a, F32) for a in (cos, sin, qd, kd, dec, dm))
    gn = lp['ret_gn_g'].reshape(1, RT_W)
    full = lambda a: pl.BlockSpec(a.shape, lambda b, j: (0,) * a.ndim)
    posspec = pl.BlockSpec((c, RT_QK), lambda b, j: (j, 0))
    sspec = pl.BlockSpec((nb, RT_HEADS, RT_DK, RT_DV), lambda b, j: (b, 0, 0, 0))
    o, s1 = pl.pallas_call(
        functools.partial(_ret_kernel, c=c, nb=nb),
        out_shape=[jax.ShapeDtypeStruct((bsz // nb, nb, nc, c, RT_W), BF16),
                   jax.ShapeDtypeStruct((bsz, RT_HEADS, RT_DK, RT_DV), F32)],
        grid=(bsz // nb, nc),
        in_specs=[_slot_spec(nb, c, RT_COLS), posspec, posspec,
                  full(qd), full(kd), full(dec), full(dm), full(gn), sspec],
        out_specs=[_slot_spec(nb, c, RT_W), sspec],
        scratch_shapes=[pltpu.VMEM((nb, RT_QK, RT_W), F32)],
        compiler_params=_cparams(("parallel", "arbitrary")),
        name="ret_chunk",
    )(_slot_view(prt, bsz, nb, nc, c), cos, sin, qd, kd, dec, dm, gn, s0)
    return o.reshape(bsz * lpad, RT_W), s1


def _merge_kernel(x_ref, o1, o2, o3, o4, gate_ref, w1, w2, w3, w4, wo_ref, out_ref, *, d):
    acc = None
    for i, (o, w) in enumerate(((o1, w1), (o2, w2), (o3, w3), (o4, w4))):
        term = _sigmoid(gate_ref[:, i * d:(i + 1) * d]) * _dot(o[...], w[...])
        acc = term if acc is None else acc + term
    out_ref[...] = x_ref[...] + _dot(acc.astype(BF16), wo_ref[...])


def _merge(x, outs, gate, wouts, wo):
    n, d = x.shape
    tm = _pick(n, (256, 128, 64, 32, 16))
    row = lambda w: pl.BlockSpec((tm, w), lambda i: (i, 0))
    full = lambda a: pl.BlockSpec(a.shape, lambda i: (0,) * a.ndim)
    return pl.pallas_call(
        functools.partial(_merge_kernel, d=d),
        out_shape=jax.ShapeDtypeStruct((n, d), F32),
        grid=(n // tm,),
        in_specs=[row(d)] + [row(o.shape[1]) for o in outs] + [row(N_BRANCH * d)]
        + [full(w) for w in wouts] + [full(wo)],
        out_specs=row(d),
        compiler_params=_cparams(("parallel",)),
        name="merge",
    )(x, *outs, gate, *wouts, wo)


def _route_kernel(x_ref, g_ref, rg_ref, re_ref, rb_ref, xn_ref, col_ref, row_ref, cnt_ref):
    tm = x_ref.shape[0]
    lane = _iota((tm, LANES), 1)
    lanef = lane.astype(F32)
    x = x_ref[...]
    xn = x * lax.rsqrt(jnp.mean(x * x, axis=-1, keepdims=True) + NORM_EPS) * g_ref[...]
    xn_ref[...] = xn.astype(BF16)
    rb = rb_ref[...]
    neg = jnp.float32(-jnp.inf)
    glog = jnp.where(lane < N_GROUPS, _dot(xn, rg_ref[...], HI) + rb[0:1], neg)
    gmax = jnp.max(glog, axis=-1, keepdims=True)
    gsum = jnp.sum(jnp.exp(glog - gmax), axis=-1, keepdims=True)
    gidx = jnp.min(jnp.where(glog == gmax, lanef, float(LANES)), axis=-1, keepdims=True)
    gp = 1.0 / gsum
    in_group = (lanef >= gidx * EXPERTS_PER_GROUP) & (lanef < (gidx + 1.0) * EXPERTS_PER_GROUP)
    elog = jnp.where(in_group, _dot(xn, re_ref[...], HI) + rb[1:2], neg)
    emax = jnp.max(elog, axis=-1, keepdims=True)
    eexp = jnp.exp(elog - emax)
    ep = eexp / jnp.sum(eexp, axis=-1, keepdims=True)
    ep = jnp.where(in_group, ep, -1.0)
    p1 = jnp.max(ep, axis=-1, keepdims=True)
    i1 = jnp.min(jnp.where(ep == p1, lanef, float(LANES)), axis=-1, keepdims=True)
    ep2 = jnp.where(lanef == i1, -1.0, ep)
    p2 = jnp.max(ep2, axis=-1, keepdims=True)
    i2 = jnp.min(jnp.where(ep2 == p2, lanef, float(LANES)), axis=-1, keepdims=True)
    denom = p1 + p2
    wt1 = gp * p1 / denom
    wt2 = gp * p2 / denom

    onehot = ((lanef == i1) | (lanef == i2)).astype(F32)
    cnt = jnp.sum(onehot, axis=0, keepdims=True)
    before = (_iota((LANES, LANES), 0) < _iota((LANES, LANES), 1)).astype(BF16)
    off = _segsum(jnp.broadcast_to(cnt, (SUBLANES, LANES)), before)[0:1]
    tri = (_iota((tm, tm), 0) >= _iota((tm, tm), 1)).astype(BF16)
    slot = off + _dot(tri, onehot.astype(BF16)) - 1.0
    pos1 = jnp.sum(jnp.where(lanef == i1, slot, 0.0), axis=-1, keepdims=True)
    pos2 = jnp.sum(jnp.where(lanef == i2, slot, 0.0), axis=-1, keepdims=True)
    col = jnp.where(lane == 0, pos1, jnp.where(lane == 1, pos2, jnp.where(lane == 2, wt1,
                                                                         jnp.where(lane == 3, wt2, 0.0))))
    col_ref[...] = col
    row_ref[0] = jnp.transpose(col)[0:SUBLANES]
    cnt_ref[0] = jnp.concatenate([cnt, off, jnp.zeros((SUBLANES - 2, LANES), F32)], axis=0)


def _moe_kernel(cnt_sm, off_sm, x_ref, xn_ref, col_ref, row_ref, wg_ref, wu_ref, wd_ref, out_ref,
                xs_scr, ys_scr, ws_scr, *, pb, rb):
    i = pl.program_id(0)
    e = pl.program_id(1)
    tm, d = x_ref.shape
    ns = 2 * tm

    @pl.when(e == 0)
    def _():
        rowd = row_ref[0]
        pos1, pos2, wt1, wt2 = rowd[0:1], rowd[1:2], rowd[2:3], rowd[3:4]
        for blk in range(ns // pb):
            sid = (_iota((pb, tm), 0) + blk * pb).astype(F32)
            m1 = sid == pos1
            m2 = sid == pos2
            xs_scr[blk * pb:(blk + 1) * pb, :] = _dot((m1 | m2).astype(BF16), xn_ref[...]).astype(BF16)
            wsl = jnp.sum(jnp.where(m1, wt1, 0.0) + jnp.where(m2, wt2, 0.0), axis=-1, keepdims=True)
            ws_scr[blk * pb:(blk + 1) * pb, :] = jnp.broadcast_to(wsl, (pb, LANES))
        ys_scr[...] = jnp.zeros_like(ys_scr)

    cnt = cnt_sm[i, e]
    off = off_sm[i, e]

    @pl.when(cnt > 0)
    def _():
        start = (off // BF16_ROWS) * BF16_ROWS

        def body(j, carry):
            own = start + j * rb
            r0 = pl.multiple_of(jnp.minimum(own, ns - rb), BF16_ROWS)
            xb = xs_scr[pl.ds(r0, rb), :]
            hid = _silu(_dot(xb, wg_ref[0])) * _dot(xb, wu_ref[0])
            y = _dot(hid.astype(BF16), wd_ref[0])
            srow = r0 + _iota((rb, d), 0)
            mine = (srow >= jnp.maximum(off, own)) & (srow < off + cnt)
            ys_scr[pl.ds(r0, rb), :] += jnp.where(mine, y, 0.0)
            return carry

        lax.fori_loop(0, (off + cnt - start + rb - 1) // rb, body, 0)

    @pl.when(e == pl.num_programs(1) - 1)
    def _():
        col = col_ref[...]
        pos1, pos2 = col[:, 0:1], col[:, 1:2]
        acc = x_ref[...]
        for blk in range(ns // pb):
            sid = (_iota((tm, pb), 1) + blk * pb).astype(F32)
            pt = ((sid == pos1) | (sid == pos2)).astype(BF16)
            ysw = ys_scr[blk * pb:(blk + 1) * pb, :] * ws_scr[blk * pb:(blk + 1) * pb, 0:1]
            hi, lo = _split2(ysw)
            acc = acc + _dot(pt, hi) + _dot(pt, lo)
        out_ref[...] = acc


def _moe(x, lp, experts):
    n, d = x.shape
    tm = _pick(n, (MOE_TILE, 512, 256, 128, 64))
    nt = n // tm
    de = experts[0].shape[-1]
    rg = jnp.zeros((d, LANES), F32).at[:, :N_GROUPS].set(lp['router_group'])
    re = jnp.zeros((d, LANES), F32).at[:, :N_EXPERTS].set(lp['router_expert'])
    rbias = jnp.zeros((SUBLANES, LANES), F32)
    rbias = rbias.at[0, :N_GROUPS].set(lp['router_group_b']).at[1, :N_EXPERTS].set(lp['router_expert_b'])
    g = lp['norm2_g'].reshape(1, d)
    full1 = lambda a: pl.BlockSpec(a.shape, lambda i: (0,) * a.ndim)
    xn, col, row, cnt = pl.pallas_call(
        _route_kernel,
        out_shape=[jax.ShapeDtypeStruct((n, d), BF16), jax.ShapeDtypeStruct((n, LANES), F32),
                   jax.ShapeDtypeStruct((nt, SUBLANES, tm), F32), jax.ShapeDtypeStruct((nt, SUBLANES, LANES), F32)],
        grid=(nt,),
        in_specs=[pl.BlockSpec((tm, d), lambda i: (i, 0)), full1(g), full1(rg), full1(re), full1(rbias)],
        out_specs=[pl.BlockSpec((tm, d), lambda i: (i, 0)), pl.BlockSpec((tm, LANES), lambda i: (i, 0)),
                   pl.BlockSpec((1, SUBLANES, tm), lambda i: (i, 0, 0)),
                   pl.BlockSpec((1, SUBLANES, LANES), lambda i: (i, 0, 0))],
        compiler_params=_cparams(("parallel",)),
        name="moe_route",
    )(x, g, rg, re, rbias)
    cnt_i = cnt[:, 0, :N_EXPERTS].astype(jnp.int32)
    off_i = cnt[:, 1, :N_EXPERTS].astype(jnp.int32)
    ns = 2 * tm
    pb = min(MOE_SLOT_BLOCK, ns)
    rb = min(MOE_ROW_BLOCK, ns)
    grid_spec = pltpu.PrefetchScalarGridSpec(
        num_scalar_prefetch=2,
        grid=(nt, N_EXPERTS),
        in_specs=[pl.BlockSpec((tm, d), lambda i, e, c, o: (i, 0)),
                  pl.BlockSpec((tm, d), lambda i, e, c, o: (i, 0)),
                  pl.BlockSpec((tm, LANES), lambda i, e, c, o: (i, 0)),
                  pl.BlockSpec((1, SUBLANES, tm), lambda i, e, c, o: (i, 0, 0)),
                  pl.BlockSpec((1, d, de), lambda i, e, c, o: (e, 0, 0)),
                  pl.BlockSpec((1, d, de), lambda i, e, c, o: (e, 0, 0)),
                  pl.BlockSpec((1, de, d), lambda i, e, c, o: (e, 0, 0))],
        out_specs=pl.BlockSpec((tm, d), lambda i, e, c, o: (i, 0)),
        scratch_shapes=[pltpu.VMEM((ns, d), BF16), pltpu.VMEM((ns, d), F32), pltpu.VMEM((ns, LANES), F32)],
    )
    return pl.pallas_call(
        functools.partial(_moe_kernel, pb=pb, rb=rb),
        out_shape=jax.ShapeDtypeStruct((n, d), F32),
        grid_spec=grid_spec,
        compiler_params=pltpu.CompilerParams(dimension_semantics=("parallel", "arbitrary"),
                                             vmem_limit_bytes=MOE_VMEM_LIMIT),
        name="moe",
    )(cnt_i, off_i, x, xn, col, row, *experts)


def _layer(x, wts, lp, lb, gr):
    b, seq, lpad = gr['bsz'], gr['seq'], gr['lpad']
    shift0, wkv0, conv0, gdn0, hgrn0, ret0 = gr['states']
    xn = _rmsnorm(x, lp['norm1_g'], BF16)
    p_rw, p_gqkv, p_gz, p_gba, p_hg, p_rt, p_gate = (
        _matmul(xn, w, f"proj{i}") for i, w in enumerate(wts['proj']))
    o_rw, shift1, wkv1 = _wkv_group(p_rw, lp, b, lpad, seq, 0, shift0, wkv0)
    o_gd, gdn1, conv1 = _gdn_group(p_gqkv, p_gz, p_gba, lp, b, lpad, seq, conv0, gdn0)
    o_hg, hgrn1 = _hgrn_group(p_hg, lb, lp, b, lpad, seq, hgrn0)
    o_rt, ret1 = _ret_group(p_rt, lp, b, lpad, seq, gr['pos0'], ret0)
    x = _merge(x, [o_rw, o_gd, o_hg, o_rt], p_gate, wts['out'], wts['w_o'])
    x = _moe(x, lp, wts['experts'])
    return x, (shift1, wkv1, conv1, gdn1, hgrn1, ret1)


def _layer_weights(lp, d):
    offs = np.cumsum([0, RW_COLS, GD_QKV, GD_W, 2 * GD_HEADS, HG_COLS, RT_COLS, N_BRANCH * d])
    seg = [lp['w_in'][:, offs[i]:offs[i + 1]] for i in range(7)]
    seg[3] = jnp.pad(seg[3], ((0, 0), (0, LANES - 2 * GD_HEADS)))
    return dict(
        proj=[s.astype(BF16) for s in seg],
        out=[lp[n].astype(BF16) for n in ('w_out_rwkv', 'w_out_gdn', 'w_out_hgrn', 'w_out_ret')],
        w_o=lp['w_o'].astype(BF16),
        experts=[lp[n].astype(BF16) for n in ('moe_w_gate', 'moe_w_up', 'moe_w_down')])


def kernel(x_prompt, x_sample, state_rwkv_shift, state_rwkv_wkv, state_gdn_conv, state_gdn, state_hgrn, state_ret, norm1_g, w_in, rwkv_mu, rwkv_w0, rwkv_w2, rwkv_a0, rwkv_a2, rwkv_g2, rwkv_k_k, rwkv_k_a, rwkv_r_k, rwkv_ln_g, rwkv_ln_b, w_out_rwkv, gdn_conv, gdn_a_log, gdn_dt_bias, gdn_norm_g, w_out_gdn, hgrn_lb_logits, hgrn_norm_g, w_out_hgrn, ret_gn_g, w_out_ret, w_o, norm2_g, router_group, router_group_b, router_expert, router_expert_b, moe_w_gate, moe_w_up, moe_w_down, final_norm_g):
    params = dict(norm1_g=norm1_g, w_in=w_in, rwkv_mu=rwkv_mu, rwkv_w0=rwkv_w0, rwkv_w2=rwkv_w2,
                  rwkv_a0=rwkv_a0, rwkv_a2=rwkv_a2, rwkv_g2=rwkv_g2, rwkv_k_k=rwkv_k_k,
                  rwkv_k_a=rwkv_k_a, rwkv_r_k=rwkv_r_k, rwkv_ln_g=rwkv_ln_g, rwkv_ln_b=rwkv_ln_b,
                  w_out_rwkv=w_out_rwkv, gdn_conv=gdn_conv, gdn_a_log=gdn_a_log,
                  gdn_dt_bias=gdn_dt_bias, gdn_norm_g=gdn_norm_g, w_out_gdn=w_out_gdn,
                  hgrn_norm_g=hgrn_norm_g, w_out_hgrn=w_out_hgrn, ret_gn_g=ret_gn_g,
                  w_out_ret=w_out_ret, w_o=w_o, norm2_g=norm2_g, router_group=router_group,
                  router_group_b=router_group_b, router_expert=router_expert,
                  router_expert_b=router_expert_b, moe_w_gate=moe_w_gate, moe_w_up=moe_w_up,
                  moe_w_down=moe_w_down)
    depth = w_in.shape[0]
    bp, lp_len, d = x_prompt.shape
    bs, ls, _ = x_sample.shape
    ls_pad = -(-ls // SAMPLE_PAD_LEN) * SAMPLE_PAD_LEN
    sm = jax.nn.softmax(hgrn_lb_logits.astype(F32), axis=0)
    lower_bounds = jnp.cumsum(sm, axis=0) - sm[0]

    sample_states = (state_rwkv_shift, state_rwkv_wkv, state_gdn_conv, state_gdn, state_hgrn, state_ret)
    xp = x_prompt.reshape(bp * lp_len, d)
    xs = jnp.pad(x_sample, ((0, 0), (0, ls_pad - ls), (0, 0))).reshape(bs * ls_pad, d)

    prompt_out = [[] for _ in sample_states]
    sample_out = [[] for _ in sample_states]
    for layer in range(depth):
        lpar = {name: arr[layer] for name, arr in params.items()}
        wts = _layer_weights(lpar, d)
        prompt = dict(bsz=bp, seq=lp_len, lpad=lp_len, pos0=0,
                      states=tuple(jnp.zeros((bp,) + s.shape[2:], F32) for s in sample_states))
        sample = dict(bsz=bs, seq=ls, lpad=ls_pad, pos0=PAST_LEN,
                      states=tuple(s[layer].astype(F32) for s in sample_states))
        xp, new_p = _layer(xp, wts, lpar, lower_bounds[layer], prompt)
        xs, new_s = _layer(xs, wts, lpar, lower_bounds[layer], sample)
        for lst, n in zip(prompt_out, new_p):
            lst.append(n)
        for lst, n in zip(sample_out, new_s):
            lst.append(n)

    y_prompt = _rmsnorm(xp, final_norm_g, F32).reshape(bp, lp_len, d)
    y_sample = _rmsnorm(xs, final_norm_g, F32).reshape(bs, ls_pad, d)[:, :ls]
    p_states = [jnp.stack(lst) for lst in prompt_out]
    s_states = [jnp.stack(lst).astype(o.dtype) for lst, o in zip(sample_out, sample_states)]
    return (y_prompt, y_sample, *p_states, *s_states)
```

```python
import functools
import math

import numpy as np
import jax
import jax.numpy as jnp
from jax import lax
from jax.experimental import pallas as pl
from jax.experimental.pallas import tpu as pltpu

F32 = jnp.float32
BF16 = jnp.bfloat16
HI = lax.Precision.HIGHEST

NORM_EPS = 1e-6
LB_FLOOR = 1e-30
PAST_LEN = 16384
RW_HEADS = 8
RW_HD = 64
RW_W = RW_HEADS * RW_HD
RW_DECAY_LORA = 64
RW_AAA_LORA = 64
RW_GATE_LORA = 128
RW_COLS = 3 * RW_W + RW_DECAY_LORA + RW_AAA_LORA + RW_GATE_LORA
RW_GN_EPS = 64e-5
GD_HEADS = 4
GD_DK = 128
GD_DV = 128
GD_QKV = GD_HEADS * (2 * GD_DK + GD_DV)
GD_W = GD_HEADS * GD_DV
CONV_W = 4
HG_HEADS = 4
HG_DF = 128
HG_DV = 128
HG_W = HG_HEADS * HG_DV
HG_COLS = 2 * HG_HEADS * HG_DF + 2 * HG_W
HG_SUB = 16
RT_HEADS = 4
RT_DK = 64
RT_DV = 128
RT_W = RT_HEADS * RT_DV
RT_QK = RT_HEADS * RT_DK
RT_COLS = 2 * RT_QK + 2 * RT_W
ROPE_BASE = 10000.0
N_BRANCH = 4
N_GROUPS = 4
EXPERTS_PER_GROUP = 8
N_EXPERTS = N_GROUPS * EXPERTS_PER_GROUP

LANES = 128
SUBLANES = 8
SAMPLE_PAD_LEN = 8
CHUNK_ROWS = 64
SLOTS_MIN = 2
VMEM_LIMIT = 48 * 1024 * 1024
MOE_TILE = 1024
MOE_SLOT_BLOCK = 256
MOE_ROW_BLOCK = 128
BF16_ROWS = 16
MOE_VMEM_LIMIT = 56 * 1024 * 1024


def _pick(n, cands):
    for c in cands:
        if n % c == 0:
            return c
    raise ValueError(f"no tile for {n} in {cands}")


def _cparams(sem):
    return pltpu.CompilerParams(dimension_semantics=sem, vmem_limit_bytes=VMEM_LIMIT)


def _dot(a, b, prec=None):
    return lax.dot_general(a, b, (((1,), (0,)), ((), ())), precision=prec, preferred_element_type=F32)


def _softplus(x):
    return jnp.maximum(x, 0.0) + jnp.log1p(jnp.exp(-jnp.abs(x)))


def _sigmoid(x):
    return jax.nn.sigmoid(x)


def _silu(x):
    return x * jax.nn.sigmoid(x)


def _segsum(x, hm):
    hi = x.astype(BF16)
    r1 = x - hi.astype(F32)
    mid = r1.astype(BF16)
    lo = (r1 - mid.astype(F32)).astype(BF16)
    return _dot(hi, hm) + _dot(mid, hm) + _dot(lo, hm)


_DIMS = {'nn': (((1,), (0,)), ((), ())), 'nt': (((1,), (1,)), ((), ())), 'tn': (((0,), (0,)), ((), ()))}


def _split2(x):
    hi = x.astype(BF16)
    return hi, (x - hi.astype(F32)).astype(BF16)


def _dot3(a, b, form='nn'):
    ah, al = _split2(a)
    bh, bl = _split2(b)
    f = lambda x, y: lax.dot_general(x, y, _DIMS[form], preferred_element_type=F32)
    free = 1 if form == 'tn' else 0
    m = a.shape[free]
    both = f(jnp.concatenate([ah, al], axis=free), bh)
    return both[0:m] + both[m:2 * m] + f(ah, bl)


def _mdot(mask, x):
    hi = x.astype(BF16)
    r1 = x - hi.astype(F32)
    mid = r1.astype(BF16)
    lo = (r1 - mid.astype(F32)).astype(BF16)
    return _dot(mask, hi) + _dot(mask, mid) + _dot(mask, lo)


def _unit_lower_inverses(ms, rid, cid, c, expand=None):
    prod = _dot3 if expand is None else (lambda x, y: _dot3(x, expand(y)))
    same = lambda s: (rid // s) == (cid // s)
    ns = [jnp.where(same(SUBLANES), -m, 0.0) for m in ms]
    n2s = [prod(n, n) for n in ns]
    n4s = [prod(n2, n2) for n2 in n2s]
    eye = (rid == cid).astype(F32)
    ps = [eye + n for n in ns]
    ps = [p + prod(p, n2) for p, n2 in zip(ps, n2s)]
    ps = [p + prod(p, n4) for p, n4 in zip(ps, n4s)]
    s = SUBLANES
    while s < c:
        offs = [jnp.where(same(2 * s) & jnp.logical_not(same(s)), m, 0.0) for m in ms]
        ts = [prod(p, off) for p, off in zip(ps, offs)]
        ps = [p - prod(t, p) for p, t in zip(ps, ts)]
        s *= 2
    return ps


def _iota(shape, dim):
    return lax.broadcasted_iota(jnp.int32, shape, dim)


def _rms_kernel(x_ref, g_ref, o_ref):
    x = x_ref[...]
    ms = jnp.mean(x * x, axis=-1, keepdims=True)
    o_ref[...] = (x * lax.rsqrt(ms + NORM_EPS) * g_ref[...]).astype(o_ref.dtype)


def _rmsnorm(x, g, out_dtype):
    n, d = x.shape
    tm = _pick(n, (1024, 512, 256, 128, 64, 32, 16))
    return pl.pallas_call(
        _rms_kernel,
        out_shape=jax.ShapeDtypeStruct((n, d), out_dtype),
        grid=(n // tm,),
        in_specs=[pl.BlockSpec((tm, d), lambda i: (i, 0)), pl.BlockSpec((1, d), lambda i: (0, 0))],
        out_specs=pl.BlockSpec((tm, d), lambda i: (i, 0)),
        compiler_params=_cparams(("parallel",)),
        name="rmsnorm",
    )(x, g.reshape(1, d))


def _mm_kernel(x_ref, w_ref, o_ref):
    o_ref[...] = _dot(x_ref[...], w_ref[...])


def _matmul(x, w, name):
    n, k = x.shape
    m = w.shape[1]
    tm = _pick(n, (2048, 1024, 512, 256, 128, 64, 32, 16))
    tn = _pick(m, (1024, 896, 768, 512, 256, 128))
    return pl.pallas_call(
        _mm_kernel,
        out_shape=jax.ShapeDtypeStruct((n, m), F32),
        grid=(n // tm, m // tn),
        in_specs=[pl.BlockSpec((tm, k), lambda i, j: (i, 0)), pl.BlockSpec((k, tn), lambda i, j: (0, j))],
        out_specs=pl.BlockSpec((tm, tn), lambda i, j: (i, j)),
        compiler_params=_cparams(("parallel", "parallel")),
        name=name,
    )(x, w)


def _head_sum_matrix(width, seg):
    i = np.arange(width)
    return jnp.asarray((i[:, None] // seg) == (i[None, :] // seg), BF16)


def _wkv_chunk_kernel(p_ref, first_ref, mu_ref, vec_ref, w2_ref, a2_ref, g2_ref, hm_ref, ln_ref, s0_ref,
                      o_ref, s1_ref, shift_ref, carry_scr, ht_scr, *, cs, nseq, ns, n_valid, multi_chunk):
    ci = pl.program_id(1)
    rows = cs * nseq
    npair = RW_HEADS // 2
    f_zero = jnp.zeros((RW_HD, RW_HD), F32)
    slots = range(ns)
    pairs = range(npair)
    seqs = [(s, q) for s in slots for q in range(nseq)]
    sidx = lambda s, q: s * nseq + q

    @pl.when(ci == 0)
    def _():
        for s, q in seqs:
            carry_scr[sidx(s, q)] = first_ref[sidx(s, q)]
            for pr in pairs:
                top = jnp.concatenate([s0_ref[sidx(s, q), 2 * pr], f_zero], axis=1)
                bot = jnp.concatenate([f_zero, s0_ref[sidx(s, q), 2 * pr + 1]], axis=1)
                ht_scr[sidx(s, q), pr] = jnp.concatenate([top, bot], axis=0)

    vec = vec_ref[...]
    w0, a0, k_k, k_a, r_k = vec[0:1], vec[1:2], vec[2:3], vec[3:4], vec[4:5]
    hm = hm_ref[...]
    mu = mu_ref[...]
    row_w = _iota((rows, RW_COLS), 0)
    row_f = _iota((rows, RW_W), 0)
    rid = _iota((rows, rows), 0)
    cid = _iota((rows, rows), 1)
    tri = ((rid >= cid) & (rid // cs == cid // cs)).astype(BF16)

    ps_, vs_, gs_, bonus_ = [], [], [], []
    abar, bbar, kbar, rbar, btil, ktil, w_last = [], [], [], [], [], [], []
    for s in slots:
        p = p_ref[0, s, 0]
        prev = pltpu.roll(p, 1, 0)
        for q in range(nseq):
            prev = jnp.where(row_w == q * cs, carry_scr[sidx(s, q)], prev)
        h = p + (prev - p) * mu
        r = h[:, 0:RW_W]
        k = h[:, RW_W:2 * RW_W]
        v = h[:, 2 * RW_W:3 * RW_W]
        lo = h[:, 3 * RW_W:3 * RW_W + LANES]
        gl = h[:, 3 * RW_W + LANES:3 * RW_W + 2 * LANES]
        w_log = -_softplus(-(w0 + _dot3(jnp.tanh(lo), w2_ref[...]))) - 0.5
        logw = -jnp.exp(w_log)
        a = _sigmoid(a0 + _dot3(lo, a2_ref[...]))
        gs_.append(_dot3(_sigmoid(gl), g2_ref[...]))
        kk = k * k_k
        kk = kk * lax.rsqrt(_segsum(kk * kk, hm) + NORM_EPS)
        km = k * (1.0 + (a - 1.0) * k_a)
        bb = kk * a
        alpha = -kk
        bonus_.append(_segsum(r * km * r_k, hm) * v)
        if n_valid < cs:
            valid = (row_f % cs) < n_valid
            logw, alpha, bb, km = (jnp.where(valid, t, 0.0) for t in (logw, alpha, bb, km))
        gcum = _mdot(tri, logw)
        g_last = jnp.concatenate(
            [jnp.broadcast_to(gcum[q * cs + cs - 1:q * cs + cs], (cs, RW_W)) for q in range(nseq)], axis=0)
        emg = jnp.exp(-gcum)
        etil = jnp.exp(g_last - gcum)
        ps_.append(p)
        vs_.append(v)
        abar.append(alpha * jnp.exp(gcum - logw))
        bbar.append(bb * emg)
        kbar.append(km * emg)
        rbar.append(r * jnp.exp(gcum))
        btil.append(bb * etil)
        ktil.append(km * etil)
        w_last.append(jnp.exp(g_last))

    prow = _iota((rows, LANES), 0)
    plane = _iota((rows, LANES), 1)
    pcol = plane % RW_HD
    same_seq = (prow // cs) == (pcol // cs)
    strict = (pcol < prow) & same_seq
    incl = (pcol <= prow) & same_seq
    first_head = plane < RW_HD
    blk_mask = (_iota((LANES, LANES), 0) // RW_HD) == (_iota((LANES, LANES), 1) // RW_HD)

    def expand(x):
        return jnp.concatenate([jnp.where(first_head, x, 0.0), jnp.where(first_head, 0.0, x)], axis=0)

    items = [(s, pr) for s in slots for pr in pairs]
    ps = lambda x, pr: x[:, pr * LANES:(pr + 1) * LANES]
    seq_rows = lambda x, q: x[q * cs:(q + 1) * cs]

    lhs = [jnp.concatenate([ps(abar[s], pr), ps(rbar[s], pr)], axis=0) for s, pr in items]
    xb = [_dot3(l, expand(ps(bbar[s], pr)), 'nt') for l, (s, pr) in zip(lhs, items)]
    xk = [_dot3(l, expand(ps(kbar[s], pr)), 'nt') for l, (s, pr) in zip(lhs, items)]
    a_m = [jnp.where(strict, x[0:rows], 0.0) for x in xb]
    rb_m = [jnp.where(incl, x[rows:2 * rows], 0.0) for x in xb]
    b_m = [jnp.where(strict, x[0:rows], 0.0) for x in xk]
    rk_m = [jnp.where(incl, x[rows:2 * rows], 0.0) for x in xk]
    pinv = _unit_lower_inverses([-m for m in a_m], prow, pcol, cs, expand=expand)

    ht_old = {(s, q, pr): ht_scr[sidx(s, q), pr] for s, q in seqs for pr in pairs}
    xh = {(s, q, pr): _dot3(jnp.concatenate([seq_rows(ps(abar[s], pr), q), seq_rows(ps(rbar[s], pr), q)], axis=0),
                            ht_old[(s, q, pr)], 'nt') for s, q in seqs for pr in pairs}
    ah = [jnp.concatenate([xh[(s, q, pr)][0:cs] for q in range(nseq)], axis=0) for s, pr in items]
    rh = [jnp.concatenate([xh[(s, q, pr)][cs:2 * cs] for q in range(nseq)], axis=0) for s, pr in items]
    vexp = [expand(ps(vs_[s], pr)) for s, pr in items]
    rhs = [a_ + _dot3(b_, ve) for a_, b_, ve in zip(ah, b_m, vexp)]
    u = [_dot3(pi, expand(rh_)) for pi, rh_ in zip(pinv, rhs)]
    o_items = [rh_ + _dot3(jnp.concatenate([rb_, rk_], axis=1), jnp.concatenate([expand(u_), ve], axis=0))
               for rh_, rb_, rk_, u_, ve in zip(rh, rb_m, rk_m, u, vexp)]
    for i, (s, pr) in enumerate(items):
        for q in range(nseq):
            uv = jnp.concatenate([seq_rows(u[i], q), seq_rows(ps(vs_[s], pr), q)], axis=0)
            bk = jnp.concatenate([seq_rows(ps(btil[s], pr), q), seq_rows(ps(ktil[s], pr), q)], axis=0)
            upd = jnp.where(blk_mask, _dot3(uv, bk, 'tn'), 0.0)
            ht_scr[sidx(s, q), pr] = ht_old[(s, q, pr)] * seq_rows(ps(w_last[s], pr), q)[0:1] + upd

    ln = ln_ref[...]
    for s in slots:
        o = jnp.concatenate(o_items[s * npair:(s + 1) * npair], axis=1)
        mean = _segsum(o, hm) * (1.0 / RW_HD)
        xc = o - mean
        var = _segsum(xc * xc, hm) * (1.0 / RW_HD)
        y = xc * lax.rsqrt(var + RW_GN_EPS) * ln[0:1] + ln[1:2]
        o_ref[0, s, 0] = ((y + bonus_[s]) * gs_[s]).astype(o_ref.dtype)

    if multi_chunk:
        for s, q in seqs:
            carry_scr[sidx(s, q)] = ps_[s][q * cs + cs - 1:q * cs + cs]

    @pl.when(ci == pl.num_programs(1) - 1)
    def _():
        for s, q in seqs:
            shift_ref[sidx(s, q)] = ps_[s][q * cs + n_valid - 1:q * cs + n_valid]
            for pr in pairs:
                ht = ht_scr[sidx(s, q), pr]
                s1_ref[sidx(s, q), 2 * pr] = ht[0:RW_HD, 0:RW_HD]
                s1_ref[sidx(s, q), 2 * pr + 1] = ht[RW_HD:2 * RW_HD, RW_HD:2 * RW_HD]


def _wkv_group(p_rw, lp, bsz, lpad, n_valid_len, shift0, s0):
    rows = RW_HD
    if lpad >= rows:
        cs, nseq = rows, 1
    else:
        cs, nseq = lpad, rows // lpad
    nc = lpad // cs
    n_valid = cs if nc > 1 else n_valid_len
    units = bsz // nseq
    ns = SLOTS_MIN if units % SLOTS_MIN == 0 else 1
    vec = jnp.zeros((SUBLANES, RW_W), F32)
    vec = vec.at[0].set(lp['rwkv_w0']).at[1].set(lp['rwkv_a0']).at[2].set(lp['rwkv_k_k'])
    vec = vec.at[3].set(lp['rwkv_k_a']).at[4].set(lp['rwkv_r_k'].reshape(RW_W))
    zeros = jnp.zeros((RW_DECAY_LORA, RW_W), F32)
    w2p = jnp.concatenate([lp['rwkv_w2'], zeros], axis=0)
    a2p = jnp.concatenate([zeros, lp['rwkv_a2']], axis=0)
    hm = _head_sum_matrix(RW_W, RW_HD)
    ln = jnp.zeros((SUBLANES, RW_W), F32).at[0].set(lp['rwkv_ln_g']).at[1].set(lp['rwkv_ln_b'])
    mu = lp['rwkv_mu'].reshape(1, RW_COLS)
    first = shift0.reshape(bsz, 1, RW_COLS)
    full = lambda a: pl.BlockSpec(a.shape, lambda b, j: (0,) * a.ndim)
    nq = ns * nseq
    sspec = pl.BlockSpec((nq, RW_HEADS, RW_HD, RW_HD), lambda b, j: (b, 0, 0, 0))
    fspec = pl.BlockSpec((nq, 1, RW_COLS), lambda b, j: (b, 0, 0))
    o, s1, shift1 = pl.pallas_call(
        functools.partial(_wkv_chunk_kernel, cs=cs, nseq=nseq, ns=ns, n_valid=n_valid, multi_chunk=nc > 1),
        out_shape=[jax.ShapeDtypeStruct((units // ns, ns, nc, rows, RW_W), BF16),
                   jax.ShapeDtypeStruct((bsz, RW_HEADS, RW_HD, RW_HD), F32),
                   jax.ShapeDtypeStruct((bsz, 1, RW_COLS), F32)],
        grid=(units // ns, nc),
        in_specs=[_slot_spec(ns, rows, RW_COLS), fspec, full(mu), full(vec),
                  full(w2p), full(a2p), full(lp['rwkv_g2']), full(hm), full(ln), sspec],
        out_specs=[_slot_spec(ns, rows, RW_W), sspec, fspec],
        scratch_shapes=[pltpu.VMEM((nq, 1, RW_COLS), F32),
                        pltpu.VMEM((nq, RW_HEADS // 2, LANES, LANES), F32)],
        compiler_params=_cparams(("parallel", "arbitrary")),
        name="wkv_chunk",
    )(_slot_view(p_rw, units, ns, nc, rows), first, mu, vec, w2p, a2p, lp['rwkv_g2'], hm, ln, s0)
    return o.reshape(bsz * lpad, RW_W), shift1.reshape(bsz, RW_COLS), s1


def _gdn_kernel(qkv_ref, z_ref, ba_ref, convw_ref, hp_ref, ng_ref, tail0_ref, s0_ref,
                o_ref, s1_ref, conv_ref, ext_scr, s_scr, *, c, nb, n_valid):
    ci = pl.program_id(1)
    last = ci == pl.num_programs(1) - 1
    slots = range(nb)
    heads = range(GD_HEADS)
    items = [(s, h) for s in slots for h in heads]

    @pl.when(ci == 0)
    def _():
        for s in slots:
            ext_scr[s, 0:SUBLANES, :] = tail0_ref[s]
            s_scr[s] = s0_ref[s]

    cw = convw_ref[...]
    hp = hp_ref[...]
    ng = ng_ref[...]
    off = SUBLANES - (CONV_W - 1)
    rid = _iota((c, c), 0)
    cid = _iota((c, c), 1)
    incl = rid >= cid
    tri = incl.astype(BF16)
    strict_l = (rid > cid).astype(F32)
    kbase, vbase = GD_HEADS * GD_DK, 2 * GD_HEADS * GD_DK

    cqs, beta_alls, g_alls, gcum_alls = [], [], [], []
    for s in slots:
        ext_scr[s, SUBLANES:SUBLANES + c, :] = qkv_ref[0, s, 0]
        cq = ext_scr[s, off:off + c, :] * cw[0:1]
        for j in range(1, CONV_W):
            cq = cq + ext_scr[s, off + j:off + j + c, :] * cw[j:j + 1]

        @pl.when(last)
        def _():
            conv_ref[s] = ext_scr[s, off + n_valid:off + n_valid + CONV_W - 1, :]

        ext_scr[s, 0:SUBLANES, :] = ext_scr[s, c:c + SUBLANES, :]
        cqs.append(_silu(cq))
        ba = ba_ref[0, s, 0]
        beta_all = _sigmoid(ba)
        g_all = -jnp.exp(hp[0:1]) * _softplus(ba + hp[1:2])
        if n_valid < c:
            valid = _iota((c, LANES), 0) < n_valid
            beta_all = jnp.where(valid, beta_all, 0.0)
            g_all = jnp.where(valid, g_all, 0.0)
        beta_alls.append(beta_all)
        g_alls.append(g_all)
        gcum_alls.append(_mdot(tri, g_all))

    qs = [cqs[s][:, h * GD_DK:(h + 1) * GD_DK] for s, h in items]
    ks = [cqs[s][:, kbase + h * GD_DK:kbase + (h + 1) * GD_DK] for s, h in items]
    vs = [cqs[s][:, vbase + h * GD_DV:vbase + (h + 1) * GD_DV] for s, h in items]
    qs = [q * lax.rsqrt(jnp.sum(q * q, axis=-1, keepdims=True) + NORM_EPS) * (GD_DK ** -0.5) for q in qs]
    ks = [k * lax.rsqrt(jnp.sum(k * k, axis=-1, keepdims=True) + NORM_EPS) for k in ks]
    betas = [beta_alls[s][:, h:h + 1] for s, h in items]
    g_cols = [g_alls[s][:, GD_HEADS + h:GD_HEADS + h + 1] for s, h in items]
    gcs = [gcum_alls[s][:, GD_HEADS + h:GD_HEADS + h + 1] for s, h in items]
    glasts = [gc[c - 1:c, :] for gc in gcs]
    egcs = [jnp.exp(gc) for gc in gcs]
    s_old = [s_scr[s, h] for s, h in items]
    decs = [jnp.where(incl, jnp.exp(_mdot(tri, g * strict_l)), 0.0) for g in g_cols]
    qk_kts = [_dot3(jnp.concatenate([q, k], axis=0), k, 'nt') for q, k in zip(qs, ks)]
    ms = [strict_l * b * qk[c:2 * c] * dec for b, qk, dec in zip(betas, qk_kts, decs)]
    rhss = [jnp.concatenate([v * b, k * (b * e)], axis=1) for v, k, b, e in zip(vs, ks, betas, egcs)]
    if c >= 2 * SUBLANES:
        xss = [_dot3(p, rhs) for p, rhs in zip(_unit_lower_inverses(ms, rid, cid, c), rhss)]
    else:
        strict_u = (rid < cid).astype(F32)
        dts = [_mdot((rid < cid).astype(BF16), g * (rid <= cid).astype(F32)) for g in g_cols]
        mts = [strict_u * _dot3(k, k * b, 'nt') * jnp.exp(jnp.minimum(dt, 0.0))
               for k, b, dt in zip(ks, betas, dts)]
        xss = rhss
        for i in range(1, c):
            row_i = _iota((c, GD_DV + GD_DK), 0) == i
            xss = [jnp.where(row_i, xs - jnp.sum(mt[:, i:i + 1] * xs, axis=0, keepdims=True), xs)
                   for xs, mt in zip(xss, mts)]
    ws_qss = [_dot3(jnp.concatenate([xs[:, GD_DV:GD_DV + GD_DK], q * e], axis=0), s)
              for xs, q, e, s in zip(xss, qs, egcs, s_old)]
    v_news = [xs[:, 0:GD_DV] - wq[0:c] for xs, wq in zip(xss, ws_qss)]
    os_ = [wq[c:2 * c] + _dot3(qk[0:c] * dec, vn) for wq, qk, dec, vn in zip(ws_qss, qk_kts, decs, v_news)]
    s_new = [s * jnp.exp(gl) + _dot3(k * jnp.exp(gl - gc), vn, 'tn')
             for s, gl, gc, k, vn in zip(s_old, glasts, gcs, ks, v_news)]
    os_ = [o * lax.rsqrt(jnp.mean(o * o, axis=-1, keepdims=True) + NORM_EPS) * ng for o in os_]
    for s in slots:
        z = z_ref[0, s, 0]
        o_ref[0, s, 0] = jnp.concatenate(
            [(os_[s * GD_HEADS + h] * _silu(z[:, h * GD_DV:(h + 1) * GD_DV])).astype(o_ref.dtype) for h in heads],
            axis=1)
    for i, (s, h) in enumerate(items):
        s_scr[s, h] = s_new[i]

    @pl.when(last)
    def _():
        for s in slots:
            s1_ref[s] = s_scr[s]


def _slot_plan(bsz, lpad):
    c = _pick(lpad, (CHUNK_ROWS, 32, 16, 8))
    nb = max(SLOTS_MIN, CHUNK_ROWS // c)
    if bsz % nb:
        nb = 1
    return c, lpad // c, nb


def _slot_view(a, bsz, nb, nc, c):
    return a.reshape(bsz // nb, nb, nc, c, a.shape[-1])


def _slot_spec(nb, c, w):
    return pl.BlockSpec((1, nb, 1, c, w), lambda b, j: (b, 0, j, 0, 0))


def _gdn_group(pqkv, pz, pba, lp, bsz, lpad, n_valid_len, conv0, s0):
    c, nc, nb = _slot_plan(bsz, lpad)
    n_valid = c if nc > 1 else n_valid_len
    tail0 = jnp.concatenate([jnp.zeros((bsz, SUBLANES - (CONV_W - 1), GD_QKV), F32), conv0], axis=1)
    hp = jnp.zeros((SUBLANES, LANES), F32)
    hp = hp.at[0, GD_HEADS:2 * GD_HEADS].set(lp['gdn_a_log']).at[1, GD_HEADS:2 * GD_HEADS].set(lp['gdn_dt_bias'])
    ng = lp['gdn_norm_g'].reshape(1, GD_DV)
    view = lambda a: _slot_view(a, bsz, nb, nc, c)
    full = lambda a: pl.BlockSpec(a.shape, lambda b, j: (0,) * a.ndim)
    sspec = pl.BlockSpec((nb, GD_HEADS, GD_DK, GD_DV), lambda b, j: (b, 0, 0, 0))
    o, s1, conv1 = pl.pallas_call(
        functools.partial(_gdn_kernel, c=c, nb=nb, n_valid=n_valid),
        out_shape=[jax.ShapeDtypeStruct((bsz // nb, nb, nc, c, GD_W), BF16),
                   jax.ShapeDtypeStruct((bsz, GD_HEADS, GD_DK, GD_DV), F32),
                   jax.ShapeDtypeStruct((bsz, CONV_W - 1, GD_QKV), F32)],
        grid=(bsz // nb, nc),
        in_specs=[_slot_spec(nb, c, GD_QKV), _slot_spec(nb, c, GD_W), _slot_spec(nb, c, LANES),
                  full(lp['gdn_conv']), full(hp), full(ng),
                  pl.BlockSpec((nb, SUBLANES, GD_QKV), lambda b, j: (b, 0, 0)), sspec],
        out_specs=[_slot_spec(nb, c, GD_W), sspec,
                   pl.BlockSpec((nb, CONV_W - 1, GD_QKV), lambda b, j: (b, 0, 0))],
        scratch_shapes=[pltpu.VMEM((nb, c + SUBLANES, GD_QKV), F32),
                        pltpu.VMEM((nb, GD_HEADS, GD_DK, GD_DV), F32)],
        compiler_params=_cparams(("parallel", "arbitrary")),
        name="gdn_chunk",
    )(view(pqkv), view(pz), view(pba), lp['gdn_conv'], hp, ng, tail0, s0)
    return o.reshape(bsz * lpad, GD_W), s1, conv1


def _hgrn_kernel(p_ref, lbv_ref, ng_ref, s0_ref, o_ref, s1_ref, st_scr, *, c, nb, n_valid):
    ci = pl.program_id(1)
    slots = range(nb)
    heads = range(HG_HEADS)
    items = [(s, h) for s in slots for h in heads]

    @pl.when(ci == 0)
    def _():
        for s, h in items:
            st_scr[s, h] = jnp.transpose(s0_ref[s, h])

    rid = _iota((c, c), 0)
    cid = _iota((c, c), 1)
    tri = (rid >= cid).astype(BF16)
    w = HG_HEADS * HG_DF
    rows = _iota((c, w), 0)
    sb = min(c, HG_SUB)
    row_in_sub = rows % sb
    lbv = lbv_ref[...]
    hs = lambda x, h: x[:, h * HG_DF:(h + 1) * HG_DF]

    qs, ks, vs, bcs = [], [], [], []
    for s in slots:
        p = p_ref[0, s, 0]
        pf = p[:, w:2 * w]
        a = lbv[0:1]
        b = lbv[1:2] - _softplus(-pf)
        logf = jnp.maximum(a, b) + jnp.log1p(jnp.exp(-jnp.abs(a - b)))
        k = lbv[2:3] * _sigmoid(-pf)
        if n_valid < c:
            logf = jnp.where(rows < n_valid, logf, 0.0)
            k = jnp.where(rows < n_valid, k, 0.0)
        qs.append(_silu(p[:, 0:w]))
        ks.append(k)
        vs.append(p[:, 2 * w:2 * w + HG_W])
        bcs.append(_mdot(tri, logf))
    st_old = [st_scr[s, h] for s, h in items]
    qes = [q * jnp.exp(bc) for q, bc in zip(qs, bcs)]
    os_ = [_dot3(hs(qes[s], h), st, 'nt') for (s, h), st in zip(items, st_old)]
    for delta in range(sb):
        prods, v_ss = [], []
        for s in slots:
            if delta == 0:
                k_s, b_s, v_s = ks[s], bcs[s], vs[s]
            else:
                k_s, b_s, v_s = (pltpu.roll(t, delta, 0) for t in (ks[s], bcs[s], vs[s]))
            prods.append(jnp.where(row_in_sub >= delta,
                                   qs[s] * k_s * jnp.exp(jnp.minimum(bcs[s] - b_s, 0.0)), 0.0))
            v_ss.append(v_s)
        os_ = [o + jnp.sum(hs(prods[s], h), axis=-1, keepdims=True) * hs(v_ss[s], h)
               for (s, h), o in zip(items, os_)]
    if c > sb:
        parts = [[jnp.zeros((sb, HG_DV), F32)] for _ in items]
        for r0 in range(sb, c, sb):
            qis, kps = [], []
            for s in slots:
                ref = bcs[s][r0 - 1:r0]
                qis.append(qs[s][r0:r0 + sb] * jnp.exp(bcs[s][r0:r0 + sb] - ref))
                kps.append(ks[s][0:r0] * jnp.exp(ref - bcs[s][0:r0]))
            att = [_dot3(hs(qis[s], h), hs(kps[s], h), 'nt') for s, h in items]
            for i, (s, h) in enumerate(items):
                parts[i].append(_dot3(att[i], hs(vs[s], h)[0:r0]))
        os_ = [o + jnp.concatenate(p, axis=0) for o, p in zip(os_, parts)]
    blasts = [bc[c - 1:c] for bc in bcs]
    kds = [k * jnp.exp(bl - bc) for k, bl, bc in zip(ks, blasts, bcs)]
    ebs = [jnp.exp(bl) for bl in blasts]
    st_new = [st * hs(ebs[s], h) + _dot3(hs(vs[s], h), hs(kds[s], h), 'tn') for (s, h), st in zip(items, st_old)]
    os_ = [o * lax.rsqrt(jnp.mean(o * o, axis=-1, keepdims=True) + NORM_EPS) for o in os_]
    ng = ng_ref[...]
    for s in slots:
        gate = _sigmoid(p_ref[0, s, 0][:, 2 * w + HG_W:2 * w + 2 * HG_W])
        o_all = jnp.concatenate(os_[s * HG_HEADS:(s + 1) * HG_HEADS], axis=1) * ng * gate
        o_ref[0, s, 0] = o_all.astype(o_ref.dtype)
    for i, (s, h) in enumerate(items):
        st_scr[s, h] = st_new[i]

    @pl.when(ci == pl.num_programs(1) - 1)
    def _():
        for s, h in items:
            s1_ref[s, h] = jnp.transpose(st_scr[s, h])


def _hgrn_group(phg, lb, lp, bsz, lpad, n_valid_len, s0):
    c, nc, nb = _slot_plan(bsz, lpad)
    n_valid = c if nc > 1 else n_valid_len
    lbv = jnp.zeros((SUBLANES, HG_HEADS * HG_DF), F32)
    lbv = lbv.at[0].set(jnp.log(jnp.maximum(lb, LB_FLOOR))).at[1].set(jnp.log1p(-lb)).at[2].set(1.0 - lb)
    ng = lp['hgrn_norm_g'].reshape(1, HG_W)
    full = lambda a: pl.BlockSpec(a.shape, lambda b, j: (0,) * a.ndim)
    sspec = pl.BlockSpec((nb, HG_HEADS, HG_DF, HG_DV), lambda b, j: (b, 0, 0, 0))
    o, s1 = pl.pallas_call(
        functools.partial(_hgrn_kernel, c=c, nb=nb, n_valid=n_valid),
        out_shape=[jax.ShapeDtypeStruct((bsz // nb, nb, nc, c, HG_W), BF16),
                   jax.ShapeDtypeStruct((bsz, HG_HEADS, HG_DF, HG_DV), F32)],
        grid=(bsz // nb, nc),
        in_specs=[_slot_spec(nb, c, HG_COLS), full(lbv), full(ng), sspec],
        out_specs=[_slot_spec(nb, c, HG_W), sspec],
        scratch_shapes=[pltpu.VMEM((nb, HG_HEADS, HG_DV, HG_DF), F32)],
        compiler_params=_cparams(("parallel", "arbitrary")),
        name="hgrn_chunk",
    )(_slot_view(phg, bsz, nb, nc, c), lbv, ng, s0)
    return o.reshape(bsz * lpad, HG_W), s1


def _ret_kernel(p_ref, cos_ref, sin_ref, qd_ref, kd_ref, dec_ref, dm_ref, gn_ref, s0_ref,
                o_ref, s1_ref, s_scr, *, c, nb):
    ci = pl.program_id(1)
    slots = range(nb)
    heads = range(RT_HEADS)
    items = [(s, h) for s in slots for h in heads]

    @pl.when(ci == 0)
    def _():
        for s in slots:
            s_scr[s] = jnp.zeros((RT_QK, RT_W), F32)
            for h in heads:
                s_scr[s, h * RT_DK:(h + 1) * RT_DK, h * RT_DV:(h + 1) * RT_DV] = s0_ref[s, h]

    cos = cos_ref[...]
    sin = sin_ref[...]
    lane = _iota((c, RT_QK), 1)
    first_half = (lane % RT_DK) < (RT_DK // 2)

    def rope(x):
        partner = jnp.where(first_half, pltpu.roll(x, RT_QK - RT_DK // 2, 1), pltpu.roll(x, RT_DK // 2, 1))
        return x * cos + partner * sin

    ps = [p_ref[0, s, 0] for s in slots]
    qs = [rope(p[:, 0:RT_QK]) for p in ps]
    ks = [rope(p[:, RT_QK:2 * RT_QK]) * (RT_DK ** -0.5) for p in ps]
    pvs = [p[:, 2 * RT_QK:2 * RT_QK + RT_W] for p in ps]
    sblks = [s_scr[s] for s in slots]
    qd = qd_ref[...]
    o_inters = [_dot3(q * qd, sblk) for q, sblk in zip(qs, sblks)]
    qks = [_dot3(jnp.concatenate([jnp.where(lane // RT_DK == h, q, 0.0) for h in heads], axis=0), k, 'nt')
           for q, k in zip(qs, ks)]
    attns = [qks[s][h * c:(h + 1) * c] * dec_ref[h] for s, h in items]
    os_ = [o_inters[s][:, h * RT_DV:(h + 1) * RT_DV] + _dot3(attn, pvs[s][:, h * RT_DV:(h + 1) * RT_DV])
           for (s, h), attn in zip(items, attns)]
    dm = dm_ref[...]
    kd = kd_ref[...]
    s_new = [sblk * dm + jnp.where(dm > 0.0, _dot3(k * kd, pv, 'tn'), 0.0) for sblk, k, pv in zip(sblks, ks, pvs)]
    gn = gn_ref[...]
    normed = []
    for o in os_:
        xc = o - jnp.mean(o, axis=-1, keepdims=True)
        normed.append(xc * lax.rsqrt(jnp.mean(xc * xc, axis=-1, keepdims=True) + NORM_EPS))
    for s in slots:
        gate = _silu(ps[s][:, 2 * RT_QK + RT_W:2 * RT_QK + 2 * RT_W])
        o_all = jnp.concatenate(normed[s * RT_HEADS:(s + 1) * RT_HEADS], axis=1) * gn * gate
        o_ref[0, s, 0] = o_all.astype(o_ref.dtype)
        s_scr[s] = s_new[s]

    @pl.when(ci == pl.num_programs(1) - 1)
    def _():
        for s, h in items:
            s1_ref[s, h] = s_scr[s, h * RT_DK:(h + 1) * RT_DK, h * RT_DV:(h + 1) * RT_DV]


def _ret_group(prt, lp, bsz, lpad, n_valid_len, pos0, s0):
    c, nc, nb = _slot_plan(bsz, lpad)
    n_valid = c if nc > 1 else n_valid_len
    half = RT_DK // 2
    inv = ROPE_BASE ** (-np.arange(half, dtype=np.float64) / half)
    ang = (pos0 + np.arange(lpad, dtype=np.float64))[:, None] * inv[None, :]
    cos = np.tile(np.cos(ang), (1, 2 * RT_HEADS))
    sin = np.tile(np.concatenate([-np.sin(ang), np.sin(ang)], axis=1), (1, RT_HEADS))
    loggamma = np.log(1.0 - np.exp2(-5.0 - np.arange(RT_HEADS, dtype=np.float64)))
    gcum = loggamma[:, None] * np.arange(1, c + 1, dtype=np.float64)[None, :]
    idx = np.arange(c)
    dec = np.where(idx[:, None] >= idx[None, :], np.exp(gcum[:, :, None] - gcum[:, None, :]), 0.0)
    qd = np.repeat(np.exp(gcum).T, RT_DK, axis=1)
    kdec = np.where(idx[None, :] < n_valid, np.exp(gcum[:, n_valid - 1:n_valid] - gcum), 0.0)
    kd = np.repeat(kdec.T, RT_DK, axis=1)
    sdec = np.exp(gcum[:, n_valid - 1])
    dm = np.zeros((RT_QK, RT_W))
    for h in range(RT_HEADS):
        dm[h * RT_DK:(h + 1) * RT_DK, h * RT_DV:(h + 1) * RT_DV] = sdec[h]
    cos, sin, qd, kd, dec, dm = (jnp.asarray(a, F32) for a in (cos, sin, qd, kd, dec, dm))
    gn = lp['ret_gn_g'].reshape(1, RT_W)
    full = lambda a: pl.BlockSpec(a.shape, lambda b, j: (0,) * a.ndim)
    posspec = pl.BlockSpec((c, RT_QK), lambda b, j: (j, 0))
    sspec = pl.BlockSpec((nb, RT_HEADS, RT_DK, RT_DV), lambda b, j: (b, 0, 0, 0))
    o, s1 = pl.pallas_call(
        functools.partial(_ret_kernel, c=c, nb=nb),
        out_shape=[jax.ShapeDtypeStruct((bsz // nb, nb, nc, c, RT_W), BF16),
                   jax.ShapeDtypeStruct((bsz, RT_HEADS, RT_DK, RT_DV), F32)],
        grid=(bsz // nb, nc),
        in_specs=[_slot_spec(nb, c, RT_COLS), posspec, posspec,
                  full(qd), full(kd), full(dec), full(dm), full(gn), sspec],
        out_specs=[_slot_spec(nb, c, RT_W), sspec],
        scratch_shapes=[pltpu.VMEM((nb, RT_QK, RT_W), F32)],
        compiler_params=_cparams(("parallel", "arbitrary")),
        name="ret_chunk",
    )(_slot_view(prt, bsz, nb, nc, c), cos, sin, qd, kd, dec, dm, gn, s0)
    return o.reshape(bsz * lpad, RT_W), s1


def _merge_kernel(x_ref, o1, o2, o3, o4, gate_ref, w1, w2, w3, w4, wo_ref, out_ref, *, d):
    acc = None
    for i, (o, w) in enumerate(((o1, w1), (o2, w2), (o3, w3), (o4, w4))):
        term = _sigmoid(gate_ref[:, i * d:(i + 1) * d]) * _dot(o[...], w[...])
        acc = term if acc is None else acc + term
    out_ref[...] = x_ref[...] + _dot(acc.astype(BF16), wo_ref[...])


def _merge(x, outs, gate, wouts, wo):
    n, d = x.shape
    tm = _pick(n, (256, 128, 64, 32, 16))
    row = lambda w: pl.BlockSpec((tm, w), lambda i: (i, 0))
    full = lambda a: pl.BlockSpec(a.shape, lambda i: (0,) * a.ndim)
    return pl.pallas_call(
        functools.partial(_merge_kernel, d=d),
        out_shape=jax.ShapeDtypeStruct((n, d), F32),
        grid=(n // tm,),
        in_specs=[row(d)] + [row(o.shape[1]) for o in outs] + [row(N_BRANCH * d)]
        + [full(w) for w in wouts] + [full(wo)],
        out_specs=row(d),
        compiler_params=_cparams(("parallel",)),
        name="merge",
    )(x, *outs, gate, *wouts, wo)


def _route_kernel(x_ref, g_ref, rg_ref, re_ref, rb_ref, xn_ref, col_ref, row_ref, cnt_ref):
    tm = x_ref.shape[0]
    lane = _iota((tm, LANES), 1)
    lanef = lane.astype(F32)
    x = x_ref[...]
    xn = x * lax.rsqrt(jnp.mean(x * x, axis=-1, keepdims=True) + NORM_EPS) * g_ref[...]
    xn_ref[...] = xn.astype(BF16)
    rb = rb_ref[...]
    neg = jnp.float32(-jnp.inf)
    glog = jnp.where(lane < N_GROUPS, _dot(xn, rg_ref[...], HI) + rb[0:1], neg)
    gmax = jnp.max(glog, axis=-1, keepdims=True)
    gsum = jnp.sum(jnp.exp(glog - gmax), axis=-1, keepdims=True)
    gidx = jnp.min(jnp.where(glog == gmax, lanef, float(LANES)), axis=-1, keepdims=True)
    gp = 1.0 / gsum
    in_group = (lanef >= gidx * EXPERTS_PER_GROUP) & (lanef < (gidx + 1.0) * EXPERTS_PER_GROUP)
    elog = jnp.where(in_group, _dot(xn, re_ref[...], HI) + rb[1:2], neg)
    emax = jnp.max(elog, axis=-1, keepdims=True)
    eexp = jnp.exp(elog - emax)
    ep = eexp / jnp.sum(eexp, axis=-1, keepdims=True)
    ep = jnp.where(in_group, ep, -1.0)
    p1 = jnp.max(ep, axis=-1, keepdims=True)
    i1 = jnp.min(jnp.where(ep == p1, lanef, float(LANES)), axis=-1, keepdims=True)
    ep2 = jnp.where(lanef == i1, -1.0, ep)
    p2 = jnp.max(ep2, axis=-1, keepdims=True)
    i2 = jnp.min(jnp.where(ep2 == p2, lanef, float(LANES)), axis=-1, keepdims=True)
    denom = p1 + p2
    wt1 = gp * p1 / denom
    wt2 = gp * p2 / denom

    onehot = ((lanef == i1) | (lanef == i2)).astype(F32)
    cnt = jnp.sum(onehot, axis=0, keepdims=True)
    before = (_iota((LANES, LANES), 0) < _iota((LANES, LANES), 1)).astype(BF16)
    off = _segsum(jnp.broadcast_to(cnt, (SUBLANES, LANES)), before)[0:1]
    tri = (_iota((tm, tm), 0) >= _iota((tm, tm), 1)).astype(BF16)
    slot = off + _dot(tri, onehot.astype(BF16)) - 1.0
    pos1 = jnp.sum(jnp.where(lanef == i1, slot, 0.0), axis=-1, keepdims=True)
    pos2 = jnp.sum(jnp.where(lanef == i2, slot, 0.0), axis=-1, keepdims=True)
    col = jnp.where(lane == 0, pos1, jnp.where(lane == 1, pos2, jnp.where(lane == 2, wt1,
                                                                         jnp.where(lane == 3, wt2, 0.0))))
    col_ref[...] = col
    row_ref[0] = jnp.transpose(col)[0:SUBLANES]
    cnt_ref[0] = jnp.concatenate([cnt, off, jnp.zeros((SUBLANES - 2, LANES), F32)], axis=0)


def _moe_kernel(cnt_sm, off_sm, x_ref, xn_ref, col_ref, row_ref, wg_ref, wu_ref, wd_ref, out_ref,
                xs_scr, ys_scr, ws_scr, *, pb, rb):
    i = pl.program_id(0)
    e = pl.program_id(1)
    tm, d = x_ref.shape
    ns = 2 * tm

    @pl.when(e == 0)
    def _():
        rowd = row_ref[0]
        pos1, pos2, wt1, wt2 = rowd[0:1], rowd[1:2], rowd[2:3], rowd[3:4]
        for blk in range(ns // pb):
            sid = (_iota((pb, tm), 0) + blk * pb).astype(F32)
            m1 = sid == pos1
            m2 = sid == pos2
            xs_scr[blk * pb:(blk + 1) * pb, :] = _dot((m1 | m2).astype(BF16), xn_ref[...]).astype(BF16)
            wsl = jnp.sum(jnp.where(m1, wt1, 0.0) + jnp.where(m2, wt2, 0.0), axis=-1, keepdims=True)
            ws_scr[blk * pb:(blk + 1) * pb, :] = jnp.broadcast_to(wsl, (pb, LANES))
        ys_scr[...] = jnp.zeros_like(ys_scr)

    cnt = cnt_sm[i, e]
    off = off_sm[i, e]

    @pl.when(cnt > 0)
    def _():
        start = (off // BF16_ROWS) * BF16_ROWS

        def body(j, carry):
            own = start + j * rb
            r0 = pl.multiple_of(jnp.minimum(own, ns - rb), BF16_ROWS)
            xb = xs_scr[pl.ds(r0, rb), :]
            hid = _silu(_dot(xb, wg_ref[0])) * _dot(xb, wu_ref[0])
            y = _dot(hid.astype(BF16), wd_ref[0])
            srow = r0 + _iota((rb, d), 0)
            mine = (srow >= jnp.maximum(off, own)) & (srow < off + cnt)
            ys_scr[pl.ds(r0, rb), :] += jnp.where(mine, y, 0.0)
            return carry

        lax.fori_loop(0, (off + cnt - start + rb - 1) // rb, body, 0)

    @pl.when(e == pl.num_programs(1) - 1)
    def _():
        col = col_ref[...]
        pos1, pos2 = col[:, 0:1], col[:, 1:2]
        acc = x_ref[...]
        for blk in range(ns // pb):
            sid = (_iota((tm, pb), 1) + blk * pb).astype(F32)
            pt = ((sid == pos1) | (sid == pos2)).astype(BF16)
            ysw = ys_scr[blk * pb:(blk + 1) * pb, :] * ws_scr[blk * pb:(blk + 1) * pb, 0:1]
            hi, lo = _split2(ysw)
            acc = acc + _dot(pt, hi) + _dot(pt, lo)
        out_ref[...] = acc


def _moe(x, lp, experts):
    n, d = x.shape
    tm = _pick(n, (MOE_TILE, 512, 256, 128, 64))
    nt = n // tm
    de = experts[0].shape[-1]
    rg = jnp.zeros((d, LANES), F32).at[:, :N_GROUPS].set(lp['router_group'])
    re = jnp.zeros((d, LANES), F32).at[:, :N_EXPERTS].set(lp['router_expert'])
    rbias = jnp.zeros((SUBLANES, LANES), F32)
    rbias = rbias.at[0, :N_GROUPS].set(lp['router_group_b']).at[1, :N_EXPERTS].set(lp['router_expert_b'])
    g = lp['norm2_g'].reshape(1, d)
    full1 = lambda a: pl.BlockSpec(a.shape, lambda i: (0,) * a.ndim)
    xn, col, row, cnt = pl.pallas_call(
        _route_kernel,
        out_shape=[jax.ShapeDtypeStruct((n, d), BF16), jax.ShapeDtypeStruct((n, LANES), F32),
                   jax.ShapeDtypeStruct((nt, SUBLANES, tm), F32), jax.ShapeDtypeStruct((nt, SUBLANES, LANES), F32)],
        grid=(nt,),
        in_specs=[pl.BlockSpec((tm, d), lambda i: (i, 0)), full1(g), full1(rg), full1(re), full1(rbias)],
        out_specs=[pl.BlockSpec((tm, d), lambda i: (i, 0)), pl.BlockSpec((tm, LANES), lambda i: (i, 0)),
                   pl.BlockSpec((1, SUBLANES, tm), lambda i: (i, 0, 0)),
                   pl.BlockSpec((1, SUBLANES, LANES), lambda i: (i, 0, 0))],
        compiler_params=_cparams(("parallel",)),
        name="moe_route",
    )(x, g, rg, re, rbias)
    cnt_i = cnt[:, 0, :N_EXPERTS].astype(jnp.int32)
    off_i = cnt[:, 1, :N_EXPERTS].astype(jnp.int32)
    ns = 2 * tm
    pb = min(MOE_SLOT_BLOCK, ns)
    rb = min(MOE_ROW_BLOCK, ns)
    grid_spec = pltpu.PrefetchScalarGridSpec(
        num_scalar_prefetch=2,
        grid=(nt, N_EXPERTS),
        in_specs=[pl.BlockSpec((tm, d), lambda i, e, c, o: (i, 0)),
                  pl.BlockSpec((tm, d), lambda i, e, c, o: (i, 0)),
                  pl.BlockSpec((tm, LANES), lambda i, e, c, o: (i, 0)),
                  pl.BlockSpec((1, SUBLANES, tm), lambda i, e, c, o: (i, 0, 0)),
                  pl.BlockSpec((1, d, de), lambda i, e, c, o: (e, 0, 0)),
                  pl.BlockSpec((1, d, de), lambda i, e, c, o: (e, 0, 0)),
                  pl.BlockSpec((1, de, d), lambda i, e, c, o: (e, 0, 0))],
        out_specs=pl.BlockSpec((tm, d), lambda i, e, c, o: (i, 0)),
        scratch_shapes=[pltpu.VMEM((ns, d), BF16), pltpu.VMEM((ns, d), F32), pltpu.VMEM((ns, LANES), F32)],
    )
    return pl.pallas_call(
        functools.partial(_moe_kernel, pb=pb, rb=rb),
        out_shape=jax.ShapeDtypeStruct((n, d), F32),
        grid_spec=grid_spec,
        compiler_params=pltpu.CompilerParams(dimension_semantics=("parallel", "arbitrary"),
                                             vmem_limit_bytes=MOE_VMEM_LIMIT),
        name="moe",
    )(cnt_i, off_i, x, xn, col, row, *experts)


def _layer(x, wts, lp, lb, gr):
    b, seq, lpad = gr['bsz'], gr['seq'], gr['lpad']
    shift0, wkv0, conv0, gdn0, hgrn0, ret0 = gr['states']
    xn = _rmsnorm(x, lp['norm1_g'], BF16)
    p_rw, p_gqkv, p_gz, p_gba, p_hg, p_rt, p_gate = (
        _matmul(xn, w, f"proj{i}") for i, w in enumerate(wts['proj']))
    o_rw, shift1, wkv1 = _wkv_group(p_rw, lp, b, lpad, seq, shift0, wkv0)
    o_gd, gdn1, conv1 = _gdn_group(p_gqkv, p_gz, p_gba, lp, b, lpad, seq, conv0, gdn0)
    o_hg, hgrn1 = _hgrn_group(p_hg, lb, lp, b, lpad, seq, hgrn0)
    o_rt, ret1 = _ret_group(p_rt, lp, b, lpad, seq, gr['pos0'], ret0)
    x = _merge(x, [o_rw, o_gd, o_hg, o_rt], p_gate, wts['out'], wts['w_o'])
    x = _moe(x, lp, wts['experts'])
    return x, (shift1, wkv1, conv1, gdn1, hgrn1, ret1)


def _layer_weights(lp, d):
    offs = np.cumsum([0, RW_COLS, GD_QKV, GD_W, 2 * GD_HEADS, HG_COLS, RT_COLS, N_BRANCH * d])
    seg = [lp['w_in'][:, offs[i]:offs[i + 1]] for i in range(7)]
    seg[3] = jnp.pad(seg[3], ((0, 0), (0, LANES - 2 * GD_HEADS)))
    return dict(
        proj=[s.astype(BF16) for s in seg],
        out=[lp[n].astype(BF16) for n in ('w_out_rwkv', 'w_out_gdn', 'w_out_hgrn', 'w_out_ret')],
        w_o=lp['w_o'].astype(BF16),
        experts=[lp[n].astype(BF16) for n in ('moe_w_gate', 'moe_w_up', 'moe_w_down')])


def kernel(x_prompt, x_sample, state_rwkv_shift, state_rwkv_wkv, state_gdn_conv, state_gdn, state_hgrn, state_ret, norm1_g, w_in, rwkv_mu, rwkv_w0, rwkv_w2, rwkv_a0, rwkv_a2, rwkv_g2, rwkv_k_k, rwkv_k_a, rwkv_r_k, rwkv_ln_g, rwkv_ln_b, w_out_rwkv, gdn_conv, gdn_a_log, gdn_dt_bias, gdn_norm_g, w_out_gdn, hgrn_lb_logits, hgrn_norm_g, w_out_hgrn, ret_gn_g, w_out_ret, w_o, norm2_g, router_group, router_group_b, router_expert, router_expert_b, moe_w_gate, moe_w_up, moe_w_down, final_norm_g):
    params = dict(norm1_g=norm1_g, w_in=w_in, rwkv_mu=rwkv_mu, rwkv_w0=rwkv_w0, rwkv_w2=rwkv_w2,
                  rwkv_a0=rwkv_a0, rwkv_a2=rwkv_a2, rwkv_g2=rwkv_g2, rwkv_k_k=rwkv_k_k,
                  rwkv_k_a=rwkv_k_a, rwkv_r_k=rwkv_r_k, rwkv_ln_g=rwkv_ln_g, rwkv_ln_b=rwkv_ln_b,
                  w_out_rwkv=w_out_rwkv, gdn_conv=gdn_conv, gdn_a_log=gdn_a_log,
                  gdn_dt_bias=gdn_dt_bias, gdn_norm_g=gdn_norm_g, w_out_gdn=w_out_gdn,
                  hgrn_norm_g=hgrn_norm_g, w_out_hgrn=w_out_hgrn, ret_gn_g=ret_gn_g,
                  w_out_ret=w_out_ret, w_o=w_o, norm2_g=norm2_g, router_group=router_group,
                  router_group_b=router_group_b, router_expert=router_expert,
                  router_expert_b=router_expert_b, moe_w_gate=moe_w_gate, moe_w_up=moe_w_up,
                  moe_w_down=moe_w_down)
    depth = w_in.shape[0]
    bp, lp_len, d = x_prompt.shape
    bs, ls, _ = x_sample.shape
    ls_pad = -(-ls // SAMPLE_PAD_LEN) * SAMPLE_PAD_LEN
    sm = jax.nn.softmax(hgrn_lb_logits.astype(F32), axis=0)
    lower_bounds = jnp.cumsum(sm, axis=0) - sm[0]

    sample_states = (state_rwkv_shift, state_rwkv_wkv, state_gdn_conv, state_gdn, state_hgrn, state_ret)
    xp = x_prompt.reshape(bp * lp_len, d)
    xs = jnp.pad(x_sample, ((0, 0), (0, ls_pad - ls), (0, 0))).reshape(bs * ls_pad, d)

    prompt_out = [[] for _ in sample_states]
    sample_out = [[] for _ in sample_states]
    for layer in range(depth):
        lpar = {name: arr[layer] for name, arr in params.items()}
        wts = _layer_weights(lpar, d)
        prompt = dict(bsz=bp, seq=lp_len, lpad=lp_len, pos0=0,
                      states=tuple(jnp.zeros((bp,) + s.shape[2:], F32) for s in sample_states))
        sample = dict(bsz=bs, seq=ls, lpad=ls_pad, pos0=PAST_LEN,
                      states=tuple(s[layer].astype(F32) for s in sample_states))
        xp, new_p = _layer(xp, wts, lpar, lower_bounds[layer], prompt)
        xs, new_s = _layer(xs, wts, lpar, lower_bounds[layer], sample)
        for lst, n in zip(prompt_out, new_p):
            lst.append(n)
        for lst, n in zip(sample_out, new_s):
            lst.append(n)

    y_prompt = _rmsnorm(xp, final_norm_g, F32).reshape(bp, lp_len, d)
    y_sample = _rmsnorm(xs, final_norm_g, F32).reshape(bs, ls_pad, d)[:, :ls]
    p_states = [jnp.stack(lst) for lst in prompt_out]
    s_states = [jnp.stack(lst).astype(o.dtype) for lst, o in zip(sample_out, sample_states)]
    return (y_prompt, y_sample, *p_states, *s_states)
```

```python
import functools
import math

import numpy as np
import jax
import jax.numpy as jnp
from jax import lax
from jax.experimental import pallas as pl
from jax.experimental.pallas import tpu as pltpu

F32 = jnp.float32
BF16 = jnp.bfloat16

NORM_EPS = 1e-6
LB_FLOOR = 1e-30
PAST_LEN = 16384
RW_HEADS = 8
RW_HD = 64
RW_W = RW_HEADS * RW_HD
RW_DECAY_LORA = 64
RW_AAA_LORA = 64
RW_GATE_LORA = 128
RW_COLS = 3 * RW_W + RW_DECAY_LORA + RW_AAA_LORA + RW_GATE_LORA
RW_GN_EPS = 64e-5
GD_HEADS = 4
GD_DK = 128
GD_DV = 128
GD_QKV = GD_HEADS * (2 * GD_DK + GD_DV)
GD_W = GD_HEADS * GD_DV
CONV_W = 4
HG_HEADS = 4
HG_DF = 128
HG_DV = 128
HG_W = HG_HEADS * HG_DV
HG_COLS = 2 * HG_HEADS * HG_DF + 2 * HG_W
HG_SUB = 16
RT_HEADS = 4
RT_DK = 64
RT_DV = 128
RT_W = RT_HEADS * RT_DV
RT_QK = RT_HEADS * RT_DK
RT_COLS = 2 * RT_QK + 2 * RT_W
ROPE_BASE = 10000.0
N_BRANCH = 4
N_GROUPS = 4
EXPERTS_PER_GROUP = 8
N_EXPERTS = N_GROUPS * EXPERTS_PER_GROUP

LANES = 128
SUBLANES = 8
SAMPLE_PAD_LEN = 8
CHUNK_ROWS = 64
SLOTS_MIN = 4
WKV_SLOTS = 2
VMEM_LIMIT = 48 * 1024 * 1024
MOE_TILE = 1024
MOE_SLOT_BLOCK = 256
MOE_ROW_BLOCK = 128
BF16_ROWS = 16
MOE_VMEM_LIMIT = 56 * 1024 * 1024


def _pick(n, cands):
    for c in cands:
        if n % c == 0:
            return c
    raise ValueError(f"no tile for {n} in {cands}")


def _cparams(sem):
    return pltpu.CompilerParams(dimension_semantics=sem, vmem_limit_bytes=VMEM_LIMIT)


def _dot(a, b):
    return lax.dot_general(a, b, (((1,), (0,)), ((), ())), preferred_element_type=F32)


def _softplus(x):
    return jnp.maximum(x, 0.0) + jnp.log1p(jnp.exp(-jnp.abs(x)))


def _sigmoid(x):
    return jax.nn.sigmoid(x)


def _silu(x):
    return x * jax.nn.sigmoid(x)


def _segsum(x, hm):
    hi = x.astype(BF16)
    r1 = x - hi.astype(F32)
    mid = r1.astype(BF16)
    lo = (r1 - mid.astype(F32)).astype(BF16)
    return _dot(hi, hm) + _dot(mid, hm) + _dot(lo, hm)


_DIMS = {'nn': (((1,), (0,)), ((), ())), 'nt': (((1,), (1,)), ((), ())), 'tn': (((0,), (0,)), ((), ()))}


def _split2(x):
    hi = x.astype(BF16)
    return hi, (x - hi.astype(F32)).astype(BF16)


def _dot3(a, b, form='nn'):
    ah, al = _split2(a)
    bh, bl = _split2(b)
    f = lambda x, y: lax.dot_general(x, y, _DIMS[form], preferred_element_type=F32)
    free = 1 if form == 'tn' else 0
    m = a.shape[free]
    both = f(jnp.concatenate([ah, al], axis=free), bh)
    return both[0:m] + both[m:2 * m] + f(ah, bl)


def _mdot(mask, x):
    hi = x.astype(BF16)
    r1 = x - hi.astype(F32)
    mid = r1.astype(BF16)
    lo = (r1 - mid.astype(F32)).astype(BF16)
    return _dot(mask, hi) + _dot(mask, mid) + _dot(mask, lo)


def _unit_lower_inverses(ms, rid, cid, c, expand=None):
    prod = _dot3 if expand is None else (lambda x, y: _dot3(x, expand(y)))
    same = lambda s: (rid // s) == (cid // s)
    ns = [jnp.where(same(SUBLANES), -m, 0.0) for m in ms]
    n2s = [prod(n, n) for n in ns]
    n4s = [prod(n2, n2) for n2 in n2s]
    eye = (rid == cid).astype(F32)
    ps = [eye + n for n in ns]
    ps = [p + prod(p, n2) for p, n2 in zip(ps, n2s)]
    ps = [p + prod(p, n4) for p, n4 in zip(ps, n4s)]
    s = SUBLANES
    while s < c:
        offs = [jnp.where(same(2 * s) & jnp.logical_not(same(s)), m, 0.0) for m in ms]
        ts = [prod(p, off) for p, off in zip(ps, offs)]
        ps = [p - prod(t, p) for p, t in zip(ps, ts)]
        s *= 2
    return ps


def _iota(shape, dim):
    return lax.broadcasted_iota(jnp.int32, shape, dim)


def _rms_kernel(x_ref, g_ref, o_ref):
    x = x_ref[...]
    ms = jnp.mean(x * x, axis=-1, keepdims=True)
    o_ref[...] = (x * lax.rsqrt(ms + NORM_EPS) * g_ref[...]).astype(o_ref.dtype)


def _rmsnorm(x, g, out_dtype):
    n, d = x.shape
    tm = _pick(n, (1024, 512, 256, 128, 64, 32, 16))
    return pl.pallas_call(
        _rms_kernel,
        out_shape=jax.ShapeDtypeStruct((n, d), out_dtype),
        grid=(n // tm,),
        in_specs=[pl.BlockSpec((tm, d), lambda i: (i, 0)), pl.BlockSpec((1, d), lambda i: (0, 0))],
        out_specs=pl.BlockSpec((tm, d), lambda i: (i, 0)),
        compiler_params=_cparams(("parallel",)),
        name="rmsnorm",
    )(x, g.reshape(1, d))


def _mm_kernel(x_ref, w_ref, o_ref):
    o_ref[...] = _dot(x_ref[...], w_ref[...])


def _matmul(x, w, name):
    n, k = x.shape
    m = w.shape[1]
    tm = _pick(n, (2048, 1024, 512, 256, 128, 64, 32, 16))
    tn = _pick(m, (1024, 896, 768, 512, 256, 128))
    return pl.pallas_call(
        _mm_kernel,
        out_shape=jax.ShapeDtypeStruct((n, m), F32),
        grid=(n // tm, m // tn),
        in_specs=[pl.BlockSpec((tm, k), lambda i, j: (i, 0)), pl.BlockSpec((k, tn), lambda i, j: (0, j))],
        out_specs=pl.BlockSpec((tm, tn), lambda i, j: (i, j)),
        compiler_params=_cparams(("parallel", "parallel")),
        name=name,
    )(x, w)


def _head_sum_matrix(width, seg):
    i = np.arange(width)
    return jnp.asarray((i[:, None] // seg) == (i[None, :] // seg), BF16)


def _wkv_chunk_kernel(p_ref, first_ref, mu_ref, vec_ref, w2_ref, a2_ref, g2_ref, hm_ref, ln_ref, s0_ref,
                      o_ref, s1_ref, shift_ref, carry_scr, ht_scr, *, cs, nseq, ns, n_valid, multi_chunk):
    ci = pl.program_id(1)
    rows = cs * nseq
    npair = RW_HEADS // 2
    f_zero = jnp.zeros((RW_HD, RW_HD), F32)
    slots = range(ns)
    pairs = range(npair)
    seqs = [(s, q) for s in slots for q in range(nseq)]
    sidx = lambda s, q: s * nseq + q

    @pl.when(ci == 0)
    def _():
        for s, q in seqs:
            carry_scr[sidx(s, q)] = first_ref[sidx(s, q)]
            for pr in pairs:
                top = jnp.concatenate([s0_ref[sidx(s, q), 2 * pr], f_zero], axis=1)
                bot = jnp.concatenate([f_zero, s0_ref[sidx(s, q), 2 * pr + 1]], axis=1)
                ht_scr[sidx(s, q), pr] = jnp.concatenate([top, bot], axis=0)

    vec = vec_ref[...]
    w0, a0, k_k, k_a, r_k = vec[0:1], vec[1:2], vec[2:3], vec[3:4], vec[4:5]
    hm = hm_ref[...]
    mu = mu_ref[...]
    row_w = _iota((rows, RW_COLS), 0)
    row_f = _iota((rows, RW_W), 0)
    rid = _iota((rows, rows), 0)
    cid = _iota((rows, rows), 1)
    tri = ((rid >= cid) & (rid // cs == cid // cs)).astype(BF16)

    ps_, vs_, gs_, bonus_ = [], [], [], []
    abar, bbar, kbar, rbar, btil, ktil, w_last = [], [], [], [], [], [], []
    for s in slots:
        p = p_ref[0, s, 0]
        prev = pltpu.roll(p, 1, 0)
        for q in range(nseq):
            prev = jnp.where(row_w == q * cs, carry_scr[sidx(s, q)], prev)
        h = p + (prev - p) * mu
        r = h[:, 0:RW_W]
        k = h[:, RW_W:2 * RW_W]
        v = h[:, 2 * RW_W:3 * RW_W]
        lo = h[:, 3 * RW_W:3 * RW_W + LANES]
        gl = h[:, 3 * RW_W + LANES:3 * RW_W + 2 * LANES]
        w_log = -_softplus(-(w0 + _dot3(jnp.tanh(lo), w2_ref[...]))) - 0.5
        logw = -jnp.exp(w_log)
        a = _sigmoid(a0 + _dot3(lo, a2_ref[...]))
        gs_.append(_dot3(_sigmoid(gl), g2_ref[...]))
        kk = k * k_k
        kk = kk * lax.rsqrt(_segsum(kk * kk, hm) + NORM_EPS)
        km = k * (1.0 + (a - 1.0) * k_a)
        bb = kk * a
        alpha = -kk
        bonus_.append(_segsum(r * km * r_k, hm) * v)
        if n_valid < cs:
            valid = (row_f % cs) < n_valid
            logw, alpha, bb, km = (jnp.where(valid, t, 0.0) for t in (logw, alpha, bb, km))
        gcum = _mdot(tri, logw)
        g_last = jnp.concatenate(
            [jnp.broadcast_to(gcum[q * cs + cs - 1:q * cs + cs], (cs, RW_W)) for q in range(nseq)], axis=0)
        emg = jnp.exp(-gcum)
        etil = jnp.exp(g_last - gcum)
        ps_.append(p)
        vs_.append(v)
        abar.append(alpha * jnp.exp(gcum - logw))
        bbar.append(bb * emg)
        kbar.append(km * emg)
        rbar.append(r * jnp.exp(gcum))
        btil.append(bb * etil)
        ktil.append(km * etil)
        w_last.append(jnp.exp(g_last))

    prow = _iota((rows, LANES), 0)
    plane = _iota((rows, LANES), 1)
    pcol = plane % RW_HD
    same_seq = (prow // cs) == (pcol // cs)
    strict = (pcol < prow) & same_seq
    incl = (pcol <= prow) & same_seq
    first_head = plane < RW_HD
    blk_mask = (_iota((LANES, LANES), 0) // RW_HD) == (_iota((LANES, LANES), 1) // RW_HD)

    def expand(x):
        return jnp.concatenate([jnp.where(first_head, x, 0.0), jnp.where(first_head, 0.0, x)], axis=0)

    items = [(s, pr) for s in slots for pr in pairs]
    ps = lambda x, pr: x[:, pr * LANES:(pr + 1) * LANES]
    seq_rows = lambda x, q: x[q * cs:(q + 1) * cs]

    lhs = [jnp.concatenate([ps(abar[s], pr), ps(rbar[s], pr)], axis=0) for s, pr in items]
    xb = [_dot3(l, expand(ps(bbar[s], pr)), 'nt') for l, (s, pr) in zip(lhs, items)]
    xk = [_dot3(l, expand(ps(kbar[s], pr)), 'nt') for l, (s, pr) in zip(lhs, items)]
    a_m = [jnp.where(strict, x[0:rows], 0.0) for x in xb]
    rb_m = [jnp.where(incl, x[rows:2 * rows], 0.0) for x in xb]
    b_m = [jnp.where(strict, x[0:rows], 0.0) for x in xk]
    rk_m = [jnp.where(incl, x[rows:2 * rows], 0.0) for x in xk]
    pinv = _unit_lower_inverses([-m for m in a_m], prow, pcol, cs, expand=expand)

    ht_old = {(s, q, pr): ht_scr[sidx(s, q), pr] for s, q in seqs for pr in pairs}
    xh = {(s, q, pr): _dot3(jnp.concatenate([seq_rows(ps(abar[s], pr), q), seq_rows(ps(rbar[s], pr), q)], axis=0),
                            ht_old[(s, q, pr)], 'nt') for s, q in seqs for pr in pairs}
    ah = [jnp.concatenate([xh[(s, q, pr)][0:cs] for q in range(nseq)], axis=0) for s, pr in items]
    rh = [jnp.concatenate([xh[(s, q, pr)][cs:2 * cs] for q in range(nseq)], axis=0) for s, pr in items]
    vexp = [expand(ps(vs_[s], pr)) for s, pr in items]
    rhs = [a_ + _dot3(b_, ve) for a_, b_, ve in zip(ah, b_m, vexp)]
    u = [_dot3(pi, expand(rh_)) for pi, rh_ in zip(pinv, rhs)]
    o_items = [rh_ + _dot3(jnp.concatenate([rb_, rk_], axis=1), jnp.concatenate([expand(u_), ve], axis=0))
               for rh_, rb_, rk_, u_, ve in zip(rh, rb_m, rk_m, u, vexp)]
    for i, (s, pr) in enumerate(items):
        for q in range(nseq):
            uv = jnp.concatenate([seq_rows(u[i], q), seq_rows(ps(vs_[s], pr), q)], axis=0)
            bk = jnp.concatenate([seq_rows(ps(btil[s], pr), q), seq_rows(ps(ktil[s], pr), q)], axis=0)
            upd = jnp.where(blk_mask, _dot3(uv, bk, 'tn'), 0.0)
            ht_scr[sidx(s, q), pr] = ht_old[(s, q, pr)] * seq_rows(ps(w_last[s], pr), q)[0:1] + upd

    ln = ln_ref[...]
    for s in slots:
        o = jnp.concatenate(o_items[s * npair:(s + 1) * npair], axis=1)
        mean = _segsum(o, hm) * (1.0 / RW_HD)
        xc = o - mean
        var = _segsum(xc * xc, hm) * (1.0 / RW_HD)
        y = xc * lax.rsqrt(var + RW_GN_EPS) * ln[0:1] + ln[1:2]
        o_ref[0, s, 0] = ((y + bonus_[s]) * gs_[s]).astype(o_ref.dtype)

    if multi_chunk:
        for s, q in seqs:
            carry_scr[sidx(s, q)] = ps_[s][q * cs + cs - 1:q * cs + cs]

    @pl.when(ci == pl.num_programs(1) - 1)
    def _():
        for s, q in seqs:
            shift_ref[sidx(s, q)] = ps_[s][q * cs + n_valid - 1:q * cs + n_valid]
            for pr in pairs:
                ht = ht_scr[sidx(s, q), pr]
                s1_ref[sidx(s, q), 2 * pr] = ht[0:RW_HD, 0:RW_HD]
                s1_ref[sidx(s, q), 2 * pr + 1] = ht[RW_HD:2 * RW_HD, RW_HD:2 * RW_HD]


def _wkv_group(p_rw, lp, bsz, lpad, n_valid_len, shift0, s0):
    rows = RW_HD
    if lpad >= rows:
        cs, nseq = rows, 1
    else:
        cs, nseq = lpad, rows // lpad
    nc = lpad // cs
    n_valid = cs if nc > 1 else n_valid_len
    units = bsz // nseq
    ns = WKV_SLOTS if units % WKV_SLOTS == 0 else 1
    vec = jnp.zeros((SUBLANES, RW_W), F32)
    vec = vec.at[0].set(lp['rwkv_w0']).at[1].set(lp['rwkv_a0']).at[2].set(lp['rwkv_k_k'])
    vec = vec.at[3].set(lp['rwkv_k_a']).at[4].set(lp['rwkv_r_k'].reshape(RW_W))
    zeros = jnp.zeros((RW_DECAY_LORA, RW_W), F32)
    w2p = jnp.concatenate([lp['rwkv_w2'], zeros], axis=0)
    a2p = jnp.concatenate([zeros, lp['rwkv_a2']], axis=0)
    hm = _head_sum_matrix(RW_W, RW_HD)
    ln = jnp.zeros((SUBLANES, RW_W), F32).at[0].set(lp['rwkv_ln_g']).at[1].set(lp['rwkv_ln_b'])
    mu = lp['rwkv_mu'].reshape(1, RW_COLS)
    first = shift0.reshape(bsz, 1, RW_COLS)
    full = lambda a: pl.BlockSpec(a.shape, lambda b, j: (0,) * a.ndim)
    nq = ns * nseq
    sspec = pl.BlockSpec((nq, RW_HEADS, RW_HD, RW_HD), lambda b, j: (b, 0, 0, 0))
    fspec = pl.BlockSpec((nq, 1, RW_COLS), lambda b, j: (b, 0, 0))
    o, s1, shift1 = pl.pallas_call(
        functools.partial(_wkv_chunk_kernel, cs=cs, nseq=nseq, ns=ns, n_valid=n_valid, multi_chunk=nc > 1),
        out_shape=[jax.ShapeDtypeStruct((units // ns, ns, nc, rows, RW_W), BF16),
                   jax.ShapeDtypeStruct((bsz, RW_HEADS, RW_HD, RW_HD), F32),
                   jax.ShapeDtypeStruct((bsz, 1, RW_COLS), F32)],
        grid=(units // ns, nc),
        in_specs=[_slot_spec(ns, rows, RW_COLS), fspec, full(mu), full(vec),
                  full(w2p), full(a2p), full(lp['rwkv_g2']), full(hm), full(ln), sspec],
        out_specs=[_slot_spec(ns, rows, RW_W), sspec, fspec],
        scratch_shapes=[pltpu.VMEM((nq, 1, RW_COLS), F32),
                        pltpu.VMEM((nq, RW_HEADS // 2, LANES, LANES), F32)],
        compiler_params=_cparams(("parallel", "arbitrary")),
        name="wkv_chunk",
    )(_slot_view(p_rw, units, ns, nc, rows), first, mu, vec, w2p, a2p, lp['rwkv_g2'], hm, ln, s0)
    return o.reshape(bsz * lpad, RW_W), shift1.reshape(bsz, RW_COLS), s1


def _gdn_kernel(qkv_ref, z_ref, ba_ref, convw_ref, hp_ref, ng_ref, tail0_ref, s0_ref,
                o_ref, s1_ref, conv_ref, ext_scr, s_scr, *, c, nb, n_valid):
    ci = pl.program_id(1)
    last = ci == pl.num_programs(1) - 1
    slots = range(nb)
    heads = range(GD_HEADS)
    items = [(s, h) for s in slots for h in heads]

    @pl.when(ci == 0)
    def _():
        for s in slots:
            ext_scr[s, 0:SUBLANES, :] = tail0_ref[s]
            s_scr[s] = s0_ref[s]

    cw = convw_ref[...]
    hp = hp_ref[...]
    ng = ng_ref[...]
    off = SUBLANES - (CONV_W - 1)
    rid = _iota((c, c), 0)
    cid = _iota((c, c), 1)
    incl = rid >= cid
    tri = incl.astype(BF16)
    strict_l = (rid > cid).astype(F32)
    kbase, vbase = GD_HEADS * GD_DK, 2 * GD_HEADS * GD_DK

    cqs, beta_alls, g_alls, gcum_alls = [], [], [], []
    for s in slots:
        ext_scr[s, SUBLANES:SUBLANES + c, :] = qkv_ref[0, s, 0]
        cq = ext_scr[s, off:off + c, :] * cw[0:1]
        for j in range(1, CONV_W):
            cq = cq + ext_scr[s, off + j:off + j + c, :] * cw[j:j + 1]

        @pl.when(last)
        def _():
            conv_ref[s] = ext_scr[s, off + n_valid:off + n_valid + CONV_W - 1, :]

        ext_scr[s, 0:SUBLANES, :] = ext_scr[s, c:c + SUBLANES, :]
        cqs.append(_silu(cq))
        ba = ba_ref[0, s, 0]
        beta_all = _sigmoid(ba)
        g_all = -jnp.exp(hp[0:1]) * _softplus(ba + hp[1:2])
        if n_valid < c:
            valid = _iota((c, LANES), 0) < n_valid
            beta_all = jnp.where(valid, beta_all, 0.0)
            g_all = jnp.where(valid, g_all, 0.0)
        beta_alls.append(beta_all)
        g_alls.append(g_all)
        gcum_alls.append(_mdot(tri, g_all))

    qs = [cqs[s][:, h * GD_DK:(h + 1) * GD_DK] for s, h in items]
    ks = [cqs[s][:, kbase + h * GD_DK:kbase + (h + 1) * GD_DK] for s, h in items]
    vs = [cqs[s][:, vbase + h * GD_DV:vbase + (h + 1) * GD_DV] for s, h in items]
    qs = [q * lax.rsqrt(jnp.sum(q * q, axis=-1, keepdims=True) + NORM_EPS) * (GD_DK ** -0.5) for q in qs]
    ks = [k * lax.rsqrt(jnp.sum(k * k, axis=-1, keepdims=True) + NORM_EPS) for k in ks]
    betas = [beta_alls[s][:, h:h + 1] for s, h in items]
    g_cols = [g_alls[s][:, GD_HEADS + h:GD_HEADS + h + 1] for s, h in items]
    gcs = [gcum_alls[s][:, GD_HEADS + h:GD_HEADS + h + 1] for s, h in items]
    glasts = [gc[c - 1:c, :] for gc in gcs]
    egcs = [jnp.exp(gc) for gc in gcs]
    s_old = [s_scr[s, h] for s, h in items]
    decs = [jnp.where(incl, jnp.exp(_mdot(tri, g * strict_l)), 0.0) for g in g_cols]
    qk_kts = [_dot3(jnp.concatenate([q, k], axis=0), k, 'nt') for q, k in zip(qs, ks)]
    ms = [strict_l * b * qk[c:2 * c] * dec for b, qk, dec in zip(betas, qk_kts, decs)]
    rhss = [jnp.concatenate([v * b, k * (b * e)], axis=1) for v, k, b, e in zip(vs, ks, betas, egcs)]
    if c >= 2 * SUBLANES:
        xss = [_dot3(p, rhs) for p, rhs in zip(_unit_lower_inverses(ms, rid, cid, c), rhss)]
    else:
        strict_u = (rid < cid).astype(F32)
        dts = [_mdot((rid < cid).astype(BF16), g * (rid <= cid).astype(F32)) for g in g_cols]
        mts = [strict_u * _dot3(k, k * b, 'nt') * jnp.exp(jnp.minimum(dt, 0.0))
               for k, b, dt in zip(ks, betas, dts)]
        xss = rhss
        for i in range(1, c):
            row_i = _iota((c, GD_DV + GD_DK), 0) == i
            xss = [jnp.where(row_i, xs - jnp.sum(mt[:, i:i + 1] * xs, axis=0, keepdims=True), xs)
                   for xs, mt in zip(xss, mts)]
    ws_qss = [_dot3(jnp.concatenate([xs[:, GD_DV:GD_DV + GD_DK], q * e], axis=0), s)
              for xs, q, e, s in zip(xss, qs, egcs, s_old)]
    v_news = [xs[:, 0:GD_DV] - wq[0:c] for xs, wq in zip(xss, ws_qss)]
    os_ = [wq[c:2 * c] + _dot3(qk[0:c] * dec, vn) for wq, qk, dec, vn in zip(ws_qss, qk_kts, decs, v_news)]
    s_new = [s * jnp.exp(gl) + _dot3(k * jnp.exp(gl - gc), vn, 'tn')
             for s, gl, gc, k, vn in zip(s_old, glasts, gcs, ks, v_news)]
    os_ = [o * lax.rsqrt(jnp.mean(o * o, axis=-1, keepdims=True) + NORM_EPS) * ng for o in os_]
    for s in slots:
        z = z_ref[0, s, 0]
        o_ref[0, s, 0] = jnp.concatenate(
            [(os_[s * GD_HEADS + h] * _silu(z[:, h * GD_DV:(h + 1) * GD_DV])).astype(o_ref.dtype) for h in heads],
            axis=1)
    for i, (s, h) in enumerate(items):
        s_scr[s, h] = s_new[i]

    @pl.when(last)
    def _():
        for s in slots:
            s1_ref[s] = s_scr[s]


def _slot_plan(bsz, lpad):
    c = _pick(lpad, (CHUNK_ROWS, 32, 16, 8))
    nb = max(SLOTS_MIN, CHUNK_ROWS // c)
    if bsz % nb:
        nb = 1
    return c, lpad // c, nb


def _slot_view(a, bsz, nb, nc, c):
    return a.reshape(bsz // nb, nb, nc, c, a.shape[-1])


def _slot_spec(nb, c, w):
    return pl.BlockSpec((1, nb, 1, c, w), lambda b, j: (b, 0, j, 0, 0))


def _gdn_group(pqkv, pz, pba, lp, bsz, lpad, n_valid_len, conv0, s0):
    c, nc, nb = _slot_plan(bsz, lpad)
    n_valid = c if nc > 1 else n_valid_len
    tail0 = jnp.concatenate([jnp.zeros((bsz, SUBLANES - (CONV_W - 1), GD_QKV), F32), conv0], axis=1)
    hp = jnp.zeros((SUBLANES, LANES), F32)
    hp = hp.at[0, GD_HEADS:2 * GD_HEADS].set(lp['gdn_a_log']).at[1, GD_HEADS:2 * GD_HEADS].set(lp['gdn_dt_bias'])
    ng = lp['gdn_norm_g'].reshape(1, GD_DV)
    view = lambda a: _slot_view(a, bsz, nb, nc, c)
    full = lambda a: pl.BlockSpec(a.shape, lambda b, j: (0,) * a.ndim)
    sspec = pl.BlockSpec((nb, GD_HEADS, GD_DK, GD_DV), lambda b, j: (b, 0, 0, 0))
    o, s1, conv1 = pl.pallas_call(
        functools.partial(_gdn_kernel, c=c, nb=nb, n_valid=n_valid),
        out_shape=[jax.ShapeDtypeStruct((bsz // nb, nb, nc, c, GD_W), BF16),
                   jax.ShapeDtypeStruct((bsz, GD_HEADS, GD_DK, GD_DV), F32),
                   jax.ShapeDtypeStruct((bsz, CONV_W - 1, GD_QKV), F32)],
        grid=(bsz // nb, nc),
        in_specs=[_slot_spec(nb, c, GD_QKV), _slot_spec(nb, c, GD_W), _slot_spec(nb, c, LANES),
                  full(lp['gdn_conv']), full(hp), full(ng),
                  pl.BlockSpec((nb, SUBLANES, GD_QKV), lambda b, j: (b, 0, 0)), sspec],
        out_specs=[_slot_spec(nb, c, GD_W), sspec,
                   pl.BlockSpec((nb, CONV_W - 1, GD_QKV), lambda b, j: (b, 0, 0))],
        scratch_shapes=[pltpu.VMEM((nb, c + SUBLANES, GD_QKV), F32),
                        pltpu.VMEM((nb, GD_HEADS, GD_DK, GD_DV), F32)],
        compiler_params=_cparams(("parallel", "arbitrary")),
        name="gdn_chunk",
    )(view(pqkv), view(pz), view(pba), lp['gdn_conv'], hp, ng, tail0, s0)
    return o.reshape(bsz * lpad, GD_W), s1, conv1


def _hgrn_kernel(p_ref, lbv_ref, ng_ref, s0_ref, o_ref, s1_ref, st_scr, *, c, nb, n_valid):
    ci = pl.program_id(1)
    slots = range(nb)
    heads = range(HG_HEADS)
    items = [(s, h) for s in slots for h in heads]

    @pl.when(ci == 0)
    def _():
        for s, h in items:
            st_scr[s, h] = jnp.transpose(s0_ref[s, h])

    rid = _iota((c, c), 0)
    cid = _iota((c, c), 1)
    tri = (rid >= cid).astype(BF16)
    w = HG_HEADS * HG_DF
    rows = _iota((c, w), 0)
    sb = min(c, HG_SUB)
    row_in_sub = rows % sb
    lbv = lbv_ref[...]
    hs = lambda x, h: x[:, h * HG_DF:(h + 1) * HG_DF]

    qs, ks, vs, bcs = [], [], [], []
    for s in slots:
        p = p_ref[0, s, 0]
        pf = p[:, w:2 * w]
        a = lbv[0:1]
        b = lbv[1:2] - _softplus(-pf)
        logf = jnp.maximum(a, b) + jnp.log1p(jnp.exp(-jnp.abs(a - b)))
        k = lbv[2:3] * _sigmoid(-pf)
        if n_valid < c:
            logf = jnp.where(rows < n_valid, logf, 0.0)
            k = jnp.where(rows < n_valid, k, 0.0)
        qs.append(_silu(p[:, 0:w]))
        ks.append(k)
        vs.append(p[:, 2 * w:2 * w + HG_W])
        bcs.append(_mdot(tri, logf))
    st_old = [st_scr[s, h] for s, h in items]
    qes = [q * jnp.exp(bc) for q, bc in zip(qs, bcs)]
    os_ = [_dot3(hs(qes[s], h), st, 'nt') for (s, h), st in zip(items, st_old)]
    for delta in range(sb):
        prods, v_ss = [], []
        for s in slots:
            if delta == 0:
                k_s, b_s, v_s = ks[s], bcs[s], vs[s]
            else:
                k_s, b_s, v_s = (pltpu.roll(t, delta, 0) for t in (ks[s], bcs[s], vs[s]))
            prods.append(jnp.where(row_in_sub >= delta,
                                   qs[s] * k_s * jnp.exp(jnp.minimum(bcs[s] - b_s, 0.0)), 0.0))
            v_ss.append(v_s)
        os_ = [o + jnp.sum(hs(prods[s], h), axis=-1, keepdims=True) * hs(v_ss[s], h)
               for (s, h), o in zip(items, os_)]
    if c > sb:
        parts = [[jnp.zeros((sb, HG_DV), F32)] for _ in items]
        for r0 in range(sb, c, sb):
            qis, kps = [], []
            for s in slots:
                ref = bcs[s][r0 - 1:r0]
                qis.append(qs[s][r0:r0 + sb] * jnp.exp(bcs[s][r0:r0 + sb] - ref))
                kps.append(ks[s][0:r0] * jnp.exp(ref - bcs[s][0:r0]))
            att = [_dot3(hs(qis[s], h), hs(kps[s], h), 'nt') for s, h in items]
            for i, (s, h) in enumerate(items):
                parts[i].append(_dot3(att[i], hs(vs[s], h)[0:r0]))
        os_ = [o + jnp.concatenate(p, axis=0) for o, p in zip(os_, parts)]
    blasts = [bc[c - 1:c] for bc in bcs]
    kds = [k * jnp.exp(bl - bc) for k, bl, bc in zip(ks, blasts, bcs)]
    ebs = [jnp.exp(bl) for bl in blasts]
    st_new = [st * hs(ebs[s], h) + _dot3(hs(vs[s], h), hs(kds[s], h), 'tn') for (s, h), st in zip(items, st_old)]
    os_ = [o * lax.rsqrt(jnp.mean(o * o, axis=-1, keepdims=True) + NORM_EPS) for o in os_]
    ng = ng_ref[...]
    for s in slots:
        gate = _sigmoid(p_ref[0, s, 0][:, 2 * w + HG_W:2 * w + 2 * HG_W])
        o_all = jnp.concatenate(os_[s * HG_HEADS:(s + 1) * HG_HEADS], axis=1) * ng * gate
        o_ref[0, s, 0] = o_all.astype(o_ref.dtype)
    for i, (s, h) in enumerate(items):
        st_scr[s, h] = st_new[i]

    @pl.when(ci == pl.num_programs(1) - 1)
    def _():
        for s, h in items:
            s1_ref[s, h] = jnp.transpose(st_scr[s, h])


def _hgrn_group(phg, lb, lp, bsz, lpad, n_valid_len, s0):
    c, nc, nb = _slot_plan(bsz, lpad)
    n_valid = c if nc > 1 else n_valid_len
    lbv = jnp.zeros((SUBLANES, HG_HEADS * HG_DF), F32)
    lbv = lbv.at[0].set(jnp.log(jnp.maximum(lb, LB_FLOOR))).at[1].set(jnp.log1p(-lb)).at[2].set(1.0 - lb)
    ng = lp['hgrn_norm_g'].reshape(1, HG_W)
    full = lambda a: pl.BlockSpec(a.shape, lambda b, j: (0,) * a.ndim)
    sspec = pl.BlockSpec((nb, HG_HEADS, HG_DF, HG_DV), lambda b, j: (b, 0, 0, 0))
    o, s1 = pl.pallas_call(
        functools.partial(_hgrn_kernel, c=c, nb=nb, n_valid=n_valid),
        out_shape=[jax.ShapeDtypeStruct((bsz // nb, nb, nc, c, HG_W), BF16),
                   jax.ShapeDtypeStruct((bsz, HG_HEADS, HG_DF, HG_DV), F32)],
        grid=(bsz // nb, nc),
        in_specs=[_slot_spec(nb, c, HG_COLS), full(lbv), full(ng), sspec],
        out_specs=[_slot_spec(nb, c, HG_W), sspec],
        scratch_shapes=[pltpu.VMEM((nb, HG_HEADS, HG_DV, HG_DF), F32)],
        compiler_params=_cparams(("parallel", "arbitrary")),
        name="hgrn_chunk",
    )(_slot_view(phg, bsz, nb, nc, c), lbv, ng, s0)
    return o.reshape(bsz * lpad, HG_W), s1


def _ret_kernel(p_ref, cos_ref, sin_ref, qd_ref, kd_ref, dec_ref, dm_ref, gn_ref, s0_ref,
                o_ref, s1_ref, s_scr, *, c, nb):
    ci = pl.program_id(1)
    slots = range(nb)
    heads = range(RT_HEADS)
    items = [(s, h) for s in slots for h in heads]

    @pl.when(ci == 0)
    def _():
        for s in slots:
            s_scr[s] = jnp.zeros((RT_QK, RT_W), F32)
            for h in heads:
                s_scr[s, h * RT_DK:(h + 1) * RT_DK, h * RT_DV:(h + 1) * RT_DV] = s0_ref[s, h]

    cos = cos_ref[...]
    sin = sin_ref[...]
    lane = _iota((c, RT_QK), 1)
    first_half = (lane % RT_DK) < (RT_DK // 2)

    def rope(x):
        partner = jnp.where(first_half, pltpu.roll(x, RT_QK - RT_DK // 2, 1), pltpu.roll(x, RT_DK // 2, 1))
        return x * cos + partner * sin

    ps = [p_ref[0, s, 0] for s in slots]
    qs = [rope(p[:, 0:RT_QK]) for p in ps]
    ks = [rope(p[:, RT_QK:2 * RT_QK]) * (RT_DK ** -0.5) for p in ps]
    pvs = [p[:, 2 * RT_QK:2 * RT_QK + RT_W] for p in ps]
    sblks = [s_scr[s] for s in slots]
    qd = qd_ref[...]
    o_inters = [_dot3(q * qd, sblk) for q, sblk in zip(qs, sblks)]
    qks = [_dot3(jnp.concatenate([jnp.where(lane // RT_DK == h, q, 0.0) for h in heads], axis=0), k, 'nt')
           for q, k in zip(qs, ks)]
    attns = [qks[s][h * c:(h + 1) * c] * dec_ref[h] for s, h in items]
    os_ = [o_inters[s][:, h * RT_DV:(h + 1) * RT_DV] + _dot3(attn, pvs[s][:, h * RT_DV:(h + 1) * RT_DV])
           for (s, h), attn in zip(items, attns)]
    dm = dm_ref[...]
    kd = kd_ref[...]
    s_new = [sblk * dm + jnp.where(dm > 0.0, _dot3(k * kd, pv, 'tn'), 0.0) for sblk, k, pv in zip(sblks, ks, pvs)]
    gn = gn_ref[...]
    normed = []
    for o in os_:
        xc = o - jnp.mean(o, axis=-1, keepdims=True)
        normed.append(xc * lax.rsqrt(jnp.mean(xc * xc, axis=-1, keepdims=True) + NORM_EPS))
    for s in slots:
        gate = _silu(ps[s][:, 2 * RT_QK + RT_W:2 * RT_QK + 2 * RT_W])
        o_all = jnp.concatenate(normed[s * RT_HEADS:(s + 1) * RT_HEADS], axis=1) * gn * gate
        o_ref[0, s, 0] = o_all.astype(o_ref.dtype)
        s_scr[s] = s_new[s]

    @pl.when(ci == pl.num_programs(1) - 1)
    def _():
        for s, h in items:
            s1_ref[s, h] = s_scr[s, h * RT_DK:(h + 1) * RT_DK, h * RT_DV:(h + 1) * RT_DV]


def _ret_group(prt, lp, bsz, lpad, n_valid_len, pos0, s0):
    c, nc, nb = _slot_plan(bsz, lpad)
    n_valid = c if nc > 1 else n_valid_len
    half = RT_DK // 2
    inv = ROPE_BASE ** (-np.arange(half, dtype=np.float64) / half)
    ang = (pos0 + np.arange(lpad, dtype=np.float64))[:, None] * inv[None, :]
    cos = np.tile(np.cos(ang), (1, 2 * RT_HEADS))
    sin = np.tile(np.concatenate([-np.sin(ang), np.sin(ang)], axis=1), (1, RT_HEADS))
    loggamma = np.log(1.0 - np.exp2(-5.0 - np.arange(RT_HEADS, dtype=np.float64)))
    gcum = loggamma[:, None] * np.arange(1, c + 1, dtype=np.float64)[None, :]
    idx = np.arange(c)
    dec = np.where(idx[:, None] >= idx[None, :], np.exp(gcum[:, :, None] - gcum[:, None, :]), 0.0)
    qd = np.repeat(np.exp(gcum).T, RT_DK, axis=1)
    kdec = np.where(idx[None, :] < n_valid, np.exp(gcum[:, n_valid - 1:n_valid] - gcum), 0.0)
    kd = np.repeat(kdec.T, RT_DK, axis=1)
    sdec = np.exp(gcum[:, n_valid - 1])
    dm = np.zeros((RT_QK, RT_W))
    for h in range(RT_HEADS):
        dm[h * RT_DK:(h + 1) * RT_DK, h * RT_DV:(h + 1) * RT_DV] = sdec[h]
    cos, sin, qd, kd, dec, dm = (jnp.asarray(a, F32) for a in (cos, sin, qd, kd, dec, dm))
    gn = lp['ret_gn_g'].reshape(1, RT_W)
    full = lambda a: pl.BlockSpec(a.shape, lambda b, j: (0,) * a.ndim)
    posspec = pl.BlockSpec((c, RT_QK), lambda b, j: (j, 0))
    sspec = pl.BlockSpec((nb, RT_HEADS, RT_DK, RT_DV), lambda b, j: (b, 0, 0, 0))
    o, s1 = pl.pallas_call(
        functools.partial(_ret_kernel, c=c, nb=nb),
        out_shape=[jax.ShapeDtypeStruct((bsz // nb, nb, nc, c, RT_W), BF16),
                   jax.ShapeDtypeStruct((bsz, RT_HEADS, RT_DK, RT_DV), F32)],
        grid=(bsz // nb, nc),
        in_specs=[_slot_spec(nb, c, RT_COLS), posspec, posspec,
                  full(qd), full(kd), full(dec), full(dm), full(gn), sspec],
        out_specs=[_slot_spec(nb, c, RT_W), sspec],
        scratch_shapes=[pltpu.VMEM((nb, RT_QK, RT_W), F32)],
        compiler_params=_cparams(("parallel", "arbitrary")),
        name="ret_chunk",
    )(_slot_view(prt, bsz, nb, nc, c), cos, sin, qd, kd, dec, dm, gn, s0)
    return o.reshape(bsz * lpad, RT_W), s1


def _merge_kernel(x_ref, o1, o2, o3, o4, gate_ref, w1, w2, w3, w4, wo_ref, out_ref, *, d):
    acc = None
    for i, (o, w) in enumerate(((o1, w1), (o2, w2), (o3, w3), (o4, w4))):
        term = _sigmoid(gate_ref[:, i * d:(i + 1) * d]) * _dot(o[...], w[...])
        acc = term if acc is None else acc + term
    out_ref[...] = x_ref[...] + _dot(acc.astype(BF16), wo_ref[...])


def _merge(x, outs, gate, wouts, wo):
    n, d = x.shape
    tm = _pick(n, (256, 128, 64, 32, 16))
    row = lambda w: pl.BlockSpec((tm, w), lambda i: (i, 0))
    full = lambda a: pl.BlockSpec(a.shape, lambda i: (0,) * a.ndim)
    return pl.pallas_call(
        functools.partial(_merge_kernel, d=d),
        out_shape=jax.ShapeDtypeStruct((n, d), F32),
        grid=(n // tm,),
        in_specs=[row(d)] + [row(o.shape[1]) for o in outs] + [row(N_BRANCH * d)]
        + [full(w) for w in wouts] + [full(wo)],
        out_specs=row(d),
        compiler_params=_cparams(("parallel",)),
        name="merge",
    )(x, *outs, gate, *wouts, wo)


def _route_kernel(x_ref, g_ref, rw_ref, rb_ref, xn_ref, col_ref, row_ref, cnt_ref):
    tm = x_ref.shape[0]
    lane = _iota((tm, LANES), 1)
    lanef = lane.astype(F32)
    x = x_ref[...]
    xn = x * lax.rsqrt(jnp.mean(x * x, axis=-1, keepdims=True) + NORM_EPS) * g_ref[...]
    xn_ref[...] = xn.astype(BF16)
    rb = rb_ref[...]
    neg = jnp.float32(-jnp.inf)
    logits = _dot3(xn, rw_ref[...])
    glog = jnp.where(lane < N_GROUPS, logits[:, 0:LANES] + rb[0:1], neg)
    gmax = jnp.max(glog, axis=-1, keepdims=True)
    gsum = jnp.sum(jnp.exp(glog - gmax), axis=-1, keepdims=True)
    gidx = jnp.min(jnp.where(glog == gmax, lanef, float(LANES)), axis=-1, keepdims=True)
    gp = 1.0 / gsum
    in_group = (lanef >= gidx * EXPERTS_PER_GROUP) & (lanef < (gidx + 1.0) * EXPERTS_PER_GROUP)
    elog = jnp.where(in_group, logits[:, LANES:2 * LANES] + rb[1:2], neg)
    emax = jnp.max(elog, axis=-1, keepdims=True)
    eexp = jnp.exp(elog - emax)
    ep = eexp / jnp.sum(eexp, axis=-1, keepdims=True)
    ep = jnp.where(in_group, ep, -1.0)
    p1 = jnp.max(ep, axis=-1, keepdims=True)
    i1 = jnp.min(jnp.where(ep == p1, lanef, float(LANES)), axis=-1, keepdims=True)
    ep2 = jnp.where(lanef == i1, -1.0, ep)
    p2 = jnp.max(ep2, axis=-1, keepdims=True)
    i2 = jnp.min(jnp.where(ep2 == p2, lanef, float(LANES)), axis=-1, keepdims=True)
    denom = p1 + p2
    wt1 = gp * p1 / denom
    wt2 = gp * p2 / denom

    onehot = ((lanef == i1) | (lanef == i2)).astype(F32)
    cnt = jnp.sum(onehot, axis=0, keepdims=True)
    before = (_iota((LANES, LANES), 0) < _iota((LANES, LANES), 1)).astype(BF16)
    off = _segsum(jnp.broadcast_to(cnt, (SUBLANES, LANES)), before)[0:1]
    tri = (_iota((tm, tm), 0) >= _iota((tm, tm), 1)).astype(BF16)
    slot = off + _dot(tri, onehot.astype(BF16)) - 1.0
    pos1 = jnp.sum(jnp.where(lanef == i1, slot, 0.0), axis=-1, keepdims=True)
    pos2 = jnp.sum(jnp.where(lanef == i2, slot, 0.0), axis=-1, keepdims=True)
    col = jnp.where(lane == 0, pos1, jnp.where(lane == 1, pos2, jnp.where(lane == 2, wt1,
                                                                         jnp.where(lane == 3, wt2, 0.0))))
    col_ref[...] = col
    row_ref[0] = jnp.transpose(col)[0:SUBLANES]
    cnt_ref[0] = jnp.concatenate([cnt, off, jnp.zeros((SUBLANES - 2, LANES), F32)], axis=0)


def _moe_kernel(cnt_sm, off_sm, x_ref, xn_ref, col_ref, row_ref, wg_ref, wu_ref, wd_ref, out_ref,
                xs_scr, ys_scr, ws_scr, *, pb, rb):
    i = pl.program_id(0)
    e = pl.program_id(1)
    tm, d = x_ref.shape
    ns = 2 * tm

    @pl.when(e == 0)
    def _():
        rowd = row_ref[0]
        pos1, pos2, wt1, wt2 = rowd[0:1], rowd[1:2], rowd[2:3], rowd[3:4]
        for blk in range(ns // pb):
            sid = (_iota((pb, tm), 0) + blk * pb).astype(F32)
            m1 = sid == pos1
            m2 = sid == pos2
            xs_scr[blk * pb:(blk + 1) * pb, :] = _dot((m1 | m2).astype(BF16), xn_ref[...]).astype(BF16)
            wsl = jnp.sum(jnp.where(m1, wt1, 0.0) + jnp.where(m2, wt2, 0.0), axis=-1, keepdims=True)
            ws_scr[blk * pb:(blk + 1) * pb, :] = jnp.broadcast_to(wsl, (pb, LANES))
        ys_scr[...] = jnp.zeros_like(ys_scr)

    cnt = cnt_sm[i, e]
    off = off_sm[i, e]

    @pl.when(cnt > 0)
    def _():
        start = (off // BF16_ROWS) * BF16_ROWS

        def body(j, carry):
            own = start + j * rb
            r0 = pl.multiple_of(jnp.minimum(own, ns - rb), BF16_ROWS)
            xb = xs_scr[pl.ds(r0, rb), :]
            hid = _silu(_dot(xb, wg_ref[0])) * _dot(xb, wu_ref[0])
            y = _dot(hid.astype(BF16), wd_ref[0])
            srow = r0 + _iota((rb, d), 0)
            mine = (srow >= jnp.maximum(off, own)) & (srow < off + cnt)
            ys_scr[pl.ds(r0, rb), :] += jnp.where(mine, y, 0.0)
            return carry

        lax.fori_loop(0, (off + cnt - start + rb - 1) // rb, body, 0)

    @pl.when(e == pl.num_programs(1) - 1)
    def _():
        col = col_ref[...]
        pos1, pos2 = col[:, 0:1], col[:, 1:2]
        acc = x_ref[...]
        for blk in range(ns // pb):
            sid = (_iota((tm, pb), 1) + blk * pb).astype(F32)
            pt = ((sid == pos1) | (sid == pos2)).astype(BF16)
            ysw = ys_scr[blk * pb:(blk + 1) * pb, :] * ws_scr[blk * pb:(blk + 1) * pb, 0:1]
            hi, lo = _split2(ysw)
            acc = acc + _dot(pt, hi) + _dot(pt, lo)
        out_ref[...] = acc


def _moe(x, lp, experts):
    n, d = x.shape
    tm = _pick(n, (MOE_TILE, 512, 256, 128, 64))
    nt = n // tm
    de = experts[0].shape[-1]
    rw = jnp.zeros((d, 2 * LANES), F32)
    rw = rw.at[:, :N_GROUPS].set(lp['router_group']).at[:, LANES:LANES + N_EXPERTS].set(lp['router_expert'])
    rbias = jnp.zeros((SUBLANES, LANES), F32)
    rbias = rbias.at[0, :N_GROUPS].set(lp['router_group_b']).at[1, :N_EXPERTS].set(lp['router_expert_b'])
    g = lp['norm2_g'].reshape(1, d)
    full1 = lambda a: pl.BlockSpec(a.shape, lambda i: (0,) * a.ndim)
    xn, col, row, cnt = pl.pallas_call(
        _route_kernel,
        out_shape=[jax.ShapeDtypeStruct((n, d), BF16), jax.ShapeDtypeStruct((n, LANES), F32),
                   jax.ShapeDtypeStruct((nt, SUBLANES, tm), F32), jax.ShapeDtypeStruct((nt, SUBLANES, LANES), F32)],
        grid=(nt,),
        in_specs=[pl.BlockSpec((tm, d), lambda i: (i, 0)), full1(g), full1(rw), full1(rbias)],
        out_specs=[pl.BlockSpec((tm, d), lambda i: (i, 0)), pl.BlockSpec((tm, LANES), lambda i: (i, 0)),
                   pl.BlockSpec((1, SUBLANES, tm), lambda i: (i, 0, 0)),
                   pl.BlockSpec((1, SUBLANES, LANES), lambda i: (i, 0, 0))],
        compiler_params=_cparams(("parallel",)),
        name="moe_route",
    )(x, g, rw, rbias)
    cnt_i = cnt[:, 0, :N_EXPERTS].astype(jnp.int32)
    off_i = cnt[:, 1, :N_EXPERTS].astype(jnp.int32)
    ns = 2 * tm
    pb = min(MOE_SLOT_BLOCK, ns)
    rb = min(MOE_ROW_BLOCK, ns)
    grid_spec = pltpu.PrefetchScalarGridSpec(
        num_scalar_prefetch=2,
        grid=(nt, N_EXPERTS),
        in_specs=[pl.BlockSpec((tm, d), lambda i, e, c, o: (i, 0)),
                  pl.BlockSpec((tm, d), lambda i, e, c, o: (i, 0)),
                  pl.BlockSpec((tm, LANES), lambda i, e, c, o: (i, 0)),
                  pl.BlockSpec((1, SUBLANES, tm), lambda i, e, c, o: (i, 0, 0)),
                  pl.BlockSpec((1, d, de), lambda i, e, c, o: (e, 0, 0)),
                  pl.BlockSpec((1, d, de), lambda i, e, c, o: (e, 0, 0)),
                  pl.BlockSpec((1, de, d), lambda i, e, c, o: (e, 0, 0))],
        out_specs=pl.BlockSpec((tm, d), lambda i, e, c, o: (i, 0)),
        scratch_shapes=[pltpu.VMEM((ns, d), BF16), pltpu.VMEM((ns, d), F32), pltpu.VMEM((ns, LANES), F32)],
    )
    return pl.pallas_call(
        functools.partial(_moe_kernel, pb=pb, rb=rb),
        out_shape=jax.ShapeDtypeStruct((n, d), F32),
        grid_spec=grid_spec,
        compiler_params=pltpu.CompilerParams(dimension_semantics=("parallel", "arbitrary"),
                                             vmem_limit_bytes=MOE_VMEM_LIMIT),
        name="moe",
    )(cnt_i, off_i, x, xn, col, row, *experts)


def _layer(x, wts, lp, lb, gr):
    b, seq, lpad = gr['bsz'], gr['seq'], gr['lpad']
    shift0, wkv0, conv0, gdn0, hgrn0, ret0 = gr['states']
    xn = _rmsnorm(x, lp['norm1_g'], BF16)
    p_rw, p_gqkv, p_gz, p_gba, p_hg, p_rt, p_gate = (
        _matmul(xn, w, f"proj{i}") for i, w in enumerate(wts['proj']))
    o_rw, shift1, wkv1 = _wkv_group(p_rw, lp, b, lpad, seq, shift0, wkv0)
    o_gd, gdn1, conv1 = _gdn_group(p_gqkv, p_gz, p_gba, lp, b, lpad, seq, conv0, gdn0)
    o_hg, hgrn1 = _hgrn_group(p_hg, lb, lp, b, lpad, seq, hgrn0)
    o_rt, ret1 = _ret_group(p_rt, lp, b, lpad, seq, gr['pos0'], ret0)
    x = _merge(x, [o_rw, o_gd, o_hg, o_rt], p_gate, wts['out'], wts['w_o'])
    x = _moe(x, lp, wts['experts'])
    return x, (shift1, wkv1, conv1, gdn1, hgrn1, ret1)


def _layer_weights(lp, d):
    offs = np.cumsum([0, RW_COLS, GD_QKV, GD_W, 2 * GD_HEADS, HG_COLS, RT_COLS, N_BRANCH * d])
    seg = [lp['w_in'][:, offs[i]:offs[i + 1]] for i in range(7)]
    seg[3] = jnp.pad(seg[3], ((0, 0), (0, LANES - 2 * GD_HEADS)))
    return dict(
        proj=[s.astype(BF16) for s in seg],
        out=[lp[n].astype(BF16) for n in ('w_out_rwkv', 'w_out_gdn', 'w_out_hgrn', 'w_out_ret')],
        w_o=lp['w_o'].astype(BF16),
        experts=[lp[n].astype(BF16) for n in ('moe_w_gate', 'moe_w_up', 'moe_w_down')])


def kernel(x_prompt, x_sample, state_rwkv_shift, state_rwkv_wkv, state_gdn_conv, state_gdn, state_hgrn, state_ret, norm1_g, w_in, rwkv_mu, rwkv_w0, rwkv_w2, rwkv_a0, rwkv_a2, rwkv_g2, rwkv_k_k, rwkv_k_a, rwkv_r_k, rwkv_ln_g, rwkv_ln_b, w_out_rwkv, gdn_conv, gdn_a_log, gdn_dt_bias, gdn_norm_g, w_out_gdn, hgrn_lb_logits, hgrn_norm_g, w_out_hgrn, ret_gn_g, w_out_ret, w_o, norm2_g, router_group, router_group_b, router_expert, router_expert_b, moe_w_gate, moe_w_up, moe_w_down, final_norm_g):
    params = dict(norm1_g=norm1_g, w_in=w_in, rwkv_mu=rwkv_mu, rwkv_w0=rwkv_w0, rwkv_w2=rwkv_w2,
                  rwkv_a0=rwkv_a0, rwkv_a2=rwkv_a2, rwkv_g2=rwkv_g2, rwkv_k_k=rwkv_k_k,
                  rwkv_k_a=rwkv_k_a, rwkv_r_k=rwkv_r_k, rwkv_ln_g=rwkv_ln_g, rwkv_ln_b=rwkv_ln_b,
                  w_out_rwkv=w_out_rwkv, gdn_conv=gdn_conv, gdn_a_log=gdn_a_log,
                  gdn_dt_bias=gdn_dt_bias, gdn_norm_g=gdn_norm_g, w_out_gdn=w_out_gdn,
                  hgrn_norm_g=hgrn_norm_g, w_out_hgrn=w_out_hgrn, ret_gn_g=ret_gn_g,
                  w_out_ret=w_out_ret, w_o=w_o, norm2_g=norm2_g, router_group=router_group,
                  router_group_b=router_group_b, router_expert=router_expert,
                  router_expert_b=router_expert_b, moe_w_gate=moe_w_gate, moe_w_up=moe_w_up,
                  moe_w_down=moe_w_down)
    depth = w_in.shape[0]
    bp, lp_len, d = x_prompt.shape
    bs, ls, _ = x_sample.shape
    ls_pad = -(-ls // SAMPLE_PAD_LEN) * SAMPLE_PAD_LEN
    sm = jax.nn.softmax(hgrn_lb_logits.astype(F32), axis=0)
    lower_bounds = jnp.cumsum(sm, axis=0) - sm[0]

    sample_states = (state_rwkv_shift, state_rwkv_wkv, state_gdn_conv, state_gdn, state_hgrn, state_ret)
    xp = x_prompt.reshape(bp * lp_len, d)
    xs = jnp.pad(x_sample, ((0, 0), (0, ls_pad - ls), (0, 0))).reshape(bs * ls_pad, d)

    prompt_out = [[] for _ in sample_states]
    sample_out = [[] for _ in sample_states]
    for layer in range(depth):
        lpar = {name: arr[layer] for name, arr in params.items()}
        wts = _layer_weights(lpar, d)
        prompt = dict(bsz=bp, seq=lp_len, lpad=lp_len, pos0=0,
                      states=tuple(jnp.zeros((bp,) + s.shape[2:], F32) for s in sample_states))
        sample = dict(bsz=bs, seq=ls, lpad=ls_pad, pos0=PAST_LEN,
                      states=tuple(s[layer].astype(F32) for s in sample_states))
        xp, new_p = _layer(xp, wts, lpar, lower_bounds[layer], prompt)
        xs, new_s = _layer(xs, wts, lpar, lower_bounds[layer], sample)
        for lst, n in zip(prompt_out, new_p):
            lst.append(n)
        for lst, n in zip(sample_out, new_s):
            lst.append(n)

    y_prompt = _rmsnorm(xp, final_norm_g, F32).reshape(bp, lp_len, d)
    y_sample = _rmsnorm(xs, final_norm_g, F32).reshape(bs, ls_pad, d)[:, :ls]
    p_states = [jnp.stack(lst) for lst in prompt_out]
    s_states = [jnp.stack(lst).astype(o.dtype) for lst, o in zip(sample_out, sample_states)]
    return (y_prompt, y_sample, *p_states, *s_states)
```

```python
import functools
import math

import numpy as np
import jax
import jax.numpy as jnp
from jax import lax
from jax.experimental import pallas as pl
from jax.experimental.pallas import tpu as pltpu

F32 = jnp.float32
BF16 = jnp.bfloat16

NORM_EPS = 1e-6
LB_FLOOR = 1e-30
PAST_LEN = 16384
RW_HEADS = 8
RW_HD = 64
RW_W = RW_HEADS * RW_HD
RW_DECAY_LORA = 64
RW_AAA_LORA = 64
RW_GATE_LORA = 128
RW_COLS = 3 * RW_W + RW_DECAY_LORA + RW_AAA_LORA + RW_GATE_LORA
RW_GN_EPS = 64e-5
GD_HEADS = 4
GD_DK = 128
GD_DV = 128
GD_QKV = GD_HEADS * (2 * GD_DK + GD_DV)
GD_W = GD_HEADS * GD_DV
CONV_W = 4
HG_HEADS = 4
HG_DF = 128
HG_DV = 128
HG_W = HG_HEADS * HG_DV
HG_COLS = 2 * HG_HEADS * HG_DF + 2 * HG_W
HG_SUB = 16
RT_HEADS = 4
RT_DK = 64
RT_DV = 128
RT_W = RT_HEADS * RT_DV
RT_QK = RT_HEADS * RT_DK
RT_COLS = 2 * RT_QK + 2 * RT_W
ROPE_BASE = 10000.0
N_BRANCH = 4
N_GROUPS = 4
EXPERTS_PER_GROUP = 8
N_EXPERTS = N_GROUPS * EXPERTS_PER_GROUP

LANES = 128
SUBLANES = 8
SAMPLE_PAD_LEN = 8
CHUNK_ROWS = 64
SLOTS_MIN = 4
WKV_SLOTS = 2
VMEM_LIMIT = 48 * 1024 * 1024
MOE_TILE = 1024
MOE_SLOT_BLOCK = 256
MOE_ROW_BLOCK = 128
MOE_EXPERTS_PER_STEP = 2
BF16_ROWS = 16
MOE_VMEM_LIMIT = 56 * 1024 * 1024


def _pick(n, cands):
    for c in cands:
        if n % c == 0:
            return c
    raise ValueError(f"no tile for {n} in {cands}")


def _cparams(sem):
    return pltpu.CompilerParams(dimension_semantics=sem, vmem_limit_bytes=VMEM_LIMIT)


def _dot(a, b):
    return lax.dot_general(a, b, (((1,), (0,)), ((), ())), preferred_element_type=F32)


def _softplus(x):
    return jnp.maximum(x, 0.0) + jnp.log1p(jnp.exp(-jnp.abs(x)))


def _sigmoid(x):
    return jax.nn.sigmoid(x)


def _silu(x):
    return x * jax.nn.sigmoid(x)


def _segsum(x, hm):
    hi = x.astype(BF16)
    r1 = x - hi.astype(F32)
    mid = r1.astype(BF16)
    lo = (r1 - mid.astype(F32)).astype(BF16)
    return _dot(hi, hm) + _dot(mid, hm) + _dot(lo, hm)


_DIMS = {'nn': (((1,), (0,)), ((), ())), 'nt': (((1,), (1,)), ((), ())), 'tn': (((0,), (0,)), ((), ()))}


def _split2(x):
    hi = x.astype(BF16)
    return hi, (x - hi.astype(F32)).astype(BF16)


def _dot3(a, b, form='nn'):
    ah, al = _split2(a)
    bh, bl = _split2(b)
    f = lambda x, y: lax.dot_general(x, y, _DIMS[form], preferred_element_type=F32)
    free = 1 if form == 'tn' else 0
    m = a.shape[free]
    both = f(jnp.concatenate([ah, al], axis=free), bh)
    return both[0:m] + both[m:2 * m] + f(ah, bl)


def _mdot(mask, x):
    hi = x.astype(BF16)
    r1 = x - hi.astype(F32)
    mid = r1.astype(BF16)
    lo = (r1 - mid.astype(F32)).astype(BF16)
    return _dot(mask, hi) + _dot(mask, mid) + _dot(mask, lo)


def _unit_lower_inverses(ms, rid, cid, c, expand=None):
    prod = _dot3 if expand is None else (lambda x, y: _dot3(x, expand(y)))
    same = lambda s: (rid // s) == (cid // s)
    ns = [jnp.where(same(SUBLANES), -m, 0.0) for m in ms]
    n2s = [prod(n, n) for n in ns]
    n4s = [prod(n2, n2) for n2 in n2s]
    eye = (rid == cid).astype(F32)
    ps = [eye + n for n in ns]
    ps = [p + prod(p, n2) for p, n2 in zip(ps, n2s)]
    ps = [p + prod(p, n4) for p, n4 in zip(ps, n4s)]
    s = SUBLANES
    while s < c:
        offs = [jnp.where(same(2 * s) & jnp.logical_not(same(s)), m, 0.0) for m in ms]
        ts = [prod(p, off) for p, off in zip(ps, offs)]
        ps = [p - prod(t, p) for p, t in zip(ps, ts)]
        s *= 2
    return ps


def _iota(shape, dim):
    return lax.broadcasted_iota(jnp.int32, shape, dim)


def _rms_kernel(x_ref, g_ref, o_ref):
    x = x_ref[...]
    ms = jnp.mean(x * x, axis=-1, keepdims=True)
    o_ref[...] = (x * lax.rsqrt(ms + NORM_EPS) * g_ref[...]).astype(o_ref.dtype)


def _rmsnorm(x, g, out_dtype):
    n, d = x.shape
    tm = _pick(n, (1024, 512, 256, 128, 64, 32, 16))
    return pl.pallas_call(
        _rms_kernel,
        out_shape=jax.ShapeDtypeStruct((n, d), out_dtype),
        grid=(n // tm,),
        in_specs=[pl.BlockSpec((tm, d), lambda i: (i, 0)), pl.BlockSpec((1, d), lambda i: (0, 0))],
        out_specs=pl.BlockSpec((tm, d), lambda i: (i, 0)),
        compiler_params=_cparams(("parallel",)),
        name="rmsnorm",
    )(x, g.reshape(1, d))


def _mm_kernel(x_ref, w_ref, o_ref):
    o_ref[...] = _dot(x_ref[...], w_ref[...])


def _matmul(x, w, name):
    n, k = x.shape
    m = w.shape[1]
    tm = _pick(n, (2048, 1024, 512, 256, 128, 64, 32, 16))
    tn = _pick(m, (1024, 896, 768, 512, 256, 128))
    return pl.pallas_call(
        _mm_kernel,
        out_shape=jax.ShapeDtypeStruct((n, m), F32),
        grid=(n // tm, m // tn),
        in_specs=[pl.BlockSpec((tm, k), lambda i, j: (i, 0)), pl.BlockSpec((k, tn), lambda i, j: (0, j))],
        out_specs=pl.BlockSpec((tm, tn), lambda i, j: (i, j)),
        compiler_params=_cparams(("parallel", "parallel")),
        name=name,
    )(x, w)


def _head_sum_matrix(width, seg):
    i = np.arange(width)
    return jnp.asarray((i[:, None] // seg) == (i[None, :] // seg), BF16)


def _wkv_chunk_kernel(p_ref, first_ref, mu_ref, vec_ref, w2_ref, a2_ref, g2_ref, hm_ref, ln_ref, s0_ref,
                      o_ref, s1_ref, shift_ref, carry_scr, ht_scr, *, cs, nseq, ns, n_valid, multi_chunk):
    ci = pl.program_id(1)
    rows = cs * nseq
    npair = RW_HEADS // 2
    f_zero = jnp.zeros((RW_HD, RW_HD), F32)
    slots = range(ns)
    pairs = range(npair)
    seqs = [(s, q) for s in slots for q in range(nseq)]
    sidx = lambda s, q: s * nseq + q

    @pl.when(ci == 0)
    def _():
        for s, q in seqs:
            carry_scr[sidx(s, q)] = first_ref[sidx(s, q)]
            for pr in pairs:
                top = jnp.concatenate([s0_ref[sidx(s, q), 2 * pr], f_zero], axis=1)
                bot = jnp.concatenate([f_zero, s0_ref[sidx(s, q), 2 * pr + 1]], axis=1)
                ht_scr[sidx(s, q), pr] = jnp.concatenate([top, bot], axis=0)

    vec = vec_ref[...]
    w0, a0, k_k, k_a, r_k = vec[0:1], vec[1:2], vec[2:3], vec[3:4], vec[4:5]
    hm = hm_ref[...]
    mu = mu_ref[...]
    row_w = _iota((rows, RW_COLS), 0)
    row_f = _iota((rows, RW_W), 0)
    rid = _iota((rows, rows), 0)
    cid = _iota((rows, rows), 1)
    tri = ((rid >= cid) & (rid // cs == cid // cs)).astype(BF16)

    ps_, vs_, gs_, bonus_ = [], [], [], []
    abar, bbar, kbar, rbar, btil, ktil, w_last = [], [], [], [], [], [], []
    for s in slots:
        p = p_ref[0, s, 0]
        prev = pltpu.roll(p, 1, 0)
        for q in range(nseq):
            prev = jnp.where(row_w == q * cs, carry_scr[sidx(s, q)], prev)
        h = p + (prev - p) * mu
        r = h[:, 0:RW_W]
        k = h[:, RW_W:2 * RW_W]
        v = h[:, 2 * RW_W:3 * RW_W]
        lo = h[:, 3 * RW_W:3 * RW_W + LANES]
        gl = h[:, 3 * RW_W + LANES:3 * RW_W + 2 * LANES]
        w_log = -_softplus(-(w0 + _dot3(jnp.tanh(lo), w2_ref[...]))) - 0.5
        logw = -jnp.exp(w_log)
        a = _sigmoid(a0 + _dot3(lo, a2_ref[...]))
        gs_.append(_dot3(_sigmoid(gl), g2_ref[...]))
        kk = k * k_k
        kk = kk * lax.rsqrt(_segsum(kk * kk, hm) + NORM_EPS)
        km = k * (1.0 + (a - 1.0) * k_a)
        bb = kk * a
        alpha = -kk
        bonus_.append(_segsum(r * km * r_k, hm) * v)
        if n_valid < cs:
            valid = (row_f % cs) < n_valid
            logw, alpha, bb, km = (jnp.where(valid, t, 0.0) for t in (logw, alpha, bb, km))
        gcum = _mdot(tri, logw)
        g_last = jnp.concatenate(
            [jnp.broadcast_to(gcum[q * cs + cs - 1:q * cs + cs], (cs, RW_W)) for q in range(nseq)], axis=0)
        emg = jnp.exp(-gcum)
        etil = jnp.exp(g_last - gcum)
        ps_.append(p)
        vs_.append(v)
        abar.append(alpha * jnp.exp(gcum - logw))
        bbar.append(bb * emg)
        kbar.append(km * emg)
        rbar.append(r * jnp.exp(gcum))
        btil.append(bb * etil)
        ktil.append(km * etil)
        w_last.append(jnp.exp(g_last))

    prow = _iota((rows, LANES), 0)
    plane = _iota((rows, LANES), 1)
    pcol = plane % RW_HD
    same_seq = (prow // cs) == (pcol // cs)
    strict = (pcol < prow) & same_seq
    incl = (pcol <= prow) & same_seq
    first_head = plane < RW_HD
    blk_mask = (_iota((LANES, LANES), 0) // RW_HD) == (_iota((LANES, LANES), 1) // RW_HD)

    def expand(x):
        return jnp.concatenate([jnp.where(first_head, x, 0.0), jnp.where(first_head, 0.0, x)], axis=0)

    items = [(s, pr) for s in slots for pr in pairs]
    ps = lambda x, pr: x[:, pr * LANES:(pr + 1) * LANES]
    seq_rows = lambda x, q: x[q * cs:(q + 1) * cs]

    lhs = [jnp.concatenate([ps(abar[s], pr), ps(rbar[s], pr)], axis=0) for s, pr in items]
    xb = [_dot3(l, expand(ps(bbar[s], pr)), 'nt') for l, (s, pr) in zip(lhs, items)]
    xk = [_dot3(l, expand(ps(kbar[s], pr)), 'nt') for l, (s, pr) in zip(lhs, items)]
    a_m = [jnp.where(strict, x[0:rows], 0.0) for x in xb]
    rb_m = [jnp.where(incl, x[rows:2 * rows], 0.0) for x in xb]
    b_m = [jnp.where(strict, x[0:rows], 0.0) for x in xk]
    rk_m = [jnp.where(incl, x[rows:2 * rows], 0.0) for x in xk]
    pinv = _unit_lower_inverses([-m for m in a_m], prow, pcol, cs, expand=expand)

    ht_old = {(s, q, pr): ht_scr[sidx(s, q), pr] for s, q in seqs for pr in pairs}
    xh = {(s, q, pr): _dot3(jnp.concatenate([seq_rows(ps(abar[s], pr), q), seq_rows(ps(rbar[s], pr), q)], axis=0),
                            ht_old[(s, q, pr)], 'nt') for s, q in seqs for pr in pairs}
    ah = [jnp.concatenate([xh[(s, q, pr)][0:cs] for q in range(nseq)], axis=0) for s, pr in items]
    rh = [jnp.concatenate([xh[(s, q, pr)][cs:2 * cs] for q in range(nseq)], axis=0) for s, pr in items]
    vexp = [expand(ps(vs_[s], pr)) for s, pr in items]
    rhs = [a_ + _dot3(b_, ve) for a_, b_, ve in zip(ah, b_m, vexp)]
    u = [_dot3(pi, expand(rh_)) for pi, rh_ in zip(pinv, rhs)]
    o_items = [rh_ + _dot3(jnp.concatenate([rb_, rk_], axis=1), jnp.concatenate([expand(u_), ve], axis=0))
               for rh_, rb_, rk_, u_, ve in zip(rh, rb_m, rk_m, u, vexp)]
    for i, (s, pr) in enumerate(items):
        for q in range(nseq):
            uv = jnp.concatenate([seq_rows(u[i], q), seq_rows(ps(vs_[s], pr), q)], axis=0)
            bk = jnp.concatenate([seq_rows(ps(btil[s], pr), q), seq_rows(ps(ktil[s], pr), q)], axis=0)
            upd = jnp.where(blk_mask, _dot3(uv, bk, 'tn'), 0.0)
            ht_scr[sidx(s, q), pr] = ht_old[(s, q, pr)] * seq_rows(ps(w_last[s], pr), q)[0:1] + upd

    ln = ln_ref[...]
    for s in slots:
        o = jnp.concatenate(o_items[s * npair:(s + 1) * npair], axis=1)
        mean = _segsum(o, hm) * (1.0 / RW_HD)
        xc = o - mean
        var = _segsum(xc * xc, hm) * (1.0 / RW_HD)
        y = xc * lax.rsqrt(var + RW_GN_EPS) * ln[0:1] + ln[1:2]
        o_ref[0, s, 0] = ((y + bonus_[s]) * gs_[s]).astype(o_ref.dtype)

    if multi_chunk:
        for s, q in seqs:
            carry_scr[sidx(s, q)] = ps_[s][q * cs + cs - 1:q * cs + cs]

    @pl.when(ci == pl.num_programs(1) - 1)
    def _():
        for s, q in seqs:
            shift_ref[sidx(s, q)] = ps_[s][q * cs + n_valid - 1:q * cs + n_valid]
            for pr in pairs:
                ht = ht_scr[sidx(s, q), pr]
                s1_ref[sidx(s, q), 2 * pr] = ht[0:RW_HD, 0:RW_HD]
                s1_ref[sidx(s, q), 2 * pr + 1] = ht[RW_HD:2 * RW_HD, RW_HD:2 * RW_HD]


def _wkv_group(p_rw, lp, bsz, lpad, n_valid_len, shift0, s0):
    rows = RW_HD
    if lpad >= rows:
        cs, nseq = rows, 1
    else:
        cs, nseq = lpad, rows // lpad
    nc = lpad // cs
    n_valid = cs if nc > 1 else n_valid_len
    units = bsz // nseq
    ns = WKV_SLOTS if units % WKV_SLOTS == 0 else 1
    vec = jnp.zeros((SUBLANES, RW_W), F32)
    vec = vec.at[0].set(lp['rwkv_w0']).at[1].set(lp['rwkv_a0']).at[2].set(lp['rwkv_k_k'])
    vec = vec.at[3].set(lp['rwkv_k_a']).at[4].set(lp['rwkv_r_k'].reshape(RW_W))
    zeros = jnp.zeros((RW_DECAY_LORA, RW_W), F32)
    w2p = jnp.concatenate([lp['rwkv_w2'], zeros], axis=0)
    a2p = jnp.concatenate([zeros, lp['rwkv_a2']], axis=0)
    hm = _head_sum_matrix(RW_W, RW_HD)
    ln = jnp.zeros((SUBLANES, RW_W), F32).at[0].set(lp['rwkv_ln_g']).at[1].set(lp['rwkv_ln_b'])
    mu = lp['rwkv_mu'].reshape(1, RW_COLS)
    first = shift0.reshape(bsz, 1, RW_COLS)
    full = lambda a: pl.BlockSpec(a.shape, lambda b, j: (0,) * a.ndim)
    nq = ns * nseq
    sspec = pl.BlockSpec((nq, RW_HEADS, RW_HD, RW_HD), lambda b, j: (b, 0, 0, 0))
    fspec = pl.BlockSpec((nq, 1, RW_COLS), lambda b, j: (b, 0, 0))
    o, s1, shift1 = pl.pallas_call(
        functools.partial(_wkv_chunk_kernel, cs=cs, nseq=nseq, ns=ns, n_valid=n_valid, multi_chunk=nc > 1),
        out_shape=[jax.ShapeDtypeStruct((units // ns, ns, nc, rows, RW_W), BF16),
                   jax.ShapeDtypeStruct((bsz, RW_HEADS, RW_HD, RW_HD), F32),
                   jax.ShapeDtypeStruct((bsz, 1, RW_COLS), F32)],
        grid=(units // ns, nc),
        in_specs=[_slot_spec(ns, rows, RW_COLS), fspec, full(mu), full(vec),
                  full(w2p), full(a2p), full(lp['rwkv_g2']), full(hm), full(ln), sspec],
        out_specs=[_slot_spec(ns, rows, RW_W), sspec, fspec],
        scratch_shapes=[pltpu.VMEM((nq, 1, RW_COLS), F32),
                        pltpu.VMEM((nq, RW_HEADS // 2, LANES, LANES), F32)],
        compiler_params=_cparams(("parallel", "arbitrary")),
        name="wkv_chunk",
    )(_slot_view(p_rw, units, ns, nc, rows), first, mu, vec, w2p, a2p, lp['rwkv_g2'], hm, ln, s0)
    return o.reshape(bsz * lpad, RW_W), shift1.reshape(bsz, RW_COLS), s1


def _gdn_kernel(qkv_ref, z_ref, ba_ref, convw_ref, hp_ref, ng_ref, tail0_ref, s0_ref,
                o_ref, s1_ref, conv_ref, ext_scr, s_scr, *, c, nb, n_valid):
    ci = pl.program_id(1)
    last = ci == pl.num_programs(1) - 1
    slots = range(nb)
    heads = range(GD_HEADS)
    items = [(s, h) for s in slots for h in heads]

    @pl.when(ci == 0)
    def _():
        for s in slots:
            ext_scr[s, 0:SUBLANES, :] = tail0_ref[s]
            s_scr[s] = s0_ref[s]

    cw = convw_ref[...]
    hp = hp_ref[...]
    ng = ng_ref[...]
    off = SUBLANES - (CONV_W - 1)
    rid = _iota((c, c), 0)
    cid = _iota((c, c), 1)
    incl = rid >= cid
    tri = incl.astype(BF16)
    strict_l = (rid > cid).astype(F32)
    kbase, vbase = GD_HEADS * GD_DK, 2 * GD_HEADS * GD_DK

    cqs, beta_alls, g_alls, gcum_alls = [], [], [], []
    for s in slots:
        ext_scr[s, SUBLANES:SUBLANES + c, :] = qkv_ref[0, s, 0]
        cq = ext_scr[s, off:off + c, :] * cw[0:1]
        for j in range(1, CONV_W):
            cq = cq + ext_scr[s, off + j:off + j + c, :] * cw[j:j + 1]

        @pl.when(last)
        def _():
            conv_ref[s] = ext_scr[s, off + n_valid:off + n_valid + CONV_W - 1, :]

        ext_scr[s, 0:SUBLANES, :] = ext_scr[s, c:c + SUBLANES, :]
        cqs.append(_silu(cq))
        ba = ba_ref[0, s, 0]
        beta_all = _sigmoid(ba)
        g_all = -jnp.exp(hp[0:1]) * _softplus(ba + hp[1:2])
        if n_valid < c:
            valid = _iota((c, LANES), 0) < n_valid
            beta_all = jnp.where(valid, beta_all, 0.0)
            g_all = jnp.where(valid, g_all, 0.0)
        beta_alls.append(beta_all)
        g_alls.append(g_all)
        gcum_alls.append(_mdot(tri, g_all))

    qs = [cqs[s][:, h * GD_DK:(h + 1) * GD_DK] for s, h in items]
    ks = [cqs[s][:, kbase + h * GD_DK:kbase + (h + 1) * GD_DK] for s, h in items]
    vs = [cqs[s][:, vbase + h * GD_DV:vbase + (h + 1) * GD_DV] for s, h in items]
    qs = [q * lax.rsqrt(jnp.sum(q * q, axis=-1, keepdims=True) + NORM_EPS) * (GD_DK ** -0.5) for q in qs]
    ks = [k * lax.rsqrt(jnp.sum(k * k, axis=-1, keepdims=True) + NORM_EPS) for k in ks]
    betas = [beta_alls[s][:, h:h + 1] for s, h in items]
    g_cols = [g_alls[s][:, GD_HEADS + h:GD_HEADS + h + 1] for s, h in items]
    gcs = [gcum_alls[s][:, GD_HEADS + h:GD_HEADS + h + 1] for s, h in items]
    glasts = [gc[c - 1:c, :] for gc in gcs]
    egcs = [jnp.exp(gc) for gc in gcs]
    s_old = [s_scr[s, h] for s, h in items]
    decs = [jnp.where(incl, jnp.exp(_mdot(tri, g * strict_l)), 0.0) for g in g_cols]
    qk_kts = [_dot3(jnp.concatenate([q, k], axis=0), k, 'nt') for q, k in zip(qs, ks)]
    ms = [strict_l * b * qk[c:2 * c] * dec for b, qk, dec in zip(betas, qk_kts, decs)]
    rhss = [jnp.concatenate([v * b, k * (b * e)], axis=1) for v, k, b, e in zip(vs, ks, betas, egcs)]
    if c >= 2 * SUBLANES:
        xss = [_dot3(p, rhs) for p, rhs in zip(_unit_lower_inverses(ms, rid, cid, c), rhss)]
    else:
        strict_u = (rid < cid).astype(F32)
        dts = [_mdot((rid < cid).astype(BF16), g * (rid <= cid).astype(F32)) for g in g_cols]
        mts = [strict_u * _dot3(k, k * b, 'nt') * jnp.exp(jnp.minimum(dt, 0.0))
               for k, b, dt in zip(ks, betas, dts)]
        xss = rhss
        for i in range(1, c):
            row_i = _iota((c, GD_DV + GD_DK), 0) == i
            xss = [jnp.where(row_i, xs - jnp.sum(mt[:, i:i + 1] * xs, axis=0, keepdims=True), xs)
                   for xs, mt in zip(xss, mts)]
    ws_qss = [_dot3(jnp.concatenate([xs[:, GD_DV:GD_DV + GD_DK], q * e], axis=0), s)
              for xs, q, e, s in zip(xss, qs, egcs, s_old)]
    v_news = [xs[:, 0:GD_DV] - wq[0:c] for xs, wq in zip(xss, ws_qss)]
    os_ = [wq[c:2 * c] + _dot3(qk[0:c] * dec, vn) for wq, qk, dec, vn in zip(ws_qss, qk_kts, decs, v_news)]
    s_new = [s * jnp.exp(gl) + _dot3(k * jnp.exp(gl - gc), vn, 'tn')
             for s, gl, gc, k, vn in zip(s_old, glasts, gcs, ks, v_news)]
    os_ = [o * lax.rsqrt(jnp.mean(o * o, axis=-1, keepdims=True) + NORM_EPS) * ng for o in os_]
    for s in slots:
        z = z_ref[0, s, 0]
        o_ref[0, s, 0] = jnp.concatenate(
            [(os_[s * GD_HEADS + h] * _silu(z[:, h * GD_DV:(h + 1) * GD_DV])).astype(o_ref.dtype) for h in heads],
            axis=1)
    for i, (s, h) in enumerate(items):
        s_scr[s, h] = s_new[i]

    @pl.when(last)
    def _():
        for s in slots:
            s1_ref[s] = s_scr[s]


def _slot_plan(bsz, lpad):
    c = _pick(lpad, (CHUNK_ROWS, 32, 16, 8))
    nb = max(SLOTS_MIN, CHUNK_ROWS // c)
    if bsz % nb:
        nb = 1
    return c, lpad // c, nb


def _slot_view(a, bsz, nb, nc, c):
    return a.reshape(bsz // nb, nb, nc, c, a.shape[-1])


def _slot_spec(nb, c, w):
    return pl.BlockSpec((1, nb, 1, c, w), lambda b, j: (b, 0, j, 0, 0))


def _gdn_group(pqkv, pz, pba, lp, bsz, lpad, n_valid_len, conv0, s0):
    c, nc, nb = _slot_plan(bsz, lpad)
    n_valid = c if nc > 1 else n_valid_len
    tail0 = jnp.concatenate([jnp.zeros((bsz, SUBLANES - (CONV_W - 1), GD_QKV), F32), conv0], axis=1)
    hp = jnp.zeros((SUBLANES, LANES), F32)
    hp = hp.at[0, GD_HEADS:2 * GD_HEADS].set(lp['gdn_a_log']).at[1, GD_HEADS:2 * GD_HEADS].set(lp['gdn_dt_bias'])
    ng = lp['gdn_norm_g'].reshape(1, GD_DV)
    view = lambda a: _slot_view(a, bsz, nb, nc, c)
    full = lambda a: pl.BlockSpec(a.shape, lambda b, j: (0,) * a.ndim)
    sspec = pl.BlockSpec((nb, GD_HEADS, GD_DK, GD_DV), lambda b, j: (b, 0, 0, 0))
    o, s1, conv1 = pl.pallas_call(
        functools.partial(_gdn_kernel, c=c, nb=nb, n_valid=n_valid),
        out_shape=[jax.ShapeDtypeStruct((bsz // nb, nb, nc, c, GD_W), BF16),
                   jax.ShapeDtypeStruct((bsz, GD_HEADS, GD_DK, GD_DV), F32),
                   jax.ShapeDtypeStruct((bsz, CONV_W - 1, GD_QKV), F32)],
        grid=(bsz // nb, nc),
        in_specs=[_slot_spec(nb, c, GD_QKV), _slot_spec(nb, c, GD_W), _slot_spec(nb, c, LANES),
                  full(lp['gdn_conv']), full(hp), full(ng),
                  pl.BlockSpec((nb, SUBLANES, GD_QKV), lambda b, j: (b, 0, 0)), sspec],
        out_specs=[_slot_spec(nb, c, GD_W), sspec,
                   pl.BlockSpec((nb, CONV_W - 1, GD_QKV), lambda b, j: (b, 0, 0))],
        scratch_shapes=[pltpu.VMEM((nb, c + SUBLANES, GD_QKV), F32),
                        pltpu.VMEM((nb, GD_HEADS, GD_DK, GD_DV), F32)],
        compiler_params=_cparams(("parallel", "arbitrary")),
        name="gdn_chunk",
    )(view(pqkv), view(pz), view(pba), lp['gdn_conv'], hp, ng, tail0, s0)
    return o.reshape(bsz * lpad, GD_W), s1, conv1


def _hgrn_kernel(p_ref, lbv_ref, ng_ref, s0_ref, o_ref, s1_ref, st_scr, *, c, nb, n_valid):
    ci = pl.program_id(1)
    slots = range(nb)
    heads = range(HG_HEADS)
    items = [(s, h) for s in slots for h in heads]

    @pl.when(ci == 0)
    def _():
        for s, h in items:
            st_scr[s, h] = jnp.transpose(s0_ref[s, h])

    rid = _iota((c, c), 0)
    cid = _iota((c, c), 1)
    tri = (rid >= cid).astype(BF16)
    w = HG_HEADS * HG_DF
    rows = _iota((c, w), 0)
    sb = min(c, HG_SUB)
    row_in_sub = rows % sb
    lbv = lbv_ref[...]
    hs = lambda x, h: x[:, h * HG_DF:(h + 1) * HG_DF]

    qs, ks, vs, bcs = [], [], [], []
    for s in slots:
        p = p_ref[0, s, 0]
        pf = p[:, w:2 * w]
        a = lbv[0:1]
        b = lbv[1:2] - _softplus(-pf)
        logf = jnp.maximum(a, b) + jnp.log1p(jnp.exp(-jnp.abs(a - b)))
        k = lbv[2:3] * _sigmoid(-pf)
        if n_valid < c:
            logf = jnp.where(rows < n_valid, logf, 0.0)
            k = jnp.where(rows < n_valid, k, 0.0)
        qs.append(_silu(p[:, 0:w]))
        ks.append(k)
        vs.append(p[:, 2 * w:2 * w + HG_W])
        bcs.append(_mdot(tri, logf))
    st_old = [st_scr[s, h] for s, h in items]
    qes = [q * jnp.exp(bc) for q, bc in zip(qs, bcs)]
    os_ = [_dot3(hs(qes[s], h), st, 'nt') for (s, h), st in zip(items, st_old)]
    for delta in range(sb):
        prods, v_ss = [], []
        for s in slots:
            if delta == 0:
                k_s, b_s, v_s = ks[s], bcs[s], vs[s]
            else:
                k_s, b_s, v_s = (pltpu.roll(t, delta, 0) for t in (ks[s], bcs[s], vs[s]))
            prods.append(jnp.where(row_in_sub >= delta,
                                   qs[s] * k_s * jnp.exp(jnp.minimum(bcs[s] - b_s, 0.0)), 0.0))
            v_ss.append(v_s)
        os_ = [o + jnp.sum(hs(prods[s], h), axis=-1, keepdims=True) * hs(v_ss[s], h)
               for (s, h), o in zip(items, os_)]
    if c > sb:
        parts = [[jnp.zeros((sb, HG_DV), F32)] for _ in items]
        for r0 in range(sb, c, sb):
            qis, kps = [], []
            for s in slots:
                ref = bcs[s][r0 - 1:r0]
                qis.append(qs[s][r0:r0 + sb] * jnp.exp(bcs[s][r0:r0 + sb] - ref))
                kps.append(ks[s][0:r0] * jnp.exp(ref - bcs[s][0:r0]))
            att = [_dot3(hs(qis[s], h), hs(kps[s], h), 'nt') for s, h in items]
            for i, (s, h) in enumerate(items):
                parts[i].append(_dot3(att[i], hs(vs[s], h)[0:r0]))
        os_ = [o + jnp.concatenate(p, axis=0) for o, p in zip(os_, parts)]
    blasts = [bc[c - 1:c] for bc in bcs]
    kds = [k * jnp.exp(bl - bc) for k, bl, bc in zip(ks, blasts, bcs)]
    ebs = [jnp.exp(bl) for bl in blasts]
    st_new = [st * hs(ebs[s], h) + _dot3(hs(vs[s], h), hs(kds[s], h), 'tn') for (s, h), st in zip(items, st_old)]
    os_ = [o * lax.rsqrt(jnp.mean(o * o, axis=-1, keepdims=True) + NORM_EPS) for o in os_]
    ng = ng_ref[...]
    for s in slots:
        gate = _sigmoid(p_ref[0, s, 0][:, 2 * w + HG_W:2 * w + 2 * HG_W])
        o_all = jnp.concatenate(os_[s * HG_HEADS:(s + 1) * HG_HEADS], axis=1) * ng * gate
        o_ref[0, s, 0] = o_all.astype(o_ref.dtype)
    for i, (s, h) in enumerate(items):
        st_scr[s, h] = st_new[i]

    @pl.when(ci == pl.num_programs(1) - 1)
    def _():
        for s, h in items:
            s1_ref[s, h] = jnp.transpose(st_scr[s, h])


def _hgrn_group(phg, lb, lp, bsz, lpad, n_valid_len, s0):
    c, nc, nb = _slot_plan(bsz, lpad)
    n_valid = c if nc > 1 else n_valid_len
    lbv = jnp.zeros((SUBLANES, HG_HEADS * HG_DF), F32)
    lbv = lbv.at[0].set(jnp.log(jnp.maximum(lb, LB_FLOOR))).at[1].set(jnp.log1p(-lb)).at[2].set(1.0 - lb)
    ng = lp['hgrn_norm_g'].reshape(1, HG_W)
    full = lambda a: pl.BlockSpec(a.shape, lambda b, j: (0,) * a.ndim)
    sspec = pl.BlockSpec((nb, HG_HEADS, HG_DF, HG_DV), lambda b, j: (b, 0, 0, 0))
    o, s1 = pl.pallas_call(
        functools.partial(_hgrn_kernel, c=c, nb=nb, n_valid=n_valid),
        out_shape=[jax.ShapeDtypeStruct((bsz // nb, nb, nc, c, HG_W), BF16),
                   jax.ShapeDtypeStruct((bsz, HG_HEADS, HG_DF, HG_DV), F32)],
        grid=(bsz // nb, nc),
        in_specs=[_slot_spec(nb, c, HG_COLS), full(lbv), full(ng), sspec],
        out_specs=[_slot_spec(nb, c, HG_W), sspec],
        scratch_shapes=[pltpu.VMEM((nb, HG_HEADS, HG_DV, HG_DF), F32)],
        compiler_params=_cparams(("parallel", "arbitrary")),
        name="hgrn_chunk",
    )(_slot_view(phg, bsz, nb, nc, c), lbv, ng, s0)
    return o.reshape(bsz * lpad, HG_W), s1


def _ret_kernel(p_ref, cos_ref, sin_ref, qd_ref, kd_ref, dec_ref, dm_ref, gn_ref, s0_ref,
                o_ref, s1_ref, s_scr, *, c, nb):
    ci = pl.program_id(1)
    slots = range(nb)
    heads = range(RT_HEADS)
    items = [(s, h) for s in slots for h in heads]

    @pl.when(ci == 0)
    def _():
        for s in slots:
            s_scr[s] = jnp.zeros((RT_QK, RT_W), F32)
            for h in heads:
                s_scr[s, h * RT_DK:(h + 1) * RT_DK, h * RT_DV:(h + 1) * RT_DV] = s0_ref[s, h]

    cos = cos_ref[...]
    sin = sin_ref[...]
    lane = _iota((c, RT_QK), 1)
    first_half = (lane % RT_DK) < (RT_DK // 2)

    def rope(x):
        partner = jnp.where(first_half, pltpu.roll(x, RT_QK - RT_DK // 2, 1), pltpu.roll(x, RT_DK // 2, 1))
        return x * cos + partner * sin

    ps = [p_ref[0, s, 0] for s in slots]
    qs = [rope(p[:, 0:RT_QK]) for p in ps]
    ks = [rope(p[:, RT_QK:2 * RT_QK]) * (RT_DK ** -0.5) for p in ps]
    pvs = [p[:, 2 * RT_QK:2 * RT_QK + RT_W] for p in ps]
    sblks = [s_scr[s] for s in slots]
    qd = qd_ref[...]
    o_inters = [_dot3(q * qd, sblk) for q, sblk in zip(qs, sblks)]
    qks = [_dot3(jnp.concatenate([jnp.where(lane // RT_DK == h, q, 0.0) for h in heads], axis=0), k, 'nt')
           for q, k in zip(qs, ks)]
    attns = [qks[s][h * c:(h + 1) * c] * dec_ref[h] for s, h in items]
    os_ = [o_inters[s][:, h * RT_DV:(h + 1) * RT_DV] + _dot3(attn, pvs[s][:, h * RT_DV:(h + 1) * RT_DV])
           for (s, h), attn in zip(items, attns)]
    dm = dm_ref[...]
    kd = kd_ref[...]
    s_new = [sblk * dm + jnp.where(dm > 0.0, _dot3(k * kd, pv, 'tn'), 0.0) for sblk, k, pv in zip(sblks, ks, pvs)]
    gn = gn_ref[...]
    normed = []
    for o in os_:
        xc = o - jnp.mean(o, axis=-1, keepdims=True)
        normed.append(xc * lax.rsqrt(jnp.mean(xc * xc, axis=-1, keepdims=True) + NORM_EPS))
    for s in slots:
        gate = _silu(ps[s][:, 2 * RT_QK + RT_W:2 * RT_QK + 2 * RT_W])
        o_all = jnp.concatenate(normed[s * RT_HEADS:(s + 1) * RT_HEADS], axis=1) * gn * gate
        o_ref[0, s, 0] = o_all.astype(o_ref.dtype)
        s_scr[s] = s_new[s]

    @pl.when(ci == pl.num_programs(1) - 1)
    def _():
        for s, h in items:
            s1_ref[s, h] = s_scr[s, h * RT_DK:(h + 1) * RT_DK, h * RT_DV:(h + 1) * RT_DV]


def _ret_group(prt, lp, bsz, lpad, n_valid_len, pos0, s0):
    c, nc, nb = _slot_plan(bsz, lpad)
    n_valid = c if nc > 1 else n_valid_len
    half = RT_DK // 2
    inv = ROPE_BASE ** (-np.arange(half, dtype=np.float64) / half)
    ang = (pos0 + np.arange(lpad, dtype=np.float64))[:, None] * inv[None, :]
    cos = np.tile(np.cos(ang), (1, 2 * RT_HEADS))
    sin = np.tile(np.concatenate([-np.sin(ang), np.sin(ang)], axis=1), (1, RT_HEADS))
    loggamma = np.log(1.0 - np.exp2(-5.0 - np.arange(RT_HEADS, dtype=np.float64)))
    gcum = loggamma[:, None] * np.arange(1, c + 1, dtype=np.float64)[None, :]
    idx = np.arange(c)
    dec = np.where(idx[:, None] >= idx[None, :], np.exp(gcum[:, :, None] - gcum[:, None, :]), 0.0)
    qd = np.repeat(np.exp(gcum).T, RT_DK, axis=1)
    kdec = np.where(idx[None, :] < n_valid, np.exp(gcum[:, n_valid - 1:n_valid] - gcum), 0.0)
    kd = np.repeat(kdec.T, RT_DK, axis=1)
    sdec = np.exp(gcum[:, n_valid - 1])
    dm = np.zeros((RT_QK, RT_W))
    for h in range(RT_HEADS):
        dm[h * RT_DK:(h + 1) * RT_DK, h * RT_DV:(h + 1) * RT_DV] = sdec[h]
    cos, sin, qd, kd, dec, dm = (jnp.asarray(a, F32) for a in (cos, sin, qd, kd, dec, dm))
    gn = lp['ret_gn_g'].reshape(1, RT_W)
    full = lambda a: pl.BlockSpec(a.shape, lambda b, j: (0,) * a.ndim)
    posspec = pl.BlockSpec((c, RT_QK), lambda b, j: (j, 0))
    sspec = pl.BlockSpec((nb, RT_HEADS, RT_DK, RT_DV), lambda b, j: (b, 0, 0, 0))
    o, s1 = pl.pallas_call(
        functools.partial(_ret_kernel, c=c, nb=nb),
        out_shape=[jax.ShapeDtypeStruct((bsz // nb, nb, nc, c, RT_W), BF16),
                   jax.ShapeDtypeStruct((bsz, RT_HEADS, RT_DK, RT_DV), F32)],
        grid=(bsz // nb, nc),
        in_specs=[_slot_spec(nb, c, RT_COLS), posspec, posspec,
                  full(qd), full(kd), full(dec), full(dm), full(gn), sspec],
        out_specs=[_slot_spec(nb, c, RT_W), sspec],
        scratch_shapes=[pltpu.VMEM((nb, RT_QK, RT_W), F32)],
        compiler_params=_cparams(("parallel", "arbitrary")),
        name="ret_chunk",
    )(_slot_view(prt, bsz, nb, nc, c), cos, sin, qd, kd, dec, dm, gn, s0)
    return o.reshape(bsz * lpad, RT_W), s1


def _merge_kernel(x_ref, o1, o2, o3, o4, gate_ref, w1, w2, w3, w4, wo_ref, out_ref, *, d):
    acc = None
    for i, (o, w) in enumerate(((o1, w1), (o2, w2), (o3, w3), (o4, w4))):
        term = _sigmoid(gate_ref[:, i * d:(i + 1) * d]) * _dot(o[...], w[...])
        acc = term if acc is None else acc + term
    out_ref[...] = x_ref[...] + _dot(acc.astype(BF16), wo_ref[...])


def _merge(x, outs, gate, wouts, wo):
    n, d = x.shape
    tm = _pick(n, (256, 128, 64, 32, 16))
    row = lambda w: pl.BlockSpec((tm, w), lambda i: (i, 0))
    full = lambda a: pl.BlockSpec(a.shape, lambda i: (0,) * a.ndim)
    return pl.pallas_call(
        functools.partial(_merge_kernel, d=d),
        out_shape=jax.ShapeDtypeStruct((n, d), F32),
        grid=(n // tm,),
        in_specs=[row(d)] + [row(o.shape[1]) for o in outs] + [row(N_BRANCH * d)]
        + [full(w) for w in wouts] + [full(wo)],
        out_specs=row(d),
        compiler_params=_cparams(("parallel",)),
        name="merge",
    )(x, *outs, gate, *wouts, wo)


def _route_kernel(x_ref, g_ref, rw_ref, rb_ref, xn_ref, col_ref, row_ref, cnt_ref):
    tm = x_ref.shape[0]
    lane = _iota((tm, LANES), 1)
    lanef = lane.astype(F32)
    x = x_ref[...]
    xn = x * lax.rsqrt(jnp.mean(x * x, axis=-1, keepdims=True) + NORM_EPS) * g_ref[...]
    xn_ref[...] = xn.astype(BF16)
    rb = rb_ref[...]
    neg = jnp.float32(-jnp.inf)
    logits = _dot3(xn, rw_ref[...])
    glog = jnp.where(lane < N_GROUPS, logits[:, 0:LANES] + rb[0:1], neg)
    gmax = jnp.max(glog, axis=-1, keepdims=True)
    gsum = jnp.sum(jnp.exp(glog - gmax), axis=-1, keepdims=True)
    gidx = jnp.min(jnp.where(glog == gmax, lanef, float(LANES)), axis=-1, keepdims=True)
    gp = 1.0 / gsum
    in_group = (lanef >= gidx * EXPERTS_PER_GROUP) & (lanef < (gidx + 1.0) * EXPERTS_PER_GROUP)
    elog = jnp.where(in_group, logits[:, LANES:2 * LANES] + rb[1:2], neg)
    emax = jnp.max(elog, axis=-1, keepdims=True)
    eexp = jnp.exp(elog - emax)
    ep = eexp / jnp.sum(eexp, axis=-1, keepdims=True)
    ep = jnp.where(in_group, ep, -1.0)
    p1 = jnp.max(ep, axis=-1, keepdims=True)
    i1 = jnp.min(jnp.where(ep == p1, lanef, float(LANES)), axis=-1, keepdims=True)
    ep2 = jnp.where(lanef == i1, -1.0, ep)
    p2 = jnp.max(ep2, axis=-1, keepdims=True)
    i2 = jnp.min(jnp.where(ep2 == p2, lanef, float(LANES)), axis=-1, keepdims=True)
    denom = p1 + p2
    wt1 = gp * p1 / denom
    wt2 = gp * p2 / denom

    onehot = ((lanef == i1) | (lanef == i2)).astype(F32)
    cnt = jnp.sum(onehot, axis=0, keepdims=True)
    before = (_iota((LANES, LANES), 0) < _iota((LANES, LANES), 1)).astype(BF16)
    off = _segsum(jnp.broadcast_to(cnt, (SUBLANES, LANES)), before)[0:1]
    tri = (_iota((tm, tm), 0) >= _iota((tm, tm), 1)).astype(BF16)
    slot = off + _dot(tri, onehot.astype(BF16)) - 1.0
    pos1 = jnp.sum(jnp.where(lanef == i1, slot, 0.0), axis=-1, keepdims=True)
    pos2 = jnp.sum(jnp.where(lanef == i2, slot, 0.0), axis=-1, keepdims=True)
    col = jnp.where(lane == 0, pos1, jnp.where(lane == 1, pos2, jnp.where(lane == 2, wt1,
                                                                         jnp.where(lane == 3, wt2, 0.0))))
    col_ref[...] = col
    row_ref[0] = jnp.transpose(col)[0:SUBLANES]
    cnt_ref[0] = jnp.concatenate([cnt, off, jnp.zeros((SUBLANES - 2, LANES), F32)], axis=0)


def _moe_kernel(cnt_sm, off_sm, x_ref, xn_ref, col_ref, row_ref, wg_ref, wu_ref, wd_ref, out_ref,
                xs_scr, ys_scr, ws_scr, *, pb, rb, eps):
    i = pl.program_id(0)
    eg = pl.program_id(1)
    tm, d = x_ref.shape
    ns = 2 * tm

    @pl.when(eg == 0)
    def _():
        rowd = row_ref[0]
        pos1, pos2, wt1, wt2 = rowd[0:1], rowd[1:2], rowd[2:3], rowd[3:4]
        for blk in range(ns // pb):
            sid = (_iota((pb, tm), 0) + blk * pb).astype(F32)
            m1 = sid == pos1
            m2 = sid == pos2
            xs_scr[blk * pb:(blk + 1) * pb, :] = _dot((m1 | m2).astype(BF16), xn_ref[...]).astype(BF16)
            wsl = jnp.sum(jnp.where(m1, wt1, 0.0) + jnp.where(m2, wt2, 0.0), axis=-1, keepdims=True)
            ws_scr[blk * pb:(blk + 1) * pb, :] = jnp.broadcast_to(wsl, (pb, LANES))
        ys_scr[...] = jnp.zeros_like(ys_scr)

    for k in range(eps):
        cnt = cnt_sm[i, eg * eps + k]
        off = off_sm[i, eg * eps + k]
        start = (off // BF16_ROWS) * BF16_ROWS

        def body(j, carry, k=k, cnt=cnt, off=off, start=start):
            own = start + j * rb
            r0 = pl.multiple_of(jnp.minimum(own, ns - rb), BF16_ROWS)
            xb = xs_scr[pl.ds(r0, rb), :]
            hid = _silu(_dot(xb, wg_ref[k])) * _dot(xb, wu_ref[k])
            y = _dot(hid.astype(BF16), wd_ref[k])
            srow = r0 + _iota((rb, d), 0)
            mine = (srow >= jnp.maximum(off, own)) & (srow < off + cnt)
            ys_scr[pl.ds(r0, rb), :] += jnp.where(mine, y, 0.0)
            return carry

        lax.fori_loop(0, (off + cnt - start + rb - 1) // rb, body, 0)

    @pl.when(eg == pl.num_programs(1) - 1)
    def _():
        col = col_ref[...]
        pos1, pos2 = col[:, 0:1], col[:, 1:2]
        acc = x_ref[...]
        for blk in range(ns // pb):
            sid = (_iota((tm, pb), 1) + blk * pb).astype(F32)
            pt = ((sid == pos1) | (sid == pos2)).astype(BF16)
            ysw = ys_scr[blk * pb:(blk + 1) * pb, :] * ws_scr[blk * pb:(blk + 1) * pb, 0:1]
            hi, lo = _split2(ysw)
            acc = acc + _dot(pt, hi) + _dot(pt, lo)
        out_ref[...] = acc


def _moe(x, lp, experts):
    n, d = x.shape
    tm = _pick(n, (MOE_TILE, 512, 256, 128, 64))
    nt = n // tm
    de = experts[0].shape[-1]
    rw = jnp.zeros((d, 2 * LANES), F32)
    rw = rw.at[:, :N_GROUPS].set(lp['router_group']).at[:, LANES:LANES + N_EXPERTS].set(lp['router_expert'])
    rbias = jnp.zeros((SUBLANES, LANES), F32)
    rbias = rbias.at[0, :N_GROUPS].set(lp['router_group_b']).at[1, :N_EXPERTS].set(lp['router_expert_b'])
    g = lp['norm2_g'].reshape(1, d)
    full1 = lambda a: pl.BlockSpec(a.shape, lambda i: (0,) * a.ndim)
    xn, col, row, cnt = pl.pallas_call(
        _route_kernel,
        out_shape=[jax.ShapeDtypeStruct((n, d), BF16), jax.ShapeDtypeStruct((n, LANES), F32),
                   jax.ShapeDtypeStruct((nt, SUBLANES, tm), F32), jax.ShapeDtypeStruct((nt, SUBLANES, LANES), F32)],
        grid=(nt,),
        in_specs=[pl.BlockSpec((tm, d), lambda i: (i, 0)), full1(g), full1(rw), full1(rbias)],
        out_specs=[pl.BlockSpec((tm, d), lambda i: (i, 0)), pl.BlockSpec((tm, LANES), lambda i: (i, 0)),
                   pl.BlockSpec((1, SUBLANES, tm), lambda i: (i, 0, 0)),
                   pl.BlockSpec((1, SUBLANES, LANES), lambda i: (i, 0, 0))],
        compiler_params=_cparams(("parallel",)),
        name="moe_route",
    )(x, g, rw, rbias)
    cnt_i = cnt[:, 0, :N_EXPERTS].astype(jnp.int32)
    off_i = cnt[:, 1, :N_EXPERTS].astype(jnp.int32)
    ns = 2 * tm
    pb = min(MOE_SLOT_BLOCK, ns)
    rb = min(MOE_ROW_BLOCK, ns)
    grid_spec = pltpu.PrefetchScalarGridSpec(
        num_scalar_prefetch=2,
        grid=(nt, N_EXPERTS // MOE_EXPERTS_PER_STEP),
        in_specs=[pl.BlockSpec((tm, d), lambda i, e, c, o: (i, 0)),
                  pl.BlockSpec((tm, d), lambda i, e, c, o: (i, 0)),
                  pl.BlockSpec((tm, LANES), lambda i, e, c, o: (i, 0)),
                  pl.BlockSpec((1, SUBLANES, tm), lambda i, e, c, o: (i, 0, 0)),
                  pl.BlockSpec((MOE_EXPERTS_PER_STEP, d, de), lambda i, e, c, o: (e, 0, 0)),
                  pl.BlockSpec((MOE_EXPERTS_PER_STEP, d, de), lambda i, e, c, o: (e, 0, 0)),
                  pl.BlockSpec((MOE_EXPERTS_PER_STEP, de, d), lambda i, e, c, o: (e, 0, 0))],
        out_specs=pl.BlockSpec((tm, d), lambda i, e, c, o: (i, 0)),
        scratch_shapes=[pltpu.VMEM((ns, d), BF16), pltpu.VMEM((ns, d), F32), pltpu.VMEM((ns, LANES), F32)],
    )
    return pl.pallas_call(
        functools.partial(_moe_kernel, pb=pb, rb=rb, eps=MOE_EXPERTS_PER_STEP),
        out_shape=jax.ShapeDtypeStruct((n, d), F32),
        grid_spec=grid_spec,
        compiler_params=pltpu.CompilerParams(dimension_semantics=("parallel", "arbitrary"),
                                             vmem_limit_bytes=MOE_VMEM_LIMIT),
        name="moe",
    )(cnt_i, off_i, x, xn, col, row, *experts)


def _layer(x, wts, lp, lb, gr):
    b, seq, lpad = gr['bsz'], gr['seq'], gr['lpad']
    shift0, wkv0, conv0, gdn0, hgrn0, ret0 = gr['states']
    xn = _rmsnorm(x, lp['norm1_g'], BF16)
    p_rw, p_gqkv, p_gz, p_gba, p_hg, p_rt, p_gate = (
        _matmul(xn, w, f"proj{i}") for i, w in enumerate(wts['proj']))
    o_rw, shift1, wkv1 = _wkv_group(p_rw, lp, b, lpad, seq, shift0, wkv0)
    o_gd, gdn1, conv1 = _gdn_group(p_gqkv, p_gz, p_gba, lp, b, lpad, seq, conv0, gdn0)
    o_hg, hgrn1 = _hgrn_group(p_hg, lb, lp, b, lpad, seq, hgrn0)
    o_rt, ret1 = _ret_group(p_rt, lp, b, lpad, seq, gr['pos0'], ret0)
    x = _merge(x, [o_rw, o_gd, o_hg, o_rt], p_gate, wts['out'], wts['w_o'])
    x = _moe(x, lp, wts['experts'])
    return x, (shift1, wkv1, conv1, gdn1, hgrn1, ret1)


def _layer_weights(lp, d):
    offs = np.cumsum([0, RW_COLS, GD_QKV, GD_W, 2 * GD_HEADS, HG_COLS, RT_COLS, N_BRANCH * d])
    seg = [lp['w_in'][:, offs[i]:offs[i + 1]] for i in range(7)]
    seg[3] = jnp.pad(seg[3], ((0, 0), (0, LANES - 2 * GD_HEADS)))
    return dict(
        proj=[s.astype(BF16) for s in seg],
        out=[lp[n].astype(BF16) for n in ('w_out_rwkv', 'w_out_gdn', 'w_out_hgrn', 'w_out_ret')],
        w_o=lp['w_o'].astype(BF16),
        experts=[lp[n].astype(BF16) for n in ('moe_w_gate', 'moe_w_up', 'moe_w_down')])


def kernel(x_prompt, x_sample, state_rwkv_shift, state_rwkv_wkv, state_gdn_conv, state_gdn, state_hgrn, state_ret, norm1_g, w_in, rwkv_mu, rwkv_w0, rwkv_w2, rwkv_a0, rwkv_a2, rwkv_g2, rwkv_k_k, rwkv_k_a, rwkv_r_k, rwkv_ln_g, rwkv_ln_b, w_out_rwkv, gdn_conv, gdn_a_log, gdn_dt_bias, gdn_norm_g, w_out_gdn, hgrn_lb_logits, hgrn_norm_g, w_out_hgrn, ret_gn_g, w_out_ret, w_o, norm2_g, router_group, router_group_b, router_expert, router_expert_b, moe_w_gate, moe_w_up, moe_w_down, final_norm_g):
    params = dict(norm1_g=norm1_g, w_in=w_in, rwkv_mu=rwkv_mu, rwkv_w0=rwkv_w0, rwkv_w2=rwkv_w2,
                  rwkv_a0=rwkv_a0, rwkv_a2=rwkv_a2, rwkv_g2=rwkv_g2, rwkv_k_k=rwkv_k_k,
                  rwkv_k_a=rwkv_k_a, rwkv_r_k=rwkv_r_k, rwkv_ln_g=rwkv_ln_g, rwkv_ln_b=rwkv_ln_b,
                  w_out_rwkv=w_out_rwkv, gdn_conv=gdn_conv, gdn_a_log=gdn_a_log,
                  gdn_dt_bias=gdn_dt_bias, gdn_norm_g=gdn_norm_g, w_out_gdn=w_out_gdn,
                  hgrn_norm_g=hgrn_norm_g, w_out_hgrn=w_out_hgrn, ret_gn_g=ret_gn_g,
                  w_out_ret=w_out_ret, w_o=w_o, norm2_g=norm2_g, router_group=router_group,
                  router_group_b=router_group_b, router_expert=router_expert,
                  router_expert_b=router_expert_b, moe_w_gate=moe_w_gate, moe_w_up=moe_w_up,
                  moe_w_down=moe_w_down)
    depth = w_in.shape[0]
    bp, lp_len, d = x_prompt.shape
    bs, ls, _ = x_sample.shape
    ls_pad = -(-ls // SAMPLE_PAD_LEN) * SAMPLE_PAD_LEN
    sm = jax.nn.softmax(hgrn_lb_logits.astype(F32), axis=0)
    lower_bounds = jnp.cumsum(sm, axis=0) - sm[0]

    sample_states = (state_rwkv_shift, state_rwkv_wkv, state_gdn_conv, state_gdn, state_hgrn, state_ret)
    xp = x_prompt.reshape(bp * lp_len, d)
    xs = jnp.pad(x_sample, ((0, 0), (0, ls_pad - ls), (0, 0))).reshape(bs * ls_pad, d)

    prompt_out = [[] for _ in sample_states]
    sample_out = [[] for _ in sample_states]
    for layer in range(depth):
        lpar = {name: arr[layer] for name, arr in params.items()}
        wts = _layer_weights(lpar, d)
        prompt = dict(bsz=bp, seq=lp_len, lpad=lp_len, pos0=0,
                      states=tuple(jnp.zeros((bp,) + s.shape[2:], F32) for s in sample_states))
        sample = dict(bsz=bs, seq=ls, lpad=ls_pad, pos0=PAST_LEN,
                      states=tuple(s[layer].astype(F32) for s in sample_states))
        xp, new_p = _layer(xp, wts, lpar, lower_bounds[layer], prompt)
        xs, new_s = _layer(xs, wts, lpar, lower_bounds[layer], sample)
        for lst, n in zip(prompt_out, new_p):
            lst.append(n)
        for lst, n in zip(sample_out, new_s):
            lst.append(n)

    y_prompt = _rmsnorm(xp, final_norm_g, F32).reshape(bp, lp_len, d)
    y_sample = _rmsnorm(xs, final_norm_g, F32).reshape(bs, ls_pad, d)[:, :ls]
    p_states = [jnp.stack(lst) for lst in prompt_out]
    s_states = [jnp.stack(lst).astype(o.dtype) for lst, o in zip(sample_out, sample_states)]
    return (y_prompt, y_sample, *p_states, *s_states)
```

```python
import functools
import math

import numpy as np
import jax
import jax.numpy as jnp
from jax import lax
from jax.experimental import pallas as pl
from jax.experimental.pallas import tpu as pltpu

F32 = jnp.float32
BF16 = jnp.bfloat16

NORM_EPS = 1e-6
LB_FLOOR = 1e-30
PAST_LEN = 16384
RW_HEADS = 8
RW_HD = 64
RW_W = RW_HEADS * RW_HD
RW_DECAY_LORA = 64
RW_AAA_LORA = 64
RW_GATE_LORA = 128
RW_COLS = 3 * RW_W + RW_DECAY_LORA + RW_AAA_LORA + RW_GATE_LORA
RW_GN_EPS = 64e-5
GD_HEADS = 4
GD_DK = 128
GD_DV = 128
GD_QKV = GD_HEADS * (2 * GD_DK + GD_DV)
GD_W = GD_HEADS * GD_DV
CONV_W = 4
HG_HEADS = 4
HG_DF = 128
HG_DV = 128
HG_W = HG_HEADS * HG_DV
HG_COLS = 2 * HG_HEADS * HG_DF + 2 * HG_W
HG_SUB = 16
RT_HEADS = 4
RT_DK = 64
RT_DV = 128
RT_W = RT_HEADS * RT_DV
RT_QK = RT_HEADS * RT_DK
RT_COLS = 2 * RT_QK + 2 * RT_W
ROPE_BASE = 10000.0
N_BRANCH = 4
N_GROUPS = 4
EXPERTS_PER_GROUP = 8
N_EXPERTS = N_GROUPS * EXPERTS_PER_GROUP

LANES = 128
SUBLANES = 8
SAMPLE_PAD_LEN = 8
CHUNK_ROWS = 64
SLOTS_MIN = 4
WKV_SLOTS = 2
VMEM_LIMIT = 48 * 1024 * 1024
MOE_TILE = 1024
MOE_SLOT_BLOCK = 256
MOE_ROW_BLOCK = 128
MOE_EXPERTS_PER_STEP = 4
BF16_ROWS = 16
MOE_VMEM_LIMIT = 56 * 1024 * 1024


def _pick(n, cands):
    for c in cands:
        if n % c == 0:
            return c
    raise ValueError(f"no tile for {n} in {cands}")


def _cparams(sem):
    return pltpu.CompilerParams(dimension_semantics=sem, vmem_limit_bytes=VMEM_LIMIT)


def _dot(a, b):
    return lax.dot_general(a, b, (((1,), (0,)), ((), ())), preferred_element_type=F32)


def _softplus(x):
    return jnp.maximum(x, 0.0) + jnp.log1p(jnp.exp(-jnp.abs(x)))


def _sigmoid(x):
    return jax.nn.sigmoid(x)


def _silu(x):
    return x * jax.nn.sigmoid(x)


def _segsum(x, hm):
    hi = x.astype(BF16)
    r1 = x - hi.astype(F32)
    mid = r1.astype(BF16)
    lo = (r1 - mid.astype(F32)).astype(BF16)
    return _dot(hi, hm) + _dot(mid, hm) + _dot(lo, hm)


_DIMS = {'nn': (((1,), (0,)), ((), ())), 'nt': (((1,), (1,)), ((), ())), 'tn': (((0,), (0,)), ((), ()))}


def _split2(x):
    hi = x.astype(BF16)
    return hi, (x - hi.astype(F32)).astype(BF16)


def _dot3(a, b, form='nn'):
    ah, al = _split2(a)
    bh, bl = _split2(b)
    f = lambda x, y: lax.dot_general(x, y, _DIMS[form], preferred_element_type=F32)
    free = 1 if form == 'tn' else 0
    m = a.shape[free]
    both = f(jnp.concatenate([ah, al], axis=free), bh)
    return both[0:m] + both[m:2 * m] + f(ah, bl)


def _mdot(mask, x):
    hi = x.astype(BF16)
    r1 = x - hi.astype(F32)
    mid = r1.astype(BF16)
    lo = (r1 - mid.astype(F32)).astype(BF16)
    return _dot(mask, hi) + _dot(mask, mid) + _dot(mask, lo)


def _unit_lower_inverses(ms, rid, cid, c, expand=None):
    prod = _dot3 if expand is None else (lambda x, y: _dot3(x, expand(y)))
    same = lambda s: (rid // s) == (cid // s)
    ns = [jnp.where(same(SUBLANES), -m, 0.0) for m in ms]
    n2s = [prod(n, n) for n in ns]
    n4s = [prod(n2, n2) for n2 in n2s]
    eye = (rid == cid).astype(F32)
    ps = [eye + n for n in ns]
    ps = [p + prod(p, n2) for p, n2 in zip(ps, n2s)]
    ps = [p + prod(p, n4) for p, n4 in zip(ps, n4s)]
    s = SUBLANES
    while s < c:
        offs = [jnp.where(same(2 * s) & jnp.logical_not(same(s)), m, 0.0) for m in ms]
        ts = [prod(p, off) for p, off in zip(ps, offs)]
        ps = [p - prod(t, p) for p, t in zip(ps, ts)]
        s *= 2
    return ps


def _iota(shape, dim):
    return lax.broadcasted_iota(jnp.int32, shape, dim)


def _rms_kernel(x_ref, g_ref, o_ref):
    x = x_ref[...]
    ms = jnp.mean(x * x, axis=-1, keepdims=True)
    o_ref[...] = (x * lax.rsqrt(ms + NORM_EPS) * g_ref[...]).astype(o_ref.dtype)


def _rmsnorm(x, g, out_dtype):
    n, d = x.shape
    tm = _pick(n, (1024, 512, 256, 128, 64, 32, 16))
    return pl.pallas_call(
        _rms_kernel,
        out_shape=jax.ShapeDtypeStruct((n, d), out_dtype),
        grid=(n // tm,),
        in_specs=[pl.BlockSpec((tm, d), lambda i: (i, 0)), pl.BlockSpec((1, d), lambda i: (0, 0))],
        out_specs=pl.BlockSpec((tm, d), lambda i: (i, 0)),
        compiler_params=_cparams(("parallel",)),
        name="rmsnorm",
    )(x, g.reshape(1, d))


def _mm_kernel(x_ref, w_ref, o_ref):
    o_ref[...] = _dot(x_ref[...], w_ref[...])


def _matmul(x, w, name):
    n, k = x.shape
    m = w.shape[1]
    tm = _pick(n, (2048, 1024, 512, 256, 128, 64, 32, 16))
    tn = _pick(m, (1024, 896, 768, 512, 256, 128))
    return pl.pallas_call(
        _mm_kernel,
        out_shape=jax.ShapeDtypeStruct((n, m), F32),
        grid=(n // tm, m // tn),
        in_specs=[pl.BlockSpec((tm, k), lambda i, j: (i, 0)), pl.BlockSpec((k, tn), lambda i, j: (0, j))],
        out_specs=pl.BlockSpec((tm, tn), lambda i, j: (i, j)),
        compiler_params=_cparams(("parallel", "parallel")),
        name=name,
    )(x, w)


def _head_sum_matrix(width, seg):
    i = np.arange(width)
    return jnp.asarray((i[:, None] // seg) == (i[None, :] // seg), BF16)


def _wkv_chunk_kernel(p_ref, first_ref, mu_ref, vec_ref, w2_ref, a2_ref, g2_ref, hm_ref, ln_ref, s0_ref,
                      o_ref, s1_ref, shift_ref, carry_scr, ht_scr, *, cs, nseq, ns, n_valid, multi_chunk):
    ci = pl.program_id(1)
    rows = cs * nseq
    npair = RW_HEADS // 2
    f_zero = jnp.zeros((RW_HD, RW_HD), F32)
    slots = range(ns)
    pairs = range(npair)
    seqs = [(s, q) for s in slots for q in range(nseq)]
    sidx = lambda s, q: s * nseq + q

    @pl.when(ci == 0)
    def _():
        for s, q in seqs:
            carry_scr[sidx(s, q)] = first_ref[sidx(s, q)]
            for pr in pairs:
                top = jnp.concatenate([s0_ref[sidx(s, q), 2 * pr], f_zero], axis=1)
                bot = jnp.concatenate([f_zero, s0_ref[sidx(s, q), 2 * pr + 1]], axis=1)
                ht_scr[sidx(s, q), pr] = jnp.concatenate([top, bot], axis=0)

    vec = vec_ref[...]
    w0, a0, k_k, k_a, r_k = vec[0:1], vec[1:2], vec[2:3], vec[3:4], vec[4:5]
    hm = hm_ref[...]
    mu = mu_ref[...]
    row_w = _iota((rows, RW_COLS), 0)
    row_f = _iota((rows, RW_W), 0)
    rid = _iota((rows, rows), 0)
    cid = _iota((rows, rows), 1)
    tri = ((rid >= cid) & (rid // cs == cid // cs)).astype(BF16)

    ps_, vs_, gs_, bonus_ = [], [], [], []
    abar, bbar, kbar, rbar, btil, ktil, w_last = [], [], [], [], [], [], []
    for s in slots:
        p = p_ref[0, s, 0]
        prev = pltpu.roll(p, 1, 0)
        for q in range(nseq):
            prev = jnp.where(row_w == q * cs, carry_scr[sidx(s, q)], prev)
        h = p + (prev - p) * mu
        r = h[:, 0:RW_W]
        k = h[:, RW_W:2 * RW_W]
        v = h[:, 2 * RW_W:3 * RW_W]
        lo = h[:, 3 * RW_W:3 * RW_W + LANES]
        gl = h[:, 3 * RW_W + LANES:3 * RW_W + 2 * LANES]
        w_log = -_softplus(-(w0 + _dot3(jnp.tanh(lo), w2_ref[...]))) - 0.5
        logw = -jnp.exp(w_log)
        a = _sigmoid(a0 + _dot3(lo, a2_ref[...]))
        gs_.append(_dot3(_sigmoid(gl), g2_ref[...]))
        kk = k * k_k
        kk = kk * lax.rsqrt(_segsum(kk * kk, hm) + NORM_EPS)
        km = k * (1.0 + (a - 1.0) * k_a)
        bb = kk * a
        alpha = -kk
        bonus_.append(_segsum(r * km * r_k, hm) * v)
        if n_valid < cs:
            valid = (row_f % cs) < n_valid
            logw, alpha, bb, km = (jnp.where(valid, t, 0.0) for t in (logw, alpha, bb, km))
        gcum = _mdot(tri, logw)
        g_last = jnp.concatenate(
            [jnp.broadcast_to(gcum[q * cs + cs - 1:q * cs + cs], (cs, RW_W)) for q in range(nseq)], axis=0)
        emg = jnp.exp(-gcum)
        etil = jnp.exp(g_last - gcum)
        ps_.append(p)
        vs_.append(v)
        abar.append(alpha * jnp.exp(gcum - logw))
        bbar.append(bb * emg)
        kbar.append(km * emg)
        rbar.append(r * jnp.exp(gcum))
        btil.append(bb * etil)
        ktil.append(km * etil)
        w_last.append(jnp.exp(g_last))

    prow = _iota((rows, LANES), 0)
    plane = _iota((rows, LANES), 1)
    pcol = plane % RW_HD
    same_seq = (prow // cs) == (pcol // cs)
    strict = (pcol < prow) & same_seq
    incl = (pcol <= prow) & same_seq
    first_head = plane < RW_HD
    blk_mask = (_iota((LANES, LANES), 0) // RW_HD) == (_iota((LANES, LANES), 1) // RW_HD)

    def expand(x):
        return jnp.concatenate([jnp.where(first_head, x, 0.0), jnp.where(first_head, 0.0, x)], axis=0)

    items = [(s, pr) for s in slots for pr in pairs]
    ps = lambda x, pr: x[:, pr * LANES:(pr + 1) * LANES]
    seq_rows = lambda x, q: x[q * cs:(q + 1) * cs]

    lhs = [jnp.concatenate([ps(abar[s], pr), ps(rbar[s], pr)], axis=0) for s, pr in items]
    xbk = [_dot3(l, jnp.concatenate([expand(ps(bbar[s], pr)), expand(ps(kbar[s], pr))], axis=0), 'nt')
           for l, (s, pr) in zip(lhs, items)]
    xb = [x[:, 0:LANES] for x in xbk]
    xk = [x[:, LANES:2 * LANES] for x in xbk]
    a_m = [jnp.where(strict, x[0:rows], 0.0) for x in xb]
    rb_m = [jnp.where(incl, x[rows:2 * rows], 0.0) for x in xb]
    b_m = [jnp.where(strict, x[0:rows], 0.0) for x in xk]
    rk_m = [jnp.where(incl, x[rows:2 * rows], 0.0) for x in xk]
    pinv = _unit_lower_inverses([-m for m in a_m], prow, pcol, cs, expand=expand)

    ht_old = {(s, q, pr): ht_scr[sidx(s, q), pr] for s, q in seqs for pr in pairs}
    xh = {(s, q, pr): _dot3(jnp.concatenate([seq_rows(ps(abar[s], pr), q), seq_rows(ps(rbar[s], pr), q)], axis=0),
                            ht_old[(s, q, pr)], 'nt') for s, q in seqs for pr in pairs}
    ah = [jnp.concatenate([xh[(s, q, pr)][0:cs] for q in range(nseq)], axis=0) for s, pr in items]
    rh = [jnp.concatenate([xh[(s, q, pr)][cs:2 * cs] for q in range(nseq)], axis=0) for s, pr in items]
    vexp = [expand(ps(vs_[s], pr)) for s, pr in items]
    bv_rkv = [_dot3(jnp.concatenate([b_, rk_], axis=0), ve) for b_, rk_, ve in zip(b_m, rk_m, vexp)]
    rhs = [a_ + x[0:rows] for a_, x in zip(ah, bv_rkv)]
    u = [_dot3(pi, expand(rh_)) for pi, rh_ in zip(pinv, rhs)]
    o_items = [rh_ + x[rows:2 * rows] + _dot3(rb_, expand(u_)) for rh_, x, rb_, u_ in zip(rh, bv_rkv, rb_m, u)]
    for i, (s, pr) in enumerate(items):
        for q in range(nseq):
            uv = jnp.concatenate([seq_rows(u[i], q), seq_rows(ps(vs_[s], pr), q)], axis=0)
            bk = jnp.concatenate([seq_rows(ps(btil[s], pr), q), seq_rows(ps(ktil[s], pr), q)], axis=0)
            upd = jnp.where(blk_mask, _dot3(uv, bk, 'tn'), 0.0)
            ht_scr[sidx(s, q), pr] = ht_old[(s, q, pr)] * seq_rows(ps(w_last[s], pr), q)[0:1] + upd

    ln = ln_ref[...]
    for s in slots:
        o = jnp.concatenate(o_items[s * npair:(s + 1) * npair], axis=1)
        mean = _segsum(o, hm) * (1.0 / RW_HD)
        xc = o - mean
        var = _segsum(xc * xc, hm) * (1.0 / RW_HD)
        y = xc * lax.rsqrt(var + RW_GN_EPS) * ln[0:1] + ln[1:2]
        o_ref[0, s, 0] = ((y + bonus_[s]) * gs_[s]).astype(o_ref.dtype)

    if multi_chunk:
        for s, q in seqs:
            carry_scr[sidx(s, q)] = ps_[s][q * cs + cs - 1:q * cs + cs]

    @pl.when(ci == pl.num_programs(1) - 1)
    def _():
        for s, q in seqs:
            shift_ref[sidx(s, q)] = ps_[s][q * cs + n_valid - 1:q * cs + n_valid]
            for pr in pairs:
                ht = ht_scr[sidx(s, q), pr]
                s1_ref[sidx(s, q), 2 * pr] = ht[0:RW_HD, 0:RW_HD]
                s1_ref[sidx(s, q), 2 * pr + 1] = ht[RW_HD:2 * RW_HD, RW_HD:2 * RW_HD]


def _wkv_group(p_rw, lp, bsz, lpad, n_valid_len, shift0, s0):
    rows = RW_HD
    if lpad >= rows:
        cs, nseq = rows, 1
    else:
        cs, nseq = lpad, rows // lpad
    nc = lpad // cs
    n_valid = cs if nc > 1 else n_valid_len
    units = bsz // nseq
    ns = WKV_SLOTS if units % WKV_SLOTS == 0 else 1
    vec = jnp.zeros((SUBLANES, RW_W), F32)
    vec = vec.at[0].set(lp['rwkv_w0']).at[1].set(lp['rwkv_a0']).at[2].set(lp['rwkv_k_k'])
    vec = vec.at[3].set(lp['rwkv_k_a']).at[4].set(lp['rwkv_r_k'].reshape(RW_W))
    zeros = jnp.zeros((RW_DECAY_LORA, RW_W), F32)
    w2p = jnp.concatenate([lp['rwkv_w2'], zeros], axis=0)
    a2p = jnp.concatenate([zeros, lp['rwkv_a2']], axis=0)
    hm = _head_sum_matrix(RW_W, RW_HD)
    ln = jnp.zeros((SUBLANES, RW_W), F32).at[0].set(lp['rwkv_ln_g']).at[1].set(lp['rwkv_ln_b'])
    mu = lp['rwkv_mu'].reshape(1, RW_COLS)
    first = shift0.reshape(bsz, 1, RW_COLS)
    full = lambda a: pl.BlockSpec(a.shape, lambda b, j: (0,) * a.ndim)
    nq = ns * nseq
    sspec = pl.BlockSpec((nq, RW_HEADS, RW_HD, RW_HD), lambda b, j: (b, 0, 0, 0))
    fspec = pl.BlockSpec((nq, 1, RW_COLS), lambda b, j: (b, 0, 0))
    o, s1, shift1 = pl.pallas_call(
        functools.partial(_wkv_chunk_kernel, cs=cs, nseq=nseq, ns=ns, n_valid=n_valid, multi_chunk=nc > 1),
        out_shape=[jax.ShapeDtypeStruct((units // ns, ns, nc, rows, RW_W), BF16),
                   jax.ShapeDtypeStruct((bsz, RW_HEADS, RW_HD, RW_HD), F32),
                   jax.ShapeDtypeStruct((bsz, 1, RW_COLS), F32)],
        grid=(units // ns, nc),
        in_specs=[_slot_spec(ns, rows, RW_COLS), fspec, full(mu), full(vec),
                  full(w2p), full(a2p), full(lp['rwkv_g2']), full(hm), full(ln), sspec],
        out_specs=[_slot_spec(ns, rows, RW_W), sspec, fspec],
        scratch_shapes=[pltpu.VMEM((nq, 1, RW_COLS), F32),
                        pltpu.VMEM((nq, RW_HEADS // 2, LANES, LANES), F32)],
        compiler_params=_cparams(("parallel", "arbitrary")),
        name="wkv_chunk",
    )(_slot_view(p_rw, units, ns, nc, rows), first, mu, vec, w2p, a2p, lp['rwkv_g2'], hm, ln, s0)
    return o.reshape(bsz * lpad, RW_W), shift1.reshape(bsz, RW_COLS), s1


def _gdn_kernel(qkv_ref, z_ref, ba_ref, convw_ref, hp_ref, ng_ref, tail0_ref, s0_ref,
                o_ref, s1_ref, conv_ref, ext_scr, s_scr, *, c, nb, n_valid):
    ci = pl.program_id(1)
    last = ci == pl.num_programs(1) - 1
    slots = range(nb)
    heads = range(GD_HEADS)
    items = [(s, h) for s in slots for h in heads]

    @pl.when(ci == 0)
    def _():
        for s in slots:
            ext_scr[s, 0:SUBLANES, :] = tail0_ref[s]
            s_scr[s] = s0_ref[s]

    cw = convw_ref[...]
    hp = hp_ref[...]
    ng = ng_ref[...]
    off = SUBLANES - (CONV_W - 1)
    rid = _iota((c, c), 0)
    cid = _iota((c, c), 1)
    incl = rid >= cid
    tri = incl.astype(BF16)
    strict_l = (rid > cid).astype(F32)
    kbase, vbase = GD_HEADS * GD_DK, 2 * GD_HEADS * GD_DK

    cqs, beta_alls, g_alls, gcum_alls = [], [], [], []
    for s in slots:
        ext_scr[s, SUBLANES:SUBLANES + c, :] = qkv_ref[0, s, 0]
        cq = ext_scr[s, off:off + c, :] * cw[0:1]
        for j in range(1, CONV_W):
            cq = cq + ext_scr[s, off + j:off + j + c, :] * cw[j:j + 1]

        @pl.when(last)
        def _():
            conv_ref[s] = ext_scr[s, off + n_valid:off + n_valid + CONV_W - 1, :]

        ext_scr[s, 0:SUBLANES, :] = ext_scr[s, c:c + SUBLANES, :]
        cqs.append(_silu(cq))
        ba = ba_ref[0, s, 0]
        beta_all = _sigmoid(ba)
        g_all = -jnp.exp(hp[0:1]) * _softplus(ba + hp[1:2])
        if n_valid < c:
            valid = _iota((c, LANES), 0) < n_valid
            beta_all = jnp.where(valid, beta_all, 0.0)
            g_all = jnp.where(valid, g_all, 0.0)
        beta_alls.append(beta_all)
        g_alls.append(g_all)
        gcum_alls.append(_mdot(tri, g_all))

    qs = [cqs[s][:, h * GD_DK:(h + 1) * GD_DK] for s, h in items]
    ks = [cqs[s][:, kbase + h * GD_DK:kbase + (h + 1) * GD_DK] for s, h in items]
    vs = [cqs[s][:, vbase + h * GD_DV:vbase + (h + 1) * GD_DV] for s, h in items]
    qs = [q * lax.rsqrt(jnp.sum(q * q, axis=-1, keepdims=True) + NORM_EPS) * (GD_DK ** -0.5) for q in qs]
    ks = [k * lax.rsqrt(jnp.sum(k * k, axis=-1, keepdims=True) + NORM_EPS) for k in ks]
    betas = [beta_alls[s][:, h:h + 1] for s, h in items]
    g_cols = [g_alls[s][:, GD_HEADS + h:GD_HEADS + h + 1] for s, h in items]
    gcs = [gcum_alls[s][:, GD_HEADS + h:GD_HEADS + h + 1] for s, h in items]
    glasts = [gc[c - 1:c, :] for gc in gcs]
    egcs = [jnp.exp(gc) for gc in gcs]
    s_old = [s_scr[s, h] for s, h in items]
    gcum_rows = [jnp.transpose(gcum_alls[s]) for s in slots]
    decs = [jnp.where(incl, jnp.exp(jnp.minimum(gc - gcum_rows[s][GD_HEADS + h:GD_HEADS + h + 1, :], 0.0)), 0.0)
            for (s, h), gc in zip(items, gcs)]
    qk_kts = [_dot3(jnp.concatenate([q, k], axis=0), k, 'nt') for q, k in zip(qs, ks)]
    ms = [strict_l * b * qk[c:2 * c] * dec for b, qk, dec in zip(betas, qk_kts, decs)]
    rhss = [jnp.concatenate([v * b, k * (b * e)], axis=1) for v, k, b, e in zip(vs, ks, betas, egcs)]
    if c >= 2 * SUBLANES:
        xss = [_dot3(p, rhs) for p, rhs in zip(_unit_lower_inverses(ms, rid, cid, c), rhss)]
    else:
        strict_u = (rid < cid).astype(F32)
        dts = [_mdot((rid < cid).astype(BF16), g * (rid <= cid).astype(F32)) for g in g_cols]
        mts = [strict_u * _dot3(k, k * b, 'nt') * jnp.exp(jnp.minimum(dt, 0.0))
               for k, b, dt in zip(ks, betas, dts)]
        xss = rhss
        for i in range(1, c):
            row_i = _iota((c, GD_DV + GD_DK), 0) == i
            xss = [jnp.where(row_i, xs - jnp.sum(mt[:, i:i + 1] * xs, axis=0, keepdims=True), xs)
                   for xs, mt in zip(xss, mts)]
    ws_qss = [_dot3(jnp.concatenate([xs[:, GD_DV:GD_DV + GD_DK], q * e], axis=0), s)
              for xs, q, e, s in zip(xss, qs, egcs, s_old)]
    v_news = [xs[:, 0:GD_DV] - wq[0:c] for xs, wq in zip(xss, ws_qss)]
    os_ = [wq[c:2 * c] + _dot3(qk[0:c] * dec, vn) for wq, qk, dec, vn in zip(ws_qss, qk_kts, decs, v_news)]
    s_new = [s * jnp.exp(gl) + _dot3(k * jnp.exp(gl - gc), vn, 'tn')
             for s, gl, gc, k, vn in zip(s_old, glasts, gcs, ks, v_news)]
    os_ = [o * lax.rsqrt(jnp.mean(o * o, axis=-1, keepdims=True) + NORM_EPS) * ng for o in os_]
    for s in slots:
        z = z_ref[0, s, 0]
        o_ref[0, s, 0] = jnp.concatenate(
            [(os_[s * GD_HEADS + h] * _silu(z[:, h * GD_DV:(h + 1) * GD_DV])).astype(o_ref.dtype) for h in heads],
            axis=1)
    for i, (s, h) in enumerate(items):
        s_scr[s, h] = s_new[i]

    @pl.when(last)
    def _():
        for s in slots:
            s1_ref[s] = s_scr[s]


def _slot_plan(bsz, lpad):
    c = _pick(lpad, (CHUNK_ROWS, 32, 16, 8))
    nb = max(SLOTS_MIN, CHUNK_ROWS // c)
    if bsz % nb:
        nb = 1
    return c, lpad // c, nb


def _slot_view(a, bsz, nb, nc, c):
    return a.reshape(bsz // nb, nb, nc, c, a.shape[-1])


def _slot_spec(nb, c, w):
    return pl.BlockSpec((1, nb, 1, c, w), lambda b, j: (b, 0, j, 0, 0))


def _gdn_group(pqkv, pz, pba, lp, bsz, lpad, n_valid_len, conv0, s0):
    c, nc, nb = _slot_plan(bsz, lpad)
    n_valid = c if nc > 1 else n_valid_len
    tail0 = jnp.concatenate([jnp.zeros((bsz, SUBLANES - (CONV_W - 1), GD_QKV), F32), conv0], axis=1)
    hp = jnp.zeros((SUBLANES, LANES), F32)
    hp = hp.at[0, GD_HEADS:2 * GD_HEADS].set(lp['gdn_a_log']).at[1, GD_HEADS:2 * GD_HEADS].set(lp['gdn_dt_bias'])
    ng = lp['gdn_norm_g'].reshape(1, GD_DV)
    view = lambda a: _slot_view(a, bsz, nb, nc, c)
    full = lambda a: pl.BlockSpec(a.shape, lambda b, j: (0,) * a.ndim)
    sspec = pl.BlockSpec((nb, GD_HEADS, GD_DK, GD_DV), lambda b, j: (b, 0, 0, 0))
    o, s1, conv1 = pl.pallas_call(
        functools.partial(_gdn_kernel, c=c, nb=nb, n_valid=n_valid),
        out_shape=[jax.ShapeDtypeStruct((bsz // nb, nb, nc, c, GD_W), BF16),
                   jax.ShapeDtypeStruct((bsz, GD_HEADS, GD_DK, GD_DV), F32),
                   jax.ShapeDtypeStruct((bsz, CONV_W - 1, GD_QKV), F32)],
        grid=(bsz // nb, nc),
        in_specs=[_slot_spec(nb, c, GD_QKV), _slot_spec(nb, c, GD_W), _slot_spec(nb, c, LANES),
                  full(lp['gdn_conv']), full(hp), full(ng),
                  pl.BlockSpec((nb, SUBLANES, GD_QKV), lambda b, j: (b, 0, 0)), sspec],
        out_specs=[_slot_spec(nb, c, GD_W), sspec,
                   pl.BlockSpec((nb, CONV_W - 1, GD_QKV), lambda b, j: (b, 0, 0))],
        scratch_shapes=[pltpu.VMEM((nb, c + SUBLANES, GD_QKV), F32),
                        pltpu.VMEM((nb, GD_HEADS, GD_DK, GD_DV), F32)],
        compiler_params=_cparams(("parallel", "arbitrary")),
        name="gdn_chunk",
    )(view(pqkv), view(pz), view(pba), lp['gdn_conv'], hp, ng, tail0, s0)
    return o.reshape(bsz * lpad, GD_W), s1, conv1


def _hgrn_kernel(p_ref, lbv_ref, ng_ref, s0_ref, o_ref, s1_ref, st_scr, *, c, nb, n_valid):
    ci = pl.program_id(1)
    slots = range(nb)
    heads = range(HG_HEADS)
    items = [(s, h) for s in slots for h in heads]

    @pl.when(ci == 0)
    def _():
        for s, h in items:
            st_scr[s, h] = jnp.transpose(s0_ref[s, h])

    rid = _iota((c, c), 0)
    cid = _iota((c, c), 1)
    tri = (rid >= cid).astype(BF16)
    w = HG_HEADS * HG_DF
    rows = _iota((c, w), 0)
    sb = min(c, HG_SUB)
    row_in_sub = rows % sb
    lbv = lbv_ref[...]
    hs = lambda x, h: x[:, h * HG_DF:(h + 1) * HG_DF]

    qs, ks, vs, bcs = [], [], [], []
    for s in slots:
        p = p_ref[0, s, 0]
        pf = p[:, w:2 * w]
        a = lbv[0:1]
        b = lbv[1:2] - _softplus(-pf)
        logf = jnp.maximum(a, b) + jnp.log1p(jnp.exp(-jnp.abs(a - b)))
        k = lbv[2:3] * _sigmoid(-pf)
        if n_valid < c:
            logf = jnp.where(rows < n_valid, logf, 0.0)
            k = jnp.where(rows < n_valid, k, 0.0)
        qs.append(_silu(p[:, 0:w]))
        ks.append(k)
        vs.append(p[:, 2 * w:2 * w + HG_W])
        bcs.append(_mdot(tri, logf))
    st_old = [st_scr[s, h] for s, h in items]
    qes = [q * jnp.exp(bc) for q, bc in zip(qs, bcs)]
    os_ = [_dot3(hs(qes[s], h), st, 'nt') for (s, h), st in zip(items, st_old)]
    for delta in range(sb):
        prods, v_ss = [], []
        for s in slots:
            if delta == 0:
                k_s, b_s, v_s = ks[s], bcs[s], vs[s]
            else:
                k_s, b_s, v_s = (pltpu.roll(t, delta, 0) for t in (ks[s], bcs[s], vs[s]))
            prods.append(jnp.where(row_in_sub >= delta,
                                   qs[s] * k_s * jnp.exp(jnp.minimum(bcs[s] - b_s, 0.0)), 0.0))
            v_ss.append(v_s)
        os_ = [o + jnp.sum(hs(prods[s], h), axis=-1, keepdims=True) * hs(v_ss[s], h)
               for (s, h), o in zip(items, os_)]
    if c > sb:
        parts = [[jnp.zeros((sb, HG_DV), F32)] for _ in items]
        for r0 in range(sb, c, sb):
            qis, kps = [], []
            for s in slots:
                ref = bcs[s][r0 - 1:r0]
                qis.append(qs[s][r0:r0 + sb] * jnp.exp(bcs[s][r0:r0 + sb] - ref))
                kps.append(ks[s][0:r0] * jnp.exp(ref - bcs[s][0:r0]))
            att = [_dot3(hs(qis[s], h), hs(kps[s], h), 'nt') for s, h in items]
            for i, (s, h) in enumerate(items):
                parts[i].append(_dot3(att[i], hs(vs[s], h)[0:r0]))
        os_ = [o + jnp.concatenate(p, axis=0) for o, p in zip(os_, parts)]
    blasts = [bc[c - 1:c] for bc in bcs]
    kds = [k * jnp.exp(bl - bc) for k, bl, bc in zip(ks, blasts, bcs)]
    ebs = [jnp.exp(bl) for bl in blasts]
    st_new = [st * hs(ebs[s], h) + _dot3(hs(vs[s], h), hs(kds[s], h), 'tn') for (s, h), st in zip(items, st_old)]
    os_ = [o * lax.rsqrt(jnp.mean(o * o, axis=-1, keepdims=True) + NORM_EPS) for o in os_]
    ng = ng_ref[...]
    for s in slots:
        gate = _sigmoid(p_ref[0, s, 0][:, 2 * w + HG_W:2 * w + 2 * HG_W])
        o_all = jnp.concatenate(os_[s * HG_HEADS:(s + 1) * HG_HEADS], axis=1) * ng * gate
        o_ref[0, s, 0] = o_all.astype(o_ref.dtype)
    for i, (s, h) in enumerate(items):
        st_scr[s, h] = st_new[i]

    @pl.when(ci == pl.num_programs(1) - 1)
    def _():
        for s, h in items:
            s1_ref[s, h] = jnp.transpose(st_scr[s, h])


def _hgrn_group(phg, lb, lp, bsz, lpad, n_valid_len, s0):
    c, nc, nb = _slot_plan(bsz, lpad)
    n_valid = c if nc > 1 else n_valid_len
    lbv = jnp.zeros((SUBLANES, HG_HEADS * HG_DF), F32)
    lbv = lbv.at[0].set(jnp.log(jnp.maximum(lb, LB_FLOOR))).at[1].set(jnp.log1p(-lb)).at[2].set(1.0 - lb)
    ng = lp['hgrn_norm_g'].reshape(1, HG_W)
    full = lambda a: pl.BlockSpec(a.shape, lambda b, j: (0,) * a.ndim)
    sspec = pl.BlockSpec((nb, HG_HEADS, HG_DF, HG_DV), lambda b, j: (b, 0, 0, 0))
    o, s1 = pl.pallas_call(
        functools.partial(_hgrn_kernel, c=c, nb=nb, n_valid=n_valid),
        out_shape=[jax.ShapeDtypeStruct((bsz // nb, nb, nc, c, HG_W), BF16),
                   jax.ShapeDtypeStruct((bsz, HG_HEADS, HG_DF, HG_DV), F32)],
        grid=(bsz // nb, nc),
        in_specs=[_slot_spec(nb, c, HG_COLS), full(lbv), full(ng), sspec],
        out_specs=[_slot_spec(nb, c, HG_W), sspec],
        scratch_shapes=[pltpu.VMEM((nb, HG_HEADS, HG_DV, HG_DF), F32)],
        compiler_params=_cparams(("parallel", "arbitrary")),
        name="hgrn_chunk",
    )(_slot_view(phg, bsz, nb, nc, c), lbv, ng, s0)
    return o.reshape(bsz * lpad, HG_W), s1


def _ret_kernel(p_ref, cos_ref, sin_ref, qd_ref, kd_ref, dec_ref, dm_ref, gn_ref, s0_ref,
                o_ref, s1_ref, s_scr, *, c, nb):
    ci = pl.program_id(1)
    slots = range(nb)
    heads = range(RT_HEADS)
    items = [(s, h) for s in slots for h in heads]

    @pl.when(ci == 0)
    def _():
        for s in slots:
            s_scr[s] = jnp.zeros((RT_QK, RT_W), F32)
            for h in heads:
                s_scr[s, h * RT_DK:(h + 1) * RT_DK, h * RT_DV:(h + 1) * RT_DV] = s0_ref[s, h]

    cos = cos_ref[...]
    sin = sin_ref[...]
    lane = _iota((c, RT_QK), 1)
    first_half = (lane % RT_DK) < (RT_DK // 2)

    def rope(x):
        partner = jnp.where(first_half, pltpu.roll(x, RT_QK - RT_DK // 2, 1), pltpu.roll(x, RT_DK // 2, 1))
        return x * cos + partner * sin

    ps = [p_ref[0, s, 0] for s in slots]
    qs = [rope(p[:, 0:RT_QK]) for p in ps]
    ks = [rope(p[:, RT_QK:2 * RT_QK]) * (RT_DK ** -0.5) for p in ps]
    pvs = [p[:, 2 * RT_QK:2 * RT_QK + RT_W] for p in ps]
    sblks = [s_scr[s] for s in slots]
    qd = qd_ref[...]
    o_inters = [_dot3(q * qd, sblk) for q, sblk in zip(qs, sblks)]
    qks = [_dot3(jnp.concatenate([jnp.where(lane // RT_DK == h, q, 0.0) for h in heads], axis=0), k, 'nt')
           for q, k in zip(qs, ks)]
    attns = [qks[s][h * c:(h + 1) * c] * dec_ref[h] for s, h in items]
    os_ = [o_inters[s][:, h * RT_DV:(h + 1) * RT_DV] + _dot3(attn, pvs[s][:, h * RT_DV:(h + 1) * RT_DV])
           for (s, h), attn in zip(items, attns)]
    dm = dm_ref[...]
    kd = kd_ref[...]
    s_new = [sblk * dm + jnp.where(dm > 0.0, _dot3(k * kd, pv, 'tn'), 0.0) for sblk, k, pv in zip(sblks, ks, pvs)]
    gn = gn_ref[...]
    normed = []
    for o in os_:
        xc = o - jnp.mean(o, axis=-1, keepdims=True)
        normed.append(xc * lax.rsqrt(jnp.mean(xc * xc, axis=-1, keepdims=True) + NORM_EPS))
    for s in slots:
        gate = _silu(ps[s][:, 2 * RT_QK + RT_W:2 * RT_QK + 2 * RT_W])
        o_all = jnp.concatenate(normed[s * RT_HEADS:(s + 1) * RT_HEADS], axis=1) * gn * gate
        o_ref[0, s, 0] = o_all.astype(o_ref.dtype)
        s_scr[s] = s_new[s]

    @pl.when(ci == pl.num_programs(1) - 1)
    def _():
        for s, h in items:
            s1_ref[s, h] = s_scr[s, h * RT_DK:(h + 1) * RT_DK, h * RT_DV:(h + 1) * RT_DV]


def _ret_group(prt, lp, bsz, lpad, n_valid_len, pos0, s0):
    c, nc, nb = _slot_plan(bsz, lpad)
    n_valid = c if nc > 1 else n_valid_len
    half = RT_DK // 2
    inv = ROPE_BASE ** (-np.arange(half, dtype=np.float64) / half)
    ang = (pos0 + np.arange(lpad, dtype=np.float64))[:, None] * inv[None, :]
    cos = np.tile(np.cos(ang), (1, 2 * RT_HEADS))
    sin = np.tile(np.concatenate([-np.sin(ang), np.sin(ang)], axis=1), (1, RT_HEADS))
    loggamma = np.log(1.0 - np.exp2(-5.0 - np.arange(RT_HEADS, dtype=np.float64)))
    gcum = loggamma[:, None] * np.arange(1, c + 1, dtype=np.float64)[None, :]
    idx = np.arange(c)
    dec = np.where(idx[:, None] >= idx[None, :], np.exp(gcum[:, :, None] - gcum[:, None, :]), 0.0)
    qd = np.repeat(np.exp(gcum).T, RT_DK, axis=1)
    kdec = np.where(idx[None, :] < n_valid, np.exp(gcum[:, n_valid - 1:n_valid] - gcum), 0.0)
    kd = np.repeat(kdec.T, RT_DK, axis=1)
    sdec = np.exp(gcum[:, n_valid - 1])
    dm = np.zeros((RT_QK, RT_W))
    for h in range(RT_HEADS):
        dm[h * RT_DK:(h + 1) * RT_DK, h * RT_DV:(h + 1) * RT_DV] = sdec[h]
    cos, sin, qd, kd, dec, dm = (jnp.asarray(a, F32) for a in (cos, sin, qd, kd, dec, dm))
    gn = lp['ret_gn_g'].reshape(1, RT_W)
    full = lambda a: pl.BlockSpec(a.shape, lambda b, j: (0,) * a.ndim)
    posspec = pl.BlockSpec((c, RT_QK), lambda b, j: (j, 0))
    sspec = pl.BlockSpec((nb, RT_HEADS, RT_DK, RT_DV), lambda b, j: (b, 0, 0, 0))
    o, s1 = pl.pallas_call(
        functools.partial(_ret_kernel, c=c, nb=nb),
        out_shape=[jax.ShapeDtypeStruct((bsz // nb, nb, nc, c, RT_W), BF16),
                   jax.ShapeDtypeStruct((bsz, RT_HEADS, RT_DK, RT_DV), F32)],
        grid=(bsz // nb, nc),
        in_specs=[_slot_spec(nb, c, RT_COLS), posspec, posspec,
                  full(qd), full(kd), full(dec), full(dm), full(gn), sspec],
        out_specs=[_slot_spec(nb, c, RT_W), sspec],
        scratch_shapes=[pltpu.VMEM((nb, RT_QK, RT_W), F32)],
        compiler_params=_cparams(("parallel", "arbitrary")),
        name="ret_chunk",
    )(_slot_view(prt, bsz, nb, nc, c), cos, sin, qd, kd, dec, dm, gn, s0)
    return o.reshape(bsz * lpad, RT_W), s1


def _merge_kernel(x_ref, o1, o2, o3, o4, gate_ref, w1, w2, w3, w4, wo_ref, out_ref, *, d):
    acc = None
    for i, (o, w) in enumerate(((o1, w1), (o2, w2), (o3, w3), (o4, w4))):
        term = _sigmoid(gate_ref[:, i * d:(i + 1) * d]) * _dot(o[...], w[...])
        acc = term if acc is None else acc + term
    out_ref[...] = x_ref[...] + _dot(acc.astype(BF16), wo_ref[...])


def _merge(x, outs, gate, wouts, wo):
    n, d = x.shape
    tm = _pick(n, (256, 128, 64, 32, 16))
    row = lambda w: pl.BlockSpec((tm, w), lambda i: (i, 0))
    full = lambda a: pl.BlockSpec(a.shape, lambda i: (0,) * a.ndim)
    return pl.pallas_call(
        functools.partial(_merge_kernel, d=d),
        out_shape=jax.ShapeDtypeStruct((n, d), F32),
        grid=(n // tm,),
        in_specs=[row(d)] + [row(o.shape[1]) for o in outs] + [row(N_BRANCH * d)]
        + [full(w) for w in wouts] + [full(wo)],
        out_specs=row(d),
        compiler_params=_cparams(("parallel",)),
        name="merge",
    )(x, *outs, gate, *wouts, wo)


def _route_kernel(x_ref, g_ref, rw_ref, rb_ref, xn_ref, col_ref, row_ref, cnt_ref):
    tm = x_ref.shape[0]
    lane = _iota((tm, LANES), 1)
    lanef = lane.astype(F32)
    x = x_ref[...]
    xn = x * lax.rsqrt(jnp.mean(x * x, axis=-1, keepdims=True) + NORM_EPS) * g_ref[...]
    xn_ref[...] = xn.astype(BF16)
    rb = rb_ref[...]
    neg = jnp.float32(-jnp.inf)
    logits = _dot3(xn, rw_ref[...])
    glog = jnp.where(lane < N_GROUPS, logits[:, 0:LANES] + rb[0:1], neg)
    gmax = jnp.max(glog, axis=-1, keepdims=True)
    gsum = jnp.sum(jnp.exp(glog - gmax), axis=-1, keepdims=True)
    gidx = jnp.min(jnp.where(glog == gmax, lanef, float(LANES)), axis=-1, keepdims=True)
    gp = 1.0 / gsum
    in_group = (lanef >= gidx * EXPERTS_PER_GROUP) & (lanef < (gidx + 1.0) * EXPERTS_PER_GROUP)
    elog = jnp.where(in_group, logits[:, LANES:2 * LANES] + rb[1:2], neg)
    emax = jnp.max(elog, axis=-1, keepdims=True)
    eexp = jnp.exp(elog - emax)
    ep = eexp / jnp.sum(eexp, axis=-1, keepdims=True)
    ep = jnp.where(in_group, ep, -1.0)
    p1 = jnp.max(ep, axis=-1, keepdims=True)
    i1 = jnp.min(jnp.where(ep == p1, lanef, float(LANES)), axis=-1, keepdims=True)
    ep2 = jnp.where(lanef == i1, -1.0, ep)
    p2 = jnp.max(ep2, axis=-1, keepdims=True)
    i2 = jnp.min(jnp.where(ep2 == p2, lanef, float(LANES)), axis=-1, keepdims=True)
    denom = p1 + p2
    wt1 = gp * p1 / denom
    wt2 = gp * p2 / denom

    onehot = ((lanef == i1) | (lanef == i2)).astype(F32)
    cnt = jnp.sum(onehot, axis=0, keepdims=True)
    before = (_iota((LANES, LANES), 0) < _iota((LANES, LANES), 1)).astype(BF16)
    off = _segsum(jnp.broadcast_to(cnt, (SUBLANES, LANES)), before)[0:1]
    tri = (_iota((tm, tm), 0) >= _iota((tm, tm), 1)).astype(BF16)
    slot = off + _dot(tri, onehot.astype(BF16)) - 1.0
    pos1 = jnp.sum(jnp.where(lanef == i1, slot, 0.0), axis=-1, keepdims=True)
    pos2 = jnp.sum(jnp.where(lanef == i2, slot, 0.0), axis=-1, keepdims=True)
    col = jnp.where(lane == 0, pos1, jnp.where(lane == 1, pos2, jnp.where(lane == 2, wt1,
                                                                         jnp.where(lane == 3, wt2, 0.0))))
    col_ref[...] = col
    row_ref[0] = jnp.transpose(col)[0:SUBLANES]
    cnt_ref[0] = jnp.concatenate([cnt, off, jnp.zeros((SUBLANES - 2, LANES), F32)], axis=0)


def _moe_kernel(cnt_sm, off_sm, x_ref, xn_ref, col_ref, row_ref, wg_ref, wu_ref, wd_ref, out_ref,
                xs_scr, ys_scr, ws_scr, *, pb, rb, eps):
    i = pl.program_id(0)
    eg = pl.program_id(1)
    tm, d = x_ref.shape
    ns = 2 * tm

    @pl.when(eg == 0)
    def _():
        rowd = row_ref[0]
        pos1, pos2, wt1, wt2 = rowd[0:1], rowd[1:2], rowd[2:3], rowd[3:4]
        for blk in range(ns // pb):
            sid = (_iota((pb, tm), 0) + blk * pb).astype(F32)
            m1 = sid == pos1
            m2 = sid == pos2
            xs_scr[blk * pb:(blk + 1) * pb, :] = _dot((m1 | m2).astype(BF16), xn_ref[...]).astype(BF16)
            wsl = jnp.sum(jnp.where(m1, wt1, 0.0) + jnp.where(m2, wt2, 0.0), axis=-1, keepdims=True)
            ws_scr[blk * pb:(blk + 1) * pb, :] = jnp.broadcast_to(wsl, (pb, LANES))
        ys_scr[...] = jnp.zeros_like(ys_scr)

    for k in range(eps):
        cnt = cnt_sm[i, eg * eps + k]
        off = off_sm[i, eg * eps + k]
        start = (off // BF16_ROWS) * BF16_ROWS

        def body(j, carry, k=k, cnt=cnt, off=off, start=start):
            own = start + j * rb
            r0 = pl.multiple_of(jnp.minimum(own, ns - rb), BF16_ROWS)
            xb = xs_scr[pl.ds(r0, rb), :]
            hid = _silu(_dot(xb, wg_ref[k])) * _dot(xb, wu_ref[k])
            y = _dot(hid.astype(BF16), wd_ref[k])
            srow = r0 + _iota((rb, d), 0)
            mine = (srow >= jnp.maximum(off, own)) & (srow < off + cnt)
            ys_scr[pl.ds(r0, rb), :] += jnp.where(mine, y, 0.0)
            return carry

        lax.fori_loop(0, (off + cnt - start + rb - 1) // rb, body, 0)

    @pl.when(eg == pl.num_programs(1) - 1)
    def _():
        col = col_ref[...]
        pos1, pos2 = col[:, 0:1], col[:, 1:2]
        acc = x_ref[...]
        for blk in range(ns // pb):
            sid = (_iota((tm, pb), 1) + blk * pb).astype(F32)
            pt = ((sid == pos1) | (sid == pos2)).astype(BF16)
            ysw = ys_scr[blk * pb:(blk + 1) * pb, :] * ws_scr[blk * pb:(blk + 1) * pb, 0:1]
            hi, lo = _split2(ysw)
            acc = acc + _dot(pt, hi) + _dot(pt, lo)
        out_ref[...] = acc


def _moe(x, lp, experts):
    n, d = x.shape
    tm = _pick(n, (MOE_TILE, 512, 256, 128, 64))
    nt = n // tm
    de = experts[0].shape[-1]
    rw = jnp.zeros((d, 2 * LANES), F32)
    rw = rw.at[:, :N_GROUPS].set(lp['router_group']).at[:, LANES:LANES + N_EXPERTS].set(lp['router_expert'])
    rbias = jnp.zeros((SUBLANES, LANES), F32)
    rbias = rbias.at[0, :N_GROUPS].set(lp['router_group_b']).at[1, :N_EXPERTS].set(lp['router_expert_b'])
    g = lp['norm2_g'].reshape(1, d)
    full1 = lambda a: pl.BlockSpec(a.shape, lambda i: (0,) * a.ndim)
    xn, col, row, cnt = pl.pallas_call(
        _route_kernel,
        out_shape=[jax.ShapeDtypeStruct((n, d), BF16), jax.ShapeDtypeStruct((n, LANES), F32),
                   jax.ShapeDtypeStruct((nt, SUBLANES, tm), F32), jax.ShapeDtypeStruct((nt, SUBLANES, LANES), F32)],
        grid=(nt,),
        in_specs=[pl.BlockSpec((tm, d), lambda i: (i, 0)), full1(g), full1(rw), full1(rbias)],
        out_specs=[pl.BlockSpec((tm, d), lambda i: (i, 0)), pl.BlockSpec((tm, LANES), lambda i: (i, 0)),
                   pl.BlockSpec((1, SUBLANES, tm), lambda i: (i, 0, 0)),
                   pl.BlockSpec((1, SUBLANES, LANES), lambda i: (i, 0, 0))],
        compiler_params=_cparams(("parallel",)),
        name="moe_route",
    )(x, g, rw, rbias)
    cnt_i = cnt[:, 0, :N_EXPERTS].astype(jnp.int32)
    off_i = cnt[:, 1, :N_EXPERTS].astype(jnp.int32)
    ns = 2 * tm
    pb = min(MOE_SLOT_BLOCK, ns)
    rb = min(MOE_ROW_BLOCK, ns)
    grid_spec = pltpu.PrefetchScalarGridSpec(
        num_scalar_prefetch=2,
        grid=(nt, N_EXPERTS // MOE_EXPERTS_PER_STEP),
        in_specs=[pl.BlockSpec((tm, d), lambda i, e, c, o: (i, 0)),
                  pl.BlockSpec((tm, d), lambda i, e, c, o: (i, 0)),
                  pl.BlockSpec((tm, LANES), lambda i, e, c, o: (i, 0)),
                  pl.BlockSpec((1, SUBLANES, tm), lambda i, e, c, o: (i, 0, 0)),
                  pl.BlockSpec((MOE_EXPERTS_PER_STEP, d, de), lambda i, e, c, o: (e, 0, 0)),
                  pl.BlockSpec((MOE_EXPERTS_PER_STEP, d, de), lambda i, e, c, o: (e, 0, 0)),
                  pl.BlockSpec((MOE_EXPERTS_PER_STEP, de, d), lambda i, e, c, o: (e, 0, 0))],
        out_specs=pl.BlockSpec((tm, d), lambda i, e, c, o: (i, 0)),
        scratch_shapes=[pltpu.VMEM((ns, d), BF16), pltpu.VMEM((ns, d), F32), pltpu.VMEM((ns, LANES), F32)],
    )
    return pl.pallas_call(
        functools.partial(_moe_kernel, pb=pb, rb=rb, eps=MOE_EXPERTS_PER_STEP),
        out_shape=jax.ShapeDtypeStruct((n, d), F32),
        grid_spec=grid_spec,
        compiler_params=pltpu.CompilerParams(dimension_semantics=("parallel", "arbitrary"),
                                             vmem_limit_bytes=MOE_VMEM_LIMIT),
        name="moe",
    )(cnt_i, off_i, x, xn, col, row, *experts)


def _layer(x, wts, lp, lb, gr):
    b, seq, lpad = gr['bsz'], gr['seq'], gr['lpad']
    shift0, wkv0, conv0, gdn0, hgrn0, ret0 = gr['states']
    xn = _rmsnorm(x, lp['norm1_g'], BF16)
    p_rw, p_gqkv, p_gz, p_gba, p_hg, p_rt, p_gate = (
        _matmul(xn, w, f"proj{i}") for i, w in enumerate(wts['proj']))
    o_rw, shift1, wkv1 = _wkv_group(p_rw, lp, b, lpad, seq, shift0, wkv0)
    o_gd, gdn1, conv1 = _gdn_group(p_gqkv, p_gz, p_gba, lp, b, lpad, seq, conv0, gdn0)
    o_hg, hgrn1 = _hgrn_group(p_hg, lb, lp, b, lpad, seq, hgrn0)
    o_rt, ret1 = _ret_group(p_rt, lp, b, lpad, seq, gr['pos0'], ret0)
    x = _merge(x, [o_rw, o_gd, o_hg, o_rt], p_gate, wts['out'], wts['w_o'])
    x = _moe(x, lp, wts['experts'])
    return x, (shift1, wkv1, conv1, gdn1, hgrn1, ret1)


def _layer_weights(lp, d):
    offs = np.cumsum([0, RW_COLS, GD_QKV, GD_W, 2 * GD_HEADS, HG_COLS, RT_COLS, N_BRANCH * d])
    seg = [lp['w_in'][:, offs[i]:offs[i + 1]] for i in range(7)]
    seg[3] = jnp.pad(seg[3], ((0, 0), (0, LANES - 2 * GD_HEADS)))
    return dict(
        proj=[s.astype(BF16) for s in seg],
        out=[lp[n].astype(BF16) for n in ('w_out_rwkv', 'w_out_gdn', 'w_out_hgrn', 'w_out_ret')],
        w_o=lp['w_o'].astype(BF16),
        experts=[lp[n].astype(BF16) for n in ('moe_w_gate', 'moe_w_up', 'moe_w_down')])


def kernel(x_prompt, x_sample, state_rwkv_shift, state_rwkv_wkv, state_gdn_conv, state_gdn, state_hgrn, state_ret, norm1_g, w_in, rwkv_mu, rwkv_w0, rwkv_w2, rwkv_a0, rwkv_a2, rwkv_g2, rwkv_k_k, rwkv_k_a, rwkv_r_k, rwkv_ln_g, rwkv_ln_b, w_out_rwkv, gdn_conv, gdn_a_log, gdn_dt_bias, gdn_norm_g, w_out_gdn, hgrn_lb_logits, hgrn_norm_g, w_out_hgrn, ret_gn_g, w_out_ret, w_o, norm2_g, router_group, router_group_b, router_expert, router_expert_b, moe_w_gate, moe_w_up, moe_w_down, final_norm_g):
    params = dict(norm1_g=norm1_g, w_in=w_in, rwkv_mu=rwkv_mu, rwkv_w0=rwkv_w0, rwkv_w2=rwkv_w2,
                  rwkv_a0=rwkv_a0, rwkv_a2=rwkv_a2, rwkv_g2=rwkv_g2, rwkv_k_k=rwkv_k_k,
                  rwkv_k_a=rwkv_k_a, rwkv_r_k=rwkv_r_k, rwkv_ln_g=rwkv_ln_g, rwkv_ln_b=rwkv_ln_b,
                  w_out_rwkv=w_out_rwkv, gdn_conv=gdn_conv, gdn_a_log=gdn_a_log,
                  gdn_dt_bias=gdn_dt_bias, gdn_norm_g=gdn_norm_g, w_out_gdn=w_out_gdn,
                  hgrn_norm_g=hgrn_norm_g, w_out_hgrn=w_out_hgrn, ret_gn_g=ret_gn_g,
                  w_out_ret=w_out_ret, w_o=w_o, norm2_g=norm2_g, router_group=router_group,
                  router_group_b=router_group_b, router_expert=router_expert,
                  router_expert_b=router_expert_b, moe_w_gate=moe_w_gate, moe_w_up=moe_w_up,
                  moe_w_down=moe_w_down)
    depth = w_in.shape[0]
    bp, lp_len, d = x_prompt.shape
    bs, ls, _ = x_sample.shape
    ls_pad = -(-ls // SAMPLE_PAD_LEN) * SAMPLE_PAD_LEN
    sm = jax.nn.softmax(hgrn_lb_logits.astype(F32), axis=0)
    lower_bounds = jnp.cumsum(sm, axis=0) - sm[0]

    sample_states = (state_rwkv_shift, state_rwkv_wkv, state_gdn_conv, state_gdn, state_hgrn, state_ret)
    xp = x_prompt.reshape(bp * lp_len, d)
    xs = jnp.pad(x_sample, ((0, 0), (0, ls_pad - ls), (0, 0))).reshape(bs * ls_pad, d)

    prompt_out = [[] for _ in sample_states]
    sample_out = [[] for _ in sample_states]
    for layer in range(depth):
        lpar = {name: arr[layer] for name, arr in params.items()}
        wts = _layer_weights(lpar, d)
        prompt = dict(bsz=bp, seq=lp_len, lpad=lp_len, pos0=0,
                      states=tuple(jnp.zeros((bp,) + s.shape[2:], F32) for s in sample_states))
        sample = dict(bsz=bs, seq=ls, lpad=ls_pad, pos0=PAST_LEN,
                      states=tuple(s[layer].astype(F32) for s in sample_states))
        xp, new_p = _layer(xp, wts, lpar, lower_bounds[layer], prompt)
        xs, new_s = _layer(xs, wts, lpar, lower_bounds[layer], sample)
        for lst, n in zip(prompt_out, new_p):
            lst.append(n)
        for lst, n in zip(sample_out, new_s):
            lst.append(n)

    y_prompt = _rmsnorm(xp, final_norm_g, F32).reshape(bp, lp_len, d)
    y_sample = _rmsnorm(xs, final_norm_g, F32).reshape(bs, ls_pad, d)[:, :ls]
    p_states = [jnp.stack(lst) for lst in prompt_out]
    s_states = [jnp.stack(lst).astype(o.dtype) for lst, o in zip(sample_out, sample_states)]
    return (y_prompt, y_sample, *p_states, *s_states)
```

```python
import functools
import math

import numpy as np
import jax
import jax.numpy as jnp
from jax import lax
from jax.experimental import pallas as pl
from jax.experimental.pallas import tpu as pltpu

F32 = jnp.float32
BF16 = jnp.bfloat16

NORM_EPS = 1e-6
LB_FLOOR = 1e-30
PAST_LEN = 16384
RW_HEADS = 8
RW_HD = 64
RW_W = RW_HEADS * RW_HD
RW_DECAY_LORA = 64
RW_AAA_LORA = 64
RW_GATE_LORA = 128
RW_COLS = 3 * RW_W + RW_DECAY_LORA + RW_AAA_LORA + RW_GATE_LORA
RW_GN_EPS = 64e-5
GD_HEADS = 4
GD_DK = 128
GD_DV = 128
GD_QKV = GD_HEADS * (2 * GD_DK + GD_DV)
GD_W = GD_HEADS * GD_DV
CONV_W = 4
HG_HEADS = 4
HG_DF = 128
HG_DV = 128
HG_W = HG_HEADS * HG_DV
HG_COLS = 2 * HG_HEADS * HG_DF + 2 * HG_W
HG_SUB = 16
RT_HEADS = 4
RT_DK = 64
RT_DV = 128
RT_W = RT_HEADS * RT_DV
RT_QK = RT_HEADS * RT_DK
RT_COLS = 2 * RT_QK + 2 * RT_W
ROPE_BASE = 10000.0
N_BRANCH = 4
N_GROUPS = 4
EXPERTS_PER_GROUP = 8
N_EXPERTS = N_GROUPS * EXPERTS_PER_GROUP

LANES = 128
SUBLANES = 8
SAMPLE_PAD_LEN = 8
CHUNK_ROWS = 64
SLOTS_MIN = 4
WKV_SLOTS = 2
VMEM_LIMIT = 48 * 1024 * 1024
MOE_TILE = 1024
MOE_SLOT_BLOCK = 256
MOE_ROW_BLOCK = 128
MOE_EXPERTS_PER_STEP = 4
BF16_ROWS = 16
MOE_VMEM_LIMIT = 56 * 1024 * 1024


def _pick(n, cands):
    for c in cands:
        if n % c == 0:
            return c
    raise ValueError(f"no tile for {n} in {cands}")


def _cparams(sem):
    return pltpu.CompilerParams(dimension_semantics=sem, vmem_limit_bytes=VMEM_LIMIT)


def _dot(a, b):
    return lax.dot_general(a, b, (((1,), (0,)), ((), ())), preferred_element_type=F32)


def _softplus(x):
    return jnp.maximum(x, 0.0) + jnp.log1p(jnp.exp(-jnp.abs(x)))


def _sigmoid(x):
    return jax.nn.sigmoid(x)


def _silu(x):
    return x * jax.nn.sigmoid(x)


def _segsum(x, hm):
    hi = x.astype(BF16)
    r1 = x - hi.astype(F32)
    mid = r1.astype(BF16)
    lo = (r1 - mid.astype(F32)).astype(BF16)
    return _dot(hi, hm) + _dot(mid, hm) + _dot(lo, hm)


_DIMS = {'nn': (((1,), (0,)), ((), ())), 'nt': (((1,), (1,)), ((), ())), 'tn': (((0,), (0,)), ((), ()))}


def _split2(x):
    hi = x.astype(BF16)
    return hi, (x - hi.astype(F32)).astype(BF16)


def _dot3(a, b, form='nn'):
    ah, al = _split2(a)
    bh, bl = _split2(b)
    f = lambda x, y: lax.dot_general(x, y, _DIMS[form], preferred_element_type=F32)
    free = 1 if form == 'tn' else 0
    m = a.shape[free]
    both = f(jnp.concatenate([ah, al], axis=free), bh)
    return both[0:m] + both[m:2 * m] + f(ah, bl)


def _mdot(mask, x):
    hi = x.astype(BF16)
    r1 = x - hi.astype(F32)
    mid = r1.astype(BF16)
    lo = (r1 - mid.astype(F32)).astype(BF16)
    return _dot(mask, hi) + _dot(mask, mid) + _dot(mask, lo)


def _unit_lower_inverses(ms, rid, cid, c, expand=None):
    prod = _dot3 if expand is None else (lambda x, y: _dot3(x, expand(y)))
    same = lambda s: (rid // s) == (cid // s)
    ns = [jnp.where(same(SUBLANES), -m, 0.0) for m in ms]
    n2s = [prod(n, n) for n in ns]
    n4s = [prod(n2, n2) for n2 in n2s]
    eye = (rid == cid).astype(F32)
    ps = [eye + n for n in ns]
    ps = [p + prod(p, n2) for p, n2 in zip(ps, n2s)]
    ps = [p + prod(p, n4) for p, n4 in zip(ps, n4s)]
    s = SUBLANES
    while s < c:
        offs = [jnp.where(same(2 * s) & jnp.logical_not(same(s)), m, 0.0) for m in ms]
        ts = [prod(p, off) for p, off in zip(ps, offs)]
        ps = [p - prod(t, p) for p, t in zip(ps, ts)]
        s *= 2
    return ps


def _iota(shape, dim):
    return lax.broadcasted_iota(jnp.int32, shape, dim)


def _rms_kernel(x_ref, g_ref, o_ref):
    x = x_ref[...]
    ms = jnp.mean(x * x, axis=-1, keepdims=True)
    o_ref[...] = (x * lax.rsqrt(ms + NORM_EPS) * g_ref[...]).astype(o_ref.dtype)


def _rmsnorm(x, g, out_dtype):
    n, d = x.shape
    tm = _pick(n, (1024, 512, 256, 128, 64, 32, 16))
    return pl.pallas_call(
        _rms_kernel,
        out_shape=jax.ShapeDtypeStruct((n, d), out_dtype),
        grid=(n // tm,),
        in_specs=[pl.BlockSpec((tm, d), lambda i: (i, 0)), pl.BlockSpec((1, d), lambda i: (0, 0))],
        out_specs=pl.BlockSpec((tm, d), lambda i: (i, 0)),
        compiler_params=_cparams(("parallel",)),
        name="rmsnorm",
    )(x, g.reshape(1, d))


def _mm_kernel(x_ref, w_ref, o_ref):
    o_ref[...] = _dot(x_ref[...], w_ref[...])


def _matmul(x, w, name):
    n, k = x.shape
    m = w.shape[1]
    tm = _pick(n, (2048, 1024, 512, 256, 128, 64, 32, 16))
    tn = _pick(m, (1024, 896, 768, 512, 256, 128))
    return pl.pallas_call(
        _mm_kernel,
        out_shape=jax.ShapeDtypeStruct((n, m), F32),
        grid=(n // tm, m // tn),
        in_specs=[pl.BlockSpec((tm, k), lambda i, j: (i, 0)), pl.BlockSpec((k, tn), lambda i, j: (0, j))],
        out_specs=pl.BlockSpec((tm, tn), lambda i, j: (i, j)),
        compiler_params=_cparams(("parallel", "parallel")),
        name=name,
    )(x, w)


def _head_sum_matrix(width, seg):
    i = np.arange(width)
    return jnp.asarray((i[:, None] // seg) == (i[None, :] // seg), BF16)


def _wkv_chunk_kernel(p_ref, first_ref, mu_ref, vec_ref, w2_ref, a2_ref, g2_ref, hm_ref, ln_ref, s0_ref,
                      *rest, cs, nseq, ns, n_valid, multi_chunk, n_alias):
    o_ref, s1_ref, shift_ref, carry_scr, ht_scr = rest[n_alias:]
    ci = pl.program_id(1)
    rows = cs * nseq
    npair = RW_HEADS // 2
    f_zero = jnp.zeros((RW_HD, RW_HD), F32)
    slots = range(ns)
    pairs = range(npair)
    seqs = [(s, q) for s in slots for q in range(nseq)]
    sidx = lambda s, q: s * nseq + q

    @pl.when(ci == 0)
    def _():
        for s, q in seqs:
            carry_scr[sidx(s, q)] = first_ref[sidx(s, q)]
            for pr in pairs:
                top = jnp.concatenate([s0_ref[sidx(s, q), 2 * pr], f_zero], axis=1)
                bot = jnp.concatenate([f_zero, s0_ref[sidx(s, q), 2 * pr + 1]], axis=1)
                ht_scr[sidx(s, q), pr] = jnp.concatenate([top, bot], axis=0)

    vec = vec_ref[...]
    w0, a0, k_k, k_a, r_k = vec[0:1], vec[1:2], vec[2:3], vec[3:4], vec[4:5]
    hm = hm_ref[...]
    mu = mu_ref[...]
    row_w = _iota((rows, RW_COLS), 0)
    row_f = _iota((rows, RW_W), 0)
    rid = _iota((rows, rows), 0)
    cid = _iota((rows, rows), 1)
    tri = ((rid >= cid) & (rid // cs == cid // cs)).astype(BF16)

    ps_, vs_, gs_, bonus_ = [], [], [], []
    abar, bbar, kbar, rbar, btil, ktil, w_last = [], [], [], [], [], [], []
    for s in slots:
        p = p_ref[0, s, 0]
        prev = pltpu.roll(p, 1, 0)
        for q in range(nseq):
            prev = jnp.where(row_w == q * cs, carry_scr[sidx(s, q)], prev)
        h = p + (prev - p) * mu
        r = h[:, 0:RW_W]
        k = h[:, RW_W:2 * RW_W]
        v = h[:, 2 * RW_W:3 * RW_W]
        lo = h[:, 3 * RW_W:3 * RW_W + LANES]
        gl = h[:, 3 * RW_W + LANES:3 * RW_W + 2 * LANES]
        w_log = -_softplus(-(w0 + _dot3(jnp.tanh(lo), w2_ref[...]))) - 0.5
        logw = -jnp.exp(w_log)
        a = _sigmoid(a0 + _dot3(lo, a2_ref[...]))
        gs_.append(_dot3(_sigmoid(gl), g2_ref[...]))
        kk = k * k_k
        kk = kk * lax.rsqrt(_segsum(kk * kk, hm) + NORM_EPS)
        km = k * (1.0 + (a - 1.0) * k_a)
        bb = kk * a
        alpha = -kk
        bonus_.append(_segsum(r * km * r_k, hm) * v)
        if n_valid < cs:
            valid = (row_f % cs) < n_valid
            logw, alpha, bb, km = (jnp.where(valid, t, 0.0) for t in (logw, alpha, bb, km))
        gcum = _mdot(tri, logw)
        g_last = jnp.concatenate(
            [jnp.broadcast_to(gcum[q * cs + cs - 1:q * cs + cs], (cs, RW_W)) for q in range(nseq)], axis=0)
        emg = jnp.exp(-gcum)
        etil = jnp.exp(g_last - gcum)
        ps_.append(p)
        vs_.append(v)
        abar.append(alpha * jnp.exp(gcum - logw))
        bbar.append(bb * emg)
        kbar.append(km * emg)
        rbar.append(r * jnp.exp(gcum))
        btil.append(bb * etil)
        ktil.append(km * etil)
        w_last.append(jnp.exp(g_last))

    prow = _iota((rows, LANES), 0)
    plane = _iota((rows, LANES), 1)
    pcol = plane % RW_HD
    same_seq = (prow // cs) == (pcol // cs)
    strict = (pcol < prow) & same_seq
    incl = (pcol <= prow) & same_seq
    first_head = plane < RW_HD
    blk_mask = (_iota((LANES, LANES), 0) // RW_HD) == (_iota((LANES, LANES), 1) // RW_HD)

    def expand(x):
        return jnp.concatenate([jnp.where(first_head, x, 0.0), jnp.where(first_head, 0.0, x)], axis=0)

    items = [(s, pr) for s in slots for pr in pairs]
    ps = lambda x, pr: x[:, pr * LANES:(pr + 1) * LANES]
    seq_rows = lambda x, q: x[q * cs:(q + 1) * cs]

    lhs = [jnp.concatenate([ps(abar[s], pr), ps(rbar[s], pr)], axis=0) for s, pr in items]
    xbk = [_dot3(l, jnp.concatenate([expand(ps(bbar[s], pr)), expand(ps(kbar[s], pr))], axis=0), 'nt')
           for l, (s, pr) in zip(lhs, items)]
    xb = [x[:, 0:LANES] for x in xbk]
    xk = [x[:, LANES:2 * LANES] for x in xbk]
    a_m = [jnp.where(strict, x[0:rows], 0.0) for x in xb]
    rb_m = [jnp.where(incl, x[rows:2 * rows], 0.0) for x in xb]
    b_m = [jnp.where(strict, x[0:rows], 0.0) for x in xk]
    rk_m = [jnp.where(incl, x[rows:2 * rows], 0.0) for x in xk]
    pinv = _unit_lower_inverses([-m for m in a_m], prow, pcol, cs, expand=expand)

    ht_old = {(s, q, pr): ht_scr[sidx(s, q), pr] for s, q in seqs for pr in pairs}
    xh = {(s, q, pr): _dot3(jnp.concatenate([seq_rows(ps(abar[s], pr), q), seq_rows(ps(rbar[s], pr), q)], axis=0),
                            ht_old[(s, q, pr)], 'nt') for s, q in seqs for pr in pairs}
    ah = [jnp.concatenate([xh[(s, q, pr)][0:cs] for q in range(nseq)], axis=0) for s, pr in items]
    rh = [jnp.concatenate([xh[(s, q, pr)][cs:2 * cs] for q in range(nseq)], axis=0) for s, pr in items]
    vexp = [expand(ps(vs_[s], pr)) for s, pr in items]
    bv_rkv = [_dot3(jnp.concatenate([b_, rk_], axis=0), ve) for b_, rk_, ve in zip(b_m, rk_m, vexp)]
    rhs = [a_ + x[0:rows] for a_, x in zip(ah, bv_rkv)]
    u = [_dot3(pi, expand(rh_)) for pi, rh_ in zip(pinv, rhs)]
    o_items = [rh_ + x[rows:2 * rows] + _dot3(rb_, expand(u_)) for rh_, x, rb_, u_ in zip(rh, bv_rkv, rb_m, u)]
    for i, (s, pr) in enumerate(items):
        for q in range(nseq):
            uv = jnp.concatenate([seq_rows(u[i], q), seq_rows(ps(vs_[s], pr), q)], axis=0)
            bk = jnp.concatenate([seq_rows(ps(btil[s], pr), q), seq_rows(ps(ktil[s], pr), q)], axis=0)
            upd = jnp.where(blk_mask, _dot3(uv, bk, 'tn'), 0.0)
            ht_scr[sidx(s, q), pr] = ht_old[(s, q, pr)] * seq_rows(ps(w_last[s], pr), q)[0:1] + upd

    ln = ln_ref[...]
    for s in slots:
        o = jnp.concatenate(o_items[s * npair:(s + 1) * npair], axis=1)
        mean = _segsum(o, hm) * (1.0 / RW_HD)
        xc = o - mean
        var = _segsum(xc * xc, hm) * (1.0 / RW_HD)
        y = xc * lax.rsqrt(var + RW_GN_EPS) * ln[0:1] + ln[1:2]
        o_ref[0, s, 0] = ((y + bonus_[s]) * gs_[s]).astype(o_ref.dtype)

    if multi_chunk:
        for s, q in seqs:
            carry_scr[sidx(s, q)] = ps_[s][q * cs + cs - 1:q * cs + cs]

    @pl.when(ci == pl.num_programs(1) - 1)
    def _():
        for s, q in seqs:
            shift_ref[sidx(s, q)] = ps_[s][q * cs + n_valid - 1:q * cs + n_valid]
            for pr in pairs:
                ht = ht_scr[sidx(s, q), pr]
                s1_ref[sidx(s, q), 2 * pr] = ht[0:RW_HD, 0:RW_HD]
                s1_ref[sidx(s, q), 2 * pr + 1] = ht[RW_HD:2 * RW_HD, RW_HD:2 * RW_HD]


def _state_io(st, per_step, tail):
    arr, idx = st['src']
    layer = st['layer']
    zeros = (0,) * len(tail)
    in_spec = pl.BlockSpec((None, per_step) + tail, lambda b, j: (idx, b) + zeros)
    out_spec = pl.BlockSpec((None, per_step) + tail, lambda b, j: (layer, b) + zeros)
    out_shape = jax.ShapeDtypeStruct((st['depth'], arr.shape[1]) + tail, F32)
    alias = [] if st['prev'] is None else [st['prev']]
    return arr, in_spec, out_spec, out_shape, alias, [pl.BlockSpec(memory_space=pl.ANY)] * len(alias)


def _wkv_group(p_rw, lp, bsz, lpad, n_valid_len, shift0, st):
    rows = RW_HD
    if lpad >= rows:
        cs, nseq = rows, 1
    else:
        cs, nseq = lpad, rows // lpad
    nc = lpad // cs
    n_valid = cs if nc > 1 else n_valid_len
    units = bsz // nseq
    ns = WKV_SLOTS if units % WKV_SLOTS == 0 else 1
    vec = jnp.zeros((SUBLANES, RW_W), F32)
    vec = vec.at[0].set(lp['rwkv_w0']).at[1].set(lp['rwkv_a0']).at[2].set(lp['rwkv_k_k'])
    vec = vec.at[3].set(lp['rwkv_k_a']).at[4].set(lp['rwkv_r_k'].reshape(RW_W))
    zeros = jnp.zeros((RW_DECAY_LORA, RW_W), F32)
    w2p = jnp.concatenate([lp['rwkv_w2'], zeros], axis=0)
    a2p = jnp.concatenate([zeros, lp['rwkv_a2']], axis=0)
    hm = _head_sum_matrix(RW_W, RW_HD)
    ln = jnp.zeros((SUBLANES, RW_W), F32).at[0].set(lp['rwkv_ln_g']).at[1].set(lp['rwkv_ln_b'])
    mu = lp['rwkv_mu'].reshape(1, RW_COLS)
    first = shift0.reshape(bsz, 1, RW_COLS)
    full = lambda a: pl.BlockSpec(a.shape, lambda b, j: (0,) * a.ndim)
    nq = ns * nseq
    s0, s_in, s_out, s_shape, alias, alias_specs = _state_io(st, nq, (RW_HEADS, RW_HD, RW_HD))
    fspec = pl.BlockSpec((nq, 1, RW_COLS), lambda b, j: (b, 0, 0))
    inputs = [_slot_view(p_rw, units, ns, nc, rows), first, mu, vec, w2p, a2p, lp['rwkv_g2'], hm, ln, s0]
    o, s1, shift1 = pl.pallas_call(
        functools.partial(_wkv_chunk_kernel, cs=cs, nseq=nseq, ns=ns, n_valid=n_valid, multi_chunk=nc > 1,
                          n_alias=len(alias)),
        out_shape=[jax.ShapeDtypeStruct((units // ns, ns, nc, rows, RW_W), BF16), s_shape,
                   jax.ShapeDtypeStruct((bsz, 1, RW_COLS), F32)],
        grid=(units // ns, nc),
        in_specs=[_slot_spec(ns, rows, RW_COLS), fspec, full(mu), full(vec),
                  full(w2p), full(a2p), full(lp['rwkv_g2']), full(hm), full(ln), s_in] + alias_specs,
        out_specs=[_slot_spec(ns, rows, RW_W), s_out, fspec],
        scratch_shapes=[pltpu.VMEM((nq, 1, RW_COLS), F32),
                        pltpu.VMEM((nq, RW_HEADS // 2, LANES, LANES), F32)],
        input_output_aliases={len(inputs): 1} if alias else {},
        compiler_params=_cparams(("parallel", "arbitrary")),
        name="wkv_chunk",
    )(*inputs, *alias)
    return o.reshape(bsz * lpad, RW_W), shift1.reshape(bsz, RW_COLS), s1


def _gdn_kernel(qkv_ref, z_ref, ba_ref, convw_ref, hp_ref, ng_ref, tail0_ref, s0_ref,
                *rest, c, nb, n_valid, n_alias):
    o_ref, s1_ref, conv_ref, ext_scr, s_scr = rest[n_alias:]
    ci = pl.program_id(1)
    last = ci == pl.num_programs(1) - 1
    slots = range(nb)
    heads = range(GD_HEADS)
    items = [(s, h) for s in slots for h in heads]

    @pl.when(ci == 0)
    def _():
        for s in slots:
            ext_scr[s, 0:SUBLANES, :] = tail0_ref[s]
            s_scr[s] = s0_ref[s]

    cw = convw_ref[...]
    hp = hp_ref[...]
    ng = ng_ref[...]
    off = SUBLANES - (CONV_W - 1)
    rid = _iota((c, c), 0)
    cid = _iota((c, c), 1)
    incl = rid >= cid
    tri = incl.astype(BF16)
    strict_l = (rid > cid).astype(F32)
    kbase, vbase = GD_HEADS * GD_DK, 2 * GD_HEADS * GD_DK

    cqs, beta_alls, g_alls, gcum_alls = [], [], [], []
    for s in slots:
        ext_scr[s, SUBLANES:SUBLANES + c, :] = qkv_ref[0, s, 0]
        cq = ext_scr[s, off:off + c, :] * cw[0:1]
        for j in range(1, CONV_W):
            cq = cq + ext_scr[s, off + j:off + j + c, :] * cw[j:j + 1]

        @pl.when(last)
        def _():
            conv_ref[s] = ext_scr[s, off + n_valid:off + n_valid + CONV_W - 1, :]

        ext_scr[s, 0:SUBLANES, :] = ext_scr[s, c:c + SUBLANES, :]
        cqs.append(_silu(cq))
        ba = ba_ref[0, s, 0]
        beta_all = _sigmoid(ba)
        g_all = -jnp.exp(hp[0:1]) * _softplus(ba + hp[1:2])
        if n_valid < c:
            valid = _iota((c, LANES), 0) < n_valid
            beta_all = jnp.where(valid, beta_all, 0.0)
            g_all = jnp.where(valid, g_all, 0.0)
        beta_alls.append(beta_all)
        g_alls.append(g_all)
        gcum_alls.append(_mdot(tri, g_all))

    qs = [cqs[s][:, h * GD_DK:(h + 1) * GD_DK] for s, h in items]
    ks = [cqs[s][:, kbase + h * GD_DK:kbase + (h + 1) * GD_DK] for s, h in items]
    vs = [cqs[s][:, vbase + h * GD_DV:vbase + (h + 1) * GD_DV] for s, h in items]
    qs = [q * lax.rsqrt(jnp.sum(q * q, axis=-1, keepdims=True) + NORM_EPS) * (GD_DK ** -0.5) for q in qs]
    ks = [k * lax.rsqrt(jnp.sum(k * k, axis=-1, keepdims=True) + NORM_EPS) for k in ks]
    betas = [beta_alls[s][:, h:h + 1] for s, h in items]
    g_cols = [g_alls[s][:, GD_HEADS + h:GD_HEADS + h + 1] for s, h in items]
    gcs = [gcum_alls[s][:, GD_HEADS + h:GD_HEADS + h + 1] for s, h in items]
    glasts = [gc[c - 1:c, :] for gc in gcs]
    egcs = [jnp.exp(gc) for gc in gcs]
    s_old = [s_scr[s, h] for s, h in items]
    gcum_rows = [jnp.transpose(gcum_alls[s]) for s in slots]
    decs = [jnp.where(incl, jnp.exp(jnp.minimum(gc - gcum_rows[s][GD_HEADS + h:GD_HEADS + h + 1, :], 0.0)), 0.0)
            for (s, h), gc in zip(items, gcs)]
    qk_kts = [_dot3(jnp.concatenate([q, k], axis=0), k, 'nt') for q, k in zip(qs, ks)]
    ms = [strict_l * b * qk[c:2 * c] * dec for b, qk, dec in zip(betas, qk_kts, decs)]
    rhss = [jnp.concatenate([v * b, k * (b * e)], axis=1) for v, k, b, e in zip(vs, ks, betas, egcs)]
    if c >= 2 * SUBLANES:
        xss = [_dot3(p, rhs) for p, rhs in zip(_unit_lower_inverses(ms, rid, cid, c), rhss)]
    else:
        strict_u = (rid < cid).astype(F32)
        dts = [_mdot((rid < cid).astype(BF16), g * (rid <= cid).astype(F32)) for g in g_cols]
        mts = [strict_u * _dot3(k, k * b, 'nt') * jnp.exp(jnp.minimum(dt, 0.0))
               for k, b, dt in zip(ks, betas, dts)]
        xss = rhss
        for i in range(1, c):
            row_i = _iota((c, GD_DV + GD_DK), 0) == i
            xss = [jnp.where(row_i, xs - jnp.sum(mt[:, i:i + 1] * xs, axis=0, keepdims=True), xs)
                   for xs, mt in zip(xss, mts)]
    ws_qss = [_dot3(jnp.concatenate([xs[:, GD_DV:GD_DV + GD_DK], q * e], axis=0), s)
              for xs, q, e, s in zip(xss, qs, egcs, s_old)]
    v_news = [xs[:, 0:GD_DV] - wq[0:c] for xs, wq in zip(xss, ws_qss)]
    os_ = [wq[c:2 * c] + _dot3(qk[0:c] * dec, vn) for wq, qk, dec, vn in zip(ws_qss, qk_kts, decs, v_news)]
    s_new = [s * jnp.exp(gl) + _dot3(k * jnp.exp(gl - gc), vn, 'tn')
             for s, gl, gc, k, vn in zip(s_old, glasts, gcs, ks, v_news)]
    os_ = [o * lax.rsqrt(jnp.mean(o * o, axis=-1, keepdims=True) + NORM_EPS) * ng for o in os_]
    for s in slots:
        z = z_ref[0, s, 0]
        o_ref[0, s, 0] = jnp.concatenate(
            [(os_[s * GD_HEADS + h] * _silu(z[:, h * GD_DV:(h + 1) * GD_DV])).astype(o_ref.dtype) for h in heads],
            axis=1)
    for i, (s, h) in enumerate(items):
        s_scr[s, h] = s_new[i]

    @pl.when(last)
    def _():
        for s in slots:
            s1_ref[s] = s_scr[s]


def _slot_plan(bsz, lpad):
    c = _pick(lpad, (CHUNK_ROWS, 32, 16, 8))
    nb = max(SLOTS_MIN, CHUNK_ROWS // c)
    if bsz % nb:
        nb = 1
    return c, lpad // c, nb


def _slot_view(a, bsz, nb, nc, c):
    return a.reshape(bsz // nb, nb, nc, c, a.shape[-1])


def _slot_spec(nb, c, w):
    return pl.BlockSpec((1, nb, 1, c, w), lambda b, j: (b, 0, j, 0, 0))


def _gdn_group(pqkv, pz, pba, lp, bsz, lpad, n_valid_len, conv0, st):
    c, nc, nb = _slot_plan(bsz, lpad)
    n_valid = c if nc > 1 else n_valid_len
    tail0 = jnp.concatenate([jnp.zeros((bsz, SUBLANES - (CONV_W - 1), GD_QKV), F32), conv0], axis=1)
    hp = jnp.zeros((SUBLANES, LANES), F32)
    hp = hp.at[0, GD_HEADS:2 * GD_HEADS].set(lp['gdn_a_log']).at[1, GD_HEADS:2 * GD_HEADS].set(lp['gdn_dt_bias'])
    ng = lp['gdn_norm_g'].reshape(1, GD_DV)
    view = lambda a: _slot_view(a, bsz, nb, nc, c)
    full = lambda a: pl.BlockSpec(a.shape, lambda b, j: (0,) * a.ndim)
    s0, s_in, s_out, s_shape, alias, alias_specs = _state_io(st, nb, (GD_HEADS, GD_DK, GD_DV))
    inputs = [view(pqkv), view(pz), view(pba), lp['gdn_conv'], hp, ng, tail0, s0]
    o, s1, conv1 = pl.pallas_call(
        functools.partial(_gdn_kernel, c=c, nb=nb, n_valid=n_valid, n_alias=len(alias)),
        out_shape=[jax.ShapeDtypeStruct((bsz // nb, nb, nc, c, GD_W), BF16), s_shape,
                   jax.ShapeDtypeStruct((bsz, CONV_W - 1, GD_QKV), F32)],
        grid=(bsz // nb, nc),
        in_specs=[_slot_spec(nb, c, GD_QKV), _slot_spec(nb, c, GD_W), _slot_spec(nb, c, LANES),
                  full(lp['gdn_conv']), full(hp), full(ng),
                  pl.BlockSpec((nb, SUBLANES, GD_QKV), lambda b, j: (b, 0, 0)), s_in] + alias_specs,
        out_specs=[_slot_spec(nb, c, GD_W), s_out,
                   pl.BlockSpec((nb, CONV_W - 1, GD_QKV), lambda b, j: (b, 0, 0))],
        scratch_shapes=[pltpu.VMEM((nb, c + SUBLANES, GD_QKV), F32),
                        pltpu.VMEM((nb, GD_HEADS, GD_DK, GD_DV), F32)],
        input_output_aliases={len(inputs): 1} if alias else {},
        compiler_params=_cparams(("parallel", "arbitrary")),
        name="gdn_chunk",
    )(*inputs, *alias)
    return o.reshape(bsz * lpad, GD_W), s1, conv1


def _hgrn_kernel(p_ref, lbv_ref, ng_ref, s0_ref, *rest, c, nb, n_valid, n_alias):
    o_ref, s1_ref, st_scr = rest[n_alias:]
    ci = pl.program_id(1)
    slots = range(nb)
    heads = range(HG_HEADS)
    items = [(s, h) for s in slots for h in heads]

    @pl.when(ci == 0)
    def _():
        for s, h in items:
            st_scr[s, h] = jnp.transpose(s0_ref[s, h])

    rid = _iota((c, c), 0)
    cid = _iota((c, c), 1)
    tri = (rid >= cid).astype(BF16)
    w = HG_HEADS * HG_DF
    rows = _iota((c, w), 0)
    sb = min(c, HG_SUB)
    row_in_sub = rows % sb
    lbv = lbv_ref[...]
    hs = lambda x, h: x[:, h * HG_DF:(h + 1) * HG_DF]

    qs, ks, vs, bcs = [], [], [], []
    for s in slots:
        p = p_ref[0, s, 0]
        pf = p[:, w:2 * w]
        a = lbv[0:1]
        b = lbv[1:2] - _softplus(-pf)
        logf = jnp.maximum(a, b) + jnp.log1p(jnp.exp(-jnp.abs(a - b)))
        k = lbv[2:3] * _sigmoid(-pf)
        if n_valid < c:
            logf = jnp.where(rows < n_valid, logf, 0.0)
            k = jnp.where(rows < n_valid, k, 0.0)
        qs.append(_silu(p[:, 0:w]))
        ks.append(k)
        vs.append(p[:, 2 * w:2 * w + HG_W])
        bcs.append(_mdot(tri, logf))
    st_old = [st_scr[s, h] for s, h in items]
    qes = [q * jnp.exp(bc) for q, bc in zip(qs, bcs)]
    os_ = [_dot3(hs(qes[s], h), st, 'nt') for (s, h), st in zip(items, st_old)]
    for delta in range(sb):
        prods, v_ss = [], []
        for s in slots:
            if delta == 0:
                k_s, b_s, v_s = ks[s], bcs[s], vs[s]
            else:
                k_s, b_s, v_s = (pltpu.roll(t, delta, 0) for t in (ks[s], bcs[s], vs[s]))
            prods.append(jnp.where(row_in_sub >= delta,
                                   qs[s] * k_s * jnp.exp(jnp.minimum(bcs[s] - b_s, 0.0)), 0.0))
            v_ss.append(v_s)
        os_ = [o + jnp.sum(hs(prods[s], h), axis=-1, keepdims=True) * hs(v_ss[s], h)
               for (s, h), o in zip(items, os_)]
    if c > sb:
        parts = [[jnp.zeros((sb, HG_DV), F32)] for _ in items]
        for r0 in range(sb, c, sb):
            qis, kps = [], []
            for s in slots:
                ref = bcs[s][r0 - 1:r0]
                qis.append(qs[s][r0:r0 + sb] * jnp.exp(bcs[s][r0:r0 + sb] - ref))
                kps.append(ks[s][0:r0] * jnp.exp(ref - bcs[s][0:r0]))
            att = [_dot3(hs(qis[s], h), hs(kps[s], h), 'nt') for s, h in items]
            for i, (s, h) in enumerate(items):
                parts[i].append(_dot3(att[i], hs(vs[s], h)[0:r0]))
        os_ = [o + jnp.concatenate(p, axis=0) for o, p in zip(os_, parts)]
    blasts = [bc[c - 1:c] for bc in bcs]
    kds = [k * jnp.exp(bl - bc) for k, bl, bc in zip(ks, blasts, bcs)]
    ebs = [jnp.exp(bl) for bl in blasts]
    st_new = [st * hs(ebs[s], h) + _dot3(hs(vs[s], h), hs(kds[s], h), 'tn') for (s, h), st in zip(items, st_old)]
    os_ = [o * lax.rsqrt(jnp.mean(o * o, axis=-1, keepdims=True) + NORM_EPS) for o in os_]
    ng = ng_ref[...]
    for s in slots:
        gate = _sigmoid(p_ref[0, s, 0][:, 2 * w + HG_W:2 * w + 2 * HG_W])
        o_all = jnp.concatenate(os_[s * HG_HEADS:(s + 1) * HG_HEADS], axis=1) * ng * gate
        o_ref[0, s, 0] = o_all.astype(o_ref.dtype)
    for i, (s, h) in enumerate(items):
        st_scr[s, h] = st_new[i]

    @pl.when(ci == pl.num_programs(1) - 1)
    def _():
        for s, h in items:
            s1_ref[s, h] = jnp.transpose(st_scr[s, h])


def _hgrn_group(phg, lb, lp, bsz, lpad, n_valid_len, st):
    c, nc, nb = _slot_plan(bsz, lpad)
    n_valid = c if nc > 1 else n_valid_len
    lbv = jnp.zeros((SUBLANES, HG_HEADS * HG_DF), F32)
    lbv = lbv.at[0].set(jnp.log(jnp.maximum(lb, LB_FLOOR))).at[1].set(jnp.log1p(-lb)).at[2].set(1.0 - lb)
    ng = lp['hgrn_norm_g'].reshape(1, HG_W)
    full = lambda a: pl.BlockSpec(a.shape, lambda b, j: (0,) * a.ndim)
    s0, s_in, s_out, s_shape, alias, alias_specs = _state_io(st, nb, (HG_HEADS, HG_DF, HG_DV))
    inputs = [_slot_view(phg, bsz, nb, nc, c), lbv, ng, s0]
    o, s1 = pl.pallas_call(
        functools.partial(_hgrn_kernel, c=c, nb=nb, n_valid=n_valid, n_alias=len(alias)),
        out_shape=[jax.ShapeDtypeStruct((bsz // nb, nb, nc, c, HG_W), BF16), s_shape],
        grid=(bsz // nb, nc),
        in_specs=[_slot_spec(nb, c, HG_COLS), full(lbv), full(ng), s_in] + alias_specs,
        out_specs=[_slot_spec(nb, c, HG_W), s_out],
        scratch_shapes=[pltpu.VMEM((nb, HG_HEADS, HG_DV, HG_DF), F32)],
        input_output_aliases={len(inputs): 1} if alias else {},
        compiler_params=_cparams(("parallel", "arbitrary")),
        name="hgrn_chunk",
    )(*inputs, *alias)
    return o.reshape(bsz * lpad, HG_W), s1


def _ret_kernel(p_ref, cos_ref, sin_ref, qd_ref, kd_ref, dec_ref, dm_ref, gn_ref, s0_ref,
                *rest, c, nb, n_alias):
    o_ref, s1_ref, s_scr = rest[n_alias:]
    ci = pl.program_id(1)
    slots = range(nb)
    heads = range(RT_HEADS)
    items = [(s, h) for s in slots for h in heads]

    @pl.when(ci == 0)
    def _():
        for s in slots:
            s_scr[s] = jnp.zeros((RT_QK, RT_W), F32)
            for h in heads:
                s_scr[s, h * RT_DK:(h + 1) * RT_DK, h * RT_DV:(h + 1) * RT_DV] = s0_ref[s, h]

    cos = cos_ref[...]
    sin = sin_ref[...]
    lane = _iota((c, RT_QK), 1)
    first_half = (lane % RT_DK) < (RT_DK // 2)

    def rope(x):
        partner = jnp.where(first_half, pltpu.roll(x, RT_QK - RT_DK // 2, 1), pltpu.roll(x, RT_DK // 2, 1))
        return x * cos + partner * sin

    ps = [p_ref[0, s, 0] for s in slots]
    qs = [rope(p[:, 0:RT_QK]) for p in ps]
    ks = [rope(p[:, RT_QK:2 * RT_QK]) * (RT_DK ** -0.5) for p in ps]
    pvs = [p[:, 2 * RT_QK:2 * RT_QK + RT_W] for p in ps]
    sblks = [s_scr[s] for s in slots]
    qd = qd_ref[...]
    o_inters = [_dot3(q * qd, sblk) for q, sblk in zip(qs, sblks)]
    qks = [_dot3(jnp.concatenate([jnp.where(lane // RT_DK == h, q, 0.0) for h in heads], axis=0), k, 'nt')
           for q, k in zip(qs, ks)]
    attns = [qks[s][h * c:(h + 1) * c] * dec_ref[h] for s, h in items]
    os_ = [o_inters[s][:, h * RT_DV:(h + 1) * RT_DV] + _dot3(attn, pvs[s][:, h * RT_DV:(h + 1) * RT_DV])
           for (s, h), attn in zip(items, attns)]
    dm = dm_ref[...]
    kd = kd_ref[...]
    s_new = [sblk * dm + jnp.where(dm > 0.0, _dot3(k * kd, pv, 'tn'), 0.0) for sblk, k, pv in zip(sblks, ks, pvs)]
    gn = gn_ref[...]
    normed = []
    for o in os_:
        xc = o - jnp.mean(o, axis=-1, keepdims=True)
        normed.append(xc * lax.rsqrt(jnp.mean(xc * xc, axis=-1, keepdims=True) + NORM_EPS))
    for s in slots:
        gate = _silu(ps[s][:, 2 * RT_QK + RT_W:2 * RT_QK + 2 * RT_W])
        o_all = jnp.concatenate(normed[s * RT_HEADS:(s + 1) * RT_HEADS], axis=1) * gn * gate
        o_ref[0, s, 0] = o_all.astype(o_ref.dtype)
        s_scr[s] = s_new[s]

    @pl.when(ci == pl.num_programs(1) - 1)
    def _():
        for s, h in items:
            s1_ref[s, h] = s_scr[s, h * RT_DK:(h + 1) * RT_DK, h * RT_DV:(h + 1) * RT_DV]


def _ret_group(prt, lp, bsz, lpad, n_valid_len, pos0, st):
    c, nc, nb = _slot_plan(bsz, lpad)
    n_valid = c if nc > 1 else n_valid_len
    half = RT_DK // 2
    inv = ROPE_BASE ** (-np.arange(half, dtype=np.float64) / half)
    ang = (pos0 + np.arange(lpad, dtype=np.float64))[:, None] * inv[None, :]
    cos = np.tile(np.cos(ang), (1, 2 * RT_HEADS))
    sin = np.tile(np.concatenate([-np.sin(ang), np.sin(ang)], axis=1), (1, RT_HEADS))
    loggamma = np.log(1.0 - np.exp2(-5.0 - np.arange(RT_HEADS, dtype=np.float64)))
    gcum = loggamma[:, None] * np.arange(1, c + 1, dtype=np.float64)[None, :]
    idx = np.arange(c)
    dec = np.where(idx[:, None] >= idx[None, :], np.exp(gcum[:, :, None] - gcum[:, None, :]), 0.0)
    qd = np.repeat(np.exp(gcum).T, RT_DK, axis=1)
    kdec = np.where(idx[None, :] < n_valid, np.exp(gcum[:, n_valid - 1:n_valid] - gcum), 0.0)
    kd = np.repeat(kdec.T, RT_DK, axis=1)
    sdec = np.exp(gcum[:, n_valid - 1])
    dm = np.zeros((RT_QK, RT_W))
    for h in range(RT_HEADS):
        dm[h * RT_DK:(h + 1) * RT_DK, h * RT_DV:(h + 1) * RT_DV] = sdec[h]
    cos, sin, qd, kd, dec, dm = (jnp.asarray(a, F32) for a in (cos, sin, qd, kd, dec, dm))
    gn = lp['ret_gn_g'].reshape(1, RT_W)
    full = lambda a: pl.BlockSpec(a.shape, lambda b, j: (0,) * a.ndim)
    posspec = pl.BlockSpec((c, RT_QK), lambda b, j: (j, 0))
    s0, s_in, s_out, s_shape, alias, alias_specs = _state_io(st, nb, (RT_HEADS, RT_DK, RT_DV))
    inputs = [_slot_view(prt, bsz, nb, nc, c), cos, sin, qd, kd, dec, dm, gn, s0]
    o, s1 = pl.pallas_call(
        functools.partial(_ret_kernel, c=c, nb=nb, n_alias=len(alias)),
        out_shape=[jax.ShapeDtypeStruct((bsz // nb, nb, nc, c, RT_W), BF16), s_shape],
        grid=(bsz // nb, nc),
        in_specs=[_slot_spec(nb, c, RT_COLS), posspec, posspec,
                  full(qd), full(kd), full(dec), full(dm), full(gn), s_in] + alias_specs,
        out_specs=[_slot_spec(nb, c, RT_W), s_out],
        scratch_shapes=[pltpu.VMEM((nb, RT_QK, RT_W), F32)],
        input_output_aliases={len(inputs): 1} if alias else {},
        compiler_params=_cparams(("parallel", "arbitrary")),
        name="ret_chunk",
    )(*inputs, *alias)
    return o.reshape(bsz * lpad, RT_W), s1


def _merge_kernel(x_ref, o1, o2, o3, o4, gate_ref, w1, w2, w3, w4, wo_ref, out_ref, *, d):
    acc = None
    for i, (o, w) in enumerate(((o1, w1), (o2, w2), (o3, w3), (o4, w4))):
        term = _sigmoid(gate_ref[:, i * d:(i + 1) * d]) * _dot(o[...], w[...])
        acc = term if acc is None else acc + term
    out_ref[...] = x_ref[...] + _dot(acc.astype(BF16), wo_ref[...])


def _merge(x, outs, gate, wouts, wo):
    n, d = x.shape
    tm = _pick(n, (256, 128, 64, 32, 16))
    row = lambda w: pl.BlockSpec((tm, w), lambda i: (i, 0))
    full = lambda a: pl.BlockSpec(a.shape, lambda i: (0,) * a.ndim)
    return pl.pallas_call(
        functools.partial(_merge_kernel, d=d),
        out_shape=jax.ShapeDtypeStruct((n, d), F32),
        grid=(n // tm,),
        in_specs=[row(d)] + [row(o.shape[1]) for o in outs] + [row(N_BRANCH * d)]
        + [full(w) for w in wouts] + [full(wo)],
        out_specs=row(d),
        compiler_params=_cparams(("parallel",)),
        name="merge",
    )(x, *outs, gate, *wouts, wo)


def _route_kernel(x_ref, g_ref, rw_ref, rb_ref, xn_ref, col_ref, row_ref, cnt_ref):
    tm = x_ref.shape[0]
    lane = _iota((tm, LANES), 1)
    lanef = lane.astype(F32)
    x = x_ref[...]
    xn = x * lax.rsqrt(jnp.mean(x * x, axis=-1, keepdims=True) + NORM_EPS) * g_ref[...]
    xn_ref[...] = xn.astype(BF16)
    rb = rb_ref[...]
    neg = jnp.float32(-jnp.inf)
    logits = _dot3(xn, rw_ref[...])
    glog = jnp.where(lane < N_GROUPS, logits[:, 0:LANES] + rb[0:1], neg)
    gmax = jnp.max(glog, axis=-1, keepdims=True)
    gsum = jnp.sum(jnp.exp(glog - gmax), axis=-1, keepdims=True)
    gidx = jnp.min(jnp.where(glog == gmax, lanef, float(LANES)), axis=-1, keepdims=True)
    gp = 1.0 / gsum
    in_group = (lanef >= gidx * EXPERTS_PER_GROUP) & (lanef < (gidx + 1.0) * EXPERTS_PER_GROUP)
    elog = jnp.where(in_group, logits[:, LANES:2 * LANES] + rb[1:2], neg)
    emax = jnp.max(elog, axis=-1, keepdims=True)
    eexp = jnp.exp(elog - emax)
    ep = eexp / jnp.sum(eexp, axis=-1, keepdims=True)
    ep = jnp.where(in_group, ep, -1.0)
    p1 = jnp.max(ep, axis=-1, keepdims=True)
    i1 = jnp.min(jnp.where(ep == p1, lanef, float(LANES)), axis=-1, keepdims=True)
    ep2 = jnp.where(lanef == i1, -1.0, ep)
    p2 = jnp.max(ep2, axis=-1, keepdims=True)
    i2 = jnp.min(jnp.where(ep2 == p2, lanef, float(LANES)), axis=-1, keepdims=True)
    denom = p1 + p2
    wt1 = gp * p1 / denom
    wt2 = gp * p2 / denom

    onehot = ((lanef == i1) | (lanef == i2)).astype(F32)
    cnt = jnp.sum(onehot, axis=0, keepdims=True)
    before = (_iota((LANES, LANES), 0) < _iota((LANES, LANES), 1)).astype(BF16)
    off = _segsum(jnp.broadcast_to(cnt, (SUBLANES, LANES)), before)[0:1]
    tri = (_iota((tm, tm), 0) >= _iota((tm, tm), 1)).astype(BF16)
    slot = off + _dot(tri, onehot.astype(BF16)) - 1.0
    pos1 = jnp.sum(jnp.where(lanef == i1, slot, 0.0), axis=-1, keepdims=True)
    pos2 = jnp.sum(jnp.where(lanef == i2, slot, 0.0), axis=-1, keepdims=True)
    col = jnp.where(lane == 0, pos1, jnp.where(lane == 1, pos2, jnp.where(lane == 2, wt1,
                                                                         jnp.where(lane == 3, wt2, 0.0))))
    col_ref[...] = col
    row_ref[0] = jnp.transpose(col)[0:SUBLANES]
    cnt_ref[0] = jnp.concatenate([cnt, off, jnp.zeros((SUBLANES - 2, LANES), F32)], axis=0)


def _moe_kernel(cnt_sm, off_sm, x_ref, xn_ref, col_ref, row_ref, wg_ref, wu_ref, wd_ref, out_ref,
                xs_scr, ys_scr, ws_scr, *, pb, rb, eps):
    i = pl.program_id(0)
    eg = pl.program_id(1)
    tm, d = x_ref.shape
    ns = 2 * tm

    @pl.when(eg == 0)
    def _():
        rowd = row_ref[0]
        pos1, pos2, wt1, wt2 = rowd[0:1], rowd[1:2], rowd[2:3], rowd[3:4]
        for blk in range(ns // pb):
            sid = (_iota((pb, tm), 0) + blk * pb).astype(F32)
            m1 = sid == pos1
            m2 = sid == pos2
            xs_scr[blk * pb:(blk + 1) * pb, :] = _dot((m1 | m2).astype(BF16), xn_ref[...]).astype(BF16)
            wsl = jnp.sum(jnp.where(m1, wt1, 0.0) + jnp.where(m2, wt2, 0.0), axis=-1, keepdims=True)
            ws_scr[blk * pb:(blk + 1) * pb, :] = jnp.broadcast_to(wsl, (pb, LANES))
        ys_scr[...] = jnp.zeros_like(ys_scr)

    for k in range(eps):
        cnt = cnt_sm[i, eg * eps + k]
        off = off_sm[i, eg * eps + k]
        start = (off // BF16_ROWS) * BF16_ROWS

        def body(j, carry, k=k, cnt=cnt, off=off, start=start):
            own = start + j * rb
            r0 = pl.multiple_of(jnp.minimum(own, ns - rb), BF16_ROWS)
            xb = xs_scr[pl.ds(r0, rb), :]
            hid = _silu(_dot(xb, wg_ref[k])) * _dot(xb, wu_ref[k])
            y = _dot(hid.astype(BF16), wd_ref[k])
            srow = r0 + _iota((rb, d), 0)
            mine = (srow >= jnp.maximum(off, own)) & (srow < off + cnt)
            ys_scr[pl.ds(r0, rb), :] += jnp.where(mine, y, 0.0)
            return carry

        lax.fori_loop(0, (off + cnt - start + rb - 1) // rb, body, 0)

    @pl.when(eg == pl.num_programs(1) - 1)
    def _():
        col = col_ref[...]
        pos1, pos2 = col[:, 0:1], col[:, 1:2]
        acc = x_ref[...]
        for blk in range(ns // pb):
            sid = (_iota((tm, pb), 1) + blk * pb).astype(F32)
            pt = ((sid == pos1) | (sid == pos2)).astype(BF16)
            ysw = ys_scr[blk * pb:(blk + 1) * pb, :] * ws_scr[blk * pb:(blk + 1) * pb, 0:1]
            hi, lo = _split2(ysw)
            acc = acc + _dot(pt, hi) + _dot(pt, lo)
        out_ref[...] = acc


def _moe(x, lp, experts):
    n, d = x.shape
    tm = _pick(n, (MOE_TILE, 512, 256, 128, 64))
    nt = n // tm
    de = experts[0].shape[-1]
    rw = jnp.zeros((d, 2 * LANES), F32)
    rw = rw.at[:, :N_GROUPS].set(lp['router_group']).at[:, LANES:LANES + N_EXPERTS].set(lp['router_expert'])
    rbias = jnp.zeros((SUBLANES, LANES), F32)
    rbias = rbias.at[0, :N_GROUPS].set(lp['router_group_b']).at[1, :N_EXPERTS].set(lp['router_expert_b'])
    g = lp['norm2_g'].reshape(1, d)
    full1 = lambda a: pl.BlockSpec(a.shape, lambda i: (0,) * a.ndim)
    xn, col, row, cnt = pl.pallas_call(
        _route_kernel,
        out_shape=[jax.ShapeDtypeStruct((n, d), BF16), jax.ShapeDtypeStruct((n, LANES), F32),
                   jax.ShapeDtypeStruct((nt, SUBLANES, tm), F32), jax.ShapeDtypeStruct((nt, SUBLANES, LANES), F32)],
        grid=(nt,),
        in_specs=[pl.BlockSpec((tm, d), lambda i: (i, 0)), full1(g), full1(rw), full1(rbias)],
        out_specs=[pl.BlockSpec((tm, d), lambda i: (i, 0)), pl.BlockSpec((tm, LANES), lambda i: (i, 0)),
                   pl.BlockSpec((1, SUBLANES, tm), lambda i: (i, 0, 0)),
                   pl.BlockSpec((1, SUBLANES, LANES), lambda i: (i, 0, 0))],
        compiler_params=_cparams(("parallel",)),
        name="moe_route",
    )(x, g, rw, rbias)
    cnt_i = cnt[:, 0, :N_EXPERTS].astype(jnp.int32)
    off_i = cnt[:, 1, :N_EXPERTS].astype(jnp.int32)
    ns = 2 * tm
    pb = min(MOE_SLOT_BLOCK, ns)
    rb = min(MOE_ROW_BLOCK, ns)
    grid_spec = pltpu.PrefetchScalarGridSpec(
        num_scalar_prefetch=2,
        grid=(nt, N_EXPERTS // MOE_EXPERTS_PER_STEP),
        in_specs=[pl.BlockSpec((tm, d), lambda i, e, c, o: (i, 0)),
                  pl.BlockSpec((tm, d), lambda i, e, c, o: (i, 0)),
                  pl.BlockSpec((tm, LANES), lambda i, e, c, o: (i, 0)),
                  pl.BlockSpec((1, SUBLANES, tm), lambda i, e, c, o: (i, 0, 0)),
                  pl.BlockSpec((MOE_EXPERTS_PER_STEP, d, de), lambda i, e, c, o: (e, 0, 0)),
                  pl.BlockSpec((MOE_EXPERTS_PER_STEP, d, de), lambda i, e, c, o: (e, 0, 0)),
                  pl.BlockSpec((MOE_EXPERTS_PER_STEP, de, d), lambda i, e, c, o: (e, 0, 0))],
        out_specs=pl.BlockSpec((tm, d), lambda i, e, c, o: (i, 0)),
        scratch_shapes=[pltpu.VMEM((ns, d), BF16), pltpu.VMEM((ns, d), F32), pltpu.VMEM((ns, LANES), F32)],
    )
    return pl.pallas_call(
        functools.partial(_moe_kernel, pb=pb, rb=rb, eps=MOE_EXPERTS_PER_STEP),
        out_shape=jax.ShapeDtypeStruct((n, d), F32),
        grid_spec=grid_spec,
        compiler_params=pltpu.CompilerParams(dimension_semantics=("parallel", "arbitrary"),
                                             vmem_limit_bytes=MOE_VMEM_LIMIT),
        name="moe",
    )(cnt_i, off_i, x, xn, col, row, *experts)


def _layer(x, wts, lp, lb, gr):
    b, seq, lpad, big = gr['bsz'], gr['seq'], gr['lpad'], gr['big']
    xn = _rmsnorm(x, lp['norm1_g'], BF16)
    p_rw, p_gqkv, p_gz, p_gba, p_hg, p_rt, p_gate = (
        _matmul(xn, w, f"proj{i}") for i, w in enumerate(wts['proj']))
    o_rw, shift1, wkv1 = _wkv_group(p_rw, lp, b, lpad, seq, gr['shift0'], big['wkv'])
    o_gd, gdn1, conv1 = _gdn_group(p_gqkv, p_gz, p_gba, lp, b, lpad, seq, gr['conv0'], big['gdn'])
    o_hg, hgrn1 = _hgrn_group(p_hg, lb, lp, b, lpad, seq, big['hgrn'])
    o_rt, ret1 = _ret_group(p_rt, lp, b, lpad, seq, gr['pos0'], big['ret'])
    x = _merge(x, [o_rw, o_gd, o_hg, o_rt], p_gate, wts['out'], wts['w_o'])
    x = _moe(x, lp, wts['experts'])
    return x, shift1, conv1, dict(wkv=wkv1, gdn=gdn1, hgrn=hgrn1, ret=ret1)


def _layer_weights(lp, d):
    offs = np.cumsum([0, RW_COLS, GD_QKV, GD_W, 2 * GD_HEADS, HG_COLS, RT_COLS, N_BRANCH * d])
    seg = [lp['w_in'][:, offs[i]:offs[i + 1]] for i in range(7)]
    seg[3] = jnp.pad(seg[3], ((0, 0), (0, LANES - 2 * GD_HEADS)))
    return dict(
        proj=[s.astype(BF16) for s in seg],
        out=[lp[n].astype(BF16) for n in ('w_out_rwkv', 'w_out_gdn', 'w_out_hgrn', 'w_out_ret')],
        w_o=lp['w_o'].astype(BF16),
        experts=[lp[n].astype(BF16) for n in ('moe_w_gate', 'moe_w_up', 'moe_w_down')])


def kernel(x_prompt, x_sample, state_rwkv_shift, state_rwkv_wkv, state_gdn_conv, state_gdn, state_hgrn, state_ret, norm1_g, w_in, rwkv_mu, rwkv_w0, rwkv_w2, rwkv_a0, rwkv_a2, rwkv_g2, rwkv_k_k, rwkv_k_a, rwkv_r_k, rwkv_ln_g, rwkv_ln_b, w_out_rwkv, gdn_conv, gdn_a_log, gdn_dt_bias, gdn_norm_g, w_out_gdn, hgrn_lb_logits, hgrn_norm_g, w_out_hgrn, ret_gn_g, w_out_ret, w_o, norm2_g, router_group, router_group_b, router_expert, router_expert_b, moe_w_gate, moe_w_up, moe_w_down, final_norm_g):
    params = dict(norm1_g=norm1_g, w_in=w_in, rwkv_mu=rwkv_mu, rwkv_w0=rwkv_w0, rwkv_w2=rwkv_w2,
                  rwkv_a0=rwkv_a0, rwkv_a2=rwkv_a2, rwkv_g2=rwkv_g2, rwkv_k_k=rwkv_k_k,
                  rwkv_k_a=rwkv_k_a, rwkv_r_k=rwkv_r_k, rwkv_ln_g=rwkv_ln_g, rwkv_ln_b=rwkv_ln_b,
                  w_out_rwkv=w_out_rwkv, gdn_conv=gdn_conv, gdn_a_log=gdn_a_log,
                  gdn_dt_bias=gdn_dt_bias, gdn_norm_g=gdn_norm_g, w_out_gdn=w_out_gdn,
                  hgrn_norm_g=hgrn_norm_g, w_out_hgrn=w_out_hgrn, ret_gn_g=ret_gn_g,
                  w_out_ret=w_out_ret, w_o=w_o, norm2_g=norm2_g, router_group=router_group,
                  router_group_b=router_group_b, router_expert=router_expert,
                  router_expert_b=router_expert_b, moe_w_gate=moe_w_gate, moe_w_up=moe_w_up,
                  moe_w_down=moe_w_down)
    depth = w_in.shape[0]
    bp, lp_len, d = x_prompt.shape
    bs, ls, _ = x_sample.shape
    ls_pad = -(-ls // SAMPLE_PAD_LEN) * SAMPLE_PAD_LEN
    sm = jax.nn.softmax(hgrn_lb_logits.astype(F32), axis=0)
    lower_bounds = jnp.cumsum(sm, axis=0) - sm[0]

    big_in = dict(wkv=state_rwkv_wkv, gdn=state_gdn, hgrn=state_hgrn, ret=state_ret)
    xp = x_prompt.reshape(bp * lp_len, d)
    xs = jnp.pad(x_sample, ((0, 0), (0, ls_pad - ls), (0, 0))).reshape(bs * ls_pad, d)

    p_zero = {k: jnp.zeros((1, bp) + a.shape[2:], F32) for k, a in big_in.items()}
    p_big, s_big = dict.fromkeys(big_in), dict.fromkeys(big_in)
    p_shift, p_conv, s_shift, s_conv = [], [], [], []
    for layer in range(depth):
        lpar = {name: arr[layer] for name, arr in params.items()}
        wts = _layer_weights(lpar, d)
        prompt = dict(bsz=bp, seq=lp_len, lpad=lp_len, pos0=0,
                      shift0=jnp.zeros((bp,) + state_rwkv_shift.shape[2:], F32),
                      conv0=jnp.zeros((bp,) + state_gdn_conv.shape[2:], F32),
                      big={k: dict(src=(p_zero[k], 0), layer=layer, depth=depth, prev=p_big[k]) for k in big_in})
        sample = dict(bsz=bs, seq=ls, lpad=ls_pad, pos0=PAST_LEN,
                      shift0=state_rwkv_shift[layer].astype(F32), conv0=state_gdn_conv[layer].astype(F32),
                      big={k: dict(src=(big_in[k].astype(F32), layer), layer=layer, depth=depth, prev=s_big[k])
                           for k in big_in})
        xp, sh, cv, p_big = _layer(xp, wts, lpar, lower_bounds[layer], prompt)
        p_shift.append(sh)
        p_conv.append(cv)
        xs, sh, cv, s_big = _layer(xs, wts, lpar, lower_bounds[layer], sample)
        s_shift.append(sh)
        s_conv.append(cv)

    y_prompt = _rmsnorm(xp, final_norm_g, F32).reshape(bp, lp_len, d)
    y_sample = _rmsnorm(xs, final_norm_g, F32).reshape(bs, ls_pad, d)[:, :ls]
    return (y_prompt, y_sample,
            jnp.stack(p_shift), p_big['wkv'], jnp.stack(p_conv), p_big['gdn'], p_big['hgrn'], p_big['ret'],
            jnp.stack(s_shift), s_big['wkv'], jnp.stack(s_conv), s_big['gdn'], s_big['hgrn'], s_big['ret'])
```

```python
import functools
import math

import numpy as np
import jax
import jax.numpy as jnp
from jax import lax
from jax.experimental import pallas as pl
from jax.experimental.pallas import tpu as pltpu

F32 = jnp.float32
BF16 = jnp.bfloat16

NORM_EPS = 1e-6
LB_FLOOR = 1e-30
PAST_LEN = 16384
RW_HEADS = 8
RW_HD = 64
RW_W = RW_HEADS * RW_HD
RW_DECAY_LORA = 64
RW_AAA_LORA = 64
RW_GATE_LORA = 128
RW_COLS = 3 * RW_W + RW_DECAY_LORA + RW_AAA_LORA + RW_GATE_LORA
RW_GN_EPS = 64e-5
GD_HEADS = 4
GD_DK = 128
GD_DV = 128
GD_QKV = GD_HEADS * (2 * GD_DK + GD_DV)
GD_W = GD_HEADS * GD_DV
CONV_W = 4
HG_HEADS = 4
HG_DF = 128
HG_DV = 128
HG_W = HG_HEADS * HG_DV
HG_COLS = 2 * HG_HEADS * HG_DF + 2 * HG_W
HG_SUB = 16
RT_HEADS = 4
RT_DK = 64
RT_DV = 128
RT_W = RT_HEADS * RT_DV
RT_QK = RT_HEADS * RT_DK
RT_COLS = 2 * RT_QK + 2 * RT_W
ROPE_BASE = 10000.0
N_BRANCH = 4
N_GROUPS = 4
EXPERTS_PER_GROUP = 8
N_EXPERTS = N_GROUPS * EXPERTS_PER_GROUP

LANES = 128
SUBLANES = 8
SAMPLE_PAD_LEN = 8
CHUNK_ROWS = 64
SLOTS_MIN = 4
WKV_SLOTS = 2
VMEM_LIMIT = 48 * 1024 * 1024
MOE_TILE = 1024
MOE_SLOT_BLOCK = 256
MOE_ROW_BLOCK = 128
MOE_EXPERTS_PER_STEP = 4
BF16_ROWS = 16
MOE_VMEM_LIMIT = 56 * 1024 * 1024


def _pick(n, cands):
    for c in cands:
        if n % c == 0:
            return c
    raise ValueError(f"no tile for {n} in {cands}")


def _cparams(sem):
    return pltpu.CompilerParams(dimension_semantics=sem, vmem_limit_bytes=VMEM_LIMIT)


def _dot(a, b):
    return lax.dot_general(a, b, (((1,), (0,)), ((), ())), preferred_element_type=F32)


def _softplus(x):
    return jnp.maximum(x, 0.0) + jnp.log1p(jnp.exp(-jnp.abs(x)))


def _sigmoid(x):
    return jax.nn.sigmoid(x)


def _silu(x):
    return x * jax.nn.sigmoid(x)


def _segsum(x, hm):
    hi = x.astype(BF16)
    r1 = x - hi.astype(F32)
    mid = r1.astype(BF16)
    lo = (r1 - mid.astype(F32)).astype(BF16)
    return _dot(hi, hm) + _dot(mid, hm) + _dot(lo, hm)


_DIMS = {'nn': (((1,), (0,)), ((), ())), 'nt': (((1,), (1,)), ((), ())), 'tn': (((0,), (0,)), ((), ()))}


def _split2(x):
    hi = x.astype(BF16)
    return hi, (x - hi.astype(F32)).astype(BF16)


def _dot3(a, b, form='nn'):
    ah, al = _split2(a)
    bh, bl = _split2(b)
    f = lambda x, y: lax.dot_general(x, y, _DIMS[form], preferred_element_type=F32)
    free = 1 if form == 'tn' else 0
    m = a.shape[free]
    both = f(jnp.concatenate([ah, al], axis=free), bh)
    return both[0:m] + both[m:2 * m] + f(ah, bl)


def _mdot(mask, x):
    hi = x.astype(BF16)
    r1 = x - hi.astype(F32)
    mid = r1.astype(BF16)
    lo = (r1 - mid.astype(F32)).astype(BF16)
    return _dot(mask, hi) + _dot(mask, mid) + _dot(mask, lo)


def _unit_lower_inverses(ms, rid, cid, c, expand=None):
    prod = _dot3 if expand is None else (lambda x, y: _dot3(x, expand(y)))
    same = lambda s: (rid // s) == (cid // s)
    ns = [jnp.where(same(SUBLANES), -m, 0.0) for m in ms]
    n2s = [prod(n, n) for n in ns]
    n4s = [prod(n2, n2) for n2 in n2s]
    eye = (rid == cid).astype(F32)
    ps = [eye + n for n in ns]
    ps = [p + prod(p, n2) for p, n2 in zip(ps, n2s)]
    ps = [p + prod(p, n4) for p, n4 in zip(ps, n4s)]
    s = SUBLANES
    while s < c:
        offs = [jnp.where(same(2 * s) & jnp.logical_not(same(s)), m, 0.0) for m in ms]
        ts = [prod(p, off) for p, off in zip(ps, offs)]
        ps = [p - prod(t, p) for p, t in zip(ps, ts)]
        s *= 2
    return ps


def _iota(shape, dim):
    return lax.broadcasted_iota(jnp.int32, shape, dim)


def _rms_kernel(x_ref, g_ref, o_ref):
    x = x_ref[...]
    ms = jnp.mean(x * x, axis=-1, keepdims=True)
    o_ref[...] = (x * lax.rsqrt(ms + NORM_EPS) * g_ref[...]).astype(o_ref.dtype)


def _rmsnorm(x, g, out_dtype):
    n, d = x.shape
    tm = _pick(n, (1024, 512, 256, 128, 64, 32, 16))
    return pl.pallas_call(
        _rms_kernel,
        out_shape=jax.ShapeDtypeStruct((n, d), out_dtype),
        grid=(n // tm,),
        in_specs=[pl.BlockSpec((tm, d), lambda i: (i, 0)), pl.BlockSpec((1, d), lambda i: (0, 0))],
        out_specs=pl.BlockSpec((tm, d), lambda i: (i, 0)),
        compiler_params=_cparams(("parallel",)),
        name="rmsnorm",
    )(x, g.reshape(1, d))


def _mm_kernel(x_ref, w_ref, o_ref):
    o_ref[...] = _dot(x_ref[...], w_ref[...])


def _matmul(x, w, name):
    n, k = x.shape
    m = w.shape[1]
    tm = _pick(n, (2048, 1024, 512, 256, 128, 64, 32, 16))
    tn = _pick(m, (1024, 896, 768, 512, 256, 128))
    return pl.pallas_call(
        _mm_kernel,
        out_shape=jax.ShapeDtypeStruct((n, m), F32),
        grid=(n // tm, m // tn),
        in_specs=[pl.BlockSpec((tm, k), lambda i, j: (i, 0)), pl.BlockSpec((k, tn), lambda i, j: (0, j))],
        out_specs=pl.BlockSpec((tm, tn), lambda i, j: (i, j)),
        compiler_params=_cparams(("parallel", "parallel")),
        name=name,
    )(x, w)


def _head_sum_matrix(width, seg):
    i = np.arange(width)
    return jnp.asarray((i[:, None] // seg) == (i[None, :] // seg), BF16)


def _wkv_chunk_kernel(p_ref, first_ref, mu_ref, vec_ref, w2_ref, a2_ref, g2_ref, hm_ref, ln_ref, s0_ref,
                      *rest, cs, nseq, ns, n_valid, multi_chunk, n_alias):
    o_ref, s1_ref, shift_ref, carry_scr, ht_scr = rest[n_alias:]
    ci = pl.program_id(1)
    rows = cs * nseq
    npair = RW_HEADS // 2
    f_zero = jnp.zeros((RW_HD, RW_HD), F32)
    slots = range(ns)
    pairs = range(npair)
    seqs = [(s, q) for s in slots for q in range(nseq)]
    sidx = lambda s, q: s * nseq + q

    @pl.when(ci == 0)
    def _():
        for s, q in seqs:
            carry_scr[sidx(s, q)] = first_ref[sidx(s, q)]
            for pr in pairs:
                top = jnp.concatenate([s0_ref[sidx(s, q), 2 * pr], f_zero], axis=1)
                bot = jnp.concatenate([f_zero, s0_ref[sidx(s, q), 2 * pr + 1]], axis=1)
                ht_scr[sidx(s, q), pr] = jnp.concatenate([top, bot], axis=0)

    vec = vec_ref[...]
    w0, a0, k_k, k_a, r_k = vec[0:1], vec[1:2], vec[2:3], vec[3:4], vec[4:5]
    hm = hm_ref[...]
    mu = mu_ref[...]
    row_w = _iota((rows, RW_COLS), 0)
    row_f = _iota((rows, RW_W), 0)
    rid = _iota((rows, rows), 0)
    cid = _iota((rows, rows), 1)
    tri = ((rid >= cid) & (rid // cs == cid // cs)).astype(BF16)

    ps_, vs_, gs_, bonus_ = [], [], [], []
    abar, bbar, kbar, rbar, btil, ktil, w_last = [], [], [], [], [], [], []
    for s in slots:
        p = p_ref[0, s, 0]
        prev = pltpu.roll(p, 1, 0)
        for q in range(nseq):
            prev = jnp.where(row_w == q * cs, carry_scr[sidx(s, q)], prev)
        h = p + (prev - p) * mu
        r = h[:, 0:RW_W]
        k = h[:, RW_W:2 * RW_W]
        v = h[:, 2 * RW_W:3 * RW_W]
        lo = h[:, 3 * RW_W:3 * RW_W + LANES]
        gl = h[:, 3 * RW_W + LANES:3 * RW_W + 2 * LANES]
        w_log = -_softplus(-(w0 + _dot3(jnp.tanh(lo), w2_ref[...]))) - 0.5
        logw = -jnp.exp(w_log)
        a = _sigmoid(a0 + _dot3(lo, a2_ref[...]))
        gs_.append(_dot3(_sigmoid(gl), g2_ref[...]))
        kk = k * k_k
        kk = kk * lax.rsqrt(_segsum(kk * kk, hm) + NORM_EPS)
        km = k * (1.0 + (a - 1.0) * k_a)
        bb = kk * a
        alpha = -kk
        bonus_.append(_segsum(r * km * r_k, hm) * v)
        if n_valid < cs:
            valid = (row_f % cs) < n_valid
            logw, alpha, bb, km = (jnp.where(valid, t, 0.0) for t in (logw, alpha, bb, km))
        gcum = _mdot(tri, logw)
        g_last = jnp.concatenate(
            [jnp.broadcast_to(gcum[q * cs + cs - 1:q * cs + cs], (cs, RW_W)) for q in range(nseq)], axis=0)
        emg = jnp.exp(-gcum)
        etil = jnp.exp(g_last - gcum)
        ps_.append(p)
        vs_.append(v)
        abar.append(alpha * jnp.exp(gcum - logw))
        bbar.append(bb * emg)
        kbar.append(km * emg)
        rbar.append(r * jnp.exp(gcum))
        btil.append(bb * etil)
        ktil.append(km * etil)
        w_last.append(jnp.exp(g_last))

    prow = _iota((rows, LANES), 0)
    plane = _iota((rows, LANES), 1)
    pcol = plane % RW_HD
    same_seq = (prow // cs) == (pcol // cs)
    strict = (pcol < prow) & same_seq
    incl = (pcol <= prow) & same_seq
    first_head = plane < RW_HD
    blk_mask = (_iota((LANES, LANES), 0) // RW_HD) == (_iota((LANES, LANES), 1) // RW_HD)

    def expand(x):
        return jnp.concatenate([jnp.where(first_head, x, 0.0), jnp.where(first_head, 0.0, x)], axis=0)

    items = [(s, pr) for s in slots for pr in pairs]
    ps = lambda x, pr: x[:, pr * LANES:(pr + 1) * LANES]
    seq_rows = lambda x, q: x[q * cs:(q + 1) * cs]

    lhs = [jnp.concatenate([ps(abar[s], pr), ps(rbar[s], pr)], axis=0) for s, pr in items]
    xbk = [_dot3(l, jnp.concatenate([expand(ps(bbar[s], pr)), expand(ps(kbar[s], pr))], axis=0), 'nt')
           for l, (s, pr) in zip(lhs, items)]
    xb = [x[:, 0:LANES] for x in xbk]
    xk = [x[:, LANES:2 * LANES] for x in xbk]
    a_m = [jnp.where(strict, x[0:rows], 0.0) for x in xb]
    rb_m = [jnp.where(incl, x[rows:2 * rows], 0.0) for x in xb]
    b_m = [jnp.where(strict, x[0:rows], 0.0) for x in xk]
    rk_m = [jnp.where(incl, x[rows:2 * rows], 0.0) for x in xk]
    pinv = _unit_lower_inverses([-m for m in a_m], prow, pcol, cs, expand=expand)

    ht_old = {(s, q, pr): ht_scr[sidx(s, q), pr] for s, q in seqs for pr in pairs}
    xh = {(s, q, pr): _dot3(jnp.concatenate([seq_rows(ps(abar[s], pr), q), seq_rows(ps(rbar[s], pr), q)], axis=0),
                            ht_old[(s, q, pr)], 'nt') for s, q in seqs for pr in pairs}
    ah = [jnp.concatenate([xh[(s, q, pr)][0:cs] for q in range(nseq)], axis=0) for s, pr in items]
    rh = [jnp.concatenate([xh[(s, q, pr)][cs:2 * cs] for q in range(nseq)], axis=0) for s, pr in items]
    vexp = [expand(ps(vs_[s], pr)) for s, pr in items]
    bv_rkv = [_dot3(jnp.concatenate([b_, rk_], axis=0), ve) for b_, rk_, ve in zip(b_m, rk_m, vexp)]
    rhs = [a_ + x[0:rows] for a_, x in zip(ah, bv_rkv)]
    u = [_dot3(pi, expand(rh_)) for pi, rh_ in zip(pinv, rhs)]
    o_items = [rh_ + x[rows:2 * rows] + _dot3(rb_, expand(u_)) for rh_, x, rb_, u_ in zip(rh, bv_rkv, rb_m, u)]
    for i, (s, pr) in enumerate(items):
        for q in range(nseq):
            uv = jnp.concatenate([seq_rows(u[i], q), seq_rows(ps(vs_[s], pr), q)], axis=0)
            bk = jnp.concatenate([seq_rows(ps(btil[s], pr), q), seq_rows(ps(ktil[s], pr), q)], axis=0)
            upd = jnp.where(blk_mask, _dot3(uv, bk, 'tn'), 0.0)
            ht_scr[sidx(s, q), pr] = ht_old[(s, q, pr)] * seq_rows(ps(w_last[s], pr), q)[0:1] + upd

    ln = ln_ref[...]
    for s in slots:
        o = jnp.concatenate(o_items[s * npair:(s + 1) * npair], axis=1)
        mean = _segsum(o, hm) * (1.0 / RW_HD)
        xc = o - mean
        var = _segsum(xc * xc, hm) * (1.0 / RW_HD)
        y = xc * lax.rsqrt(var + RW_GN_EPS) * ln[0:1] + ln[1:2]
        o_ref[0, s, 0] = ((y + bonus_[s]) * gs_[s]).astype(o_ref.dtype)

    if multi_chunk:
        for s, q in seqs:
            carry_scr[sidx(s, q)] = ps_[s][q * cs + cs - 1:q * cs + cs]

    @pl.when(ci == pl.num_programs(1) - 1)
    def _():
        for s, q in seqs:
            shift_ref[sidx(s, q)] = ps_[s][q * cs + n_valid - 1:q * cs + n_valid]
            for pr in pairs:
                ht = ht_scr[sidx(s, q), pr]
                s1_ref[sidx(s, q), 2 * pr] = ht[0:RW_HD, 0:RW_HD]
                s1_ref[sidx(s, q), 2 * pr + 1] = ht[RW_HD:2 * RW_HD, RW_HD:2 * RW_HD]


def _state_io(st, per_step, tail):
    arr, idx = st['src']
    layer = st['layer']
    zeros = (0,) * len(tail)
    in_spec = pl.BlockSpec((None, per_step) + tail, lambda b, j: (idx, b) + zeros)
    out_spec = pl.BlockSpec((None, per_step) + tail, lambda b, j: (layer, b) + zeros)
    out_shape = jax.ShapeDtypeStruct((st['depth'], arr.shape[1]) + tail, F32)
    alias = [] if st['prev'] is None else [st['prev']]
    return arr, in_spec, out_spec, out_shape, alias, [pl.BlockSpec(memory_space=pl.ANY)] * len(alias)


def _wkv_group(p_rw, lp, bsz, lpad, n_valid_len, shift0, st):
    rows = RW_HD
    if lpad >= rows:
        cs, nseq = rows, 1
    else:
        cs, nseq = lpad, rows // lpad
    nc = lpad // cs
    n_valid = cs if nc > 1 else n_valid_len
    units = bsz // nseq
    ns = WKV_SLOTS if units % WKV_SLOTS == 0 else 1
    vec = jnp.zeros((SUBLANES, RW_W), F32)
    vec = vec.at[0].set(lp['rwkv_w0']).at[1].set(lp['rwkv_a0']).at[2].set(lp['rwkv_k_k'])
    vec = vec.at[3].set(lp['rwkv_k_a']).at[4].set(lp['rwkv_r_k'].reshape(RW_W))
    zeros = jnp.zeros((RW_DECAY_LORA, RW_W), F32)
    w2p = jnp.concatenate([lp['rwkv_w2'], zeros], axis=0)
    a2p = jnp.concatenate([zeros, lp['rwkv_a2']], axis=0)
    hm = _head_sum_matrix(RW_W, RW_HD)
    ln = jnp.zeros((SUBLANES, RW_W), F32).at[0].set(lp['rwkv_ln_g']).at[1].set(lp['rwkv_ln_b'])
    mu = lp['rwkv_mu'].reshape(1, RW_COLS)
    first = shift0.reshape(bsz, 1, RW_COLS)
    full = lambda a: pl.BlockSpec(a.shape, lambda b, j: (0,) * a.ndim)
    nq = ns * nseq
    s0, s_in, s_out, s_shape, alias, alias_specs = _state_io(st, nq, (RW_HEADS, RW_HD, RW_HD))
    fspec = pl.BlockSpec((nq, 1, RW_COLS), lambda b, j: (b, 0, 0))
    inputs = [_slot_view(p_rw, units, ns, nc, rows), first, mu, vec, w2p, a2p, lp['rwkv_g2'], hm, ln, s0]
    o, s1, shift1 = pl.pallas_call(
        functools.partial(_wkv_chunk_kernel, cs=cs, nseq=nseq, ns=ns, n_valid=n_valid, multi_chunk=nc > 1,
                          n_alias=len(alias)),
        out_shape=[jax.ShapeDtypeStruct((units // ns, ns, nc, rows, RW_W), BF16), s_shape,
                   jax.ShapeDtypeStruct((bsz, 1, RW_COLS), F32)],
        grid=(units // ns, nc),
        in_specs=[_slot_spec(ns, rows, RW_COLS), fspec, full(mu), full(vec),
                  full(w2p), full(a2p), full(lp['rwkv_g2']), full(hm), full(ln), s_in] + alias_specs,
        out_specs=[_slot_spec(ns, rows, RW_W), s_out, fspec],
        scratch_shapes=[pltpu.VMEM((nq, 1, RW_COLS), F32),
                        pltpu.VMEM((nq, RW_HEADS // 2, LANES, LANES), F32)],
        input_output_aliases={len(inputs): 1} if alias else {},
        compiler_params=_cparams(("parallel", "arbitrary")),
        name="wkv_chunk",
    )(*inputs, *alias)
    return o.reshape(bsz * lpad, RW_W), shift1.reshape(bsz, RW_COLS), s1


def _gdn_kernel(qkv_ref, z_ref, ba_ref, convw_ref, hp_ref, ng_ref, tail0_ref, s0_ref,
                *rest, c, nb, n_valid, n_alias):
    o_ref, s1_ref, conv_ref, ext_scr, s_scr = rest[n_alias:]
    ci = pl.program_id(1)
    last = ci == pl.num_programs(1) - 1
    slots = range(nb)
    heads = range(GD_HEADS)
    items = [(s, h) for s in slots for h in heads]

    @pl.when(ci == 0)
    def _():
        for s in slots:
            ext_scr[s, 0:SUBLANES, :] = tail0_ref[s]
            s_scr[s] = s0_ref[s]

    cw = convw_ref[...]
    hp = hp_ref[...]
    ng = ng_ref[...]
    off = SUBLANES - (CONV_W - 1)
    rid = _iota((c, c), 0)
    cid = _iota((c, c), 1)
    incl = rid >= cid
    tri = incl.astype(BF16)
    strict_l = (rid > cid).astype(F32)
    kbase, vbase = GD_HEADS * GD_DK, 2 * GD_HEADS * GD_DK

    cqs, beta_alls, g_alls, gcum_alls = [], [], [], []
    for s in slots:
        ext_scr[s, SUBLANES:SUBLANES + c, :] = qkv_ref[0, s, 0]
        cq = ext_scr[s, off:off + c, :] * cw[0:1]
        for j in range(1, CONV_W):
            cq = cq + ext_scr[s, off + j:off + j + c, :] * cw[j:j + 1]

        @pl.when(last)
        def _():
            conv_ref[s] = ext_scr[s, off + n_valid:off + n_valid + CONV_W - 1, :]

        ext_scr[s, 0:SUBLANES, :] = ext_scr[s, c:c + SUBLANES, :]
        cqs.append(_silu(cq))
        ba = ba_ref[0, s, 0]
        beta_all = _sigmoid(ba)
        g_all = -jnp.exp(hp[0:1]) * _softplus(ba + hp[1:2])
        if n_valid < c:
            valid = _iota((c, LANES), 0) < n_valid
            beta_all = jnp.where(valid, beta_all, 0.0)
            g_all = jnp.where(valid, g_all, 0.0)
        beta_alls.append(beta_all)
        g_alls.append(g_all)
        gcum_alls.append(_mdot(tri, g_all))

    qs = [cqs[s][:, h * GD_DK:(h + 1) * GD_DK] for s, h in items]
    ks = [cqs[s][:, kbase + h * GD_DK:kbase + (h + 1) * GD_DK] for s, h in items]
    vs = [cqs[s][:, vbase + h * GD_DV:vbase + (h + 1) * GD_DV] for s, h in items]
    qs = [q * lax.rsqrt(jnp.sum(q * q, axis=-1, keepdims=True) + NORM_EPS) * (GD_DK ** -0.5) for q in qs]
    ks = [k * lax.rsqrt(jnp.sum(k * k, axis=-1, keepdims=True) + NORM_EPS) for k in ks]
    betas = [beta_alls[s][:, h:h + 1] for s, h in items]
    g_cols = [g_alls[s][:, GD_HEADS + h:GD_HEADS + h + 1] for s, h in items]
    gcs = [gcum_alls[s][:, GD_HEADS + h:GD_HEADS + h + 1] for s, h in items]
    glasts = [gc[c - 1:c, :] for gc in gcs]
    egcs = [jnp.exp(gc) for gc in gcs]
    s_old = [s_scr[s, h] for s, h in items]
    gcum_rows = [jnp.transpose(gcum_alls[s]) for s in slots]
    decs = [jnp.where(incl, jnp.exp(jnp.minimum(gc - gcum_rows[s][GD_HEADS + h:GD_HEADS + h + 1, :], 0.0)), 0.0)
            for (s, h), gc in zip(items, gcs)]
    qk_kts = [_dot3(jnp.concatenate([q, k], axis=0), k, 'nt') for q, k in zip(qs, ks)]
    ms = [strict_l * b * qk[c:2 * c] * dec for b, qk, dec in zip(betas, qk_kts, decs)]
    rhss = [jnp.concatenate([v * b, k * (b * e)], axis=1) for v, k, b, e in zip(vs, ks, betas, egcs)]
    if c >= 2 * SUBLANES:
        xss = [_dot3(p, rhs) for p, rhs in zip(_unit_lower_inverses(ms, rid, cid, c), rhss)]
    else:
        strict_u = (rid < cid).astype(F32)
        dts = [_mdot((rid < cid).astype(BF16), g * (rid <= cid).astype(F32)) for g in g_cols]
        mts = [strict_u * _dot3(k, k * b, 'nt') * jnp.exp(jnp.minimum(dt, 0.0))
               for k, b, dt in zip(ks, betas, dts)]
        xss = rhss
        for i in range(1, c):
            row_i = _iota((c, GD_DV + GD_DK), 0) == i
            xss = [jnp.where(row_i, xs - jnp.sum(mt[:, i:i + 1] * xs, axis=0, keepdims=True), xs)
                   for xs, mt in zip(xss, mts)]
    ws_qss = [_dot3(jnp.concatenate([xs[:, GD_DV:GD_DV + GD_DK], q * e], axis=0), s)
              for xs, q, e, s in zip(xss, qs, egcs, s_old)]
    v_news = [xs[:, 0:GD_DV] - wq[0:c] for xs, wq in zip(xss, ws_qss)]
    os_ = [wq[c:2 * c] + _dot3(qk[0:c] * dec, vn) for wq, qk, dec, vn in zip(ws_qss, qk_kts, decs, v_news)]
    s_new = [s * jnp.exp(gl) + _dot3(k * jnp.exp(gl - gc), vn, 'tn')
             for s, gl, gc, k, vn in zip(s_old, glasts, gcs, ks, v_news)]
    os_ = [o * lax.rsqrt(jnp.mean(o * o, axis=-1, keepdims=True) + NORM_EPS) * ng for o in os_]
    for s in slots:
        z = z_ref[0, s, 0]
        o_ref[0, s, 0] = jnp.concatenate(
            [(os_[s * GD_HEADS + h] * _silu(z[:, h * GD_DV:(h + 1) * GD_DV])).astype(o_ref.dtype) for h in heads],
            axis=1)
    for i, (s, h) in enumerate(items):
        s_scr[s, h] = s_new[i]

    @pl.when(last)
    def _():
        for s in slots:
            s1_ref[s] = s_scr[s]


def _slot_plan(bsz, lpad):
    c = _pick(lpad, (CHUNK_ROWS, 32, 16, 8))
    nb = max(SLOTS_MIN, CHUNK_ROWS // c)
    if bsz % nb:
        nb = 1
    return c, lpad // c, nb


def _slot_view(a, bsz, nb, nc, c):
    return a.reshape(bsz // nb, nb, nc, c, a.shape[-1])


def _slot_spec(nb, c, w):
    return pl.BlockSpec((1, nb, 1, c, w), lambda b, j: (b, 0, j, 0, 0))


def _gdn_group(pqkv, pz, pba, lp, bsz, lpad, n_valid_len, conv0, st):
    c, nc, nb = _slot_plan(bsz, lpad)
    n_valid = c if nc > 1 else n_valid_len
    tail0 = jnp.concatenate([jnp.zeros((bsz, SUBLANES - (CONV_W - 1), GD_QKV), F32), conv0], axis=1)
    hp = jnp.zeros((SUBLANES, LANES), F32)
    hp = hp.at[0, GD_HEADS:2 * GD_HEADS].set(lp['gdn_a_log']).at[1, GD_HEADS:2 * GD_HEADS].set(lp['gdn_dt_bias'])
    ng = lp['gdn_norm_g'].reshape(1, GD_DV)
    view = lambda a: _slot_view(a, bsz, nb, nc, c)
    full = lambda a: pl.BlockSpec(a.shape, lambda b, j: (0,) * a.ndim)
    s0, s_in, s_out, s_shape, alias, alias_specs = _state_io(st, nb, (GD_HEADS, GD_DK, GD_DV))
    inputs = [view(pqkv), view(pz), view(pba), lp['gdn_conv'], hp, ng, tail0, s0]
    o, s1, conv1 = pl.pallas_call(
        functools.partial(_gdn_kernel, c=c, nb=nb, n_valid=n_valid, n_alias=len(alias)),
        out_shape=[jax.ShapeDtypeStruct((bsz // nb, nb, nc, c, GD_W), BF16), s_shape,
                   jax.ShapeDtypeStruct((bsz, CONV_W - 1, GD_QKV), F32)],
        grid=(bsz // nb, nc),
        in_specs=[_slot_spec(nb, c, GD_QKV), _slot_spec(nb, c, GD_W), _slot_spec(nb, c, LANES),
                  full(lp['gdn_conv']), full(hp), full(ng),
                  pl.BlockSpec((nb, SUBLANES, GD_QKV), lambda b, j: (b, 0, 0)), s_in] + alias_specs,
        out_specs=[_slot_spec(nb, c, GD_W), s_out,
                   pl.BlockSpec((nb, CONV_W - 1, GD_QKV), lambda b, j: (b, 0, 0))],
        scratch_shapes=[pltpu.VMEM((nb, c + SUBLANES, GD_QKV), F32),
                        pltpu.VMEM((nb, GD_HEADS, GD_DK, GD_DV), F32)],
        input_output_aliases={len(inputs): 1} if alias else {},
        compiler_params=_cparams(("parallel", "arbitrary")),
        name="gdn_chunk",
    )(*inputs, *alias)
    return o.reshape(bsz * lpad, GD_W), s1, conv1


def _hgrn_kernel(p_ref, lbv_ref, ng_ref, s0_ref, *rest, c, nb, n_valid, n_alias):
    o_ref, s1_ref, st_scr = rest[n_alias:]
    ci = pl.program_id(1)
    slots = range(nb)
    heads = range(HG_HEADS)
    items = [(s, h) for s in slots for h in heads]

    @pl.when(ci == 0)
    def _():
        for s, h in items:
            st_scr[s, h] = jnp.transpose(s0_ref[s, h])

    rid = _iota((c, c), 0)
    cid = _iota((c, c), 1)
    tri = (rid >= cid).astype(BF16)
    w = HG_HEADS * HG_DF
    rows = _iota((c, w), 0)
    sb = min(c, HG_SUB)
    row_in_sub = rows % sb
    lbv = lbv_ref[...]
    hs = lambda x, h: x[:, h * HG_DF:(h + 1) * HG_DF]

    qs, ks, vs, bcs = [], [], [], []
    for s in slots:
        p = p_ref[0, s, 0]
        pf = p[:, w:2 * w]
        a = lbv[0:1]
        b = lbv[1:2] - _softplus(-pf)
        logf = jnp.maximum(a, b) + jnp.log1p(jnp.exp(-jnp.abs(a - b)))
        k = lbv[2:3] * _sigmoid(-pf)
        if n_valid < c:
            logf = jnp.where(rows < n_valid, logf, 0.0)
            k = jnp.where(rows < n_valid, k, 0.0)
        qs.append(_silu(p[:, 0:w]))
        ks.append(k)
        vs.append(p[:, 2 * w:2 * w + HG_W])
        bcs.append(_mdot(tri, logf))
    st_old = [st_scr[s, h] for s, h in items]
    qes = [q * jnp.exp(bc) for q, bc in zip(qs, bcs)]
    os_ = [_dot3(hs(qes[s], h), st, 'nt') for (s, h), st in zip(items, st_old)]
    for delta in range(sb):
        prods, v_ss = [], []
        for s in slots:
            if delta == 0:
                k_s, b_s, v_s = ks[s], bcs[s], vs[s]
            else:
                k_s, b_s, v_s = (pltpu.roll(t, delta, 0) for t in (ks[s], bcs[s], vs[s]))
            prods.append(jnp.where(row_in_sub >= delta,
                                   qs[s] * k_s * jnp.exp(jnp.minimum(bcs[s] - b_s, 0.0)), 0.0))
            v_ss.append(v_s)
        os_ = [o + jnp.sum(hs(prods[s], h), axis=-1, keepdims=True) * hs(v_ss[s], h)
               for (s, h), o in zip(items, os_)]
    if c > sb:
        parts = [[jnp.zeros((sb, HG_DV), F32)] for _ in items]
        for r0 in range(sb, c, sb):
            qis, kps = [], []
            for s in slots:
                ref = bcs[s][r0 - 1:r0]
                qis.append(qs[s][r0:r0 + sb] * jnp.exp(bcs[s][r0:r0 + sb] - ref))
                kps.append(ks[s][0:r0] * jnp.exp(ref - bcs[s][0:r0]))
            att = [_dot3(hs(qis[s], h), hs(kps[s], h), 'nt') for s, h in items]
            for i, (s, h) in enumerate(items):
                parts[i].append(_dot3(att[i], hs(vs[s], h)[0:r0]))
        os_ = [o + jnp.concatenate(p, axis=0) for o, p in zip(os_, parts)]
    blasts = [bc[c - 1:c] for bc in bcs]
    kds = [k * jnp.exp(bl - bc) for k, bl, bc in zip(ks, blasts, bcs)]
    ebs = [jnp.exp(bl) for bl in blasts]
    st_new = [st * hs(ebs[s], h) + _dot3(hs(vs[s], h), hs(kds[s], h), 'tn') for (s, h), st in zip(items, st_old)]
    os_ = [o * lax.rsqrt(jnp.mean(o * o, axis=-1, keepdims=True) + NORM_EPS) for o in os_]
    ng = ng_ref[...]
    for s in slots:
        gate = _sigmoid(p_ref[0, s, 0][:, 2 * w + HG_W:2 * w + 2 * HG_W])
        o_all = jnp.concatenate(os_[s * HG_HEADS:(s + 1) * HG_HEADS], axis=1) * ng * gate
        o_ref[0, s, 0] = o_all.astype(o_ref.dtype)
    for i, (s, h) in enumerate(items):
        st_scr[s, h] = st_new[i]

    @pl.when(ci == pl.num_programs(1) - 1)
    def _():
        for s, h in items:
            s1_ref[s, h] = jnp.transpose(st_scr[s, h])


def _hgrn_group(phg, lb, lp, bsz, lpad, n_valid_len, st):
    c, nc, nb = _slot_plan(bsz, lpad)
    n_valid = c if nc > 1 else n_valid_len
    lbv = jnp.zeros((SUBLANES, HG_HEADS * HG_DF), F32)
    lbv = lbv.at[0].set(jnp.log(jnp.maximum(lb, LB_FLOOR))).at[1].set(jnp.log1p(-lb)).at[2].set(1.0 - lb)
    ng = lp['hgrn_norm_g'].reshape(1, HG_W)
    full = lambda a: pl.BlockSpec(a.shape, lambda b, j: (0,) * a.ndim)
    s0, s_in, s_out, s_shape, alias, alias_specs = _state_io(st, nb, (HG_HEADS, HG_DF, HG_DV))
    inputs = [_slot_view(phg, bsz, nb, nc, c), lbv, ng, s0]
    o, s1 = pl.pallas_call(
        functools.partial(_hgrn_kernel, c=c, nb=nb, n_valid=n_valid, n_alias=len(alias)),
        out_shape=[jax.ShapeDtypeStruct((bsz // nb, nb, nc, c, HG_W), BF16), s_shape],
        grid=(bsz // nb, nc),
        in_specs=[_slot_spec(nb, c, HG_COLS), full(lbv), full(ng), s_in] + alias_specs,
        out_specs=[_slot_spec(nb, c, HG_W), s_out],
        scratch_shapes=[pltpu.VMEM((nb, HG_HEADS, HG_DV, HG_DF), F32)],
        input_output_aliases={len(inputs): 1} if alias else {},
        compiler_params=_cparams(("parallel", "arbitrary")),
        name="hgrn_chunk",
    )(*inputs, *alias)
    return o.reshape(bsz * lpad, HG_W), s1


def _ret_kernel(p_ref, cos_ref, sin_ref, qd_ref, kd_ref, dec_ref, dm_ref, gn_ref, s0_ref,
                *rest, c, nb, n_alias):
    o_ref, s1_ref, s_scr = rest[n_alias:]
    ci = pl.program_id(1)
    slots = range(nb)
    heads = range(RT_HEADS)
    items = [(s, h) for s in slots for h in heads]

    @pl.when(ci == 0)
    def _():
        for s in slots:
            s_scr[s] = jnp.zeros((RT_QK, RT_W), F32)
            for h in heads:
                s_scr[s, h * RT_DK:(h + 1) * RT_DK, h * RT_DV:(h + 1) * RT_DV] = s0_ref[s, h]

    cos = cos_ref[...]
    sin = sin_ref[...]
    lane = _iota((c, RT_QK), 1)
    first_half = (lane % RT_DK) < (RT_DK // 2)

    def rope(x):
        partner = jnp.where(first_half, pltpu.roll(x, RT_QK - RT_DK // 2, 1), pltpu.roll(x, RT_DK // 2, 1))
        return x * cos + partner * sin

    ps = [p_ref[0, s, 0] for s in slots]
    qs = [rope(p[:, 0:RT_QK]) for p in ps]
    ks = [rope(p[:, RT_QK:2 * RT_QK]) * (RT_DK ** -0.5) for p in ps]
    pvs = [p[:, 2 * RT_QK:2 * RT_QK + RT_W] for p in ps]
    sblks = [s_scr[s] for s in slots]
    qd = qd_ref[...]
    o_inters = [_dot3(q * qd, sblk) for q, sblk in zip(qs, sblks)]
    qks = [_dot3(jnp.concatenate([jnp.where(lane // RT_DK == h, q, 0.0) for h in heads], axis=0), k, 'nt')
           for q, k in zip(qs, ks)]
    attns = [qks[s][h * c:(h + 1) * c] * dec_ref[h] for s, h in items]
    os_ = [o_inters[s][:, h * RT_DV:(h + 1) * RT_DV] + _dot3(attn, pvs[s][:, h * RT_DV:(h + 1) * RT_DV])
           for (s, h), attn in zip(items, attns)]
    dm = dm_ref[...]
    kd = kd_ref[...]
    s_new = [sblk * dm + jnp.where(dm > 0.0, _dot3(k * kd, pv, 'tn'), 0.0) for sblk, k, pv in zip(sblks, ks, pvs)]
    gn = gn_ref[...]
    normed = []
    for o in os_:
        xc = o - jnp.mean(o, axis=-1, keepdims=True)
        normed.append(xc * lax.rsqrt(jnp.mean(xc * xc, axis=-1, keepdims=True) + NORM_EPS))
    for s in slots:
        gate = _silu(ps[s][:, 2 * RT_QK + RT_W:2 * RT_QK + 2 * RT_W])
        o_all = jnp.concatenate(normed[s * RT_HEADS:(s + 1) * RT_HEADS], axis=1) * gn * gate
        o_ref[0, s, 0] = o_all.astype(o_ref.dtype)
        s_scr[s] = s_new[s]

    @pl.when(ci == pl.num_programs(1) - 1)
    def _():
        for s, h in items:
            s1_ref[s, h] = s_scr[s, h * RT_DK:(h + 1) * RT_DK, h * RT_DV:(h + 1) * RT_DV]


def _ret_group(prt, lp, bsz, lpad, n_valid_len, pos0, st):
    c, nc, nb = _slot_plan(bsz, lpad)
    n_valid = c if nc > 1 else n_valid_len
    half = RT_DK // 2
    inv = ROPE_BASE ** (-np.arange(half, dtype=np.float64) / half)
    ang = (pos0 + np.arange(lpad, dtype=np.float64))[:, None] * inv[None, :]
    cos = np.tile(np.cos(ang), (1, 2 * RT_HEADS))
    sin = np.tile(np.concatenate([-np.sin(ang), np.sin(ang)], axis=1), (1, RT_HEADS))
    loggamma = np.log(1.0 - np.exp2(-5.0 - np.arange(RT_HEADS, dtype=np.float64)))
    gcum = loggamma[:, None] * np.arange(1, c + 1, dtype=np.float64)[None, :]
    idx = np.arange(c)
    dec = np.where(idx[:, None] >= idx[None, :], np.exp(gcum[:, :, None] - gcum[:, None, :]), 0.0)
    qd = np.repeat(np.exp(gcum).T, RT_DK, axis=1)
    kdec = np.where(idx[None, :] < n_valid, np.exp(gcum[:, n_valid - 1:n_valid] - gcum), 0.0)
    kd = np.repeat(kdec.T, RT_DK, axis=1)
    sdec = np.exp(gcum[:, n_valid - 1])
    dm = np.zeros((RT_QK, RT_W))
    for h in range(RT_HEADS):
        dm[h * RT_DK:(h + 1) * RT_DK, h * RT_DV:(h + 1) * RT_DV] = sdec[h]
    cos, sin, qd, kd, dec, dm = (jnp.asarray(a, F32) for a in (cos, sin, qd, kd, dec, dm))
    gn = lp['ret_gn_g'].reshape(1, RT_W)
    full = lambda a: pl.BlockSpec(a.shape, lambda b, j: (0,) * a.ndim)
    posspec = pl.BlockSpec((c, RT_QK), lambda b, j: (j, 0))
    s0, s_in, s_out, s_shape, alias, alias_specs = _state_io(st, nb, (RT_HEADS, RT_DK, RT_DV))
    inputs = [_slot_view(prt, bsz, nb, nc, c), cos, sin, qd, kd, dec, dm, gn, s0]
    o, s1 = pl.pallas_call(
        functools.partial(_ret_kernel, c=c, nb=nb, n_alias=len(alias)),
        out_shape=[jax.ShapeDtypeStruct((bsz // nb, nb, nc, c, RT_W), BF16), s_shape],
        grid=(bsz // nb, nc),
        in_specs=[_slot_spec(nb, c, RT_COLS), posspec, posspec,
                  full(qd), full(kd), full(dec), full(dm), full(gn), s_in] + alias_specs,
        out_specs=[_slot_spec(nb, c, RT_W), s_out],
        scratch_shapes=[pltpu.VMEM((nb, RT_QK, RT_W), F32)],
        input_output_aliases={len(inputs): 1} if alias else {},
        compiler_params=_cparams(("parallel", "arbitrary")),
        name="ret_chunk",
    )(*inputs, *alias)
    return o.reshape(bsz * lpad, RT_W), s1


def _merge_kernel(x_ref, o1, o2, o3, o4, gate_ref, w1, w2, w3, w4, wo_ref, out_ref, *, d):
    acc = None
    for i, (o, w) in enumerate(((o1, w1), (o2, w2), (o3, w3), (o4, w4))):
        term = _sigmoid(gate_ref[:, i * d:(i + 1) * d]) * _dot(o[...], w[...])
        acc = term if acc is None else acc + term
    out_ref[...] = x_ref[...] + _dot(acc.astype(BF16), wo_ref[...])


def _merge(x, outs, gate, wouts, wo):
    n, d = x.shape
    tm = _pick(n, (256, 128, 64, 32, 16))
    row = lambda w: pl.BlockSpec((tm, w), lambda i: (i, 0))
    full = lambda a: pl.BlockSpec(a.shape, lambda i: (0,) * a.ndim)
    return pl.pallas_call(
        functools.partial(_merge_kernel, d=d),
        out_shape=jax.ShapeDtypeStruct((n, d), F32),
        grid=(n // tm,),
        in_specs=[row(d)] + [row(o.shape[1]) for o in outs] + [row(N_BRANCH * d)]
        + [full(w) for w in wouts] + [full(wo)],
        out_specs=row(d),
        compiler_params=_cparams(("parallel",)),
        name="merge",
    )(x, *outs, gate, *wouts, wo)


def _route_kernel(x_ref, g_ref, rw_ref, rb_ref, xn_ref, col_ref, row_ref, cnt_ref):
    tm = x_ref.shape[0]
    lane = _iota((tm, LANES), 1)
    lanef = lane.astype(F32)
    x = x_ref[...]
    xn = x * lax.rsqrt(jnp.mean(x * x, axis=-1, keepdims=True) + NORM_EPS) * g_ref[...]
    xn_ref[...] = xn.astype(BF16)
    rb = rb_ref[...]
    neg = jnp.float32(-jnp.inf)
    logits = _dot3(xn, rw_ref[...])
    glog = jnp.where(lane < N_GROUPS, logits[:, 0:LANES] + rb[0:1], neg)
    gmax = jnp.max(glog, axis=-1, keepdims=True)
    gsum = jnp.sum(jnp.exp(glog - gmax), axis=-1, keepdims=True)
    gidx = jnp.min(jnp.where(glog == gmax, lanef, float(LANES)), axis=-1, keepdims=True)
    gp = 1.0 / gsum
    in_group = (lanef >= gidx * EXPERTS_PER_GROUP) & (lanef < (gidx + 1.0) * EXPERTS_PER_GROUP)
    elog = jnp.where(in_group, logits[:, LANES:2 * LANES] + rb[1:2], neg)
    emax = jnp.max(elog, axis=-1, keepdims=True)
    eexp = jnp.exp(elog - emax)
    ep = eexp / jnp.sum(eexp, axis=-1, keepdims=True)
    ep = jnp.where(in_group, ep, -1.0)
    p1 = jnp.max(ep, axis=-1, keepdims=True)
    i1 = jnp.min(jnp.where(ep == p1, lanef, float(LANES)), axis=-1, keepdims=True)
    ep2 = jnp.where(lanef == i1, -1.0, ep)
    p2 = jnp.max(ep2, axis=-1, keepdims=True)
    i2 = jnp.min(jnp.where(ep2 == p2, lanef, float(LANES)), axis=-1, keepdims=True)
    denom = p1 + p2
    wt1 = gp * p1 / denom
    wt2 = gp * p2 / denom

    onehot = ((lanef == i1) | (lanef == i2)).astype(F32)
    cnt = jnp.sum(onehot, axis=0, keepdims=True)
    before = (_iota((LANES, LANES), 0) < _iota((LANES, LANES), 1)).astype(BF16)
    off = _segsum(jnp.broadcast_to(cnt, (SUBLANES, LANES)), before)[0:1]
    tri = (_iota((tm, tm), 0) >= _iota((tm, tm), 1)).astype(BF16)
    slot = off + _dot(tri, onehot.astype(BF16)) - 1.0
    pos1 = jnp.sum(jnp.where(lanef == i1, slot, 0.0), axis=-1, keepdims=True)
    pos2 = jnp.sum(jnp.where(lanef == i2, slot, 0.0), axis=-1, keepdims=True)
    col = jnp.where(lane == 0, pos1, jnp.where(lane == 1, pos2, jnp.where(lane == 2, wt1,
                                                                         jnp.where(lane == 3, wt2, 0.0))))
    col_ref[...] = col
    row_ref[0] = jnp.transpose(col)[0:SUBLANES]
    cnt_ref[0] = jnp.concatenate([cnt, off, jnp.zeros((SUBLANES - 2, LANES), F32)], axis=0)


def _moe_kernel(cnt_sm, off_sm, x_ref, xn_ref, col_ref, row_ref, wg_ref, wu_ref, wd_ref, out_ref,
                xs_scr, ys_scr, ws_scr, *, pb, rb, eps):
    i = pl.program_id(0)
    eg = pl.program_id(1)
    tm, d = x_ref.shape
    ns = 2 * tm

    @pl.when(eg == 0)
    def _():
        rowd = row_ref[0]
        pos1, pos2, wt1, wt2 = rowd[0:1], rowd[1:2], rowd[2:3], rowd[3:4]
        for blk in range(ns // pb):
            sid = (_iota((pb, tm), 0) + blk * pb).astype(F32)
            m1 = sid == pos1
            m2 = sid == pos2
            xs_scr[blk * pb:(blk + 1) * pb, :] = _dot((m1 | m2).astype(BF16), xn_ref[...]).astype(BF16)
            wsl = jnp.sum(jnp.where(m1, wt1, 0.0) + jnp.where(m2, wt2, 0.0), axis=-1, keepdims=True)
            ws_scr[blk * pb:(blk + 1) * pb, :] = jnp.broadcast_to(wsl, (pb, LANES))
        ys_scr[...] = jnp.zeros_like(ys_scr)

    for k in range(eps):
        cnt = cnt_sm[i, eg * eps + k]
        off = off_sm[i, eg * eps + k]
        start = (off // BF16_ROWS) * BF16_ROWS

        def body(j, carry, k=k, cnt=cnt, off=off, start=start):
            own = start + j * rb
            r0 = pl.multiple_of(jnp.minimum(own, ns - rb), BF16_ROWS)
            xb = xs_scr[pl.ds(r0, rb), :]
            hid = _silu(_dot(xb, wg_ref[k])) * _dot(xb, wu_ref[k])
            y = _dot(hid.astype(BF16), wd_ref[k])
            srow = r0 + _iota((rb, d), 0)
            mine = (srow >= jnp.maximum(off, own)) & (srow < off + cnt)
            ys_scr[pl.ds(r0, rb), :] += jnp.where(mine, y, 0.0)
            return carry

        lax.fori_loop(0, (off + cnt - start + rb - 1) // rb, body, 0)

    @pl.when(eg == pl.num_programs(1) - 1)
    def _():
        col = col_ref[...]
        pos1, pos2 = col[:, 0:1], col[:, 1:2]
        acc = x_ref[...]
        for blk in range(ns // pb):
            sid = (_iota((tm, pb), 1) + blk * pb).astype(F32)
            pt = ((sid == pos1) | (sid == pos2)).astype(BF16)
            ysw = ys_scr[blk * pb:(blk + 1) * pb, :] * ws_scr[blk * pb:(blk + 1) * pb, 0:1]
            hi, lo = _split2(ysw)
            acc = acc + _dot(pt, hi) + _dot(pt, lo)
        out_ref[...] = acc


def _moe(x, lp, experts):
    n, d = x.shape
    tm = _pick(n, (MOE_TILE, 512, 256, 128, 64))
    nt = n // tm
    de = experts[0].shape[-1]
    rw = jnp.zeros((d, 2 * LANES), F32)
    rw = rw.at[:, :N_GROUPS].set(lp['router_group']).at[:, LANES:LANES + N_EXPERTS].set(lp['router_expert'])
    rbias = jnp.zeros((SUBLANES, LANES), F32)
    rbias = rbias.at[0, :N_GROUPS].set(lp['router_group_b']).at[1, :N_EXPERTS].set(lp['router_expert_b'])
    g = lp['norm2_g'].reshape(1, d)
    full1 = lambda a: pl.BlockSpec(a.shape, lambda i: (0,) * a.ndim)
    xn, col, row, cnt = pl.pallas_call(
        _route_kernel,
        out_shape=[jax.ShapeDtypeStruct((n, d), BF16), jax.ShapeDtypeStruct((n, LANES), F32),
                   jax.ShapeDtypeStruct((nt, SUBLANES, tm), F32), jax.ShapeDtypeStruct((nt, SUBLANES, LANES), F32)],
        grid=(nt,),
        in_specs=[pl.BlockSpec((tm, d), lambda i: (i, 0)), full1(g), full1(rw), full1(rbias)],
        out_specs=[pl.BlockSpec((tm, d), lambda i: (i, 0)), pl.BlockSpec((tm, LANES), lambda i: (i, 0)),
                   pl.BlockSpec((1, SUBLANES, tm), lambda i: (i, 0, 0)),
                   pl.BlockSpec((1, SUBLANES, LANES), lambda i: (i, 0, 0))],
        compiler_params=_cparams(("parallel",)),
        name="moe_route",
    )(x, g, rw, rbias)
    cnt_i = cnt[:, 0, :N_EXPERTS].astype(jnp.int32)
    off_i = cnt[:, 1, :N_EXPERTS].astype(jnp.int32)
    ns = 2 * tm
    pb = min(MOE_SLOT_BLOCK, ns)
    rb = min(MOE_ROW_BLOCK, ns)
    grid_spec = pltpu.PrefetchScalarGridSpec(
        num_scalar_prefetch=2,
        grid=(nt, N_EXPERTS // MOE_EXPERTS_PER_STEP),
        in_specs=[pl.BlockSpec((tm, d), lambda i, e, c, o: (i, 0)),
                  pl.BlockSpec((tm, d), lambda i, e, c, o: (i, 0)),
                  pl.BlockSpec((tm, LANES), lambda i, e, c, o: (i, 0)),
                  pl.BlockSpec((1, SUBLANES, tm), lambda i, e, c, o: (i, 0, 0)),
                  pl.BlockSpec((MOE_EXPERTS_PER_STEP, d, de), lambda i, e, c, o: (e, 0, 0)),
                  pl.BlockSpec((MOE_EXPERTS_PER_STEP, d, de), lambda i, e, c, o: (e, 0, 0)),
                  pl.BlockSpec((MOE_EXPERTS_PER_STEP, de, d), lambda i, e, c, o: (e, 0, 0))],
        out_specs=pl.BlockSpec((tm, d), lambda i, e, c, o: (i, 0)),
        scratch_shapes=[pltpu.VMEM((ns, d), BF16), pltpu.VMEM((ns, d), F32), pltpu.VMEM((ns, LANES), F32)],
    )
    return pl.pallas_call(
        functools.partial(_moe_kernel, pb=pb, rb=rb, eps=MOE_EXPERTS_PER_STEP),
        out_shape=jax.ShapeDtypeStruct((n, d), F32),
        grid_spec=grid_spec,
        compiler_params=pltpu.CompilerParams(dimension_semantics=("parallel", "arbitrary"),
                                             vmem_limit_bytes=MOE_VMEM_LIMIT),
        name="moe",
    )(cnt_i, off_i, x, xn, col, row, *experts)


def _layer(x, wts, lp, lb, gr):
    b, seq, lpad, big = gr['bsz'], gr['seq'], gr['lpad'], gr['big']
    xn = _rmsnorm(x, lp['norm1_g'], BF16)
    p_rw, p_gqkv, p_gz, p_gba, p_hg, p_rt, p_gate = (
        _matmul(xn, w, f"proj{i}") for i, w in enumerate(wts['proj']))
    o_rw, shift1, wkv1 = _wkv_group(p_rw, lp, b, lpad, seq, gr['shift0'], big['wkv'])
    o_gd, gdn1, conv1 = _gdn_group(p_gqkv, p_gz, p_gba, lp, b, lpad, seq, gr['conv0'], big['gdn'])
    o_hg, hgrn1 = _hgrn_group(p_hg, lb, lp, b, lpad, seq, big['hgrn'])
    o_rt, ret1 = _ret_group(p_rt, lp, b, lpad, seq, gr['pos0'], big['ret'])
    x = _merge(x, [o_rw, o_gd, o_hg, o_rt], p_gate, wts['out'], wts['w_o'])
    x = _moe(x, lp, wts['experts'])
    return x, shift1, conv1, dict(wkv=wkv1, gdn=gdn1, hgrn=hgrn1, ret=ret1)


def _layer_weights(lp, d):
    offs = np.cumsum([0, RW_COLS, GD_QKV, GD_W, 2 * GD_HEADS, HG_COLS, RT_COLS, N_BRANCH * d])
    seg = [lp['w_in'][:, offs[i]:offs[i + 1]] for i in range(7)]
    seg[3] = jnp.pad(seg[3], ((0, 0), (0, LANES - 2 * GD_HEADS)))
    return dict(
        proj=[s.astype(BF16) for s in seg],
        out=[lp[n].astype(BF16) for n in ('w_out_rwkv', 'w_out_gdn', 'w_out_hgrn', 'w_out_ret')],
        w_o=lp['w_o'].astype(BF16),
        experts=[lp[n].astype(BF16) for n in ('moe_w_gate', 'moe_w_up', 'moe_w_down')])


def kernel(x_prompt, x_sample, state_rwkv_shift, state_rwkv_wkv, state_gdn_conv, state_gdn, state_hgrn, state_ret, norm1_g, w_in, rwkv_mu, rwkv_w0, rwkv_w2, rwkv_a0, rwkv_a2, rwkv_g2, rwkv_k_k, rwkv_k_a, rwkv_r_k, rwkv_ln_g, rwkv_ln_b, w_out_rwkv, gdn_conv, gdn_a_log, gdn_dt_bias, gdn_norm_g, w_out_gdn, hgrn_lb_logits, hgrn_norm_g, w_out_hgrn, ret_gn_g, w_out_ret, w_o, norm2_g, router_group, router_group_b, router_expert, router_expert_b, moe_w_gate, moe_w_up, moe_w_down, final_norm_g):
    params = dict(norm1_g=norm1_g, w_in=w_in, rwkv_mu=rwkv_mu, rwkv_w0=rwkv_w0, rwkv_w2=rwkv_w2,
                  rwkv_a0=rwkv_a0, rwkv_a2=rwkv_a2, rwkv_g2=rwkv_g2, rwkv_k_k=rwkv_k_k,
                  rwkv_k_a=rwkv_k_a, rwkv_r_k=rwkv_r_k, rwkv_ln_g=rwkv_ln_g, rwkv_ln_b=rwkv_ln_b,
                  w_out_rwkv=w_out_rwkv, gdn_conv=gdn_conv, gdn_a_log=gdn_a_log,
                  gdn_dt_bias=gdn_dt_bias, gdn_norm_g=gdn_norm_g, w_out_gdn=w_out_gdn,
                  hgrn_norm_g=hgrn_norm_g, w_out_hgrn=w_out_hgrn, ret_gn_g=ret_gn_g,
                  w_out_ret=w_out_ret, w_o=w_o, norm2_g=norm2_g, router_group=router_group,
                  router_group_b=router_group_b, router_expert=router_expert,
                  router_expert_b=router_expert_b, moe_w_gate=moe_w_gate, moe_w_up=moe_w_up,
                  moe_w_down=moe_w_down)
    depth = w_in.shape[0]
    bp, lp_len, d = x_prompt.shape
    bs, ls, _ = x_sample.shape
    ls_pad = -(-ls // SAMPLE_PAD_LEN) * SAMPLE_PAD_LEN
    sm = jax.nn.softmax(hgrn_lb_logits.astype(F32), axis=0)
    lower_bounds = jnp.cumsum(sm, axis=0) - sm[0]

    big_in = dict(wkv=state_rwkv_wkv, gdn=state_gdn, hgrn=state_hgrn, ret=state_ret)
    xp = x_prompt.reshape(bp * lp_len, d)
    xs = jnp.pad(x_sample, ((0, 0), (0, ls_pad - ls), (0, 0))).reshape(bs * ls_pad, d)

    p_zero = {k: jnp.zeros((1, bp) + a.shape[2:], F32) for k, a in big_in.items()}
    p_big = {k: jnp.zeros((depth, bp) + a.shape[2:], F32) for k, a in big_in.items()}
    s_big = {k: jnp.zeros((depth, bs) + a.shape[2:], F32) for k, a in big_in.items()}
    p_shift, p_conv, s_shift, s_conv = [], [], [], []
    for layer in range(depth):
        lpar = {name: arr[layer] for name, arr in params.items()}
        wts = _layer_weights(lpar, d)
        prompt = dict(bsz=bp, seq=lp_len, lpad=lp_len, pos0=0,
                      shift0=jnp.zeros((bp,) + state_rwkv_shift.shape[2:], F32),
                      conv0=jnp.zeros((bp,) + state_gdn_conv.shape[2:], F32),
                      big={k: dict(src=(p_zero[k], 0), layer=layer, depth=depth, prev=p_big[k]) for k in big_in})
        sample = dict(bsz=bs, seq=ls, lpad=ls_pad, pos0=PAST_LEN,
                      shift0=state_rwkv_shift[layer].astype(F32), conv0=state_gdn_conv[layer].astype(F32),
                      big={k: dict(src=(big_in[k].astype(F32), layer), layer=layer, depth=depth, prev=s_big[k])
                           for k in big_in})
        xp, sh, cv, p_big = _layer(xp, wts, lpar, lower_bounds[layer], prompt)
        p_shift.append(sh)
        p_conv.append(cv)
        xs, sh, cv, s_big = _layer(xs, wts, lpar, lower_bounds[layer], sample)
        s_shift.append(sh)
        s_conv.append(cv)

    y_prompt = _rmsnorm(xp, final_norm_g, F32).reshape(bp, lp_len, d)
    y_sample = _rmsnorm(xs, final_norm_g, F32).reshape(bs, ls_pad, d)[:, :ls]
    return (y_prompt, y_sample,
            jnp.stack(p_shift), p_big['wkv'], jnp.stack(p_conv), p_big['gdn'], p_big['hgrn'], p_big['ret'],
            jnp.stack(s_shift), s_big['wkv'], jnp.stack(s_conv), s_big['gdn'], s_big['hgrn'], s_big['ret'])
```

```python
import functools
import math

import numpy as np
import jax
import jax.numpy as jnp
from jax import lax
from jax.experimental import pallas as pl
from jax.experimental.pallas import tpu as pltpu

F32 = jnp.float32
BF16 = jnp.bfloat16

NORM_EPS = 1e-6
LB_FLOOR = 1e-30
PAST_LEN = 16384
RW_HEADS = 8
RW_HD = 64
RW_W = RW_HEADS * RW_HD
RW_DECAY_LORA = 64
RW_AAA_LORA = 64
RW_GATE_LORA = 128
RW_COLS = 3 * RW_W + RW_DECAY_LORA + RW_AAA_LORA + RW_GATE_LORA
RW_GN_EPS = 64e-5
GD_HEADS = 4
GD_DK = 128
GD_DV = 128
GD_QKV = GD_HEADS * (2 * GD_DK + GD_DV)
GD_W = GD_HEADS * GD_DV
CONV_W = 4
HG_HEADS = 4
HG_DF = 128
HG_DV = 128
HG_W = HG_HEADS * HG_DV
HG_COLS = 2 * HG_HEADS * HG_DF + 2 * HG_W
HG_SUB = 16
RT_HEADS = 4
RT_DK = 64
RT_DV = 128
RT_W = RT_HEADS * RT_DV
RT_QK = RT_HEADS * RT_DK
RT_COLS = 2 * RT_QK + 2 * RT_W
ROPE_BASE = 10000.0
N_BRANCH = 4
N_GROUPS = 4
EXPERTS_PER_GROUP = 8
N_EXPERTS = N_GROUPS * EXPERTS_PER_GROUP

LANES = 128
SUBLANES = 8
SAMPLE_PAD_LEN = 8
CHUNK_ROWS = 64
SLOTS_MIN = 4
WKV_SLOTS = 2
VMEM_LIMIT = 48 * 1024 * 1024
MOE_TILE = 1024
MOE_SLOT_BLOCK = 256
MOE_ROW_BLOCK = 128
MOE_EXPERTS_PER_STEP = 4
BF16_ROWS = 16
MOE_VMEM_LIMIT = 56 * 1024 * 1024


def _pick(n, cands):
    for c in cands:
        if n % c == 0:
            return c
    raise ValueError(f"no tile for {n} in {cands}")


def _cparams(sem):
    return pltpu.CompilerParams(dimension_semantics=sem, vmem_limit_bytes=VMEM_LIMIT)


def _dot(a, b):
    return lax.dot_general(a, b, (((1,), (0,)), ((), ())), preferred_element_type=F32)


def _softplus(x):
    return jnp.maximum(x, 0.0) + jnp.log1p(jnp.exp(-jnp.abs(x)))


def _sigmoid(x):
    return jax.nn.sigmoid(x)


def _silu(x):
    return x * jax.nn.sigmoid(x)


def _segsum(x, hm):
    hi = x.astype(BF16)
    r1 = x - hi.astype(F32)
    mid = r1.astype(BF16)
    lo = (r1 - mid.astype(F32)).astype(BF16)
    return _dot(hi, hm) + _dot(mid, hm) + _dot(lo, hm)


_DIMS = {'nn': (((1,), (0,)), ((), ())), 'nt': (((1,), (1,)), ((), ())), 'tn': (((0,), (0,)), ((), ()))}


def _split2(x):
    hi = x.astype(BF16)
    return hi, (x - hi.astype(F32)).astype(BF16)


def _dot3(a, b, form='nn'):
    ah, al = _split2(a)
    bh, bl = _split2(b)
    f = lambda x, y: lax.dot_general(x, y, _DIMS[form], preferred_element_type=F32)
    free = 1 if form == 'tn' else 0
    m = a.shape[free]
    both = f(jnp.concatenate([ah, al], axis=free), bh)
    return both[0:m] + both[m:2 * m] + f(ah, bl)


def _mdot(mask, x):
    hi = x.astype(BF16)
    r1 = x - hi.astype(F32)
    mid = r1.astype(BF16)
    lo = (r1 - mid.astype(F32)).astype(BF16)
    return _dot(mask, hi) + _dot(mask, mid) + _dot(mask, lo)


def _unit_lower_inverses(ms, rid, cid, c, expand=None):
    prod = _dot3 if expand is None else (lambda x, y: _dot3(x, expand(y)))
    same = lambda s: (rid // s) == (cid // s)
    ns = [jnp.where(same(SUBLANES), -m, 0.0) for m in ms]
    n2s = [prod(n, n) for n in ns]
    n4s = [prod(n2, n2) for n2 in n2s]
    eye = (rid == cid).astype(F32)
    ps = [eye + n for n in ns]
    ps = [p + prod(p, n2) for p, n2 in zip(ps, n2s)]
    ps = [p + prod(p, n4) for p, n4 in zip(ps, n4s)]
    s = SUBLANES
    while s < c:
        offs = [jnp.where(same(2 * s) & jnp.logical_not(same(s)), m, 0.0) for m in ms]
        ts = [prod(p, off) for p, off in zip(ps, offs)]
        ps = [p - prod(t, p) for p, t in zip(ps, ts)]
        s *= 2
    return ps


def _iota(shape, dim):
    return lax.broadcasted_iota(jnp.int32, shape, dim)


def _rms_kernel(x_ref, g_ref, o_ref):
    x = x_ref[...]
    ms = jnp.mean(x * x, axis=-1, keepdims=True)
    o_ref[...] = (x * lax.rsqrt(ms + NORM_EPS) * g_ref[...]).astype(o_ref.dtype)


def _rmsnorm(x, g, out_dtype):
    n, d = x.shape
    tm = _pick(n, (1024, 512, 256, 128, 64, 32, 16))
    return pl.pallas_call(
        _rms_kernel,
        out_shape=jax.ShapeDtypeStruct((n, d), out_dtype),
        grid=(n // tm,),
        in_specs=[pl.BlockSpec((tm, d), lambda i: (i, 0)), pl.BlockSpec((1, d), lambda i: (0, 0))],
        out_specs=pl.BlockSpec((tm, d), lambda i: (i, 0)),
        compiler_params=_cparams(("parallel",)),
        name="rmsnorm",
    )(x, g.reshape(1, d))


def _mm_kernel(x_ref, w_ref, o_ref):
    o_ref[...] = _dot(x_ref[...], w_ref[...])


def _matmul(x, w, name):
    n, k = x.shape
    m = w.shape[1]
    tm = _pick(n, (2048, 1024, 512, 256, 128, 64, 32, 16))
    tn = _pick(m, (1024, 896, 768, 512, 256, 128))
    return pl.pallas_call(
        _mm_kernel,
        out_shape=jax.ShapeDtypeStruct((n, m), F32),
        grid=(n // tm, m // tn),
        in_specs=[pl.BlockSpec((tm, k), lambda i, j: (i, 0)), pl.BlockSpec((k, tn), lambda i, j: (0, j))],
        out_specs=pl.BlockSpec((tm, tn), lambda i, j: (i, j)),
        compiler_params=_cparams(("parallel", "parallel")),
        name=name,
    )(x, w)


def _head_sum_matrix(width, seg):
    i = np.arange(width)
    return jnp.asarray((i[:, None] // seg) == (i[None, :] // seg), BF16)


def _wkv_chunk_kernel(p_ref, first_ref, mu_ref, vec_ref, w2_ref, a2_ref, g2_ref, hm_ref, ln_ref, s0_ref,
                      *rest, cs, nseq, ns, n_valid, multi_chunk, n_alias):
    o_ref, s1_ref, shift_ref, carry_scr, ht_scr = rest[n_alias:]
    ci = pl.program_id(1)
    rows = cs * nseq
    npair = RW_HEADS // 2
    f_zero = jnp.zeros((RW_HD, RW_HD), F32)
    slots = range(ns)
    pairs = range(npair)
    seqs = [(s, q) for s in slots for q in range(nseq)]
    sidx = lambda s, q: s * nseq + q

    @pl.when(ci == 0)
    def _():
        for s, q in seqs:
            carry_scr[sidx(s, q)] = first_ref[sidx(s, q)]
            for pr in pairs:
                top = jnp.concatenate([s0_ref[sidx(s, q), 2 * pr], f_zero], axis=1)
                bot = jnp.concatenate([f_zero, s0_ref[sidx(s, q), 2 * pr + 1]], axis=1)
                ht_scr[sidx(s, q), pr] = jnp.concatenate([top, bot], axis=0)

    vec = vec_ref[...]
    w0, a0, k_k, k_a, r_k = vec[0:1], vec[1:2], vec[2:3], vec[3:4], vec[4:5]
    hm = hm_ref[...]
    mu = mu_ref[...]
    row_w = _iota((rows, RW_COLS), 0)
    row_f = _iota((rows, RW_W), 0)
    rid = _iota((rows, rows), 0)
    cid = _iota((rows, rows), 1)
    tri = ((rid >= cid) & (rid // cs == cid // cs)).astype(BF16)

    ps_, vs_, gs_, bonus_ = [], [], [], []
    abar, bbar, kbar, rbar, btil, ktil, w_last = [], [], [], [], [], [], []
    for s in slots:
        p = p_ref[0, s, 0]
        prev = pltpu.roll(p, 1, 0)
        for q in range(nseq):
            prev = jnp.where(row_w == q * cs, carry_scr[sidx(s, q)], prev)
        h = p + (prev - p) * mu
        r = h[:, 0:RW_W]
        k = h[:, RW_W:2 * RW_W]
        v = h[:, 2 * RW_W:3 * RW_W]
        lo = h[:, 3 * RW_W:3 * RW_W + LANES]
        gl = h[:, 3 * RW_W + LANES:3 * RW_W + 2 * LANES]
        w_log = -_softplus(-(w0 + _dot3(jnp.tanh(lo), w2_ref[...]))) - 0.5
        logw = -jnp.exp(w_log)
        a = _sigmoid(a0 + _dot3(lo, a2_ref[...]))
        gs_.append(_dot3(_sigmoid(gl), g2_ref[...]))
        kk = k * k_k
        kk = kk * lax.rsqrt(_segsum(kk * kk, hm) + NORM_EPS)
        km = k * (1.0 + (a - 1.0) * k_a)
        bb = kk * a
        alpha = -kk
        bonus_.append(_segsum(r * km * r_k, hm) * v)
        if n_valid < cs:
            valid = (row_f % cs) < n_valid
            logw, alpha, bb, km = (jnp.where(valid, t, 0.0) for t in (logw, alpha, bb, km))
        gcum = _mdot(tri, logw)
        g_last = jnp.concatenate(
            [jnp.broadcast_to(gcum[q * cs + cs - 1:q * cs + cs], (cs, RW_W)) for q in range(nseq)], axis=0)
        emg = jnp.exp(-gcum)
        etil = jnp.exp(g_last - gcum)
        ps_.append(p)
        vs_.append(v)
        abar.append(alpha * jnp.exp(gcum - logw))
        bbar.append(bb * emg)
        kbar.append(km * emg)
        rbar.append(r * jnp.exp(gcum))
        btil.append(bb * etil)
        ktil.append(km * etil)
        w_last.append(jnp.exp(g_last))

    prow = _iota((rows, LANES), 0)
    plane = _iota((rows, LANES), 1)
    pcol = plane % RW_HD
    same_seq = (prow // cs) == (pcol // cs)
    strict = (pcol < prow) & same_seq
    incl = (pcol <= prow) & same_seq
    first_head = plane < RW_HD
    blk_mask = (_iota((LANES, LANES), 0) // RW_HD) == (_iota((LANES, LANES), 1) // RW_HD)

    def expand(x):
        return jnp.concatenate([jnp.where(first_head, x, 0.0), jnp.where(first_head, 0.0, x)], axis=0)

    items = [(s, pr) for s in slots for pr in pairs]
    ps = lambda x, pr: x[:, pr * LANES:(pr + 1) * LANES]
    seq_rows = lambda x, q: x[q * cs:(q + 1) * cs]

    lhs = [jnp.concatenate([ps(abar[s], pr), ps(rbar[s], pr)], axis=0) for s, pr in items]
    xbk = [_dot3(l, jnp.concatenate([expand(ps(bbar[s], pr)), expand(ps(kbar[s], pr))], axis=0), 'nt')
           for l, (s, pr) in zip(lhs, items)]
    xb = [x[:, 0:LANES] for x in xbk]
    xk = [x[:, LANES:2 * LANES] for x in xbk]
    a_m = [jnp.where(strict, x[0:rows], 0.0) for x in xb]
    rb_m = [jnp.where(incl, x[rows:2 * rows], 0.0) for x in xb]
    b_m = [jnp.where(strict, x[0:rows], 0.0) for x in xk]
    rk_m = [jnp.where(incl, x[rows:2 * rows], 0.0) for x in xk]
    pinv = _unit_lower_inverses([-m for m in a_m], prow, pcol, cs, expand=expand)

    ht_old = {(s, q, pr): ht_scr[sidx(s, q), pr] for s, q in seqs for pr in pairs}
    xh = {(s, q, pr): _dot3(jnp.concatenate([seq_rows(ps(abar[s], pr), q), seq_rows(ps(rbar[s], pr), q)], axis=0),
                            ht_old[(s, q, pr)], 'nt') for s, q in seqs for pr in pairs}
    ah = [jnp.concatenate([xh[(s, q, pr)][0:cs] for q in range(nseq)], axis=0) for s, pr in items]
    rh = [jnp.concatenate([xh[(s, q, pr)][cs:2 * cs] for q in range(nseq)], axis=0) for s, pr in items]
    vexp = [expand(ps(vs_[s], pr)) for s, pr in items]
    bv_rkv = [_dot3(jnp.concatenate([b_, rk_], axis=0), ve) for b_, rk_, ve in zip(b_m, rk_m, vexp)]
    rhs = [a_ + x[0:rows] for a_, x in zip(ah, bv_rkv)]
    u = [_dot3(pi, expand(rh_)) for pi, rh_ in zip(pinv, rhs)]
    o_items = [rh_ + x[rows:2 * rows] + _dot3(rb_, expand(u_)) for rh_, x, rb_, u_ in zip(rh, bv_rkv, rb_m, u)]
    for i, (s, pr) in enumerate(items):
        for q in range(nseq):
            uv = jnp.concatenate([seq_rows(u[i], q), seq_rows(ps(vs_[s], pr), q)], axis=0)
            bk = jnp.concatenate([seq_rows(ps(btil[s], pr), q), seq_rows(ps(ktil[s], pr), q)], axis=0)
            upd = jnp.where(blk_mask, _dot3(uv, bk, 'tn'), 0.0)
            ht_scr[sidx(s, q), pr] = ht_old[(s, q, pr)] * seq_rows(ps(w_last[s], pr), q)[0:1] + upd

    ln = ln_ref[...]
    for s in slots:
        o = jnp.concatenate(o_items[s * npair:(s + 1) * npair], axis=1)
        mean = _segsum(o, hm) * (1.0 / RW_HD)
        xc = o - mean
        var = _segsum(xc * xc, hm) * (1.0 / RW_HD)
        y = xc * lax.rsqrt(var + RW_GN_EPS) * ln[0:1] + ln[1:2]
        o_ref[0, s, 0] = ((y + bonus_[s]) * gs_[s]).astype(o_ref.dtype)

    if multi_chunk:
        for s, q in seqs:
            carry_scr[sidx(s, q)] = ps_[s][q * cs + cs - 1:q * cs + cs]

    @pl.when(ci == pl.num_programs(1) - 1)
    def _():
        for s, q in seqs:
            shift_ref[sidx(s, q)] = ps_[s][q * cs + n_valid - 1:q * cs + n_valid]
            for pr in pairs:
                ht = ht_scr[sidx(s, q), pr]
                s1_ref[sidx(s, q), 2 * pr] = ht[0:RW_HD, 0:RW_HD]
                s1_ref[sidx(s, q), 2 * pr + 1] = ht[RW_HD:2 * RW_HD, RW_HD:2 * RW_HD]


def _state_io(st, per_step, tail):
    arr, idx = st['src']
    layer = st['layer']
    zeros = (0,) * len(tail)
    in_spec = pl.BlockSpec((None, per_step) + tail, lambda b, j: (idx, b) + zeros)
    out_spec = pl.BlockSpec((None, per_step) + tail, lambda b, j: (layer, b) + zeros)
    out_shape = jax.ShapeDtypeStruct((st['depth'], arr.shape[1]) + tail, F32)
    alias = [] if st['prev'] is None else [st['prev']]
    return arr, in_spec, out_spec, out_shape, alias, [pl.BlockSpec(memory_space=pl.ANY)] * len(alias)


def _wkv_group(p_rw, lp, bsz, lpad, n_valid_len, shift0, st):
    rows = RW_HD
    if lpad >= rows:
        cs, nseq = rows, 1
    else:
        cs, nseq = lpad, rows // lpad
    nc = lpad // cs
    n_valid = cs if nc > 1 else n_valid_len
    units = bsz // nseq
    ns = WKV_SLOTS * (2 if nseq == 1 else 1)
    if units % ns:
        ns = 1
    vec = jnp.zeros((SUBLANES, RW_W), F32)
    vec = vec.at[0].set(lp['rwkv_w0']).at[1].set(lp['rwkv_a0']).at[2].set(lp['rwkv_k_k'])
    vec = vec.at[3].set(lp['rwkv_k_a']).at[4].set(lp['rwkv_r_k'].reshape(RW_W))
    zeros = jnp.zeros((RW_DECAY_LORA, RW_W), F32)
    w2p = jnp.concatenate([lp['rwkv_w2'], zeros], axis=0)
    a2p = jnp.concatenate([zeros, lp['rwkv_a2']], axis=0)
    hm = _head_sum_matrix(RW_W, RW_HD)
    ln = jnp.zeros((SUBLANES, RW_W), F32).at[0].set(lp['rwkv_ln_g']).at[1].set(lp['rwkv_ln_b'])
    mu = lp['rwkv_mu'].reshape(1, RW_COLS)
    first = shift0.reshape(bsz, 1, RW_COLS)
    full = lambda a: pl.BlockSpec(a.shape, lambda b, j: (0,) * a.ndim)
    nq = ns * nseq
    s0, s_in, s_out, s_shape, alias, alias_specs = _state_io(st, nq, (RW_HEADS, RW_HD, RW_HD))
    fspec = pl.BlockSpec((nq, 1, RW_COLS), lambda b, j: (b, 0, 0))
    inputs = [_slot_view(p_rw, units, ns, nc, rows), first, mu, vec, w2p, a2p, lp['rwkv_g2'], hm, ln, s0]
    o, s1, shift1 = pl.pallas_call(
        functools.partial(_wkv_chunk_kernel, cs=cs, nseq=nseq, ns=ns, n_valid=n_valid, multi_chunk=nc > 1,
                          n_alias=len(alias)),
        out_shape=[jax.ShapeDtypeStruct((units // ns, ns, nc, rows, RW_W), BF16), s_shape,
                   jax.ShapeDtypeStruct((bsz, 1, RW_COLS), F32)],
        grid=(units // ns, nc),
        in_specs=[_slot_spec(ns, rows, RW_COLS), fspec, full(mu), full(vec),
                  full(w2p), full(a2p), full(lp['rwkv_g2']), full(hm), full(ln), s_in] + alias_specs,
        out_specs=[_slot_spec(ns, rows, RW_W), s_out, fspec],
        scratch_shapes=[pltpu.VMEM((nq, 1, RW_COLS), F32),
                        pltpu.VMEM((nq, RW_HEADS // 2, LANES, LANES), F32)],
        input_output_aliases={len(inputs): 1} if alias else {},
        compiler_params=_cparams(("parallel", "arbitrary")),
        name="wkv_chunk",
    )(*inputs, *alias)
    return o.reshape(bsz * lpad, RW_W), shift1.reshape(bsz, RW_COLS), s1


def _gdn_kernel(qkv_ref, z_ref, ba_ref, convw_ref, hp_ref, ng_ref, tail0_ref, s0_ref,
                *rest, c, nb, n_valid, n_alias):
    o_ref, s1_ref, conv_ref, ext_scr, s_scr = rest[n_alias:]
    ci = pl.program_id(1)
    last = ci == pl.num_programs(1) - 1
    slots = range(nb)
    heads = range(GD_HEADS)
    items = [(s, h) for s in slots for h in heads]

    @pl.when(ci == 0)
    def _():
        for s in slots:
            ext_scr[s, 0:SUBLANES, :] = tail0_ref[s]
            s_scr[s] = s0_ref[s]

    cw = convw_ref[...]
    hp = hp_ref[...]
    ng = ng_ref[...]
    off = SUBLANES - (CONV_W - 1)
    rid = _iota((c, c), 0)
    cid = _iota((c, c), 1)
    incl = rid >= cid
    tri = incl.astype(BF16)
    strict_l = (rid > cid).astype(F32)
    kbase, vbase = GD_HEADS * GD_DK, 2 * GD_HEADS * GD_DK

    cqs, beta_alls, g_alls, gcum_alls = [], [], [], []
    for s in slots:
        ext_scr[s, SUBLANES:SUBLANES + c, :] = qkv_ref[0, s, 0]
        cq = ext_scr[s, off:off + c, :] * cw[0:1]
        for j in range(1, CONV_W):
            cq = cq + ext_scr[s, off + j:off + j + c, :] * cw[j:j + 1]

        @pl.when(last)
        def _():
            conv_ref[s] = ext_scr[s, off + n_valid:off + n_valid + CONV_W - 1, :]

        ext_scr[s, 0:SUBLANES, :] = ext_scr[s, c:c + SUBLANES, :]
        cqs.append(_silu(cq))
        ba = ba_ref[0, s, 0]
        beta_all = _sigmoid(ba)
        g_all = -jnp.exp(hp[0:1]) * _softplus(ba + hp[1:2])
        if n_valid < c:
            valid = _iota((c, LANES), 0) < n_valid
            beta_all = jnp.where(valid, beta_all, 0.0)
            g_all = jnp.where(valid, g_all, 0.0)
        beta_alls.append(beta_all)
        g_alls.append(g_all)
        gcum_alls.append(_mdot(tri, g_all))

    qs = [cqs[s][:, h * GD_DK:(h + 1) * GD_DK] for s, h in items]
    ks = [cqs[s][:, kbase + h * GD_DK:kbase + (h + 1) * GD_DK] for s, h in items]
    vs = [cqs[s][:, vbase + h * GD_DV:vbase + (h + 1) * GD_DV] for s, h in items]
    qs = [q * lax.rsqrt(jnp.sum(q * q, axis=-1, keepdims=True) + NORM_EPS) * (GD_DK ** -0.5) for q in qs]
    ks = [k * lax.rsqrt(jnp.sum(k * k, axis=-1, keepdims=True) + NORM_EPS) for k in ks]
    betas = [beta_alls[s][:, h:h + 1] for s, h in items]
    g_cols = [g_alls[s][:, GD_HEADS + h:GD_HEADS + h + 1] for s, h in items]
    gcs = [gcum_alls[s][:, GD_HEADS + h:GD_HEADS + h + 1] for s, h in items]
    glasts = [gc[c - 1:c, :] for gc in gcs]
    egcs = [jnp.exp(gc) for gc in gcs]
    s_old = [s_scr[s, h] for s, h in items]
    gcum_rows = [jnp.transpose(gcum_alls[s]) for s in slots]
    decs = [jnp.where(incl, jnp.exp(jnp.minimum(gc - gcum_rows[s][GD_HEADS + h:GD_HEADS + h + 1, :], 0.0)), 0.0)
            for (s, h), gc in zip(items, gcs)]
    qk_kts = [_dot3(jnp.concatenate([q, k], axis=0), k, 'nt') for q, k in zip(qs, ks)]
    ms = [strict_l * b * qk[c:2 * c] * dec for b, qk, dec in zip(betas, qk_kts, decs)]
    rhss = [jnp.concatenate([v * b, k * (b * e)], axis=1) for v, k, b, e in zip(vs, ks, betas, egcs)]
    if c >= 2 * SUBLANES:
        xss = [_dot3(p, rhs) for p, rhs in zip(_unit_lower_inverses(ms, rid, cid, c), rhss)]
    else:
        strict_u = (rid < cid).astype(F32)
        dts = [_mdot((rid < cid).astype(BF16), g * (rid <= cid).astype(F32)) for g in g_cols]
        mts = [strict_u * _dot3(k, k * b, 'nt') * jnp.exp(jnp.minimum(dt, 0.0))
               for k, b, dt in zip(ks, betas, dts)]
        xss = rhss
        for i in range(1, c):
            row_i = _iota((c, GD_DV + GD_DK), 0) == i
            xss = [jnp.where(row_i, xs - jnp.sum(mt[:, i:i + 1] * xs, axis=0, keepdims=True), xs)
                   for xs, mt in zip(xss, mts)]
    ws_qss = [_dot3(jnp.concatenate([xs[:, GD_DV:GD_DV + GD_DK], q * e], axis=0), s)
              for xs, q, e, s in zip(xss, qs, egcs, s_old)]
    v_news = [xs[:, 0:GD_DV] - wq[0:c] for xs, wq in zip(xss, ws_qss)]
    os_ = [wq[c:2 * c] + _dot3(qk[0:c] * dec, vn) for wq, qk, dec, vn in zip(ws_qss, qk_kts, decs, v_news)]
    s_new = [s * jnp.exp(gl) + _dot3(k * jnp.exp(gl - gc), vn, 'tn')
             for s, gl, gc, k, vn in zip(s_old, glasts, gcs, ks, v_news)]
    os_ = [o * lax.rsqrt(jnp.mean(o * o, axis=-1, keepdims=True) + NORM_EPS) * ng for o in os_]
    for s in slots:
        z = z_ref[0, s, 0]
        o_ref[0, s, 0] = jnp.concatenate(
            [(os_[s * GD_HEADS + h] * _silu(z[:, h * GD_DV:(h + 1) * GD_DV])).astype(o_ref.dtype) for h in heads],
            axis=1)
    for i, (s, h) in enumerate(items):
        s_scr[s, h] = s_new[i]

    @pl.when(last)
    def _():
        for s in slots:
            s1_ref[s] = s_scr[s]


def _slot_plan(bsz, lpad):
    c = _pick(lpad, (CHUNK_ROWS, 32, 16, 8))
    nb = max(SLOTS_MIN, CHUNK_ROWS // c)
    if bsz % nb:
        nb = 1
    return c, lpad // c, nb


def _slot_view(a, bsz, nb, nc, c):
    return a.reshape(bsz // nb, nb, nc, c, a.shape[-1])


def _slot_spec(nb, c, w):
    return pl.BlockSpec((1, nb, 1, c, w), lambda b, j: (b, 0, j, 0, 0))


def _gdn_group(pqkv, pz, pba, lp, bsz, lpad, n_valid_len, conv0, st):
    c, nc, nb = _slot_plan(bsz, lpad)
    n_valid = c if nc > 1 else n_valid_len
    tail0 = jnp.concatenate([jnp.zeros((bsz, SUBLANES - (CONV_W - 1), GD_QKV), F32), conv0], axis=1)
    hp = jnp.zeros((SUBLANES, LANES), F32)
    hp = hp.at[0, GD_HEADS:2 * GD_HEADS].set(lp['gdn_a_log']).at[1, GD_HEADS:2 * GD_HEADS].set(lp['gdn_dt_bias'])
    ng = lp['gdn_norm_g'].reshape(1, GD_DV)
    view = lambda a: _slot_view(a, bsz, nb, nc, c)
    full = lambda a: pl.BlockSpec(a.shape, lambda b, j: (0,) * a.ndim)
    s0, s_in, s_out, s_shape, alias, alias_specs = _state_io(st, nb, (GD_HEADS, GD_DK, GD_DV))
    inputs = [view(pqkv), view(pz), view(pba), lp['gdn_conv'], hp, ng, tail0, s0]
    o, s1, conv1 = pl.pallas_call(
        functools.partial(_gdn_kernel, c=c, nb=nb, n_valid=n_valid, n_alias=len(alias)),
        out_shape=[jax.ShapeDtypeStruct((bsz // nb, nb, nc, c, GD_W), BF16), s_shape,
                   jax.ShapeDtypeStruct((bsz, CONV_W - 1, GD_QKV), F32)],
        grid=(bsz // nb, nc),
        in_specs=[_slot_spec(nb, c, GD_QKV), _slot_spec(nb, c, GD_W), _slot_spec(nb, c, LANES),
                  full(lp['gdn_conv']), full(hp), full(ng),
                  pl.BlockSpec((nb, SUBLANES, GD_QKV), lambda b, j: (b, 0, 0)), s_in] + alias_specs,
        out_specs=[_slot_spec(nb, c, GD_W), s_out,
                   pl.BlockSpec((nb, CONV_W - 1, GD_QKV), lambda b, j: (b, 0, 0))],
        scratch_shapes=[pltpu.VMEM((nb, c + SUBLANES, GD_QKV), F32),
                        pltpu.VMEM((nb, GD_HEADS, GD_DK, GD_DV), F32)],
        input_output_aliases={len(inputs): 1} if alias else {},
        compiler_params=_cparams(("parallel", "arbitrary")),
        name="gdn_chunk",
    )(*inputs, *alias)
    return o.reshape(bsz * lpad, GD_W), s1, conv1


def _hgrn_kernel(p_ref, lbv_ref, ng_ref, s0_ref, *rest, c, nb, n_valid, n_alias):
    o_ref, s1_ref, st_scr = rest[n_alias:]
    ci = pl.program_id(1)
    slots = range(nb)
    heads = range(HG_HEADS)
    items = [(s, h) for s in slots for h in heads]

    @pl.when(ci == 0)
    def _():
        for s, h in items:
            st_scr[s, h] = jnp.transpose(s0_ref[s, h])

    rid = _iota((c, c), 0)
    cid = _iota((c, c), 1)
    tri = (rid >= cid).astype(BF16)
    w = HG_HEADS * HG_DF
    rows = _iota((c, w), 0)
    sb = min(c, HG_SUB)
    row_in_sub = rows % sb
    lbv = lbv_ref[...]
    hs = lambda x, h: x[:, h * HG_DF:(h + 1) * HG_DF]

    qs, ks, vs, fs, bcs = [], [], [], [], []
    for s in slots:
        p = p_ref[0, s, 0]
        pf = p[:, w:2 * w]
        a = lbv[0:1]
        b = lbv[1:2] - _softplus(-pf)
        logf = jnp.maximum(a, b) + jnp.log1p(jnp.exp(-jnp.abs(a - b)))
        k = lbv[2:3] * _sigmoid(-pf)
        if n_valid < c:
            logf = jnp.where(rows < n_valid, logf, 0.0)
            k = jnp.where(rows < n_valid, k, 0.0)
        qs.append(_silu(p[:, 0:w]))
        ks.append(k)
        vs.append(p[:, 2 * w:2 * w + HG_W])
        fs.append(jnp.exp(logf))
        bcs.append(_mdot(tri, logf))
    st_old = [st_scr[s, h] for s, h in items]
    qes = [q * jnp.exp(bc) for q, bc in zip(qs, bcs)]
    os_ = [_dot3(hs(qes[s], h), st, 'nt') for (s, h), st in zip(items, st_old)]
    decay = [None] * nb
    for delta in range(sb):
        prods, v_ss = [], []
        for s in slots:
            if delta == 0:
                k_s, v_s = ks[s], vs[s]
                qkd = qs[s] * k_s
            else:
                k_s, v_s = pltpu.roll(ks[s], delta, 0), pltpu.roll(vs[s], delta, 0)
                f_s = fs[s] if delta == 1 else pltpu.roll(fs[s], delta - 1, 0)
                decay[s] = f_s if delta == 1 else decay[s] * f_s
                qkd = qs[s] * k_s * decay[s]
            prods.append(jnp.where(row_in_sub >= delta, qkd, 0.0))
            v_ss.append(v_s)
        os_ = [o + jnp.sum(hs(prods[s], h), axis=-1, keepdims=True) * hs(v_ss[s], h)
               for (s, h), o in zip(items, os_)]
    if c > sb:
        parts = [[jnp.zeros((sb, HG_DV), F32)] for _ in items]
        for r0 in range(sb, c, sb):
            qis, kps = [], []
            for s in slots:
                ref = bcs[s][r0 - 1:r0]
                qis.append(qs[s][r0:r0 + sb] * jnp.exp(bcs[s][r0:r0 + sb] - ref))
                kps.append(ks[s][0:r0] * jnp.exp(ref - bcs[s][0:r0]))
            att = [_dot3(hs(qis[s], h), hs(kps[s], h), 'nt') for s, h in items]
            for i, (s, h) in enumerate(items):
                parts[i].append(_dot3(att[i], hs(vs[s], h)[0:r0]))
        os_ = [o + jnp.concatenate(p, axis=0) for o, p in zip(os_, parts)]
    blasts = [bc[c - 1:c] for bc in bcs]
    kds = [k * jnp.exp(bl - bc) for k, bl, bc in zip(ks, blasts, bcs)]
    ebs = [jnp.exp(bl) for bl in blasts]
    st_new = [st * hs(ebs[s], h) + _dot3(hs(vs[s], h), hs(kds[s], h), 'tn') for (s, h), st in zip(items, st_old)]
    os_ = [o * lax.rsqrt(jnp.mean(o * o, axis=-1, keepdims=True) + NORM_EPS) for o in os_]
    ng = ng_ref[...]
    for s in slots:
        gate = _sigmoid(p_ref[0, s, 0][:, 2 * w + HG_W:2 * w + 2 * HG_W])
        o_all = jnp.concatenate(os_[s * HG_HEADS:(s + 1) * HG_HEADS], axis=1) * ng * gate
        o_ref[0, s, 0] = o_all.astype(o_ref.dtype)
    for i, (s, h) in enumerate(items):
        st_scr[s, h] = st_new[i]

    @pl.when(ci == pl.num_programs(1) - 1)
    def _():
        for s, h in items:
            s1_ref[s, h] = jnp.transpose(st_scr[s, h])


def _hgrn_group(phg, lb, lp, bsz, lpad, n_valid_len, st):
    c, nc, nb = _slot_plan(bsz, lpad)
    n_valid = c if nc > 1 else n_valid_len
    lbv = jnp.zeros((SUBLANES, HG_HEADS * HG_DF), F32)
    lbv = lbv.at[0].set(jnp.log(jnp.maximum(lb, LB_FLOOR))).at[1].set(jnp.log1p(-lb)).at[2].set(1.0 - lb)
    ng = lp['hgrn_norm_g'].reshape(1, HG_W)
    full = lambda a: pl.BlockSpec(a.shape, lambda b, j: (0,) * a.ndim)
    s0, s_in, s_out, s_shape, alias, alias_specs = _state_io(st, nb, (HG_HEADS, HG_DF, HG_DV))
    inputs = [_slot_view(phg, bsz, nb, nc, c), lbv, ng, s0]
    o, s1 = pl.pallas_call(
        functools.partial(_hgrn_kernel, c=c, nb=nb, n_valid=n_valid, n_alias=len(alias)),
        out_shape=[jax.ShapeDtypeStruct((bsz // nb, nb, nc, c, HG_W), BF16), s_shape],
        grid=(bsz // nb, nc),
        in_specs=[_slot_spec(nb, c, HG_COLS), full(lbv), full(ng), s_in] + alias_specs,
        out_specs=[_slot_spec(nb, c, HG_W), s_out],
        scratch_shapes=[pltpu.VMEM((nb, HG_HEADS, HG_DV, HG_DF), F32)],
        input_output_aliases={len(inputs): 1} if alias else {},
        compiler_params=_cparams(("parallel", "arbitrary")),
        name="hgrn_chunk",
    )(*inputs, *alias)
    return o.reshape(bsz * lpad, HG_W), s1


def _ret_kernel(p_ref, cos_ref, sin_ref, qd_ref, kd_ref, dec_ref, dm_ref, gn_ref, s0_ref,
                *rest, c, nb, n_alias):
    o_ref, s1_ref, s_scr = rest[n_alias:]
    ci = pl.program_id(1)
    slots = range(nb)
    heads = range(RT_HEADS)
    items = [(s, h) for s in slots for h in heads]

    @pl.when(ci == 0)
    def _():
        for s in slots:
            s_scr[s] = jnp.zeros((RT_QK, RT_W), F32)
            for h in heads:
                s_scr[s, h * RT_DK:(h + 1) * RT_DK, h * RT_DV:(h + 1) * RT_DV] = s0_ref[s, h]

    cos = cos_ref[...]
    sin = sin_ref[...]
    lane = _iota((c, RT_QK), 1)
    first_half = (lane % RT_DK) < (RT_DK // 2)

    def rope(x):
        partner = jnp.where(first_half, pltpu.roll(x, RT_QK - RT_DK // 2, 1), pltpu.roll(x, RT_DK // 2, 1))
        return x * cos + partner * sin

    ps = [p_ref[0, s, 0] for s in slots]
    qs = [rope(p[:, 0:RT_QK]) for p in ps]
    ks = [rope(p[:, RT_QK:2 * RT_QK]) * (RT_DK ** -0.5) for p in ps]
    pvs = [p[:, 2 * RT_QK:2 * RT_QK + RT_W] for p in ps]
    sblks = [s_scr[s] for s in slots]
    qd = qd_ref[...]
    o_inters = [_dot3(q * qd, sblk) for q, sblk in zip(qs, sblks)]
    qks = [_dot3(jnp.concatenate([jnp.where(lane // RT_DK == h, q, 0.0) for h in heads], axis=0), k, 'nt')
           for q, k in zip(qs, ks)]
    attns = [qks[s][h * c:(h + 1) * c] * dec_ref[h] for s, h in items]
    os_ = [o_inters[s][:, h * RT_DV:(h + 1) * RT_DV] + _dot3(attn, pvs[s][:, h * RT_DV:(h + 1) * RT_DV])
           for (s, h), attn in zip(items, attns)]
    dm = dm_ref[...]
    kd = kd_ref[...]
    s_new = [sblk * dm + jnp.where(dm > 0.0, _dot3(k * kd, pv, 'tn'), 0.0) for sblk, k, pv in zip(sblks, ks, pvs)]
    gn = gn_ref[...]
    normed = []
    for o in os_:
        xc = o - jnp.mean(o, axis=-1, keepdims=True)
        normed.append(xc * lax.rsqrt(jnp.mean(xc * xc, axis=-1, keepdims=True) + NORM_EPS))
    for s in slots:
        gate = _silu(ps[s][:, 2 * RT_QK + RT_W:2 * RT_QK + 2 * RT_W])
        o_all = jnp.concatenate(normed[s * RT_HEADS:(s + 1) * RT_HEADS], axis=1) * gn * gate
        o_ref[0, s, 0] = o_all.astype(o_ref.dtype)
        s_scr[s] = s_new[s]

    @pl.when(ci == pl.num_programs(1) - 1)
    def _():
        for s, h in items:
            s1_ref[s, h] = s_scr[s, h * RT_DK:(h + 1) * RT_DK, h * RT_DV:(h + 1) * RT_DV]


def _ret_group(prt, lp, bsz, lpad, n_valid_len, pos0, st):
    c, nc, nb = _slot_plan(bsz, lpad)
    n_valid = c if nc > 1 else n_valid_len
    half = RT_DK // 2
    inv = ROPE_BASE ** (-np.arange(half, dtype=np.float64) / half)
    ang = (pos0 + np.arange(lpad, dtype=np.float64))[:, None] * inv[None, :]
    cos = np.tile(np.cos(ang), (1, 2 * RT_HEADS))
    sin = np.tile(np.concatenate([-np.sin(ang), np.sin(ang)], axis=1), (1, RT_HEADS))
    loggamma = np.log(1.0 - np.exp2(-5.0 - np.arange(RT_HEADS, dtype=np.float64)))
    gcum = loggamma[:, None] * np.arange(1, c + 1, dtype=np.float64)[None, :]
    idx = np.arange(c)
    dec = np.where(idx[:, None] >= idx[None, :], np.exp(gcum[:, :, None] - gcum[:, None, :]), 0.0)
    qd = np.repeat(np.exp(gcum).T, RT_DK, axis=1)
    kdec = np.where(idx[None, :] < n_valid, np.exp(gcum[:, n_valid - 1:n_valid] - gcum), 0.0)
    kd = np.repeat(kdec.T, RT_DK, axis=1)
    sdec = np.exp(gcum[:, n_valid - 1])
    dm = np.zeros((RT_QK, RT_W))
    for h in range(RT_HEADS):
        dm[h * RT_DK:(h + 1) * RT_DK, h * RT_DV:(h + 1) * RT_DV] = sdec[h]
    cos, sin, qd, kd, dec, dm = (jnp.asarray(a, F32) for a in (cos, sin, qd, kd, dec, dm))
    gn = lp['ret_gn_g'].reshape(1, RT_W)
    full = lambda a: pl.BlockSpec(a.shape, lambda b, j: (0,) * a.ndim)
    posspec = pl.BlockSpec((c, RT_QK), lambda b, j: (j, 0))
    s0, s_in, s_out, s_shape, alias, alias_specs = _state_io(st, nb, (RT_HEADS, RT_DK, RT_DV))
    inputs = [_slot_view(prt, bsz, nb, nc, c), cos, sin, qd, kd, dec, dm, gn, s0]
    o, s1 = pl.pallas_call(
        functools.partial(_ret_kernel, c=c, nb=nb, n_alias=len(alias)),
        out_shape=[jax.ShapeDtypeStruct((bsz // nb, nb, nc, c, RT_W), BF16), s_shape],
        grid=(bsz // nb, nc),
        in_specs=[_slot_spec(nb, c, RT_COLS), posspec, posspec,
                  full(qd), full(kd), full(dec), full(dm), full(gn), s_in] + alias_specs,
        out_specs=[_slot_spec(nb, c, RT_W), s_out],
        scratch_shapes=[pltpu.VMEM((nb, RT_QK, RT_W), F32)],
        input_output_aliases={len(inputs): 1} if alias else {},
        compiler_params=_cparams(("parallel", "arbitrary")),
        name="ret_chunk",
    )(*inputs, *alias)
    return o.reshape(bsz * lpad, RT_W), s1


def _merge_kernel(x_ref, o1, o2, o3, o4, gate_ref, w1, w2, w3, w4, wo_ref, out_ref, *, d):
    acc = None
    for i, (o, w) in enumerate(((o1, w1), (o2, w2), (o3, w3), (o4, w4))):
        term = _sigmoid(gate_ref[:, i * d:(i + 1) * d]) * _dot(o[...], w[...])
        acc = term if acc is None else acc + term
    out_ref[...] = x_ref[...] + _dot(acc.astype(BF16), wo_ref[...])


def _merge(x, outs, gate, wouts, wo):
    n, d = x.shape
    tm = _pick(n, (256, 128, 64, 32, 16))
    row = lambda w: pl.BlockSpec((tm, w), lambda i: (i, 0))
    full = lambda a: pl.BlockSpec(a.shape, lambda i: (0,) * a.ndim)
    return pl.pallas_call(
        functools.partial(_merge_kernel, d=d),
        out_shape=jax.ShapeDtypeStruct((n, d), F32),
        grid=(n // tm,),
        in_specs=[row(d)] + [row(o.shape[1]) for o in outs] + [row(N_BRANCH * d)]
        + [full(w) for w in wouts] + [full(wo)],
        out_specs=row(d),
        compiler_params=_cparams(("parallel",)),
        name="merge",
    )(x, *outs, gate, *wouts, wo)


def _route_kernel(x_ref, g_ref, rw_ref, rb_ref, xn_ref, col_ref, row_ref, cnt_ref):
    tm = x_ref.shape[0]
    lane = _iota((tm, LANES), 1)
    lanef = lane.astype(F32)
    x = x_ref[...]
    xn = x * lax.rsqrt(jnp.mean(x * x, axis=-1, keepdims=True) + NORM_EPS) * g_ref[...]
    xn_ref[...] = xn.astype(BF16)
    rb = rb_ref[...]
    neg = jnp.float32(-jnp.inf)
    logits = _dot3(xn, rw_ref[...])
    glog = jnp.where(lane < N_GROUPS, logits[:, 0:LANES] + rb[0:1], neg)
    gmax = jnp.max(glog, axis=-1, keepdims=True)
    gsum = jnp.sum(jnp.exp(glog - gmax), axis=-1, keepdims=True)
    gidx = jnp.min(jnp.where(glog == gmax, lanef, float(LANES)), axis=-1, keepdims=True)
    gp = 1.0 / gsum
    in_group = (lanef >= gidx * EXPERTS_PER_GROUP) & (lanef < (gidx + 1.0) * EXPERTS_PER_GROUP)
    elog = jnp.where(in_group, logits[:, LANES:2 * LANES] + rb[1:2], neg)
    emax = jnp.max(elog, axis=-1, keepdims=True)
    eexp = jnp.exp(elog - emax)
    ep = eexp / jnp.sum(eexp, axis=-1, keepdims=True)
    ep = jnp.where(in_group, ep, -1.0)
    p1 = jnp.max(ep, axis=-1, keepdims=True)
    i1 = jnp.min(jnp.where(ep == p1, lanef, float(LANES)), axis=-1, keepdims=True)
    ep2 = jnp.where(lanef == i1, -1.0, ep)
    p2 = jnp.max(ep2, axis=-1, keepdims=True)
    i2 = jnp.min(jnp.where(ep2 == p2, lanef, float(LANES)), axis=-1, keepdims=True)
    denom = p1 + p2
    wt1 = gp * p1 / denom
    wt2 = gp * p2 / denom

    onehot = ((lanef == i1) | (lanef == i2)).astype(F32)
    cnt = jnp.sum(onehot, axis=0, keepdims=True)
    before = (_iota((LANES, LANES), 0) < _iota((LANES, LANES), 1)).astype(BF16)
    off = _segsum(jnp.broadcast_to(cnt, (SUBLANES, LANES)), before)[0:1]
    tri = (_iota((tm, tm), 0) >= _iota((tm, tm), 1)).astype(BF16)
    slot = off + _dot(tri, onehot.astype(BF16)) - 1.0
    pos1 = jnp.sum(jnp.where(lanef == i1, slot, 0.0), axis=-1, keepdims=True)
    pos2 = jnp.sum(jnp.where(lanef == i2, slot, 0.0), axis=-1, keepdims=True)
    col = jnp.where(lane == 0, pos1, jnp.where(lane == 1, pos2, jnp.where(lane == 2, wt1,
                                                                         jnp.where(lane == 3, wt2, 0.0))))
    col_ref[...] = col
    row_ref[0] = jnp.transpose(col)[0:SUBLANES]
    cnt_ref[0] = jnp.concatenate([cnt, off, jnp.zeros((SUBLANES - 2, LANES), F32)], axis=0)


def _moe_kernel(cnt_sm, off_sm, x_ref, xn_ref, col_ref, row_ref, wg_ref, wu_ref, wd_ref, out_ref,
                xs_scr, ys_scr, ws_scr, *, pb, rb, eps):
    i = pl.program_id(0)
    eg = pl.program_id(1)
    tm, d = x_ref.shape
    ns = 2 * tm

    @pl.when(eg == 0)
    def _():
        rowd = row_ref[0]
        pos1, pos2, wt1, wt2 = rowd[0:1], rowd[1:2], rowd[2:3], rowd[3:4]
        for blk in range(ns // pb):
            sid = (_iota((pb, tm), 0) + blk * pb).astype(F32)
            m1 = sid == pos1
            m2 = sid == pos2
            xs_scr[blk * pb:(blk + 1) * pb, :] = _dot((m1 | m2).astype(BF16), xn_ref[...]).astype(BF16)
            wsl = jnp.sum(jnp.where(m1, wt1, 0.0) + jnp.where(m2, wt2, 0.0), axis=-1, keepdims=True)
            ws_scr[blk * pb:(blk + 1) * pb, :] = jnp.broadcast_to(wsl, (pb, LANES))
        ys_scr[...] = jnp.zeros_like(ys_scr)

    for k in range(eps):
        cnt = cnt_sm[i, eg * eps + k]
        off = off_sm[i, eg * eps + k]
        start = (off // BF16_ROWS) * BF16_ROWS

        def body(j, carry, k=k, cnt=cnt, off=off, start=start):
            own = start + j * rb
            r0 = pl.multiple_of(jnp.minimum(own, ns - rb), BF16_ROWS)
            xb = xs_scr[pl.ds(r0, rb), :]
            hid = _silu(_dot(xb, wg_ref[k])) * _dot(xb, wu_ref[k])
            y = _dot(hid.astype(BF16), wd_ref[k])
            srow = r0 + _iota((rb, d), 0)
            mine = (srow >= jnp.maximum(off, own)) & (srow < off + cnt)
            ys_scr[pl.ds(r0, rb), :] += jnp.where(mine, y, 0.0)
            return carry

        lax.fori_loop(0, (off + cnt - start + rb - 1) // rb, body, 0)

    @pl.when(eg == pl.num_programs(1) - 1)
    def _():
        col = col_ref[...]
        pos1, pos2 = col[:, 0:1], col[:, 1:2]
        acc = x_ref[...]
        for blk in range(ns // pb):
            sid = (_iota((tm, pb), 1) + blk * pb).astype(F32)
            pt = ((sid == pos1) | (sid == pos2)).astype(BF16)
            ysw = ys_scr[blk * pb:(blk + 1) * pb, :] * ws_scr[blk * pb:(blk + 1) * pb, 0:1]
            hi, lo = _split2(ysw)
            acc = acc + _dot(pt, hi) + _dot(pt, lo)
        out_ref[...] = acc


def _moe(x, lp, experts):
    n, d = x.shape
    tm = _pick(n, (MOE_TILE, 512, 256, 128, 64))
    nt = n // tm
    de = experts[0].shape[-1]
    rw = jnp.zeros((d, 2 * LANES), F32)
    rw = rw.at[:, :N_GROUPS].set(lp['router_group']).at[:, LANES:LANES + N_EXPERTS].set(lp['router_expert'])
    rbias = jnp.zeros((SUBLANES, LANES), F32)
    rbias = rbias.at[0, :N_GROUPS].set(lp['router_group_b']).at[1, :N_EXPERTS].set(lp['router_expert_b'])
    g = lp['norm2_g'].reshape(1, d)
    full1 = lambda a: pl.BlockSpec(a.shape, lambda i: (0,) * a.ndim)
    xn, col, row, cnt = pl.pallas_call(
        _route_kernel,
        out_shape=[jax.ShapeDtypeStruct((n, d), BF16), jax.ShapeDtypeStruct((n, LANES), F32),
                   jax.ShapeDtypeStruct((nt, SUBLANES, tm), F32), jax.ShapeDtypeStruct((nt, SUBLANES, LANES), F32)],
        grid=(nt,),
        in_specs=[pl.BlockSpec((tm, d), lambda i: (i, 0)), full1(g), full1(rw), full1(rbias)],
        out_specs=[pl.BlockSpec((tm, d), lambda i: (i, 0)), pl.BlockSpec((tm, LANES), lambda i: (i, 0)),
                   pl.BlockSpec((1, SUBLANES, tm), lambda i: (i, 0, 0)),
                   pl.BlockSpec((1, SUBLANES, LANES), lambda i: (i, 0, 0))],
        compiler_params=_cparams(("parallel",)),
        name="moe_route",
    )(x, g, rw, rbias)
    cnt_i = cnt[:, 0, :N_EXPERTS].astype(jnp.int32)
    off_i = cnt[:, 1, :N_EXPERTS].astype(jnp.int32)
    ns = 2 * tm
    pb = min(MOE_SLOT_BLOCK, ns)
    rb = min(MOE_ROW_BLOCK, ns)
    grid_spec = pltpu.PrefetchScalarGridSpec(
        num_scalar_prefetch=2,
        grid=(nt, N_EXPERTS // MOE_EXPERTS_PER_STEP),
        in_specs=[pl.BlockSpec((tm, d), lambda i, e, c, o: (i, 0)),
                  pl.BlockSpec((tm, d), lambda i, e, c, o: (i, 0)),
                  pl.BlockSpec((tm, LANES), lambda i, e, c, o: (i, 0)),
                  pl.BlockSpec((1, SUBLANES, tm), lambda i, e, c, o: (i, 0, 0)),
                  pl.BlockSpec((MOE_EXPERTS_PER_STEP, d, de), lambda i, e, c, o: (e, 0, 0)),
                  pl.BlockSpec((MOE_EXPERTS_PER_STEP, d, de), lambda i, e, c, o: (e, 0, 0)),
                  pl.BlockSpec((MOE_EXPERTS_PER_STEP, de, d), lambda i, e, c, o: (e, 0, 0))],
        out_specs=pl.BlockSpec((tm, d), lambda i, e, c, o: (i, 0)),
        scratch_shapes=[pltpu.VMEM((ns, d), BF16), pltpu.VMEM((ns, d), F32), pltpu.VMEM((ns, LANES), F32)],
    )
    return pl.pallas_call(
        functools.partial(_moe_kernel, pb=pb, rb=rb, eps=MOE_EXPERTS_PER_STEP),
        out_shape=jax.ShapeDtypeStruct((n, d), F32),
        grid_spec=grid_spec,
        compiler_params=pltpu.CompilerParams(dimension_semantics=("parallel", "arbitrary"),
                                             vmem_limit_bytes=MOE_VMEM_LIMIT),
        name="moe",
    )(cnt_i, off_i, x, xn, col, row, *experts)


def _layer(x, wts, lp, lb, gr):
    b, seq, lpad, big = gr['bsz'], gr['seq'], gr['lpad'], gr['big']
    xn = _rmsnorm(x, lp['norm1_g'], BF16)
    p_rw, p_gqkv, p_gz, p_gba, p_hg, p_rt, p_gate = (
        _matmul(xn, w, f"proj{i}") for i, w in enumerate(wts['proj']))
    o_rw, shift1, wkv1 = _wkv_group(p_rw, lp, b, lpad, seq, gr['shift0'], big['wkv'])
    o_gd, gdn1, conv1 = _gdn_group(p_gqkv, p_gz, p_gba, lp, b, lpad, seq, gr['conv0'], big['gdn'])
    o_hg, hgrn1 = _hgrn_group(p_hg, lb, lp, b, lpad, seq, big['hgrn'])
    o_rt, ret1 = _ret_group(p_rt, lp, b, lpad, seq, gr['pos0'], big['ret'])
    x = _merge(x, [o_rw, o_gd, o_hg, o_rt], p_gate, wts['out'], wts['w_o'])
    x = _moe(x, lp, wts['experts'])
    return x, shift1, conv1, dict(wkv=wkv1, gdn=gdn1, hgrn=hgrn1, ret=ret1)


def _layer_weights(lp, d):
    offs = np.cumsum([0, RW_COLS, GD_QKV, GD_W, 2 * GD_HEADS, HG_COLS, RT_COLS, N_BRANCH * d])
    seg = [lp['w_in'][:, offs[i]:offs[i + 1]] for i in range(7)]
    seg[3] = jnp.pad(seg[3], ((0, 0), (0, LANES - 2 * GD_HEADS)))
    return dict(
        proj=[s.astype(BF16) for s in seg],
        out=[lp[n].astype(BF16) for n in ('w_out_rwkv', 'w_out_gdn', 'w_out_hgrn', 'w_out_ret')],
        w_o=lp['w_o'].astype(BF16),
        experts=[lp[n].astype(BF16) for n in ('moe_w_gate', 'moe_w_up', 'moe_w_down')])


def kernel(x_prompt, x_sample, state_rwkv_shift, state_rwkv_wkv, state_gdn_conv, state_gdn, state_hgrn, state_ret, norm1_g, w_in, rwkv_mu, rwkv_w0, rwkv_w2, rwkv_a0, rwkv_a2, rwkv_g2, rwkv_k_k, rwkv_k_a, rwkv_r_k, rwkv_ln_g, rwkv_ln_b, w_out_rwkv, gdn_conv, gdn_a_log, gdn_dt_bias, gdn_norm_g, w_out_gdn, hgrn_lb_logits, hgrn_norm_g, w_out_hgrn, ret_gn_g, w_out_ret, w_o, norm2_g, router_group, router_group_b, router_expert, router_expert_b, moe_w_gate, moe_w_up, moe_w_down, final_norm_g):
    params = dict(norm1_g=norm1_g, w_in=w_in, rwkv_mu=rwkv_mu, rwkv_w0=rwkv_w0, rwkv_w2=rwkv_w2,
                  rwkv_a0=rwkv_a0, rwkv_a2=rwkv_a2, rwkv_g2=rwkv_g2, rwkv_k_k=rwkv_k_k,
                  rwkv_k_a=rwkv_k_a, rwkv_r_k=rwkv_r_k, rwkv_ln_g=rwkv_ln_g, rwkv_ln_b=rwkv_ln_b,
                  w_out_rwkv=w_out_rwkv, gdn_conv=gdn_conv, gdn_a_log=gdn_a_log,
                  gdn_dt_bias=gdn_dt_bias, gdn_norm_g=gdn_norm_g, w_out_gdn=w_out_gdn,
                  hgrn_norm_g=hgrn_norm_g, w_out_hgrn=w_out_hgrn, ret_gn_g=ret_gn_g,
                  w_out_ret=w_out_ret, w_o=w_o, norm2_g=norm2_g, router_group=router_group,
                  router_group_b=router_group_b, router_expert=router_expert,
                  router_expert_b=router_expert_b, moe_w_gate=moe_w_gate, moe_w_up=moe_w_up,
                  moe_w_down=moe_w_down)
    depth = w_in.shape[0]
    bp, lp_len, d = x_prompt.shape
    bs, ls, _ = x_sample.shape
    ls_pad = -(-ls // SAMPLE_PAD_LEN) * SAMPLE_PAD_LEN
    sm = jax.nn.softmax(hgrn_lb_logits.astype(F32), axis=0)
    lower_bounds = jnp.cumsum(sm, axis=0) - sm[0]

    big_in = dict(wkv=state_rwkv_wkv, gdn=state_gdn, hgrn=state_hgrn, ret=state_ret)
    xp = x_prompt.reshape(bp * lp_len, d)
    xs = jnp.pad(x_sample, ((0, 0), (0, ls_pad - ls), (0, 0))).reshape(bs * ls_pad, d)

    p_zero = {k: jnp.zeros((1, bp) + a.shape[2:], F32) for k, a in big_in.items()}
    p_big = {k: jnp.zeros((depth, bp) + a.shape[2:], F32) for k, a in big_in.items()}
    s_big = {k: jnp.zeros((depth, bs) + a.shape[2:], F32) for k, a in big_in.items()}
    p_shift, p_conv, s_shift, s_conv = [], [], [], []
    for layer in range(depth):
        lpar = {name: arr[layer] for name, arr in params.items()}
        wts = _layer_weights(lpar, d)
        prompt = dict(bsz=bp, seq=lp_len, lpad=lp_len, pos0=0,
                      shift0=jnp.zeros((bp,) + state_rwkv_shift.shape[2:], F32),
                      conv0=jnp.zeros((bp,) + state_gdn_conv.shape[2:], F32),
                      big={k: dict(src=(p_zero[k], 0), layer=layer, depth=depth, prev=p_big[k]) for k in big_in})
        sample = dict(bsz=bs, seq=ls, lpad=ls_pad, pos0=PAST_LEN,
                      shift0=state_rwkv_shift[layer].astype(F32), conv0=state_gdn_conv[layer].astype(F32),
                      big={k: dict(src=(big_in[k].astype(F32), layer), layer=layer, depth=depth, prev=s_big[k])
                           for k in big_in})
        xp, sh, cv, p_big = _layer(xp, wts, lpar, lower_bounds[layer], prompt)
        p_shift.append(sh)
        p_conv.append(cv)
        xs, sh, cv, s_big = _layer(xs, wts, lpar, lower_bounds[layer], sample)
        s_shift.append(sh)
        s_conv.append(cv)

    y_prompt = _rmsnorm(xp, final_norm_g, F32).reshape(bp, lp_len, d)
    y_sample = _rmsnorm(xs, final_norm_g, F32).reshape(bs, ls_pad, d)[:, :ls]
    return (y_prompt, y_sample,
            jnp.stack(p_shift), p_big['wkv'], jnp.stack(p_conv), p_big['gdn'], p_big['hgrn'], p_big['ret'],
            jnp.stack(s_shift), s_big['wkv'], jnp.stack(s_conv), s_big['gdn'], s_big['hgrn'], s_big['ret'])
```
